```python
import math, functools
import jax, jax.numpy as jnp
from jax import lax
import numpy as np

D_MODEL = 1024
BATCH = 32
SEQ = 256
DEPTH = 2
DEC_BATCH = 4
DEC_SEQ = 4096
PAST_LEN = 256

GRID_W = 64
HEAD_DIM = 64
N_Q_HEADS = 8
N_KV_HEADS = 2
Q_PER_KV = N_Q_HEADS // N_KV_HEADS
ATTN_WIDTH = N_Q_HEADS * HEAD_DIM
KV_WIDTH = N_KV_HEADS * HEAD_DIM
WINDOW = 128
BLOCK = 128
SGU_WIDTH = D_MODEL - ATTN_WIDTH
N_SGU_GROUPS = 8
SGU_GROUP_DIM = SGU_WIDTH // N_SGU_GROUPS
CHUNK = 128
IN_WIDTH = ATTN_WIDTH + 2 * KV_WIDTH + 2 * SGU_WIDTH
OFF_K = ATTN_WIDTH
OFF_V = OFF_K + KV_WIDTH
OFF_U = OFF_V + KV_WIDTH
OFF_G = OFF_U + SGU_WIDTH
D_FF = 2816
N_EXPERTS = 8
TOP_K = 2
D_FF_EXPERT = 2816
N_DENSE = (DEPTH + 1) // 2
N_MOE = DEPTH // 2
ROPE_BASE = 10000.0
ROPE_PAIRS = HEAD_DIM // 4
LN_EPS = 1e-5
DEEPNORM_ALPHA = (2 * DEPTH) ** 0.25
DEEPNORM_BETA = (8 * DEPTH) ** -0.25
ATTN_SCALE = HEAD_DIM ** -0.5
NEG_INF = -1e30

kernel_name = 'hybrid_swa_sgu_deepnorm_dit_step'


def layer_norm(x, g, b):
    xf = x.astype(jnp.float32)
    mu = jnp.mean(xf, axis=-1, keepdims=True)
    xc = xf - mu
    var = jnp.mean(xc * xc, axis=-1, keepdims=True)
    y = xc * lax.rsqrt(var + LN_EPS) * g.astype(jnp.float32) + b.astype(jnp.float32)
    return y.astype(x.dtype)


def adaln(cvec, w, b):
    m = jax.nn.silu(cvec) @ w + b
    return [t[:, None, :] for t in jnp.split(m, 6, axis=-1)]


def split_groups(z):
    B, L = z.shape[:2]
    q = z[..., :OFF_K].reshape(B, L, N_KV_HEADS, Q_PER_KV, HEAD_DIM)
    k = z[..., OFF_K:OFF_V].reshape(B, L, N_KV_HEADS, HEAD_DIM)
    v = z[..., OFF_V:OFF_U].reshape(B, L, N_KV_HEADS, HEAD_DIM)
    u = z[..., OFF_U:OFF_G]
    g = z[..., OFF_G:]
    return q, k, v, u, g


def axial_rope(x):
    T = x.shape[1]
    pos = jnp.arange(T, dtype=jnp.int32)
    row = (pos // GRID_W).astype(jnp.float32)
    col = (pos % GRID_W).astype(jnp.float32)
    inv = ROPE_BASE ** (-jnp.arange(ROPE_PAIRS, dtype=jnp.float32) / ROPE_PAIRS)

    def rot(xh, p):
        ang = p[:, None] * inv[None, :]
        cos = jnp.cos(ang)[:, None, :].astype(x.dtype)
        sin = jnp.sin(ang)[:, None, :].astype(x.dtype)
        x1, x2 = xh[..., :ROPE_PAIRS], xh[..., ROPE_PAIRS:]
        return jnp.concatenate([x1 * cos - x2 * sin, x2 * cos + x1 * sin], axis=-1)

    half = HEAD_DIM // 2
    return jnp.concatenate([rot(x[..., :half], row), rot(x[..., half:], col)], axis=-1)


def sink_softmax(s, sink_b):
    sink_col = jnp.broadcast_to(sink_b, s.shape[:-1] + (1,))
    return jax.nn.softmax(jnp.concatenate([s, sink_col], axis=-1), axis=-1)[..., :-1]


def ctx_attention(q, k, v, sink):
    B, L = q.shape[:2]
    nb = L // BLOCK
    qb = jnp.moveaxis(q.reshape(B, nb, BLOCK, N_KV_HEADS, Q_PER_KV, HEAD_DIM), 1, 0)
    kf = k.astype(jnp.float32)
    vf = v.astype(jnp.float32)
    sink_b = sink.astype(jnp.float32).reshape(N_KV_HEADS, Q_PER_KV)[None, :, :, None, None]

    def one(qblk):
        s = jnp.einsum('bqhgd,bkhd->bhgqk', qblk.astype(jnp.float32) * ATTN_SCALE, kf)
        p = sink_softmax(s, sink_b)
        return jnp.einsum('bhgqk,bkhd->bqhgd', p, vf)

    o = lax.map(one, qb)
    return jnp.moveaxis(o, 0, 1).reshape(B, L, ATTN_WIDTH).astype(q.dtype)


def latent_attention(q, k, v, kc, vc, sink):
    B, T = q.shape[:2]
    nb = T // BLOCK
    Lc = kc.shape[1]
    q = axial_rope(q.reshape(B, T, N_Q_HEADS, HEAD_DIM)).reshape(B, T, N_KV_HEADS, Q_PER_KV, HEAD_DIM)
    k = axial_rope(k)
    pad = ((0, 0), (BLOCK, BLOCK), (0, 0), (0, 0))
    kp = jnp.pad(k, pad)
    vp = jnp.pad(v, pad)
    idx = jnp.arange(nb)[:, None] * BLOCK + jnp.arange(3 * BLOCK)[None, :]
    kb = jnp.moveaxis(kp[:, idx], 1, 0)
    vb = jnp.moveaxis(vp[:, idx], 1, 0)
    qb = jnp.moveaxis(q.reshape(B, nb, BLOCK, N_KV_HEADS, Q_PER_KV, HEAD_DIM), 1, 0)
    blk = jnp.arange(nb, dtype=jnp.int32)
    kcf = kc.astype(jnp.float32)
    vcf = vc.astype(jnp.float32)
    sink_b = sink.astype(jnp.float32).reshape(N_KV_HEADS, Q_PER_KV)[None, :, :, None, None]

    def one(args):
        qblk, kblk, vblk, j = args
        qf = qblk.astype(jnp.float32) * ATTN_SCALE
        s_ctx = jnp.einsum('bqhgd,bkhd->bhgqk', qf, kcf)
        s_loc = jnp.einsum('bqhgd,bkhd->bhgqk', qf, kblk.astype(jnp.float32))
        t = j * BLOCK + jnp.arange(BLOCK, dtype=jnp.int32)
        sp = (j - 1) * BLOCK + jnp.arange(3 * BLOCK, dtype=jnp.int32)
        valid = (sp[None, :] >= 0) & (sp[None, :] < T) & (jnp.abs(t[:, None] - sp[None, :]) <= WINDOW)
        s_loc = jnp.where(valid, s_loc, NEG_INF)
        p = sink_softmax(jnp.concatenate([s_ctx, s_loc], axis=-1), sink_b)
        return (jnp.einsum('bhgqk,bkhd->bqhgd', p[..., :Lc], vcf)
                + jnp.einsum('bhgqk,bkhd->bqhgd', p[..., Lc:], vblk.astype(jnp.float32)))

    o = lax.map(one, (qb, kb, vb, blk))
    return jnp.moveaxis(o, 0, 1).reshape(B, T, ATTN_WIDTH).astype(q.dtype)


def chunk_sgu(u, g, w_s, b_s, ln_g, ln_b):
    B, L = u.shape[:2]
    nc = L // CHUNK
    u = jax.nn.gelu(u).reshape(B, L, N_SGU_GROUPS, SGU_GROUP_DIM)
    g = jax.nn.gelu(g).reshape(B, L, N_SGU_GROUPS, SGU_GROUP_DIM)
    g = layer_norm(g, ln_g.reshape(N_SGU_GROUPS, SGU_GROUP_DIM), ln_b.reshape(N_SGU_GROUPS, SGU_GROUP_DIM))
    g = g.reshape(B, nc, CHUNK, N_SGU_GROUPS, SGU_GROUP_DIM)
    mixed = jnp.einsum('hij,bcjhd->bcihd', w_s, g) + b_s.T[None, None, :, :, None]
    return (u * mixed.reshape(B, L, N_SGU_GROUPS, SGU_GROUP_DIM)).reshape(B, L, SGU_WIDTH)


def swiglu(h, wg, wu, wd):
    return (jax.nn.silu(h @ wg) * (h @ wu)) @ wd


def moe_swiglu(h, w_router, wg, wu, wd):
    B, L, D = h.shape
    hf = h.reshape(B * L, D)
    logits = (hf @ w_router).astype(jnp.float32)
    top_v, top_i = lax.top_k(logits, TOP_K)
    gates = jax.nn.softmax(top_v, axis=-1)
    combine = jnp.sum(jax.nn.one_hot(top_i, N_EXPERTS, dtype=jnp.float32) * gates[..., None], axis=1)
    combine = combine.astype(h.dtype)
    out = jnp.zeros_like(hf)
    for e in range(N_EXPERTS):
        out = out + combine[:, e:e + 1] * swiglu(hf, wg[e], wu[e], wd[e])
    return out.reshape(B, L, D)


def trunk_layer(x, mods, attend, ffn, w_in_l, w_o_l, w_s_l, b_s_l, sgu_g_l, sgu_b_l,
                ln1_g_l, ln1_b_l, ln2_g_l, ln2_b_l):
    sh1, sc1, gt1, sh2, sc2, gt2 = mods
    h = x * (1 + sc1) + sh1
    q, k, v, u, g = split_groups(h @ w_in_l)
    a = attend(q, k, v)
    s = chunk_sgu(u, g, w_s_l, b_s_l, sgu_g_l, sgu_b_l)
    y = jnp.concatenate([a, s], axis=-1) @ w_o_l
    x = layer_norm(DEEPNORM_ALPHA * x + gt1 * y, ln1_g_l, ln1_b_l)
    h = x * (1 + sc2) + sh2
    x = layer_norm(DEEPNORM_ALPHA * x + gt2 * ffn(h), ln2_g_l, ln2_b_l)
    return x, k, v


def setup_inputs(seed: int = 0) -> dict:
    key = jax.random.key(seed)
    ks = jax.random.split(key, 26)

    def nrm(k, shape, scale):
        return jax.random.normal(k, shape, jnp.float32) * scale

    D = D_MODEL
    kv_shape = (DEC_BATCH, DEPTH, PAST_LEN, N_KV_HEADS, HEAD_DIM)
    return {
        'x_prompt': nrm(ks[0], (BATCH, SEQ, D), 1.0),
        'x_sample': nrm(ks[1], (DEC_BATCH, DEC_SEQ, D), 1.0),
        'cache_k': nrm(ks[2], kv_shape, 1.0),
        'cache_v': nrm(ks[3], kv_shape, 1.0),
        'c': nrm(ks[4], (DEC_BATCH, D), 1.0),
        'c_ctx': nrm(ks[5], (D,), 1.0),
        'w_ada': nrm(ks[6], (DEPTH, D, 6 * D), 0.5 * D ** -0.5),
        'b_ada': nrm(ks[7], (DEPTH, 6 * D), 0.02),
        'w_in': nrm(ks[8], (DEPTH, D, IN_WIDTH), D ** -0.5),
        'w_o': nrm(ks[9], (DEPTH, ATTN_WIDTH + SGU_WIDTH, D), DEEPNORM_BETA * (ATTN_WIDTH + SGU_WIDTH) ** -0.5),
        'attn_sink': nrm(ks[10], (DEPTH, N_Q_HEADS), 0.5),
        'w_s': nrm(ks[11], (DEPTH, N_SGU_GROUPS, CHUNK, CHUNK), CHUNK ** -0.5),
        'b_s': 1.0 + nrm(ks[12], (DEPTH, N_SGU_GROUPS, CHUNK), 0.02),
        'sgu_ln_g': 1.0 + nrm(ks[13], (DEPTH, SGU_WIDTH), 0.02),
        'sgu_ln_b': nrm(ks[14], (DEPTH, SGU_WIDTH), 0.02),
        'ln1_g': 1.0 + nrm(ks[15], (DEPTH, D), 0.02),
        'ln1_b': nrm(ks[16], (DEPTH, D), 0.02),
        'ln2_g': 1.0 + nrm(ks[17], (DEPTH, D), 0.02),
        'ln2_b': nrm(ks[18], (DEPTH, D), 0.02),
        'w_ff_gate': nrm(ks[19], (N_DENSE, D, D_FF), D ** -0.5),
        'w_ff_up': nrm(ks[20], (N_DENSE, D, D_FF), D ** -0.5),
        'w_ff_down': nrm(ks[21], (N_DENSE, D_FF, D), DEEPNORM_BETA * D_FF ** -0.5),
        'w_router': nrm(ks[22], (N_MOE, D, N_EXPERTS), D ** -0.5),
        'w_exp_gate': nrm(ks[23], (N_MOE, N_EXPERTS, D, D_FF_EXPERT), D ** -0.5),
        'w_exp_up': nrm(ks[24], (N_MOE, N_EXPERTS, D, D_FF_EXPERT), D ** -0.5),
        'w_exp_down': nrm(ks[25], (N_MOE, N_EXPERTS, D_FF_EXPERT, D), DEEPNORM_BETA * D_FF_EXPERT ** -0.5),
    }


def reference(x_prompt, x_sample, cache_k, cache_v, c, c_ctx, w_ada, b_ada, w_in, w_o, attn_sink,
              w_s, b_s, sgu_ln_g, sgu_ln_b, ln1_g, ln1_b, ln2_g, ln2_b, w_ff_gate, w_ff_up, w_ff_down,
              w_router, w_exp_gate, w_exp_up, w_exp_down):
    xp = x_prompt
    xs = x_sample
    new_k = []
    new_v = []
    for l in range(DEPTH):
        i = l // 2
        if l % 2 == 0:
            ffn = functools.partial(swiglu, wg=w_ff_gate[i], wu=w_ff_up[i], wd=w_ff_down[i])
        else:
            ffn = functools.partial(moe_swiglu, w_router=w_router[i], wg=w_exp_gate[i],
                                    wu=w_exp_up[i], wd=w_exp_down[i])
        shared = (w_in[l], w_o[l], w_s[l], b_s[l], sgu_ln_g[l], sgu_ln_b[l],
                  ln1_g[l], ln1_b[l], ln2_g[l], ln2_b[l])
        sink = attn_sink[l]
        mods_ctx = adaln(c_ctx[None, :], w_ada[l], b_ada[l])
        xp, k_l, v_l = trunk_layer(xp, mods_ctx, lambda q, k, v: ctx_attention(q, k, v, sink), ffn, *shared)
        new_k.append(k_l)
        new_v.append(v_l)
        mods_lat = adaln(c, w_ada[l], b_ada[l])
        kc = cache_k[:, l]
        vc = cache_v[:, l]
        xs, _, _ = trunk_layer(xs, mods_lat, lambda q, k, v: latent_attention(q, k, v, kc, vc, sink), ffn, *shared)
    return (xp, xs, jnp.stack(new_k, axis=1), jnp.stack(new_v, axis=1))
```

```python
import functools

import jax
import jax.numpy as jnp
import numpy as np
from jax import lax
from jax.experimental import pallas as pl
from jax.experimental.pallas import tpu as pltpu

F32 = jnp.float32
BF16 = jnp.bfloat16

D_MODEL = 1024
HEAD_DIM = 64
N_Q_HEADS = 8
N_KV_HEADS = 2
Q_PER_KV = N_Q_HEADS // N_KV_HEADS
ATTN_WIDTH = N_Q_HEADS * HEAD_DIM
KV_WIDTH = N_KV_HEADS * HEAD_DIM
SGU_WIDTH = D_MODEL - ATTN_WIDTH
N_SGU_GROUPS = 8
SGU_GROUP_DIM = SGU_WIDTH // N_SGU_GROUPS
CHUNK = 128
BLOCK = 128
WINDOW = 128
GRID_W = 64
IN_WIDTH = ATTN_WIDTH + 2 * KV_WIDTH + 2 * SGU_WIDTH
OFF_K = ATTN_WIDTH
OFF_V = OFF_K + KV_WIDTH
OFF_U = OFF_V + KV_WIDTH
OFF_G = OFF_U + SGU_WIDTH
N_EXPERTS = 8
TOP_K = 2
ROPE_BASE = 10000.0
ROPE_PAIRS = HEAD_DIM // 4
LN_EPS = 1e-5
ATTN_SCALE = HEAD_DIM ** -0.5
NEG_INF = -1e30
N_MODS = 6

LANES = 128
V7X_VMEM_LIMIT_BYTES = 48 * 1024 * 1024

TOKEN_TILE = 256
FFN_TILE = 512
MOD_ROWS = 8
ADA_COLS = 1536


def _params(sem):
    return pltpu.CompilerParams(dimension_semantics=sem, vmem_limit_bytes=V7X_VMEM_LIMIT_BYTES)


def _gelu_tanh(x):
    return x * (0.5 * (1.0 + jnp.tanh(np.float32(np.sqrt(2.0 / np.pi)) * (x + 0.044715 * (x * x * x)))))


def _sigmoid(x):
    return 1.0 / (1.0 + jnp.exp(-x))


def _layer_norm_rows(t, g, b):
    mu = jnp.mean(t, axis=-1, keepdims=True)
    tc = t - mu
    var = jnp.mean(tc * tc, axis=-1, keepdims=True)
    return tc * lax.rsqrt(var + LN_EPS) * g + b


def _ada_kernel(c_ref, w_ref, b_ref, o_ref):
    c = c_ref[...]
    s = (c * _sigmoid(c)).astype(BF16)
    o_ref[...] = jnp.dot(s, w_ref[...].astype(BF16), preferred_element_type=F32) + b_ref[...]


def _ada_call(cvec, w_ada, b_ada):
    depth = w_ada.shape[0]
    n_out = w_ada.shape[2]
    return pl.pallas_call(
        _ada_kernel,
        grid=(depth, n_out // ADA_COLS),
        in_specs=[
            pl.BlockSpec((MOD_ROWS, D_MODEL), lambda l, j: (0, 0)),
            pl.BlockSpec((None, D_MODEL, ADA_COLS), lambda l, j: (l, 0, j)),
            pl.BlockSpec((None, 1, ADA_COLS), lambda l, j: (l, 0, j)),
        ],
        out_specs=pl.BlockSpec((None, MOD_ROWS, ADA_COLS), lambda l, j: (l, 0, j)),
        out_shape=jax.ShapeDtypeStruct((depth, MOD_ROWS, n_out), F32),
        compiler_params=_params(("arbitrary", "arbitrary")),
        name="adaln",
    )(cvec, w_ada, b_ada.reshape(depth, 1, n_out))


def _in_kernel(x_ref, mod_ref, w_ref, cos_ref, sin_ref, lng_ref, lnb_ref,
               q_ref, k_ref, v_ref, kv_ref, u_ref, g_ref):
    tm = x_ref.shape[0]
    h = x_ref[...] * (1.0 + mod_ref[1:2, :]) + mod_ref[0:1, :]
    z = jnp.dot(h.astype(BF16), w_ref[...], preferred_element_type=F32)
    cos = cos_ref[...]
    sin = sin_ref[...]
    lane = lax.broadcasted_iota(jnp.int32, (tm, LANES), 1)
    first_of_pair = (lane & (2 * ROPE_PAIRS - 1)) < ROPE_PAIRS
    lower_group = lane < SGU_GROUP_DIM

    def rope(t):
        partner = jnp.where(first_of_pair, pltpu.roll(t, LANES - ROPE_PAIRS, 1), pltpu.roll(t, ROPE_PAIRS, 1))
        return t * cos + partner * sin

    for j in range(ATTN_WIDTH // LANES):
        cols = slice(j * LANES, (j + 1) * LANES)
        q_ref[:, cols] = (rope(z[:, cols]) * ATTN_SCALE).astype(BF16)
    k_ref[...] = rope(z[:, OFF_K:OFF_V]).astype(BF16)
    v_ref[...] = z[:, OFF_V:OFF_U].astype(BF16)
    kv_ref[...] = z[:, OFF_K:OFF_U]
    u_ref[...] = _gelu_tanh(z[:, OFF_U:OFF_G])

    inv_n = 1.0 / SGU_GROUP_DIM
    for j in range(SGU_WIDTH // LANES):
        cols = slice(j * LANES, (j + 1) * LANES)
        t = _gelu_tanh(z[:, OFF_G + j * LANES:OFF_G + (j + 1) * LANES])
        s_lo = jnp.sum(jnp.where(lower_group, t, 0.0), axis=-1, keepdims=True)
        s_hi = jnp.sum(jnp.where(lower_group, 0.0, t), axis=-1, keepdims=True)
        tc = t - jnp.where(lower_group, s_lo, s_hi) * inv_n
        sq = tc * tc
        v_lo = jnp.sum(jnp.where(lower_group, sq, 0.0), axis=-1, keepdims=True)
        v_hi = jnp.sum(jnp.where(lower_group, 0.0, sq), axis=-1, keepdims=True)
        var = jnp.where(lower_group, v_lo, v_hi) * inv_n
        g_ref[:, cols] = (tc * lax.rsqrt(var + LN_EPS) * lng_ref[:, cols] + lnb_ref[:, cols]).astype(BF16)


def _in_call(x, mods, w_in, cos_t, sin_t, ln_g, ln_b, geom):
    n = x.shape[0]
    tm = TOKEN_TILE
    mod_idx, rope_idx = geom.mod_index, geom.rope_index
    row = lambda i: (i, 0)
    outs = pl.pallas_call(
        _in_kernel,
        grid=(n // tm,),
        in_specs=[
            pl.BlockSpec((tm, D_MODEL), row),
            pl.BlockSpec((None, N_MODS, D_MODEL), lambda i: (mod_idx(i), 0, 0)),
            pl.BlockSpec((D_MODEL, IN_WIDTH), lambda i: (0, 0)),
            pl.BlockSpec((tm, LANES), lambda i: (rope_idx(i), 0)),
            pl.BlockSpec((tm, LANES), lambda i: (rope_idx(i), 0)),
            pl.BlockSpec((1, SGU_WIDTH), lambda i: (0, 0)),
            pl.BlockSpec((1, SGU_WIDTH), lambda i: (0, 0)),
        ],
        out_specs=[
            pl.BlockSpec((tm, ATTN_WIDTH), row),
            pl.BlockSpec((tm, KV_WIDTH), row),
            pl.BlockSpec((tm, KV_WIDTH), row),
            pl.BlockSpec((tm, 2 * KV_WIDTH), row),
            pl.BlockSpec((tm, SGU_WIDTH), row),
            pl.BlockSpec((tm, SGU_WIDTH), row),
        ],
        out_shape=[
            jax.ShapeDtypeStruct((n, ATTN_WIDTH), BF16),
            jax.ShapeDtypeStruct((n, KV_WIDTH), BF16),
            jax.ShapeDtypeStruct((n, KV_WIDTH), BF16),
            jax.ShapeDtypeStruct((n, 2 * KV_WIDTH), F32),
            jax.ShapeDtypeStruct((n, SGU_WIDTH), F32),
            jax.ShapeDtypeStruct((n, SGU_WIDTH), BF16),
        ],
        compiler_params=_params(("arbitrary",)),
        name="in_proj",
    )(x, mods, w_in, cos_t, sin_t, ln_g, ln_b)
    return outs


def _softmax_pv(q_h, keys, values, masks, sink):
    scores = []
    for k_h, mask in zip(keys, masks):
        s = lax.dot_general(q_h, k_h, (((1,), (1,)), ((), ())), preferred_element_type=F32)
        if mask is not None:
            s = jnp.where(mask, s, NEG_INF)
        scores.append(s)
    m = jnp.max(scores[0], axis=-1, keepdims=True)
    for s in scores[1:]:
        m = jnp.maximum(m, jnp.max(s, axis=-1, keepdims=True))
    m = jnp.maximum(m, sink)
    denom = jnp.exp(sink - m)
    acc = None
    for s, v_h in zip(scores, values):
        p = jnp.exp(s - m)
        denom = denom + jnp.sum(p, axis=-1, keepdims=True)
        pv = jnp.dot(p.astype(BF16), v_h, preferred_element_type=F32)
        acc = pv if acc is None else acc + pv
    return acc / denom


def _ctx_attn_kernel(sink_ref, q_ref, k_ref, v_ref, o_ref):
    for hk in range(N_KV_HEADS):
        kv_cols = slice(hk * HEAD_DIM, (hk + 1) * HEAD_DIM)
        k_h = k_ref[:, kv_cols]
        v_h = v_ref[:, kv_cols]
        for gq in range(Q_PER_KV):
            h = hk * Q_PER_KV + gq
            cols = slice(h * HEAD_DIM, (h + 1) * HEAD_DIM)
            o = _softmax_pv(q_ref[:, cols], [k_h], [v_h], [None], sink_ref[h])
            o_ref[:, cols] = o.astype(o_ref.dtype)


def _ctx_attn_call(sink, q, k, v, n_seq, seq_len):
    n = n_seq * seq_len
    blk = lambda b: (b, 0)
    return pl.pallas_call(
        _ctx_attn_kernel,
        grid=(n_seq,),
        in_specs=[
            pl.BlockSpec(memory_space=pltpu.SMEM),
            pl.BlockSpec((seq_len, ATTN_WIDTH), blk),
            pl.BlockSpec((seq_len, KV_WIDTH), blk),
            pl.BlockSpec((seq_len, KV_WIDTH), blk),
        ],
        out_specs=pl.BlockSpec((seq_len, ATTN_WIDTH), blk),
        out_shape=jax.ShapeDtypeStruct((n, ATTN_WIDTH), BF16),
        compiler_params=_params(("arbitrary",)),
        name="ctx_attention",
    )(sink, q, k, v)


def _lat_attn_kernel(sink_ref, q_ref, kp_ref, kc_ref, kn_ref, vp_ref, vc_ref, vn_ref,
                     ck_ref, cv_ref, o_ref):
    j = pl.program_id(1)
    nb = pl.num_programs(1)
    r = lax.broadcasted_iota(jnp.int32, (BLOCK, BLOCK), 0)
    c = lax.broadcasted_iota(jnp.int32, (BLOCK, BLOCK), 1)
    mask_prev = c >= r + jnp.where(j > 0, 0, BLOCK)
    mask_next = c <= r - jnp.where(j < nb - 1, 0, BLOCK)
    for hk in range(N_KV_HEADS):
        kv_cols = slice(hk * HEAD_DIM, (hk + 1) * HEAD_DIM)
        keys = [ck_ref[:, kv_cols], kp_ref[:, kv_cols], kc_ref[:, kv_cols], kn_ref[:, kv_cols]]
        values = [cv_ref[:, kv_cols], vp_ref[:, kv_cols], vc_ref[:, kv_cols], vn_ref[:, kv_cols]]
        masks = [None, mask_prev, None, mask_next]
        for gq in range(Q_PER_KV):
            h = hk * Q_PER_KV + gq
            cols = slice(h * HEAD_DIM, (h + 1) * HEAD_DIM)
            o = _softmax_pv(q_ref[:, cols], keys, values, masks, sink_ref[h])
            o_ref[:, cols] = o.astype(o_ref.dtype)


def _lat_attn_call(sink, q, k, v, cache_k, cache_v, geom):
    nb = geom.dec_seq // BLOCK
    base = geom.n_ctx // BLOCK
    past = cache_k.shape[1]

    def at(offset):
        return lambda b, j: (base + b * nb + jnp.clip(j + offset, 0, nb - 1), 0)

    kv_spec = lambda off: pl.BlockSpec((BLOCK, KV_WIDTH), at(off))
    cache_spec = pl.BlockSpec((None, past, KV_WIDTH), lambda b, j: (b, 0, 0))
    return pl.pallas_call(
        _lat_attn_kernel,
        grid=(geom.dec_batch, nb),
        in_specs=[
            pl.BlockSpec(memory_space=pltpu.SMEM),
            pl.BlockSpec((BLOCK, ATTN_WIDTH), at(0)),
            kv_spec(-1), kv_spec(0), kv_spec(1),
            kv_spec(-1), kv_spec(0), kv_spec(1),
            cache_spec, cache_spec,
        ],
        out_specs=pl.BlockSpec((BLOCK, ATTN_WIDTH), lambda b, j: (b * nb + j, 0)),
        out_shape=jax.ShapeDtypeStruct((geom.dec_batch * geom.dec_seq, ATTN_WIDTH), BF16),
        compiler_params=_params(("arbitrary", "arbitrary")),
        name="lat_attention",
    )(sink, q, k, k, k, v, v, v, cache_k, cache_v)


def _sgu_kernel(u_ref, g_ref, ws_ref, bias_ref, s_ref):
    tm = u_ref.shape[0]
    lane = lax.broadcasted_iota(jnp.int32, (CHUNK, LANES), 1)
    lower_group = lane < SGU_GROUP_DIM
    for ch in range(tm // CHUNK):
        rows = slice(ch * CHUNK, (ch + 1) * CHUNK)
        for p in range(SGU_WIDTH // LANES):
            cols = slice(p * LANES, (p + 1) * LANES)
            g = g_ref[rows, cols]
            zero = jnp.zeros_like(g)
            mixed = (jnp.dot(ws_ref[2 * p], jnp.where(lower_group, g, zero), preferred_element_type=F32)
                     + jnp.dot(ws_ref[2 * p + 1], jnp.where(lower_group, zero, g), preferred_element_type=F32))
            s_ref[rows, cols] = (u_ref[rows, cols] * (mixed + bias_ref[:, cols])).astype(s_ref.dtype)


def _sgu_call(u, g, w_s, bias_full):
    n = u.shape[0]
    tm = TOKEN_TILE
    row = lambda i: (i, 0)
    return pl.pallas_call(
        _sgu_kernel,
        grid=(n // tm,),
        in_specs=[
            pl.BlockSpec((tm, SGU_WIDTH), row),
            pl.BlockSpec((tm, SGU_WIDTH), row),
            pl.BlockSpec((N_SGU_GROUPS, CHUNK, CHUNK), lambda i: (0, 0, 0)),
            pl.BlockSpec((CHUNK, SGU_WIDTH), lambda i: (0, 0)),
        ],
        out_specs=pl.BlockSpec((tm, SGU_WIDTH), row),
        out_shape=jax.ShapeDtypeStruct((n, SGU_WIDTH), BF16),
        compiler_params=_params(("arbitrary",)),
        name="spatial_gating",
    )(u, g, w_s, bias_full)


def _out_kernel(alpha, with_router, ctx_tiles, a_ctx_ref, a_lat_ref, s_ref, x_ref, mod_ref, wa_ref, ws_ref,
                g_ref, b_ref, *rest):
    if with_router:
        wr_ref, x1_ref, h_ref, rt_ref = rest
    else:
        x1_ref, h_ref = rest
    tile = lax.broadcasted_iota(jnp.int32, a_ctx_ref.shape, 0) * 0 + pl.program_id(0)
    a = jnp.where(tile < ctx_tiles, a_ctx_ref[...], a_lat_ref[...])
    y = (jnp.dot(a, wa_ref[...], preferred_element_type=F32)
         + jnp.dot(s_ref[...], ws_ref[...], preferred_element_type=F32))
    t = alpha * x_ref[...] + mod_ref[2:3, :] * y
    x1 = _layer_norm_rows(t, g_ref[...], b_ref[...])
    x1_ref[...] = x1
    h = x1 * (1.0 + mod_ref[4:5, :]) + mod_ref[3:4, :]
    h_ref[...] = h.astype(h_ref.dtype)
    if with_router:
        tm = h.shape[0]
        logits = jnp.dot(h, wr_ref[...], preferred_element_type=F32, precision=lax.Precision.HIGHEST)
        lane = lax.broadcasted_iota(jnp.int32, (tm, LANES), 1).astype(F32)
        neg = jnp.float32(-jnp.inf)
        lg = jnp.where(lane < N_EXPERTS, logits, neg)
        m1 = jnp.max(lg, axis=-1, keepdims=True)
        i1 = jnp.min(jnp.where(lg == m1, lane, float(LANES)), axis=-1, keepdims=True)
        lg2 = jnp.where(lane == i1, neg, lg)
        m2 = jnp.max(lg2, axis=-1, keepdims=True)
        i2 = jnp.min(jnp.where(lg2 == m2, lane, float(LANES)), axis=-1, keepdims=True)
        e2 = jnp.exp(m2 - m1)
        g1 = 1.0 / (1.0 + e2)
        g2 = e2 / (1.0 + e2)
        rt = jnp.where(lane == 0, i1, jnp.where(lane == 1, i2, jnp.where(lane == 2, g1, jnp.where(lane == 3, g2, 0.0))))
        rt_ref[...] = rt


def _out_call(a_ctx, a_lat, s, x, mods, w_o, ln_g, ln_b, w_router, alpha, geom):
    n = x.shape[0]
    tm = TOKEN_TILE
    ctx_tiles = geom.ctx_tiles
    row = lambda i: (i, 0)
    const = lambda i: (0, 0)
    with_router = w_router is not None
    in_specs = [
        pl.BlockSpec((tm, ATTN_WIDTH), lambda i: (jnp.minimum(i, ctx_tiles - 1), 0)),
        pl.BlockSpec((tm, ATTN_WIDTH), lambda i: (jnp.maximum(i - ctx_tiles, 0), 0)),
        pl.BlockSpec((tm, SGU_WIDTH), row),
        pl.BlockSpec((tm, D_MODEL), row),
        pl.BlockSpec((None, N_MODS, D_MODEL), lambda i: (geom.mod_index(i), 0, 0)),
        pl.BlockSpec((ATTN_WIDTH, D_MODEL), const),
        pl.BlockSpec((SGU_WIDTH, D_MODEL), lambda i: (1, 0)),
        pl.BlockSpec((1, D_MODEL), const),
        pl.BlockSpec((1, D_MODEL), const),
    ]
    args = [a_ctx, a_lat, s, x, mods, w_o, w_o, ln_g, ln_b]
    out_specs = [pl.BlockSpec((tm, D_MODEL), row), pl.BlockSpec((tm, D_MODEL), row)]
    out_shape = [jax.ShapeDtypeStruct((n, D_MODEL), F32),
                 jax.ShapeDtypeStruct((n, D_MODEL), F32 if with_router else BF16)]
    if with_router:
        in_specs.append(pl.BlockSpec((D_MODEL, LANES), const))
        args.append(w_router)
        out_specs.append(pl.BlockSpec((tm, LANES), row))
        out_shape.append(jax.ShapeDtypeStruct((n, LANES), F32))
    return pl.pallas_call(
        functools.partial(_out_kernel, alpha, with_router, ctx_tiles),
        grid=(n // tm,),
        in_specs=in_specs,
        out_specs=out_specs,
        out_shape=out_shape,
        compiler_params=_params(("arbitrary",)),
        name="out_proj_router" if with_router else "out_proj",
    )(*args)


def _swiglu_partial(x, wg, wu, wd):
    a = jnp.dot(x, wg, preferred_element_type=F32)
    b = jnp.dot(x, wu, preferred_element_type=F32)
    mid = (a * _sigmoid(a)) * b
    return jnp.dot(mid.astype(BF16), wd, preferred_element_type=F32)


def _ffn_kernel(alpha, h_ref, x1_ref, mod_ref, wg_ref, wu_ref, wd_ref, g_ref, b_ref, o_ref, acc_ref):
    c = pl.program_id(1)
    part = _swiglu_partial(h_ref[...], wg_ref[...], wu_ref[...], wd_ref[...])

    @pl.when(c == 0)
    def _():
        acc_ref[...] = part

    @pl.when(c > 0)
    def _():
        acc_ref[...] += part

    @pl.when(c == pl.num_programs(1) - 1)
    def _():
        t = alpha * x1_ref[...] + mod_ref[5:6, :] * acc_ref[...]
        o_ref[...] = _layer_norm_rows(t, g_ref[...], b_ref[...])


def _ffn_chunk(d_ff):
    assert d_ff % (2 * LANES) == 0
    return d_ff // 2


def _ffn_call(h, x1, mods, wg, wu, wd, ln_g, ln_b, alpha, geom):
    n = h.shape[0]
    tm = FFN_TILE
    d_ff = wg.shape[1]
    fc = _ffn_chunk(d_ff)
    ratio = tm // TOKEN_TILE
    row = lambda i, c: (i, 0)
    const = lambda i, c: (0, 0)
    return pl.pallas_call(
        functools.partial(_ffn_kernel, alpha),
        grid=(n // tm, d_ff // fc),
        in_specs=[
            pl.BlockSpec((tm, D_MODEL), row),
            pl.BlockSpec((tm, D_MODEL), row),
            pl.BlockSpec((None, N_MODS, D_MODEL), lambda i, c: (geom.mod_index(i * ratio), 0, 0)),
            pl.BlockSpec((D_MODEL, fc), lambda i, c: (0, c)),
            pl.BlockSpec((D_MODEL, fc), lambda i, c: (0, c)),
            pl.BlockSpec((fc, D_MODEL), lambda i, c: (c, 0)),
            pl.BlockSpec((1, D_MODEL), const),
            pl.BlockSpec((1, D_MODEL), const),
        ],
        out_specs=pl.BlockSpec((tm, D_MODEL), row),
        out_shape=jax.ShapeDtypeStruct((n, D_MODEL), F32),
        scratch_shapes=[pltpu.VMEM((tm, D_MODEL), F32)],
        compiler_params=_params(("arbitrary", "arbitrary")),
        name="dense_ffn",
    )(h, x1, mods, wg, wu, wd, ln_g, ln_b)


def _moe_kernel(n_chunks, te_ref, nu_ref, src_ref, dst_ref, h_hbm, wg_ref, wu_ref, wd_ref, o_hbm,
                xbuf, xb, obuf, gsem, ssem):
    del te_ref
    tr = xb.shape[0]
    t = pl.program_id(0)
    c = pl.program_id(1)
    n_used = nu_ref[0]
    valid = t < n_used
    slot = t % 2
    rows_per_step = tr // n_chunks

    def row_gather(tile, dst_slot, r):
        tok = src_ref[tile * tr + r]
        return pltpu.make_async_copy(h_hbm.at[pl.ds(tok, 1), :], xbuf.at[dst_slot, pl.ds(r, 1), :],
                                     gsem.at[dst_slot])

    def start_gather(tile, dst_slot, first_row, n_rows):
        def body(r, carry):
            row_gather(tile, dst_slot, first_row + r).start()
            return carry
        lax.fori_loop(0, n_rows, body, 0, unroll=8)

    def wait_gather(dst_slot):
        pltpu.make_async_copy(h_hbm.at[pl.ds(0, tr), :], xbuf.at[dst_slot], gsem.at[dst_slot]).wait()

    def row_scatter(src_slot, r):
        dst = dst_ref[t * tr + r]
        return pltpu.make_async_copy(obuf.at[src_slot, pl.ds(r, 1), :], o_hbm.at[pl.ds(dst, 1), :],
                                     ssem.at[src_slot])

    def wait_scatter(src_slot):
        pltpu.make_async_copy(obuf.at[src_slot], o_hbm.at[pl.ds(0, tr), :], ssem.at[src_slot]).wait()

    @pl.when((c == 0) & (t >= 2))
    def _():
        wait_scatter(slot)

    @pl.when(valid & (c == 0))
    def _():
        @pl.when(t == 0)
        def _():
            start_gather(0, 0, 0, tr)

        wait_gather(slot)
        xb[...] = xbuf[slot].astype(BF16)

    @pl.when(t + 1 < n_used)
    def _():
        start_gather(t + 1, 1 - slot, c * rows_per_step, rows_per_step)

    @pl.when(valid)
    def _():
        part = _swiglu_partial(xb[...], wg_ref[...], wu_ref[...], wd_ref[...])

        @pl.when(c == 0)
        def _():
            obuf[slot] = part

        @pl.when(c > 0)
        def _():
            obuf[slot] += part

    @pl.when(jnp.logical_not(valid) & (c == 0))
    def _():
        obuf[slot] = jnp.zeros(obuf.shape[1:], obuf.dtype)

    @pl.when(c == n_chunks - 1)
    def _():
        def body(r, carry):
            row_scatter(slot, r).start()
            return carry
        lax.fori_loop(0, tr, body, 0, unroll=8)

        @pl.when(t == pl.num_programs(0) - 1)
        def _():
            wait_scatter(1 - slot)
            wait_scatter(slot)


def _moe_call(h, wg, wu, wd, plan):
    tr = FFN_TILE
    d_ff = wg.shape[2]
    fc = _ffn_chunk(d_ff)
    n_chunks = d_ff // fc
    t_max = plan["tile_expert"].shape[0]

    def chunk(t, c, nu):
        return jnp.where(t < nu[0], c, n_chunks - 1)

    grid_spec = pltpu.PrefetchScalarGridSpec(
        num_scalar_prefetch=4,
        grid=(t_max, n_chunks),
        in_specs=[
            pl.BlockSpec(memory_space=pl.ANY),
            pl.BlockSpec((None, D_MODEL, fc), lambda t, c, te, nu, src, dst: (te[t], 0, chunk(t, c, nu))),
            pl.BlockSpec((None, D_MODEL, fc), lambda t, c, te, nu, src, dst: (te[t], 0, chunk(t, c, nu))),
            pl.BlockSpec((None, fc, D_MODEL), lambda t, c, te, nu, src, dst: (te[t], chunk(t, c, nu), 0)),
        ],
        out_specs=pl.BlockSpec(memory_space=pl.ANY),
        scratch_shapes=[
            pltpu.VMEM((2, tr, D_MODEL), F32),
            pltpu.VMEM((tr, D_MODEL), BF16),
            pltpu.VMEM((2, tr, D_MODEL), F32),
            pltpu.SemaphoreType.DMA((2,)),
            pltpu.SemaphoreType.DMA((2,)),
        ],
    )
    return pl.pallas_call(
        functools.partial(_moe_kernel, n_chunks),
        grid_spec=grid_spec,
        out_shape=jax.ShapeDtypeStruct((t_max * tr, D_MODEL), F32),
        compiler_params=_params(("arbitrary", "arbitrary")),
        name="expert_ffn",
    )(plan["tile_expert"], plan["n_used"], plan["src_token"], plan["dst_row"], h, wg, wu, wd)


def _route_plan(routing, n_tok, tr):
    n_assign = TOP_K * n_tok
    t_max = n_assign // tr + N_EXPERTS
    e_flat = routing[:, :TOP_K].astype(jnp.int32).T.reshape(-1)
    order = jnp.argsort(e_flat, stable=True).astype(jnp.int32)
    counts = jnp.sum((e_flat[:, None] == jnp.arange(N_EXPERTS, dtype=jnp.int32)[None, :]).astype(jnp.int32), axis=0)
    tiles_e = (counts + tr - 1) // tr
    tile_end = jnp.cumsum(tiles_e)
    tile_start = tile_end - tiles_e
    n_used = tile_end[-1]
    first_sorted = jnp.cumsum(counts) - counts
    t_ids = jnp.arange(t_max, dtype=jnp.int32)
    te = jnp.minimum(jnp.searchsorted(tile_end, t_ids, side="right"), N_EXPERTS - 1).astype(jnp.int32)
    te = jnp.where(t_ids < n_used, te, te[jnp.maximum(n_used - 1, 0)])
    r = jnp.arange(t_max * tr, dtype=jnp.int32)
    t_of_r = r // tr
    e_r = te[t_of_r]
    rank = r - tile_start[e_r] * tr
    real = (t_of_r < n_used) & (rank < counts[e_r])
    a_r = order[jnp.clip(first_sorted[e_r] + rank, 0, n_assign - 1)]
    src = jnp.where(real, a_r % n_tok, 0).astype(jnp.int32)
    pad_rank = jnp.cumsum((~real).astype(jnp.int32)) - 1
    dst = jnp.where(real, a_r, n_assign + pad_rank).astype(jnp.int32)
    return {"tile_expert": te, "n_used": n_used.reshape(1).astype(jnp.int32), "src_token": src, "dst_row": dst}


def _combine_kernel(alpha, e1_ref, e2_ref, rt_ref, x1_ref, mod_ref, g_ref, b_ref, o_ref):
    y = rt_ref[:, 2:3] * e1_ref[...] + rt_ref[:, 3:4] * e2_ref[...]
    t = alpha * x1_ref[...] + mod_ref[5:6, :] * y
    o_ref[...] = _layer_norm_rows(t, g_ref[...], b_ref[...])


def _combine_call(eo, routing, x1, mods, ln_g, ln_b, alpha, geom):
    n = x1.shape[0]
    tm = TOKEN_TILE
    row = lambda i: (i, 0)
    const = lambda i: (0, 0)
    return pl.pallas_call(
        functools.partial(_combine_kernel, alpha),
        grid=(n // tm,),
        in_specs=[
            pl.BlockSpec((tm, D_MODEL), row),
            pl.BlockSpec((tm, D_MODEL), lambda i: (i + n // tm, 0)),
            pl.BlockSpec((tm, LANES), row),
            pl.BlockSpec((tm, D_MODEL), row),
            pl.BlockSpec((None, N_MODS, D_MODEL), lambda i: (geom.mod_index(i), 0, 0)),
            pl.BlockSpec((1, D_MODEL), const),
            pl.BlockSpec((1, D_MODEL), const),
        ],
        out_specs=pl.BlockSpec((tm, D_MODEL), row),
        out_shape=jax.ShapeDtypeStruct((n, D_MODEL), F32),
        compiler_params=_params(("arbitrary",)),
        name="expert_combine",
    )(eo, eo, routing, x1, mods, ln_g, ln_b)


class _Geometry:
    def __init__(self, n_ctx, dec_batch, dec_seq):
        self.n_ctx = n_ctx
        self.dec_batch = dec_batch
        self.dec_seq = dec_seq
        assert n_ctx % FFN_TILE == 0 and dec_seq % FFN_TILE == 0
        self.ctx_tiles = n_ctx // TOKEN_TILE
        self.lat_tiles = dec_seq // TOKEN_TILE

    def mod_index(self, i):
        return jnp.where(i < self.ctx_tiles, 0, 1 + (i - self.ctx_tiles) // self.lat_tiles)

    def rope_index(self, i):
        return jnp.where(i < self.ctx_tiles, 0, 1 + (i - self.ctx_tiles) % self.lat_tiles)


def _rope_tables(dec_seq):
    pos = jnp.arange(dec_seq, dtype=jnp.int32)
    row = (pos // GRID_W).astype(F32)
    col = (pos % GRID_W).astype(F32)
    inv = ROPE_BASE ** (-jnp.arange(ROPE_PAIRS, dtype=F32) / ROPE_PAIRS)
    ang_r = row[:, None] * inv[None, :]
    ang_c = col[:, None] * inv[None, :]
    cos_h = jnp.concatenate([jnp.cos(ang_r), jnp.cos(ang_r), jnp.cos(ang_c), jnp.cos(ang_c)], axis=-1)
    sin_h = jnp.concatenate([-jnp.sin(ang_r), jnp.sin(ang_r), -jnp.sin(ang_c), jnp.sin(ang_c)], axis=-1)
    reps = LANES // HEAD_DIM
    cos_t = jnp.concatenate([jnp.ones((TOKEN_TILE, LANES), F32), jnp.tile(cos_h, (1, reps))], axis=0)
    sin_t = jnp.concatenate([jnp.zeros((TOKEN_TILE, LANES), F32), jnp.tile(sin_h, (1, reps))], axis=0)
    return cos_t, sin_t


def kernel(x_prompt, x_sample, cache_k, cache_v, c, c_ctx, w_ada, b_ada, w_in, w_o, attn_sink, w_s, b_s, sgu_ln_g, sgu_ln_b, ln1_g, ln1_b, ln2_g, ln2_b, w_ff_gate, w_ff_up, w_ff_down, w_router, w_exp_gate, w_exp_up, w_exp_down):
    batch, seq, d = x_prompt.shape
    dec_batch, dec_seq, _ = x_sample.shape
    depth = w_in.shape[0]
    past = cache_k.shape[2]
    assert d == D_MODEL and dec_batch + 1 <= MOD_ROWS
    n_ctx = batch * seq
    n_lat = dec_batch * dec_seq
    n_tok = n_ctx + n_lat
    geom = _Geometry(n_ctx, dec_batch, dec_seq)
    alpha = float((2 * depth) ** 0.25)

    x = jnp.concatenate([x_prompt.reshape(n_ctx, d), x_sample.reshape(n_lat, d)], axis=0)
    cvec = jnp.concatenate([c_ctx[None, :], c, jnp.zeros((MOD_ROWS - 1 - dec_batch, d), F32)], axis=0)
    mods_all = _ada_call(cvec, w_ada, b_ada).reshape(depth, MOD_ROWS, N_MODS, d)
    cos_t, sin_t = _rope_tables(dec_seq)

    new_k, new_v = [], []
    for l in range(depth):
        mods = mods_all[l]
        q, k, v, kv32, u, g = _in_call(x, mods, w_in[l].astype(BF16), cos_t, sin_t,
                                       sgu_ln_g[l].reshape(1, SGU_WIDTH), sgu_ln_b[l].reshape(1, SGU_WIDTH), geom)
        new_k.append(kv32[:n_ctx, :KV_WIDTH].reshape(batch, seq, N_KV_HEADS, HEAD_DIM))
        new_v.append(kv32[:n_ctx, KV_WIDTH:].reshape(batch, seq, N_KV_HEADS, HEAD_DIM))
        sink = attn_sink[l]
        a_ctx = _ctx_attn_call(sink, q, k, v, batch, seq)
        a_lat = _lat_attn_call(sink, q, k, v, cache_k[:, l].reshape(dec_batch, past, KV_WIDTH).astype(BF16),
                               cache_v[:, l].reshape(dec_batch, past, KV_WIDTH).astype(BF16), geom)
        bias_full = jnp.repeat(b_s[l].T, SGU_GROUP_DIM, axis=1)
        s = _sgu_call(u, g, w_s[l].astype(BF16), bias_full)
        i = l // 2
        moe = l % 2 == 1
        w_r = None
        if moe:
            w_r = jnp.pad(w_router[i], ((0, 0), (0, LANES - N_EXPERTS)))
        outs = _out_call(a_ctx, a_lat, s, x, mods, w_o[l].astype(BF16), ln1_g[l].reshape(1, d), ln1_b[l].reshape(1, d),
                         w_r, alpha, geom)
        ln_g, ln_b = ln2_g[l].reshape(1, d), ln2_b[l].reshape(1, d)
        if moe:
            x1, h, routing = outs
            plan = _route_plan(routing, n_tok, FFN_TILE)
            eo = _moe_call(h, w_exp_gate[i].astype(BF16), w_exp_up[i].astype(BF16), w_exp_down[i].astype(BF16), plan)
            x = _combine_call(eo, routing, x1, mods, ln_g, ln_b, alpha, geom)
        else:
            x1, h = outs
            x = _ffn_call(h, x1, mods, w_ff_gate[i].astype(BF16), w_ff_up[i].astype(BF16),
                          w_ff_down[i].astype(BF16), ln_g, ln_b, alpha, geom)

    y_prompt = x[:n_ctx].reshape(batch, seq, d)
    y_sample = x[n_ctx:].reshape(dec_batch, dec_seq, d)
    return (y_prompt, y_sample, jnp.stack(new_k, axis=1), jnp.stack(new_v, axis=1))
```

```python
import functools

import jax
import jax.numpy as jnp
import numpy as np
from jax import lax
from jax.experimental import pallas as pl
from jax.experimental.pallas import tpu as pltpu

F32 = jnp.float32
BF16 = jnp.bfloat16

D_MODEL = 1024
HEAD_DIM = 64
N_Q_HEADS = 8
N_KV_HEADS = 2
Q_PER_KV = N_Q_HEADS // N_KV_HEADS
ATTN_WIDTH = N_Q_HEADS * HEAD_DIM
KV_WIDTH = N_KV_HEADS * HEAD_DIM
SGU_WIDTH = D_MODEL - ATTN_WIDTH
N_SGU_GROUPS = 8
SGU_GROUP_DIM = SGU_WIDTH // N_SGU_GROUPS
CHUNK = 128
BLOCK = 128
WINDOW = 128
GRID_W = 64
IN_WIDTH = ATTN_WIDTH + 2 * KV_WIDTH + 2 * SGU_WIDTH
OFF_K = ATTN_WIDTH
OFF_V = OFF_K + KV_WIDTH
OFF_U = OFF_V + KV_WIDTH
OFF_G = OFF_U + SGU_WIDTH
N_EXPERTS = 8
TOP_K = 2
ROPE_BASE = 10000.0
ROPE_PAIRS = HEAD_DIM // 4
LN_EPS = 1e-5
ATTN_SCALE = HEAD_DIM ** -0.5
NEG_INF = -1e30
N_MODS = 6

LANES = 128
V7X_VMEM_LIMIT_BYTES = 48 * 1024 * 1024

TOKEN_TILE = 256
FFN_TILE = 512
LAT_QUERY_ROWS = 256
MOD_ROWS = 8
ADA_COLS = 1536


def _params(sem):
    return pltpu.CompilerParams(dimension_semantics=sem, vmem_limit_bytes=V7X_VMEM_LIMIT_BYTES)


def _gelu_tanh(x):
    return x * (0.5 * (1.0 + jnp.tanh(np.float32(np.sqrt(2.0 / np.pi)) * (x + 0.044715 * (x * x * x)))))


def _sigmoid(x):
    return 1.0 / (1.0 + jnp.exp(-x))


def _layer_norm_rows(t, g, b):
    mu = jnp.mean(t, axis=-1, keepdims=True)
    tc = t - mu
    var = jnp.mean(tc * tc, axis=-1, keepdims=True)
    return tc * lax.rsqrt(var + LN_EPS) * g + b


def _stream_specs(width, ctx_tiles, tm):
    return [pl.BlockSpec((tm, width), lambda i, *_: (jnp.minimum(i, ctx_tiles - 1), 0)),
            pl.BlockSpec((tm, width), lambda i, *_: (jnp.maximum(i - ctx_tiles, 0), 0))]


def _read_stream(ctx_ref, lat_ref, ctx_tiles):
    tile = lax.broadcasted_iota(jnp.int32, ctx_ref.shape, 0) * 0 + pl.program_id(0)
    return jnp.where(tile < ctx_tiles, ctx_ref[...], lat_ref[...])


def _write_stream(ctx_ref, lat_ref, ctx_tiles, value):
    @pl.when(pl.program_id(0) < ctx_tiles)
    def _():
        ctx_ref[...] = value

    @pl.when(pl.program_id(0) >= ctx_tiles)
    def _():
        lat_ref[...] = value


def _ada_kernel(c_ref, w_ref, b_ref, o_ref):
    c = c_ref[...]
    s = (c * _sigmoid(c)).astype(BF16)
    o_ref[...] = jnp.dot(s, w_ref[...].astype(BF16), preferred_element_type=F32) + b_ref[...]


def _ada_call(cvec, w_ada, b_ada):
    depth = w_ada.shape[0]
    n_out = w_ada.shape[2]
    return pl.pallas_call(
        _ada_kernel,
        grid=(depth, n_out // ADA_COLS),
        in_specs=[
            pl.BlockSpec((MOD_ROWS, D_MODEL), lambda l, j: (0, 0)),
            pl.BlockSpec((None, D_MODEL, ADA_COLS), lambda l, j: (l, 0, j)),
            pl.BlockSpec((None, 1, ADA_COLS), lambda l, j: (l, 0, j)),
        ],
        out_specs=pl.BlockSpec((None, MOD_ROWS, ADA_COLS), lambda l, j: (l, 0, j)),
        out_shape=jax.ShapeDtypeStruct((depth, MOD_ROWS, n_out), F32),
        compiler_params=_params(("arbitrary", "arbitrary")),
        name="adaln",
    )(cvec, w_ada, b_ada.reshape(depth, 1, n_out))


def _in_kernel(split_ctx_tiles, *refs):
    if split_ctx_tiles is None:
        x = refs[0][...]
        refs = refs[1:]
    else:
        x = _read_stream(refs[0], refs[1], split_ctx_tiles)
        refs = refs[2:]
    mod_ref, w_ref, cos_ref, sin_ref, lng_ref, lnb_ref, q_ref, k_ref, v_ref, kv_ref, u_ref, g_ref = refs
    tm = x.shape[0]
    h = x * (1.0 + mod_ref[1:2, :]) + mod_ref[0:1, :]
    z = jnp.dot(h.astype(BF16), w_ref[...], preferred_element_type=F32)
    cos = cos_ref[...]
    sin = sin_ref[...]
    lane = lax.broadcasted_iota(jnp.int32, (tm, LANES), 1)
    first_of_pair = (lane & (2 * ROPE_PAIRS - 1)) < ROPE_PAIRS
    lower_group = lane < SGU_GROUP_DIM

    def rope(t):
        partner = jnp.where(first_of_pair, pltpu.roll(t, LANES - ROPE_PAIRS, 1), pltpu.roll(t, ROPE_PAIRS, 1))
        return t * cos + partner * sin

    for j in range(ATTN_WIDTH // LANES):
        cols = slice(j * LANES, (j + 1) * LANES)
        q_ref[:, cols] = (rope(z[:, cols]) * ATTN_SCALE).astype(BF16)
    k_ref[...] = rope(z[:, OFF_K:OFF_V]).astype(BF16)
    v_ref[...] = z[:, OFF_V:OFF_U].astype(BF16)
    kv_ref[...] = z[:, OFF_K:OFF_U]
    u_ref[...] = _gelu_tanh(z[:, OFF_U:OFF_G])

    inv_n = 1.0 / SGU_GROUP_DIM
    for j in range(SGU_WIDTH // LANES):
        cols = slice(j * LANES, (j + 1) * LANES)
        t = _gelu_tanh(z[:, OFF_G + j * LANES:OFF_G + (j + 1) * LANES])
        s_lo = jnp.sum(jnp.where(lower_group, t, 0.0), axis=-1, keepdims=True)
        s_hi = jnp.sum(jnp.where(lower_group, 0.0, t), axis=-1, keepdims=True)
        tc = t - jnp.where(lower_group, s_lo, s_hi) * inv_n
        sq = tc * tc
        v_lo = jnp.sum(jnp.where(lower_group, sq, 0.0), axis=-1, keepdims=True)
        v_hi = jnp.sum(jnp.where(lower_group, 0.0, sq), axis=-1, keepdims=True)
        var = jnp.where(lower_group, v_lo, v_hi) * inv_n
        g_ref[:, cols] = (tc * lax.rsqrt(var + LN_EPS) * lng_ref[:, cols] + lnb_ref[:, cols]).astype(BF16)


def _in_call(x, mods, w_in, cos_t, sin_t, ln_g, ln_b, geom):
    n = geom.n_tok
    tm = TOKEN_TILE
    mod_idx, rope_idx = geom.mod_index, geom.rope_index
    row = lambda i: (i, 0)
    split = isinstance(x, tuple)
    x_args = list(x) if split else [x]
    x_specs = _stream_specs(D_MODEL, geom.ctx_tiles, tm) if split else [pl.BlockSpec((tm, D_MODEL), row)]
    outs = pl.pallas_call(
        functools.partial(_in_kernel, geom.ctx_tiles if split else None),
        grid=(n // tm,),
        in_specs=x_specs + [
            pl.BlockSpec((None, N_MODS, D_MODEL), lambda i: (mod_idx(i), 0, 0)),
            pl.BlockSpec((D_MODEL, IN_WIDTH), lambda i: (0, 0)),
            pl.BlockSpec((tm, LANES), lambda i: (rope_idx(i), 0)),
            pl.BlockSpec((tm, LANES), lambda i: (rope_idx(i), 0)),
            pl.BlockSpec((1, SGU_WIDTH), lambda i: (0, 0)),
            pl.BlockSpec((1, SGU_WIDTH), lambda i: (0, 0)),
        ],
        out_specs=[
            pl.BlockSpec((tm, ATTN_WIDTH), row),
            pl.BlockSpec((tm, KV_WIDTH), row),
            pl.BlockSpec((tm, KV_WIDTH), row),
            pl.BlockSpec((tm, 2 * KV_WIDTH), row),
            pl.BlockSpec((tm, SGU_WIDTH), row),
            pl.BlockSpec((tm, SGU_WIDTH), row),
        ],
        out_shape=[
            jax.ShapeDtypeStruct((n, ATTN_WIDTH), BF16),
            jax.ShapeDtypeStruct((n, KV_WIDTH), BF16),
            jax.ShapeDtypeStruct((n, KV_WIDTH), BF16),
            jax.ShapeDtypeStruct((n, 2 * KV_WIDTH), F32),
            jax.ShapeDtypeStruct((n, SGU_WIDTH), F32),
            jax.ShapeDtypeStruct((n, SGU_WIDTH), BF16),
        ],
        compiler_params=_params(("arbitrary",)),
        name="in_proj",
    )(*x_args, mods, w_in, cos_t, sin_t, ln_g, ln_b)
    return outs


def _group_attention(q_ref, rows, hk, k_all, v_all, block_masks, sink_ref):
    m_rows = rows.stop - rows.start
    heads = [hk * Q_PER_KV + gq for gq in range(Q_PER_KV)]
    q = jnp.concatenate([q_ref[rows, h * HEAD_DIM:(h + 1) * HEAD_DIM] for h in heads], axis=0)
    s = lax.dot_general(q, k_all, (((1,), (1,)), ((), ())), preferred_element_type=F32)
    n_blocks = k_all.shape[0] // LANES
    blocks = [s[:, b * LANES:(b + 1) * LANES] for b in range(n_blocks)]
    for b, mask in block_masks.items():
        blocks[b] = jnp.where(mask, blocks[b], NEG_INF)
    head_of_row = lax.broadcasted_iota(jnp.int32, (Q_PER_KV * m_rows, 1), 0) // m_rows
    sink = jnp.zeros((Q_PER_KV * m_rows, 1), F32)
    for gq, h in enumerate(heads):
        sink = jnp.where(head_of_row == gq, sink_ref[h], sink)
    m_el = blocks[0]
    for blk in blocks[1:]:
        m_el = jnp.maximum(m_el, blk)
    m = jnp.maximum(jnp.max(m_el, axis=-1, keepdims=True), sink)
    probs = [jnp.exp(blk - m) for blk in blocks]
    l_el = probs[0]
    for p in probs[1:]:
        l_el = l_el + p
    denom = jnp.sum(l_el, axis=-1, keepdims=True) + jnp.exp(sink - m)
    p_all = jnp.concatenate([p.astype(BF16) for p in probs], axis=1)
    o = jnp.dot(p_all, v_all, preferred_element_type=F32) / denom
    return {h: o[gq * m_rows:(gq + 1) * m_rows] for gq, h in enumerate(heads)}


def _store_heads(o_ref, rows, outs):
    for h0 in range(0, N_Q_HEADS, 2):
        pair = jnp.concatenate([outs[h0], outs[h0 + 1]], axis=1)
        o_ref[rows, h0 * HEAD_DIM:(h0 + 2) * HEAD_DIM] = pair.astype(o_ref.dtype)


def _ctx_attn_kernel(sink_ref, q_ref, k_ref, v_ref, o_ref):
    rows = slice(0, q_ref.shape[0])
    outs = {}
    for hk in range(N_KV_HEADS):
        kv_cols = slice(hk * HEAD_DIM, (hk + 1) * HEAD_DIM)
        outs.update(_group_attention(q_ref, rows, hk, k_ref[:, kv_cols], v_ref[:, kv_cols], {}, sink_ref))
    _store_heads(o_ref, rows, outs)


def _ctx_attn_call(sink, q, k, v, n_seq, seq_len):
    n = n_seq * seq_len
    blk = lambda b: (b, 0)
    return pl.pallas_call(
        _ctx_attn_kernel,
        grid=(n_seq,),
        in_specs=[
            pl.BlockSpec(memory_space=pltpu.SMEM),
            pl.BlockSpec((seq_len, ATTN_WIDTH), blk),
            pl.BlockSpec((seq_len, KV_WIDTH), blk),
            pl.BlockSpec((seq_len, KV_WIDTH), blk),
        ],
        out_specs=pl.BlockSpec((seq_len, ATTN_WIDTH), blk),
        out_shape=jax.ShapeDtypeStruct((n, ATTN_WIDTH), BF16),
        compiler_params=_params(("arbitrary",)),
        name="ctx_attention",
    )(sink, q, k, v)


def _lat_attn_kernel(sink_ref, q_ref, k_ref, v_ref, ck_ref, cv_ref, o_ref):
    blocks_per_step = q_ref.shape[0] // BLOCK
    nb = k_ref.shape[0] // BLOCK
    past_blocks = ck_ref.shape[0] // LANES
    r = lax.broadcasted_iota(jnp.int32, (Q_PER_KV * BLOCK, BLOCK), 0) & (BLOCK - 1)
    c = lax.broadcasted_iota(jnp.int32, (Q_PER_KV * BLOCK, BLOCK), 1)
    for sub in range(blocks_per_step):
        j = pl.program_id(1) * blocks_per_step + sub
        rows = slice(sub * BLOCK, (sub + 1) * BLOCK)
        mask_prev = c >= r + jnp.where(j > 0, 0, BLOCK)
        mask_next = c <= r - jnp.where(j < nb - 1, 0, BLOCK)
        prev = pl.ds(pl.multiple_of(jnp.maximum(j - 1, 0) * BLOCK, BLOCK), BLOCK)
        cur = pl.ds(pl.multiple_of(j * BLOCK, BLOCK), BLOCK)
        nxt = pl.ds(pl.multiple_of(jnp.minimum(j + 1, nb - 1) * BLOCK, BLOCK), BLOCK)
        outs = {}
        for hk in range(N_KV_HEADS):
            kv_cols = slice(hk * HEAD_DIM, (hk + 1) * HEAD_DIM)
            k_all = jnp.concatenate([ck_ref[:, kv_cols], k_ref[prev, kv_cols], k_ref[cur, kv_cols],
                                     k_ref[nxt, kv_cols]], axis=0)
            v_all = jnp.concatenate([cv_ref[:, kv_cols], v_ref[prev, kv_cols], v_ref[cur, kv_cols],
                                     v_ref[nxt, kv_cols]], axis=0)
            masks = {past_blocks: mask_prev, past_blocks + 2: mask_next}
            outs.update(_group_attention(q_ref, rows, hk, k_all, v_all, masks, sink_ref))
        _store_heads(o_ref, rows, outs)


def _lat_attn_call(sink, q, k, v, cache_k, cache_v, geom):
    qb = LAT_QUERY_ROWS
    steps = geom.dec_seq // qb
    assert geom.n_ctx % geom.dec_seq == 0 and cache_k.shape[1] % LANES == 0
    seq_base = geom.n_ctx // geom.dec_seq
    past = cache_k.shape[1]
    seq_spec = pl.BlockSpec((geom.dec_seq, KV_WIDTH), lambda b, j: (seq_base + b, 0))
    cache_spec = pl.BlockSpec((None, past, KV_WIDTH), lambda b, j: (b, 0, 0))
    return pl.pallas_call(
        _lat_attn_kernel,
        grid=(geom.dec_batch, steps),
        in_specs=[
            pl.BlockSpec(memory_space=pltpu.SMEM),
            pl.BlockSpec((qb, ATTN_WIDTH), lambda b, j: (geom.n_ctx // qb + b * steps + j, 0)),
            seq_spec, seq_spec,
            cache_spec, cache_spec,
        ],
        out_specs=pl.BlockSpec((qb, ATTN_WIDTH), lambda b, j: (b * steps + j, 0)),
        out_shape=jax.ShapeDtypeStruct((geom.dec_batch * geom.dec_seq, ATTN_WIDTH), BF16),
        compiler_params=_params(("arbitrary", "arbitrary")),
        name="lat_attention",
    )(sink, q, k, v, cache_k, cache_v)


def _sgu_kernel(u_ref, g_ref, ws_ref, bias_ref, s_ref):
    tm = u_ref.shape[0]
    lane = lax.broadcasted_iota(jnp.int32, (CHUNK, LANES), 1)
    lower_group = lane < SGU_GROUP_DIM
    for ch in range(tm // CHUNK):
        rows = slice(ch * CHUNK, (ch + 1) * CHUNK)
        for p in range(SGU_WIDTH // LANES):
            cols = slice(p * LANES, (p + 1) * LANES)
            g = g_ref[rows, cols]
            zero = jnp.zeros_like(g)
            mixed = (jnp.dot(ws_ref[2 * p], jnp.where(lower_group, g, zero), preferred_element_type=F32)
                     + jnp.dot(ws_ref[2 * p + 1], jnp.where(lower_group, zero, g), preferred_element_type=F32))
            s_ref[rows, cols] = (u_ref[rows, cols] * (mixed + bias_ref[:, cols])).astype(s_ref.dtype)


def _sgu_call(u, g, w_s, bias_full):
    n = u.shape[0]
    tm = TOKEN_TILE
    row = lambda i: (i, 0)
    return pl.pallas_call(
        _sgu_kernel,
        grid=(n // tm,),
        in_specs=[
            pl.BlockSpec((tm, SGU_WIDTH), row),
            pl.BlockSpec((tm, SGU_WIDTH), row),
            pl.BlockSpec((N_SGU_GROUPS, CHUNK, CHUNK), lambda i: (0, 0, 0)),
            pl.BlockSpec((CHUNK, SGU_WIDTH), lambda i: (0, 0)),
        ],
        out_specs=pl.BlockSpec((tm, SGU_WIDTH), row),
        out_shape=jax.ShapeDtypeStruct((n, SGU_WIDTH), BF16),
        compiler_params=_params(("arbitrary",)),
        name="spatial_gating",
    )(u, g, w_s, bias_full)


def _out_kernel(alpha, with_router, split_x, ctx_tiles, a_ctx_ref, a_lat_ref, s_ref, *refs):
    if split_x:
        x = _read_stream(refs[0], refs[1], ctx_tiles)
        refs = refs[2:]
    else:
        x = refs[0][...]
        refs = refs[1:]
    mod_ref, wa_ref, ws_ref, g_ref, b_ref = refs[:5]
    if with_router:
        wr_ref, x1_ref, h_ref, rt_ref = refs[5:]
    else:
        x1_ref, h_ref = refs[5:]
    a = _read_stream(a_ctx_ref, a_lat_ref, ctx_tiles)
    y = (jnp.dot(a, wa_ref[...], preferred_element_type=F32)
         + jnp.dot(s_ref[...], ws_ref[...], preferred_element_type=F32))
    t = alpha * x + mod_ref[2:3, :] * y
    x1 = _layer_norm_rows(t, g_ref[...], b_ref[...])
    x1_ref[...] = x1
    h = x1 * (1.0 + mod_ref[4:5, :]) + mod_ref[3:4, :]
    h_ref[...] = h.astype(h_ref.dtype)
    if with_router:
        tm = h.shape[0]
        wr = wr_ref[...]
        wr_hi = wr.astype(BF16)
        wr_lo = (wr - wr_hi.astype(F32)).astype(BF16)
        h_hi = h.astype(BF16)
        h_lo = (h - h_hi.astype(F32)).astype(BF16)
        logits = (jnp.dot(h_hi, wr_hi, preferred_element_type=F32)
                  + jnp.dot(h_lo, wr_hi, preferred_element_type=F32)
                  + jnp.dot(h_hi, wr_lo, preferred_element_type=F32))
        lane = lax.broadcasted_iota(jnp.int32, (tm, LANES), 1).astype(F32)
        neg = jnp.float32(-jnp.inf)
        lg = jnp.where(lane < N_EXPERTS, logits, neg)
        m1 = jnp.max(lg, axis=-1, keepdims=True)
        i1 = jnp.min(jnp.where(lg == m1, lane, float(LANES)), axis=-1, keepdims=True)
        lg2 = jnp.where(lane == i1, neg, lg)
        m2 = jnp.max(lg2, axis=-1, keepdims=True)
        i2 = jnp.min(jnp.where(lg2 == m2, lane, float(LANES)), axis=-1, keepdims=True)
        e2 = jnp.exp(m2 - m1)
        g1 = 1.0 / (1.0 + e2)
        g2 = e2 / (1.0 + e2)
        rt = jnp.where(lane == 0, i1, jnp.where(lane == 1, i2, jnp.where(lane == 2, g1, jnp.where(lane == 3, g2, 0.0))))
        rt_ref[...] = rt


def _out_call(a_ctx, a_lat, s, x, mods, w_o, ln_g, ln_b, w_router, alpha, geom):
    n = geom.n_tok
    tm = TOKEN_TILE
    ctx_tiles = geom.ctx_tiles
    row = lambda i: (i, 0)
    const = lambda i: (0, 0)
    with_router = w_router is not None
    split_x = isinstance(x, tuple)
    x_args = list(x) if split_x else [x]
    x_specs = _stream_specs(D_MODEL, ctx_tiles, tm) if split_x else [pl.BlockSpec((tm, D_MODEL), row)]
    in_specs = _stream_specs(ATTN_WIDTH, ctx_tiles, tm) + [pl.BlockSpec((tm, SGU_WIDTH), row)] + x_specs + [
        pl.BlockSpec((None, N_MODS, D_MODEL), lambda i: (geom.mod_index(i), 0, 0)),
        pl.BlockSpec((ATTN_WIDTH, D_MODEL), const),
        pl.BlockSpec((SGU_WIDTH, D_MODEL), lambda i: (1, 0)),
        pl.BlockSpec((1, D_MODEL), const),
        pl.BlockSpec((1, D_MODEL), const),
    ]
    args = [a_ctx, a_lat, s] + x_args + [mods, w_o, w_o, ln_g, ln_b]
    out_specs = [pl.BlockSpec((tm, D_MODEL), row), pl.BlockSpec((tm, D_MODEL), row)]
    out_shape = [jax.ShapeDtypeStruct((n, D_MODEL), F32),
                 jax.ShapeDtypeStruct((n, D_MODEL), F32 if with_router else BF16)]
    if with_router:
        in_specs.append(pl.BlockSpec((D_MODEL, LANES), const))
        args.append(w_router)
        out_specs.append(pl.BlockSpec((tm, LANES), row))
        out_shape.append(jax.ShapeDtypeStruct((n, LANES), F32))
    return pl.pallas_call(
        functools.partial(_out_kernel, alpha, with_router, split_x, ctx_tiles),
        grid=(n // tm,),
        in_specs=in_specs,
        out_specs=out_specs,
        out_shape=out_shape,
        compiler_params=_params(("arbitrary",)),
        name="out_proj_router" if with_router else "out_proj",
    )(*args)


def _swiglu_partial(x, wg, wu, wd):
    a = jnp.dot(x, wg, preferred_element_type=F32)
    b = jnp.dot(x, wu, preferred_element_type=F32)
    mid = (a * _sigmoid(a)) * b
    return jnp.dot(mid.astype(BF16), wd, preferred_element_type=F32)


def _ffn_kernel(alpha, h_ref, x1_ref, mod_ref, wg_ref, wu_ref, wd_ref, g_ref, b_ref, o_ref, acc_ref):
    c = pl.program_id(1)
    part = _swiglu_partial(h_ref[...], wg_ref[...], wu_ref[...], wd_ref[...])

    @pl.when(c == 0)
    def _():
        acc_ref[...] = part

    @pl.when(c > 0)
    def _():
        acc_ref[...] += part

    @pl.when(c == pl.num_programs(1) - 1)
    def _():
        t = alpha * x1_ref[...] + mod_ref[5:6, :] * acc_ref[...]
        o_ref[...] = _layer_norm_rows(t, g_ref[...], b_ref[...])


def _ffn_chunk(d_ff):
    assert d_ff % (2 * LANES) == 0
    return d_ff // 2


def _ffn_call(h, x1, mods, wg, wu, wd, ln_g, ln_b, alpha, geom):
    n = h.shape[0]
    tm = FFN_TILE
    d_ff = wg.shape[1]
    fc = _ffn_chunk(d_ff)
    ratio = tm // TOKEN_TILE
    row = lambda i, c: (i, 0)
    const = lambda i, c: (0, 0)
    return pl.pallas_call(
        functools.partial(_ffn_kernel, alpha),
        grid=(n // tm, d_ff // fc),
        in_specs=[
            pl.BlockSpec((tm, D_MODEL), row),
            pl.BlockSpec((tm, D_MODEL), row),
            pl.BlockSpec((None, N_MODS, D_MODEL), lambda i, c: (geom.mod_index(i * ratio), 0, 0)),
            pl.BlockSpec((D_MODEL, fc), lambda i, c: (0, c)),
            pl.BlockSpec((D_MODEL, fc), lambda i, c: (0, c)),
            pl.BlockSpec((fc, D_MODEL), lambda i, c: (c, 0)),
            pl.BlockSpec((1, D_MODEL), const),
            pl.BlockSpec((1, D_MODEL), const),
        ],
        out_specs=pl.BlockSpec((tm, D_MODEL), row),
        out_shape=jax.ShapeDtypeStruct((n, D_MODEL), F32),
        scratch_shapes=[pltpu.VMEM((tm, D_MODEL), F32)],
        compiler_params=_params(("arbitrary", "arbitrary")),
        name="dense_ffn",
    )(h, x1, mods, wg, wu, wd, ln_g, ln_b)


def _moe_kernel(n_chunks, te_ref, nu_ref, src_ref, dst_ref, h_hbm, wg_ref, wu_ref, wd_ref, o_hbm,
                xbuf, xb, obuf, gsem, ssem):
    del te_ref
    tr = xb.shape[0]
    t = pl.program_id(0)
    c = pl.program_id(1)
    n_used = nu_ref[0]
    valid = t < n_used
    slot = t % 2
    rows_per_step = tr // n_chunks

    def row_gather(tile, dst_slot, r):
        tok = src_ref[tile * tr + r]
        return pltpu.make_async_copy(h_hbm.at[pl.ds(tok, 1), :], xbuf.at[dst_slot, pl.ds(r, 1), :],
                                     gsem.at[dst_slot])

    def start_gather(tile, dst_slot, first_row, n_rows):
        def body(r, carry):
            row_gather(tile, dst_slot, first_row + r).start()
            return carry
        lax.fori_loop(0, n_rows, body, 0, unroll=8)

    def wait_gather(dst_slot):
        pltpu.make_async_copy(h_hbm.at[pl.ds(0, tr), :], xbuf.at[dst_slot], gsem.at[dst_slot]).wait()

    def row_scatter(src_slot, r):
        dst = dst_ref[t * tr + r]
        return pltpu.make_async_copy(obuf.at[src_slot, pl.ds(r, 1), :], o_hbm.at[pl.ds(dst, 1), :],
                                     ssem.at[src_slot])

    def wait_scatter(src_slot):
        pltpu.make_async_copy(obuf.at[src_slot], o_hbm.at[pl.ds(0, tr), :], ssem.at[src_slot]).wait()

    @pl.when((c == 0) & (t >= 2))
    def _():
        wait_scatter(slot)

    @pl.when(valid & (c == 0))
    def _():
        @pl.when(t == 0)
        def _():
            start_gather(0, 0, 0, tr)

        wait_gather(slot)
        xb[...] = xbuf[slot].astype(BF16)

    @pl.when(t + 1 < n_used)
    def _():
        start_gather(t + 1, 1 - slot, c * rows_per_step, rows_per_step)

    @pl.when(valid)
    def _():
        part = _swiglu_partial(xb[...], wg_ref[...], wu_ref[...], wd_ref[...])

        @pl.when(c == 0)
        def _():
            obuf[slot] = part

        @pl.when(c > 0)
        def _():
            obuf[slot] += part

    @pl.when(jnp.logical_not(valid) & (c == 0))
    def _():
        obuf[slot] = jnp.zeros(obuf.shape[1:], obuf.dtype)

    @pl.when(c == n_chunks - 1)
    def _():
        def body(r, carry):
            row_scatter(slot, r).start()
            return carry
        lax.fori_loop(0, tr, body, 0, unroll=8)

        @pl.when(t == pl.num_programs(0) - 1)
        def _():
            wait_scatter(1 - slot)
            wait_scatter(slot)


def _moe_call(h, wg, wu, wd, plan):
    tr = FFN_TILE
    d_ff = wg.shape[2]
    fc = _ffn_chunk(d_ff)
    n_chunks = d_ff // fc
    t_max = plan["tile_expert"].shape[0]

    def chunk(t, c, nu):
        return jnp.where(t < nu[0], c, n_chunks - 1)

    grid_spec = pltpu.PrefetchScalarGridSpec(
        num_scalar_prefetch=4,
        grid=(t_max, n_chunks),
        in_specs=[
            pl.BlockSpec(memory_space=pl.ANY),
            pl.BlockSpec((None, D_MODEL, fc), lambda t, c, te, nu, src, dst: (te[t], 0, chunk(t, c, nu))),
            pl.BlockSpec((None, D_MODEL, fc), lambda t, c, te, nu, src, dst: (te[t], 0, chunk(t, c, nu))),
            pl.BlockSpec((None, fc, D_MODEL), lambda t, c, te, nu, src, dst: (te[t], chunk(t, c, nu), 0)),
        ],
        out_specs=pl.BlockSpec(memory_space=pl.ANY),
        scratch_shapes=[
            pltpu.VMEM((2, tr, D_MODEL), F32),
            pltpu.VMEM((tr, D_MODEL), BF16),
            pltpu.VMEM((2, tr, D_MODEL), F32),
            pltpu.SemaphoreType.DMA((2,)),
            pltpu.SemaphoreType.DMA((2,)),
        ],
    )
    return pl.pallas_call(
        functools.partial(_moe_kernel, n_chunks),
        grid_spec=grid_spec,
        out_shape=jax.ShapeDtypeStruct((t_max * tr, D_MODEL), F32),
        compiler_params=_params(("arbitrary", "arbitrary")),
        name="expert_ffn",
    )(plan["tile_expert"], plan["n_used"], plan["src_token"], plan["dst_row"], h, wg, wu, wd)


def _route_plan(routing, n_tok, tr):
    n_assign = TOP_K * n_tok
    t_max = n_assign // tr + N_EXPERTS
    n_slots = t_max * tr
    n_pad = n_slots - n_assign
    id_bits = (n_slots - 1).bit_length()
    experts = jnp.arange(N_EXPERTS, dtype=jnp.int32)
    e_flat = routing[:, :TOP_K].astype(jnp.int32).T.reshape(-1)
    counts = jnp.sum((e_flat[:, None] == experts[None, :]).astype(jnp.int32), axis=0)
    tiles_e = (counts + tr - 1) // tr
    tile_end = jnp.cumsum(tiles_e)
    n_used = tile_end[-1]
    pad_end = jnp.cumsum(tiles_e * tr - counts)
    pad_ids = jnp.arange(n_pad, dtype=jnp.int32)
    pad_expert = jnp.sum((pad_ids[:, None] >= pad_end[None, :]).astype(jnp.int32), axis=1)
    keys = jnp.concatenate([e_flat * 2, pad_expert * 2 + 1])
    item = jnp.arange(n_slots, dtype=jnp.int32)
    slot_item = jnp.sort((keys << id_bits) | item) & ((1 << id_bits) - 1)
    real = slot_item < n_assign
    src = jnp.where(real, slot_item % n_tok, 0).astype(jnp.int32)
    dst = slot_item.astype(jnp.int32)
    t_ids = jnp.arange(t_max, dtype=jnp.int32)
    te = jnp.sum((t_ids[:, None] >= tile_end[None, :]).astype(jnp.int32), axis=1)
    last_used = jnp.max(jnp.where(tiles_e > 0, experts, 0))
    te = jnp.where(t_ids < n_used, te, last_used).astype(jnp.int32)
    return {"tile_expert": te, "n_used": n_used.reshape(1).astype(jnp.int32), "src_token": src, "dst_row": dst}


def _combine_kernel(alpha, split_ctx_tiles, e1_ref, e2_ref, rt_ref, x1_ref, mod_ref, g_ref, b_ref, *o_refs):
    y = rt_ref[:, 2:3] * e1_ref[...] + rt_ref[:, 3:4] * e2_ref[...]
    t = alpha * x1_ref[...] + mod_ref[5:6, :] * y
    out = _layer_norm_rows(t, g_ref[...], b_ref[...])
    if split_ctx_tiles is None:
        o_refs[0][...] = out
    else:
        _write_stream(o_refs[0], o_refs[1], split_ctx_tiles, out)


def _combine_call(eo, routing, x1, mods, ln_g, ln_b, alpha, geom, split_out):
    n = x1.shape[0]
    tm = TOKEN_TILE
    row = lambda i: (i, 0)
    const = lambda i: (0, 0)
    if split_out:
        out_specs = _stream_specs(D_MODEL, geom.ctx_tiles, tm)
        out_shape = [jax.ShapeDtypeStruct((geom.n_ctx, D_MODEL), F32),
                     jax.ShapeDtypeStruct((n - geom.n_ctx, D_MODEL), F32)]
    else:
        out_specs = pl.BlockSpec((tm, D_MODEL), row)
        out_shape = jax.ShapeDtypeStruct((n, D_MODEL), F32)
    return pl.pallas_call(
        functools.partial(_combine_kernel, alpha, geom.ctx_tiles if split_out else None),
        grid=(n // tm,),
        in_specs=[
            pl.BlockSpec((tm, D_MODEL), row),
            pl.BlockSpec((tm, D_MODEL), lambda i: (i + n // tm, 0)),
            pl.BlockSpec((tm, LANES), row),
            pl.BlockSpec((tm, D_MODEL), row),
            pl.BlockSpec((None, N_MODS, D_MODEL), lambda i: (geom.mod_index(i), 0, 0)),
            pl.BlockSpec((1, D_MODEL), const),
            pl.BlockSpec((1, D_MODEL), const),
        ],
        out_specs=out_specs,
        out_shape=out_shape,
        compiler_params=_params(("arbitrary",)),
        name="expert_combine",
    )(eo, eo, routing, x1, mods, ln_g, ln_b)


class _Geometry:
    def __init__(self, n_ctx, dec_batch, dec_seq):
        self.n_ctx = n_ctx
        self.dec_batch = dec_batch
        self.dec_seq = dec_seq
        self.n_tok = n_ctx + dec_batch * dec_seq
        assert n_ctx % FFN_TILE == 0 and dec_seq % FFN_TILE == 0
        self.ctx_tiles = n_ctx // TOKEN_TILE
        self.lat_tiles = dec_seq // TOKEN_TILE

    def mod_index(self, i):
        return jnp.where(i < self.ctx_tiles, 0, 1 + (i - self.ctx_tiles) // self.lat_tiles)

    def rope_index(self, i):
        return jnp.where(i < self.ctx_tiles, 0, 1 + (i - self.ctx_tiles) % self.lat_tiles)


def _rope_tables(dec_seq):
    pos = jnp.arange(dec_seq, dtype=jnp.int32)
    row = (pos // GRID_W).astype(F32)
    col = (pos % GRID_W).astype(F32)
    inv = ROPE_BASE ** (-jnp.arange(ROPE_PAIRS, dtype=F32) / ROPE_PAIRS)
    ang_r = row[:, None] * inv[None, :]
    ang_c = col[:, None] * inv[None, :]
    cos_h = jnp.concatenate([jnp.cos(ang_r), jnp.cos(ang_r), jnp.cos(ang_c), jnp.cos(ang_c)], axis=-1)
    sin_h = jnp.concatenate([-jnp.sin(ang_r), jnp.sin(ang_r), -jnp.sin(ang_c), jnp.sin(ang_c)], axis=-1)
    reps = LANES // HEAD_DIM
    cos_t = jnp.concatenate([jnp.ones((TOKEN_TILE, LANES), F32), jnp.tile(cos_h, (1, reps))], axis=0)
    sin_t = jnp.concatenate([jnp.zeros((TOKEN_TILE, LANES), F32), jnp.tile(sin_h, (1, reps))], axis=0)
    return cos_t, sin_t


def kernel(x_prompt, x_sample, cache_k, cache_v, c, c_ctx, w_ada, b_ada, w_in, w_o, attn_sink, w_s, b_s, sgu_ln_g, sgu_ln_b, ln1_g, ln1_b, ln2_g, ln2_b, w_ff_gate, w_ff_up, w_ff_down, w_router, w_exp_gate, w_exp_up, w_exp_down):
    batch, seq, d = x_prompt.shape
    dec_batch, dec_seq, _ = x_sample.shape
    depth = w_in.shape[0]
    past = cache_k.shape[2]
    assert d == D_MODEL and dec_batch + 1 <= MOD_ROWS
    n_ctx = batch * seq
    n_lat = dec_batch * dec_seq
    n_tok = n_ctx + n_lat
    geom = _Geometry(n_ctx, dec_batch, dec_seq)
    alpha = float((2 * depth) ** 0.25)

    x = (x_prompt.reshape(n_ctx, d), x_sample.reshape(n_lat, d))
    cvec = jnp.concatenate([c_ctx[None, :], c, jnp.zeros((MOD_ROWS - 1 - dec_batch, d), F32)], axis=0)
    mods_all = _ada_call(cvec, w_ada, b_ada).reshape(depth, MOD_ROWS, N_MODS, d)
    cos_t, sin_t = _rope_tables(dec_seq)

    new_k, new_v = [], []
    for l in range(depth):
        mods = mods_all[l]
        q, k, v, kv32, u, g = _in_call(x, mods, w_in[l].astype(BF16), cos_t, sin_t,
                                       sgu_ln_g[l].reshape(1, SGU_WIDTH), sgu_ln_b[l].reshape(1, SGU_WIDTH), geom)
        new_k.append(kv32[:n_ctx, :KV_WIDTH].reshape(batch, seq, N_KV_HEADS, HEAD_DIM))
        new_v.append(kv32[:n_ctx, KV_WIDTH:].reshape(batch, seq, N_KV_HEADS, HEAD_DIM))
        sink = attn_sink[l]
        a_ctx = _ctx_attn_call(sink, q, k, v, batch, seq)
        a_lat = _lat_attn_call(sink, q, k, v, cache_k[:, l].reshape(dec_batch, past, KV_WIDTH).astype(BF16),
                               cache_v[:, l].reshape(dec_batch, past, KV_WIDTH).astype(BF16), geom)
        bias_full = jnp.repeat(b_s[l].T, SGU_GROUP_DIM, axis=1)
        s = _sgu_call(u, g, w_s[l].astype(BF16), bias_full)
        i = l // 2
        moe = l % 2 == 1
        w_r = None
        if moe:
            w_r = jnp.pad(w_router[i], ((0, 0), (0, LANES - N_EXPERTS)))
        outs = _out_call(a_ctx, a_lat, s, x, mods, w_o[l].astype(BF16), ln1_g[l].reshape(1, d), ln1_b[l].reshape(1, d),
                         w_r, alpha, geom)
        ln_g, ln_b = ln2_g[l].reshape(1, d), ln2_b[l].reshape(1, d)
        if moe:
            x1, h, routing = outs
            plan = _route_plan(routing, n_tok, FFN_TILE)
            eo = _moe_call(h, w_exp_gate[i].astype(BF16), w_exp_up[i].astype(BF16), w_exp_down[i].astype(BF16), plan)
            x = _combine_call(eo, routing, x1, mods, ln_g, ln_b, alpha, geom, split_out=l == depth - 1)
        else:
            x1, h = outs
            x = _ffn_call(h, x1, mods, w_ff_gate[i].astype(BF16), w_ff_up[i].astype(BF16),
                          w_ff_down[i].astype(BF16), ln_g, ln_b, alpha, geom)

    if not isinstance(x, (tuple, list)):
        x = (x[:n_ctx], x[n_ctx:])
    y_prompt = x[0].reshape(batch, seq, d)
    y_sample = x[1].reshape(dec_batch, dec_seq, d)
    return (y_prompt, y_sample, jnp.stack(new_k, axis=1), jnp.stack(new_v, axis=1))
```

```python
import functools

import jax
import jax.numpy as jnp
import numpy as np
from jax import lax
from jax.experimental import pallas as pl
from jax.experimental.pallas import tpu as pltpu

F32 = jnp.float32
BF16 = jnp.bfloat16

D_MODEL = 1024
HEAD_DIM = 64
N_Q_HEADS = 8
N_KV_HEADS = 2
Q_PER_KV = N_Q_HEADS // N_KV_HEADS
ATTN_WIDTH = N_Q_HEADS * HEAD_DIM
KV_WIDTH = N_KV_HEADS * HEAD_DIM
SGU_WIDTH = D_MODEL - ATTN_WIDTH
N_SGU_GROUPS = 8
SGU_GROUP_DIM = SGU_WIDTH // N_SGU_GROUPS
CHUNK = 128
BLOCK = 128
WINDOW = 128
GRID_W = 64
IN_WIDTH = ATTN_WIDTH + 2 * KV_WIDTH + 2 * SGU_WIDTH
OFF_K = ATTN_WIDTH
OFF_V = OFF_K + KV_WIDTH
OFF_U = OFF_V + KV_WIDTH
OFF_G = OFF_U + SGU_WIDTH
N_EXPERTS = 8
TOP_K = 2
ROPE_BASE = 10000.0
ROPE_PAIRS = HEAD_DIM // 4
LN_EPS = 1e-5
ATTN_SCALE = HEAD_DIM ** -0.5
NEG_INF = -1e30
N_MODS = 6

LANES = 128
V7X_MXU_WIDTH = 256
V7X_VMEM_LIMIT_BYTES = 48 * 1024 * 1024

TOKEN_TILE = 256
FFN_TILE = 512
LAT_QUERY_ROWS = 256
MOD_ROWS = 8
ADA_COLS = 1536


def _params(sem):
    return pltpu.CompilerParams(dimension_semantics=sem, vmem_limit_bytes=V7X_VMEM_LIMIT_BYTES)


def _gelu_tanh(x):
    return x * (0.5 * (1.0 + jnp.tanh(np.float32(np.sqrt(2.0 / np.pi)) * (x + 0.044715 * (x * x * x)))))


def _sigmoid(x):
    return 1.0 / (1.0 + jnp.exp(-x))


def _layer_norm_rows(t, g, b):
    mu = jnp.mean(t, axis=-1, keepdims=True)
    tc = t - mu
    var = jnp.mean(tc * tc, axis=-1, keepdims=True)
    return tc * lax.rsqrt(var + LN_EPS) * g + b


def _stream_specs(width, ctx_tiles, tm):
    return [pl.BlockSpec((tm, width), lambda i, *_: (jnp.minimum(i, ctx_tiles - 1), 0)),
            pl.BlockSpec((tm, width), lambda i, *_: (jnp.maximum(i - ctx_tiles, 0), 0))]


def _read_stream(ctx_ref, lat_ref, ctx_tiles):
    tile = lax.broadcasted_iota(jnp.int32, ctx_ref.shape, 0) * 0 + pl.program_id(0)
    return jnp.where(tile < ctx_tiles, ctx_ref[...], lat_ref[...])


def _write_stream(ctx_ref, lat_ref, ctx_tiles, value):
    @pl.when(pl.program_id(0) < ctx_tiles)
    def _():
        ctx_ref[...] = value

    @pl.when(pl.program_id(0) >= ctx_tiles)
    def _():
        lat_ref[...] = value


def _ada_kernel(c_ref, w_ref, b_ref, o_ref):
    c = c_ref[...]
    s = (c * _sigmoid(c)).astype(BF16)
    o_ref[...] = jnp.dot(s, w_ref[...].astype(BF16), preferred_element_type=F32) + b_ref[...]


def _ada_call(cvec, w_ada, b_ada):
    depth = w_ada.shape[0]
    n_out = w_ada.shape[2]
    return pl.pallas_call(
        _ada_kernel,
        grid=(depth, n_out // ADA_COLS),
        in_specs=[
            pl.BlockSpec((MOD_ROWS, D_MODEL), lambda l, j: (0, 0)),
            pl.BlockSpec((None, D_MODEL, ADA_COLS), lambda l, j: (l, 0, j)),
            pl.BlockSpec((None, 1, ADA_COLS), lambda l, j: (l, 0, j)),
        ],
        out_specs=pl.BlockSpec((None, MOD_ROWS, ADA_COLS), lambda l, j: (l, 0, j)),
        out_shape=jax.ShapeDtypeStruct((depth, MOD_ROWS, n_out), F32),
        compiler_params=_params(("arbitrary", "arbitrary")),
        name="adaln",
    )(cvec, w_ada, b_ada.reshape(depth, 1, n_out))


def _in_kernel(split_ctx_tiles, *refs):
    if split_ctx_tiles is None:
        x = refs[0][...]
        refs = refs[1:]
    else:
        x = _read_stream(refs[0], refs[1], split_ctx_tiles)
        refs = refs[2:]
    mod_ref, w_ref, cos_ref, sin_ref, lng_ref, lnb_ref, q_ref, k_ref, v_ref, kv_ref, u_ref, g_ref = refs
    tm = x.shape[0]
    h = x * (1.0 + mod_ref[1:2, :]) + mod_ref[0:1, :]
    z = jnp.dot(h.astype(BF16), w_ref[...], preferred_element_type=F32)
    cos = cos_ref[...]
    sin = sin_ref[...]
    lane = lax.broadcasted_iota(jnp.int32, (tm, LANES), 1)
    first_of_pair = (lane & (2 * ROPE_PAIRS - 1)) < ROPE_PAIRS
    lower_group = lane < SGU_GROUP_DIM

    def rope(t):
        partner = jnp.where(first_of_pair, pltpu.roll(t, LANES - ROPE_PAIRS, 1), pltpu.roll(t, ROPE_PAIRS, 1))
        return t * cos + partner * sin

    for j in range(ATTN_WIDTH // LANES):
        cols = slice(j * LANES, (j + 1) * LANES)
        q_ref[:, cols] = (rope(z[:, cols]) * ATTN_SCALE).astype(BF16)
    k_ref[...] = rope(z[:, OFF_K:OFF_V]).astype(BF16)
    v_ref[...] = z[:, OFF_V:OFF_U].astype(BF16)
    kv_ref[...] = z[:, OFF_K:OFF_U]
    u_ref[...] = _gelu_tanh(z[:, OFF_U:OFF_G])

    inv_n = 1.0 / SGU_GROUP_DIM
    for j in range(SGU_WIDTH // LANES):
        cols = slice(j * LANES, (j + 1) * LANES)
        t = _gelu_tanh(z[:, OFF_G + j * LANES:OFF_G + (j + 1) * LANES])
        s_lo = jnp.sum(jnp.where(lower_group, t, 0.0), axis=-1, keepdims=True)
        s_hi = jnp.sum(jnp.where(lower_group, 0.0, t), axis=-1, keepdims=True)
        tc = t - jnp.where(lower_group, s_lo, s_hi) * inv_n
        sq = tc * tc
        v_lo = jnp.sum(jnp.where(lower_group, sq, 0.0), axis=-1, keepdims=True)
        v_hi = jnp.sum(jnp.where(lower_group, 0.0, sq), axis=-1, keepdims=True)
        var = jnp.where(lower_group, v_lo, v_hi) * inv_n
        g_ref[:, cols] = (tc * lax.rsqrt(var + LN_EPS) * lng_ref[:, cols] + lnb_ref[:, cols]).astype(BF16)


def _in_call(x, mods, w_in, cos_t, sin_t, ln_g, ln_b, geom):
    n = geom.n_tok
    tm = TOKEN_TILE
    mod_idx, rope_idx = geom.mod_index, geom.rope_index
    row = lambda i: (i, 0)
    split = isinstance(x, tuple)
    x_args = list(x) if split else [x]
    x_specs = _stream_specs(D_MODEL, geom.ctx_tiles, tm) if split else [pl.BlockSpec((tm, D_MODEL), row)]
    outs = pl.pallas_call(
        functools.partial(_in_kernel, geom.ctx_tiles if split else None),
        grid=(n // tm,),
        in_specs=x_specs + [
            pl.BlockSpec((None, N_MODS, D_MODEL), lambda i: (mod_idx(i), 0, 0)),
            pl.BlockSpec((D_MODEL, IN_WIDTH), lambda i: (0, 0)),
            pl.BlockSpec((tm, LANES), lambda i: (rope_idx(i), 0)),
            pl.BlockSpec((tm, LANES), lambda i: (rope_idx(i), 0)),
            pl.BlockSpec((1, SGU_WIDTH), lambda i: (0, 0)),
            pl.BlockSpec((1, SGU_WIDTH), lambda i: (0, 0)),
        ],
        out_specs=[
            pl.BlockSpec((tm, ATTN_WIDTH), row),
            pl.BlockSpec((tm, KV_WIDTH), row),
            pl.BlockSpec((tm, KV_WIDTH), row),
            pl.BlockSpec((tm, 2 * KV_WIDTH), row),
            pl.BlockSpec((tm, SGU_WIDTH), row),
            pl.BlockSpec((tm, SGU_WIDTH), row),
        ],
        out_shape=[
            jax.ShapeDtypeStruct((n, ATTN_WIDTH), BF16),
            jax.ShapeDtypeStruct((n, KV_WIDTH), BF16),
            jax.ShapeDtypeStruct((n, KV_WIDTH), BF16),
            jax.ShapeDtypeStruct((n, 2 * KV_WIDTH), F32),
            jax.ShapeDtypeStruct((n, SGU_WIDTH), F32),
            jax.ShapeDtypeStruct((n, SGU_WIDTH), BF16),
        ],
        compiler_params=_params(("arbitrary",)),
        name="in_proj",
    )(*x_args, mods, w_in, cos_t, sin_t, ln_g, ln_b)
    return outs


def _group_attention(q_ref, rows, hk, k_all, v_all, block_masks, sink_ref):
    m_rows = rows.stop - rows.start
    heads = [hk * Q_PER_KV + gq for gq in range(Q_PER_KV)]
    q = jnp.concatenate([q_ref[rows, h * HEAD_DIM:(h + 1) * HEAD_DIM] for h in heads], axis=0)
    s = lax.dot_general(q, k_all, (((1,), (1,)), ((), ())), preferred_element_type=F32)
    n_blocks = k_all.shape[0] // LANES
    blocks = [s[:, b * LANES:(b + 1) * LANES] for b in range(n_blocks)]
    for b, mask in block_masks.items():
        blocks[b] = jnp.where(mask, blocks[b], NEG_INF)
    head_of_row = lax.broadcasted_iota(jnp.int32, (Q_PER_KV * m_rows, 1), 0) // m_rows
    sink = jnp.zeros((Q_PER_KV * m_rows, 1), F32)
    for gq, h in enumerate(heads):
        sink = jnp.where(head_of_row == gq, sink_ref[h], sink)
    m_el = blocks[0]
    for blk in blocks[1:]:
        m_el = jnp.maximum(m_el, blk)
    m = jnp.maximum(jnp.max(m_el, axis=-1, keepdims=True), sink)
    probs = [jnp.exp(blk - m) for blk in blocks]
    l_el = probs[0]
    for p in probs[1:]:
        l_el = l_el + p
    denom = jnp.sum(l_el, axis=-1, keepdims=True) + jnp.exp(sink - m)
    p_all = jnp.concatenate([p.astype(BF16) for p in probs], axis=1)
    o = jnp.dot(p_all, v_all, preferred_element_type=F32) / denom
    return {h: o[gq * m_rows:(gq + 1) * m_rows] for gq, h in enumerate(heads)}


def _store_heads(o_ref, rows, outs):
    for h0 in range(0, N_Q_HEADS, 2):
        pair = jnp.concatenate([outs[h0], outs[h0 + 1]], axis=1)
        o_ref[rows, h0 * HEAD_DIM:(h0 + 2) * HEAD_DIM] = pair.astype(o_ref.dtype)


def _ctx_attn_kernel(sink_ref, q_ref, k_ref, v_ref, o_ref):
    rows = slice(0, q_ref.shape[0])
    outs = {}
    for hk in range(N_KV_HEADS):
        kv_cols = slice(hk * HEAD_DIM, (hk + 1) * HEAD_DIM)
        outs.update(_group_attention(q_ref, rows, hk, k_ref[:, kv_cols], v_ref[:, kv_cols], {}, sink_ref))
    _store_heads(o_ref, rows, outs)


def _ctx_attn_call(sink, q, k, v, n_seq, seq_len):
    n = n_seq * seq_len
    blk = lambda b: (b, 0)
    return pl.pallas_call(
        _ctx_attn_kernel,
        grid=(n_seq,),
        in_specs=[
            pl.BlockSpec(memory_space=pltpu.SMEM),
            pl.BlockSpec((seq_len, ATTN_WIDTH), blk),
            pl.BlockSpec((seq_len, KV_WIDTH), blk),
            pl.BlockSpec((seq_len, KV_WIDTH), blk),
        ],
        out_specs=pl.BlockSpec((seq_len, ATTN_WIDTH), blk),
        out_shape=jax.ShapeDtypeStruct((n, ATTN_WIDTH), BF16),
        compiler_params=_params(("arbitrary",)),
        name="ctx_attention",
    )(sink, q, k, v)


def _lat_attn_kernel(sink_ref, q_ref, k_ref, v_ref, ck_ref, cv_ref, o_ref):
    blocks_per_step = q_ref.shape[0] // BLOCK
    nb = k_ref.shape[0] // BLOCK
    past_blocks = ck_ref.shape[0] // LANES
    r = lax.broadcasted_iota(jnp.int32, (Q_PER_KV * BLOCK, BLOCK), 0) & (BLOCK - 1)
    c = lax.broadcasted_iota(jnp.int32, (Q_PER_KV * BLOCK, BLOCK), 1)
    for sub in range(blocks_per_step):
        j = pl.program_id(1) * blocks_per_step + sub
        rows = slice(sub * BLOCK, (sub + 1) * BLOCK)
        mask_prev = c >= r + jnp.where(j > 0, 0, BLOCK)
        mask_next = c <= r - jnp.where(j < nb - 1, 0, BLOCK)
        prev = pl.ds(pl.multiple_of(jnp.maximum(j - 1, 0) * BLOCK, BLOCK), BLOCK)
        cur = pl.ds(pl.multiple_of(j * BLOCK, BLOCK), BLOCK)
        nxt = pl.ds(pl.multiple_of(jnp.minimum(j + 1, nb - 1) * BLOCK, BLOCK), BLOCK)
        outs = {}
        for hk in range(N_KV_HEADS):
            kv_cols = slice(hk * HEAD_DIM, (hk + 1) * HEAD_DIM)
            k_all = jnp.concatenate([ck_ref[:, kv_cols], k_ref[prev, kv_cols], k_ref[cur, kv_cols],
                                     k_ref[nxt, kv_cols]], axis=0)
            v_all = jnp.concatenate([cv_ref[:, kv_cols], v_ref[prev, kv_cols], v_ref[cur, kv_cols],
                                     v_ref[nxt, kv_cols]], axis=0)
            masks = {past_blocks: mask_prev, past_blocks + 2: mask_next}
            outs.update(_group_attention(q_ref, rows, hk, k_all, v_all, masks, sink_ref))
        _store_heads(o_ref, rows, outs)


def _lat_attn_call(sink, q, k, v, cache_k, cache_v, geom):
    qb = LAT_QUERY_ROWS
    steps = geom.dec_seq // qb
    assert geom.n_ctx % geom.dec_seq == 0 and cache_k.shape[1] % LANES == 0
    seq_base = geom.n_ctx // geom.dec_seq
    past = cache_k.shape[1]
    seq_spec = pl.BlockSpec((geom.dec_seq, KV_WIDTH), lambda b, j: (seq_base + b, 0))
    cache_spec = pl.BlockSpec((None, past, KV_WIDTH), lambda b, j: (b, 0, 0))
    return pl.pallas_call(
        _lat_attn_kernel,
        grid=(geom.dec_batch, steps),
        in_specs=[
            pl.BlockSpec(memory_space=pltpu.SMEM),
            pl.BlockSpec((qb, ATTN_WIDTH), lambda b, j: (geom.n_ctx // qb + b * steps + j, 0)),
            seq_spec, seq_spec,
            cache_spec, cache_spec,
        ],
        out_specs=pl.BlockSpec((qb, ATTN_WIDTH), lambda b, j: (b * steps + j, 0)),
        out_shape=jax.ShapeDtypeStruct((geom.dec_batch * geom.dec_seq, ATTN_WIDTH), BF16),
        compiler_params=_params(("arbitrary", "arbitrary")),
        name="lat_attention",
    )(sink, q, k, v, cache_k, cache_v)


def _sgu_kernel(u_ref, g_ref, ws_ref, bias_ref, s_ref):
    tm = u_ref.shape[0]
    lane = lax.broadcasted_iota(jnp.int32, (CHUNK, LANES), 1)
    lower_group = lane < SGU_GROUP_DIM
    for ch in range(tm // CHUNK):
        rows = slice(ch * CHUNK, (ch + 1) * CHUNK)
        for p in range(SGU_WIDTH // LANES):
            cols = slice(p * LANES, (p + 1) * LANES)
            g = g_ref[rows, cols]
            zero = jnp.zeros_like(g)
            mixed = (jnp.dot(ws_ref[2 * p], jnp.where(lower_group, g, zero), preferred_element_type=F32)
                     + jnp.dot(ws_ref[2 * p + 1], jnp.where(lower_group, zero, g), preferred_element_type=F32))
            s_ref[rows, cols] = (u_ref[rows, cols] * (mixed + bias_ref[:, cols])).astype(s_ref.dtype)


def _sgu_call(u, g, w_s, bias_full):
    n = u.shape[0]
    tm = TOKEN_TILE
    row = lambda i: (i, 0)
    return pl.pallas_call(
        _sgu_kernel,
        grid=(n // tm,),
        in_specs=[
            pl.BlockSpec((tm, SGU_WIDTH), row),
            pl.BlockSpec((tm, SGU_WIDTH), row),
            pl.BlockSpec((N_SGU_GROUPS, CHUNK, CHUNK), lambda i: (0, 0, 0)),
            pl.BlockSpec((CHUNK, SGU_WIDTH), lambda i: (0, 0)),
        ],
        out_specs=pl.BlockSpec((tm, SGU_WIDTH), row),
        out_shape=jax.ShapeDtypeStruct((n, SGU_WIDTH), BF16),
        compiler_params=_params(("arbitrary",)),
        name="spatial_gating",
    )(u, g, w_s, bias_full)


def _out_kernel(alpha, with_router, split_x, ctx_tiles, a_ctx_ref, a_lat_ref, s_ref, *refs):
    if split_x:
        x = _read_stream(refs[0], refs[1], ctx_tiles)
        refs = refs[2:]
    else:
        x = refs[0][...]
        refs = refs[1:]
    mod_ref, wa_ref, ws_ref, g_ref, b_ref = refs[:5]
    if with_router:
        wr_ref, x1_ref, h_ref, rt_ref = refs[5:]
    else:
        x1_ref, h_ref = refs[5:]
    a = _read_stream(a_ctx_ref, a_lat_ref, ctx_tiles)
    y = (jnp.dot(a, wa_ref[...], preferred_element_type=F32)
         + jnp.dot(s_ref[...], ws_ref[...], preferred_element_type=F32))
    t = alpha * x + mod_ref[2:3, :] * y
    x1 = _layer_norm_rows(t, g_ref[...], b_ref[...])
    x1_ref[...] = x1
    h = x1 * (1.0 + mod_ref[4:5, :]) + mod_ref[3:4, :]
    h_ref[...] = h.astype(h_ref.dtype)
    if with_router:
        tm = h.shape[0]
        wr = wr_ref[...]
        wr_hi = wr.astype(BF16)
        wr_lo = (wr - wr_hi.astype(F32)).astype(BF16)
        h_hi = h.astype(BF16)
        h_lo = (h - h_hi.astype(F32)).astype(BF16)
        logits = (jnp.dot(h_hi, wr_hi, preferred_element_type=F32)
                  + jnp.dot(h_lo, wr_hi, preferred_element_type=F32)
                  + jnp.dot(h_hi, wr_lo, preferred_element_type=F32))
        lane = lax.broadcasted_iota(jnp.int32, (tm, LANES), 1).astype(F32)
        neg = jnp.float32(-jnp.inf)
        lg = jnp.where(lane < N_EXPERTS, logits, neg)
        m1 = jnp.max(lg, axis=-1, keepdims=True)
        i1 = jnp.min(jnp.where(lg == m1, lane, float(LANES)), axis=-1, keepdims=True)
        lg2 = jnp.where(lane == i1, neg, lg)
        m2 = jnp.max(lg2, axis=-1, keepdims=True)
        i2 = jnp.min(jnp.where(lg2 == m2, lane, float(LANES)), axis=-1, keepdims=True)
        e2 = jnp.exp(m2 - m1)
        g1 = 1.0 / (1.0 + e2)
        g2 = e2 / (1.0 + e2)
        rt = jnp.where(lane == 0, i1, jnp.where(lane == 1, i2, jnp.where(lane == 2, g1, jnp.where(lane == 3, g2, 0.0))))
        rt_ref[...] = rt


def _out_call(a_ctx, a_lat, s, x, mods, w_o, ln_g, ln_b, w_router, alpha, geom):
    n = geom.n_tok
    tm = TOKEN_TILE
    ctx_tiles = geom.ctx_tiles
    row = lambda i: (i, 0)
    const = lambda i: (0, 0)
    with_router = w_router is not None
    split_x = isinstance(x, tuple)
    x_args = list(x) if split_x else [x]
    x_specs = _stream_specs(D_MODEL, ctx_tiles, tm) if split_x else [pl.BlockSpec((tm, D_MODEL), row)]
    in_specs = _stream_specs(ATTN_WIDTH, ctx_tiles, tm) + [pl.BlockSpec((tm, SGU_WIDTH), row)] + x_specs + [
        pl.BlockSpec((None, N_MODS, D_MODEL), lambda i: (geom.mod_index(i), 0, 0)),
        pl.BlockSpec((ATTN_WIDTH, D_MODEL), const),
        pl.BlockSpec((SGU_WIDTH, D_MODEL), lambda i: (1, 0)),
        pl.BlockSpec((1, D_MODEL), const),
        pl.BlockSpec((1, D_MODEL), const),
    ]
    args = [a_ctx, a_lat, s] + x_args + [mods, w_o, w_o, ln_g, ln_b]
    out_specs = [pl.BlockSpec((tm, D_MODEL), row), pl.BlockSpec((tm, D_MODEL), row)]
    out_shape = [jax.ShapeDtypeStruct((n, D_MODEL), F32),
                 jax.ShapeDtypeStruct((n, D_MODEL), F32 if with_router else BF16)]
    if with_router:
        in_specs.append(pl.BlockSpec((D_MODEL, LANES), const))
        args.append(w_router)
        out_specs.append(pl.BlockSpec((tm, LANES), row))
        out_shape.append(jax.ShapeDtypeStruct((n, LANES), F32))
    return pl.pallas_call(
        functools.partial(_out_kernel, alpha, with_router, split_x, ctx_tiles),
        grid=(n // tm,),
        in_specs=in_specs,
        out_specs=out_specs,
        out_shape=out_shape,
        compiler_params=_params(("arbitrary",)),
        name="out_proj_router" if with_router else "out_proj",
    )(*args)


def _swiglu_blocks(width):
    return [slice(s, min(s + V7X_MXU_WIDTH, width)) for s in range(0, width, V7X_MXU_WIDTH)]


def _swiglu_partial(x, wg_ref, wu_ref, wd_ref, side_work=None):
    out = None
    for n, cols in enumerate(_swiglu_blocks(wg_ref.shape[1])):
        a = jnp.dot(x, wg_ref[:, cols], preferred_element_type=F32)
        if side_work is not None:
            side_work(3 * n)
        b = jnp.dot(x, wu_ref[:, cols], preferred_element_type=F32)
        if side_work is not None:
            side_work(3 * n + 1)
        mid = ((a * _sigmoid(a)) * b).astype(BF16)
        part = jnp.dot(mid, wd_ref[cols, :], preferred_element_type=F32)
        if side_work is not None:
            side_work(3 * n + 2)
        out = part if out is None else out + part
    return out


def _ffn_kernel(alpha, h_ref, x1_ref, mod_ref, wg_ref, wu_ref, wd_ref, g_ref, b_ref, o_ref, acc_ref):
    c = pl.program_id(1)
    part = _swiglu_partial(h_ref[...], wg_ref, wu_ref, wd_ref)

    @pl.when(c == 0)
    def _():
        acc_ref[...] = part

    @pl.when(c > 0)
    def _():
        acc_ref[...] += part

    @pl.when(c == pl.num_programs(1) - 1)
    def _():
        t = alpha * x1_ref[...] + mod_ref[5:6, :] * acc_ref[...]
        o_ref[...] = _layer_norm_rows(t, g_ref[...], b_ref[...])


def _ffn_chunk(d_ff):
    assert d_ff % (2 * LANES) == 0
    return d_ff // 2


def _ffn_call(h, x1, mods, wg, wu, wd, ln_g, ln_b, alpha, geom):
    n = h.shape[0]
    tm = FFN_TILE
    d_ff = wg.shape[1]
    fc = _ffn_chunk(d_ff)
    ratio = tm // TOKEN_TILE
    row = lambda i, c: (i, 0)
    const = lambda i, c: (0, 0)
    return pl.pallas_call(
        functools.partial(_ffn_kernel, alpha),
        grid=(n // tm, d_ff // fc),
        in_specs=[
            pl.BlockSpec((tm, D_MODEL), row),
            pl.BlockSpec((tm, D_MODEL), row),
            pl.BlockSpec((None, N_MODS, D_MODEL), lambda i, c: (geom.mod_index(i * ratio), 0, 0)),
            pl.BlockSpec((D_MODEL, fc), lambda i, c: (0, c)),
            pl.BlockSpec((D_MODEL, fc), lambda i, c: (0, c)),
            pl.BlockSpec((fc, D_MODEL), lambda i, c: (c, 0)),
            pl.BlockSpec((1, D_MODEL), const),
            pl.BlockSpec((1, D_MODEL), const),
        ],
        out_specs=pl.BlockSpec((tm, D_MODEL), row),
        out_shape=jax.ShapeDtypeStruct((n, D_MODEL), F32),
        scratch_shapes=[pltpu.VMEM((tm, D_MODEL), F32)],
        compiler_params=_params(("arbitrary", "arbitrary")),
        name="dense_ffn",
    )(h, x1, mods, wg, wu, wd, ln_g, ln_b)


def _moe_kernel(n_chunks, te_ref, nu_ref, src_ref, dst_ref, h_hbm, wg_ref, wu_ref, wd_ref, o_hbm,
                xbuf, xb, obuf, gsem, ssem):
    del te_ref
    tr = xb.shape[0]
    t = pl.program_id(0)
    c = pl.program_id(1)
    n_tiles = pl.num_programs(0)
    valid = t < nu_ref[0]
    slot = t % 2
    other = 1 - slot
    share = tr // n_chunks
    first_row = c * share
    nxt_base = jnp.minimum(t + 1, n_tiles - 1) * tr
    prv_base = jnp.where(t == 0, n_tiles - 1, t - 1) * tr

    def gather_row(base, dst_slot, r, tok=None):
        tok = src_ref[base + r] if tok is None else tok
        return pltpu.make_async_copy(h_hbm.at[pl.ds(tok, 1), :],
                                     xbuf.at[dst_slot, pl.ds(r, 1), :], gsem.at[dst_slot])

    def scatter_row(base, src_slot, r, dst=None):
        dst = dst_ref[base + r] if dst is None else dst
        return pltpu.make_async_copy(obuf.at[src_slot, pl.ds(r, 1), :],
                                     o_hbm.at[pl.ds(dst, 1), :], ssem.at[src_slot])

    def wait_gather(dst_slot):
        pltpu.make_async_copy(h_hbm.at[pl.ds(0, tr), :], xbuf.at[dst_slot], gsem.at[dst_slot]).wait()

    def wait_scatter(src_slot):
        pltpu.make_async_copy(obuf.at[src_slot], o_hbm.at[pl.ds(0, tr), :], ssem.at[src_slot]).wait()

    def looped(n_rows, start_row):
        def body(r, carry):
            start_row(r)
            return carry
        lax.fori_loop(0, n_rows, body, 0, unroll=8)

    @pl.when((t == 0) & (c == 0))
    def _():
        obuf[1] = jnp.zeros(obuf.shape[1:], obuf.dtype)
        looped(tr, lambda r: gather_row(0, 0, r).start())

    @pl.when(c == 0)
    def _():
        wait_gather(slot)

        @pl.when(t >= 1)
        def _():
            wait_scatter(slot)

    def multiply_tile(slot_s, c_s):
        other_s = 1 - slot_s
        if c_s == 0:
            xb[...] = xbuf[slot_s].astype(BF16)
        n_groups = 3 * len(_swiglu_blocks(wg_ref.shape[1]))

        def row_dmas(i):
            for r in range(c_s * share + share * i // n_groups, c_s * share + share * (i + 1) // n_groups):
                gather_row(nxt_base, other_s, r).start()
                scatter_row(prv_base, other_s, r).start()

        part = _swiglu_partial(xb[...], wg_ref, wu_ref, wd_ref, row_dmas)
        if c_s == 0:
            obuf[slot_s] = part
        else:
            obuf[slot_s] += part

    for slot_s in range(2):
        for c_s in range(n_chunks):
            pl.when(valid & (slot == slot_s) & (c == c_s))(functools.partial(multiply_tile, slot_s, c_s))

    @pl.when(jnp.logical_not(valid))
    def _():
        @pl.when(c == 0)
        def _():
            obuf[slot] = jnp.zeros(obuf.shape[1:], obuf.dtype)

        looped(share, lambda r: gather_row(nxt_base, other, first_row + r).start())
        looped(share, lambda r: scatter_row(prv_base, other, first_row + r).start())

    @pl.when((t == n_tiles - 1) & (c == n_chunks - 1))
    def _():
        wait_scatter(other)
        looped(tr, lambda r: scatter_row(t * tr, slot, r).start())
        wait_scatter(slot)
        wait_gather(other)


def _moe_call(h, wg, wu, wd, plan):
    tr = FFN_TILE
    d_ff = wg.shape[2]
    fc = _ffn_chunk(d_ff)
    n_chunks = d_ff // fc
    t_max = plan["tile_expert"].shape[0]

    def chunk(t, c, nu):
        return jnp.where(t < nu[0], c, n_chunks - 1)

    grid_spec = pltpu.PrefetchScalarGridSpec(
        num_scalar_prefetch=4,
        grid=(t_max, n_chunks),
        in_specs=[
            pl.BlockSpec(memory_space=pl.ANY),
            pl.BlockSpec((None, D_MODEL, fc), lambda t, c, te, nu, src, dst: (te[t], 0, chunk(t, c, nu))),
            pl.BlockSpec((None, D_MODEL, fc), lambda t, c, te, nu, src, dst: (te[t], 0, chunk(t, c, nu))),
            pl.BlockSpec((None, fc, D_MODEL), lambda t, c, te, nu, src, dst: (te[t], chunk(t, c, nu), 0)),
        ],
        out_specs=pl.BlockSpec(memory_space=pl.ANY),
        scratch_shapes=[
            pltpu.VMEM((2, tr, D_MODEL), F32),
            pltpu.VMEM((tr, D_MODEL), BF16),
            pltpu.VMEM((2, tr, D_MODEL), F32),
            pltpu.SemaphoreType.DMA((2,)),
            pltpu.SemaphoreType.DMA((2,)),
        ],
    )
    return pl.pallas_call(
        functools.partial(_moe_kernel, n_chunks),
        grid_spec=grid_spec,
        out_shape=jax.ShapeDtypeStruct((t_max * tr, D_MODEL), F32),
        compiler_params=_params(("arbitrary", "arbitrary")),
        name="expert_ffn",
    )(plan["tile_expert"], plan["n_used"], plan["src_token"], plan["dst_row"], h, wg, wu, wd)


def _route_plan(routing, n_tok, tr):
    n_assign = TOP_K * n_tok
    t_max = n_assign // tr + N_EXPERTS
    n_slots = t_max * tr
    n_pad = n_slots - n_assign
    id_bits = (n_slots - 1).bit_length()
    experts = jnp.arange(N_EXPERTS, dtype=jnp.int32)
    e_flat = routing[:, :TOP_K].astype(jnp.int32).T.reshape(-1)
    counts = jnp.sum((e_flat[:, None] == experts[None, :]).astype(jnp.int32), axis=0)
    tiles_e = (counts + tr - 1) // tr
    tile_end = jnp.cumsum(tiles_e)
    n_used = tile_end[-1]
    pad_end = jnp.cumsum(tiles_e * tr - counts)
    pad_ids = jnp.arange(n_pad, dtype=jnp.int32)
    pad_expert = jnp.sum((pad_ids[:, None] >= pad_end[None, :]).astype(jnp.int32), axis=1)
    keys = jnp.concatenate([e_flat * 2, pad_expert * 2 + 1])
    item = jnp.arange(n_slots, dtype=jnp.int32)
    slot_item = jnp.sort((keys << id_bits) | item) & ((1 << id_bits) - 1)
    real = slot_item < n_assign
    src = jnp.where(real, slot_item % n_tok, 0).astype(jnp.int32)
    dst = slot_item.astype(jnp.int32)
    t_ids = jnp.arange(t_max, dtype=jnp.int32)
    te = jnp.sum((t_ids[:, None] >= tile_end[None, :]).astype(jnp.int32), axis=1)
    last_used = jnp.max(jnp.where(tiles_e > 0, experts, 0))
    te = jnp.where(t_ids < n_used, te, last_used).astype(jnp.int32)
    return {"tile_expert": te, "n_used": n_used.reshape(1).astype(jnp.int32), "src_token": src, "dst_row": dst}


def _combine_kernel(alpha, split_ctx_tiles, e1_ref, e2_ref, rt_ref, x1_ref, mod_ref, g_ref, b_ref, *o_refs):
    y = rt_ref[:, 2:3] * e1_ref[...] + rt_ref[:, 3:4] * e2_ref[...]
    t = alpha * x1_ref[...] + mod_ref[5:6, :] * y
    out = _layer_norm_rows(t, g_ref[...], b_ref[...])
    if split_ctx_tiles is None:
        o_refs[0][...] = out
    else:
        _write_stream(o_refs[0], o_refs[1], split_ctx_tiles, out)


def _combine_call(eo, routing, x1, mods, ln_g, ln_b, alpha, geom, split_out):
    n = x1.shape[0]
    tm = TOKEN_TILE
    row = lambda i: (i, 0)
    const = lambda i: (0, 0)
    if split_out:
        out_specs = _stream_specs(D_MODEL, geom.ctx_tiles, tm)
        out_shape = [jax.ShapeDtypeStruct((geom.n_ctx, D_MODEL), F32),
                     jax.ShapeDtypeStruct((n - geom.n_ctx, D_MODEL), F32)]
    else:
        out_specs = pl.BlockSpec((tm, D_MODEL), row)
        out_shape = jax.ShapeDtypeStruct((n, D_MODEL), F32)
    return pl.pallas_call(
        functools.partial(_combine_kernel, alpha, geom.ctx_tiles if split_out else None),
        grid=(n // tm,),
        in_specs=[
            pl.BlockSpec((tm, D_MODEL), row),
            pl.BlockSpec((tm, D_MODEL), lambda i: (i + n // tm, 0)),
            pl.BlockSpec((tm, LANES), row),
            pl.BlockSpec((tm, D_MODEL), row),
            pl.BlockSpec((None, N_MODS, D_MODEL), lambda i: (geom.mod_index(i), 0, 0)),
            pl.BlockSpec((1, D_MODEL), const),
            pl.BlockSpec((1, D_MODEL), const),
        ],
        out_specs=out_specs,
        out_shape=out_shape,
        compiler_params=_params(("arbitrary",)),
        name="expert_combine",
    )(eo, eo, routing, x1, mods, ln_g, ln_b)


class _Geometry:
    def __init__(self, n_ctx, dec_batch, dec_seq):
        self.n_ctx = n_ctx
        self.dec_batch = dec_batch
        self.dec_seq = dec_seq
        self.n_tok = n_ctx + dec_batch * dec_seq
        assert n_ctx % FFN_TILE == 0 and dec_seq % FFN_TILE == 0
        self.ctx_tiles = n_ctx // TOKEN_TILE
        self.lat_tiles = dec_seq // TOKEN_TILE

    def mod_index(self, i):
        return jnp.where(i < self.ctx_tiles, 0, 1 + (i - self.ctx_tiles) // self.lat_tiles)

    def rope_index(self, i):
        return jnp.where(i < self.ctx_tiles, 0, 1 + (i - self.ctx_tiles) % self.lat_tiles)


def _rope_tables(dec_seq):
    pos = jnp.arange(dec_seq, dtype=jnp.int32)
    row = (pos // GRID_W).astype(F32)
    col = (pos % GRID_W).astype(F32)
    inv = ROPE_BASE ** (-jnp.arange(ROPE_PAIRS, dtype=F32) / ROPE_PAIRS)
    ang_r = row[:, None] * inv[None, :]
    ang_c = col[:, None] * inv[None, :]
    cos_h = jnp.concatenate([jnp.cos(ang_r), jnp.cos(ang_r), jnp.cos(ang_c), jnp.cos(ang_c)], axis=-1)
    sin_h = jnp.concatenate([-jnp.sin(ang_r), jnp.sin(ang_r), -jnp.sin(ang_c), jnp.sin(ang_c)], axis=-1)
    reps = LANES // HEAD_DIM
    cos_t = jnp.concatenate([jnp.ones((TOKEN_TILE, LANES), F32), jnp.tile(cos_h, (1, reps))], axis=0)
    sin_t = jnp.concatenate([jnp.zeros((TOKEN_TILE, LANES), F32), jnp.tile(sin_h, (1, reps))], axis=0)
    return cos_t, sin_t


def kernel(x_prompt, x_sample, cache_k, cache_v, c, c_ctx, w_ada, b_ada, w_in, w_o, attn_sink, w_s, b_s, sgu_ln_g, sgu_ln_b, ln1_g, ln1_b, ln2_g, ln2_b, w_ff_gate, w_ff_up, w_ff_down, w_router, w_exp_gate, w_exp_up, w_exp_down):
    batch, seq, d = x_prompt.shape
    dec_batch, dec_seq, _ = x_sample.shape
    depth = w_in.shape[0]
    past = cache_k.shape[2]
    assert d == D_MODEL and dec_batch + 1 <= MOD_ROWS
    n_ctx = batch * seq
    n_lat = dec_batch * dec_seq
    n_tok = n_ctx + n_lat
    geom = _Geometry(n_ctx, dec_batch, dec_seq)
    alpha = float((2 * depth) ** 0.25)

    x = (x_prompt.reshape(n_ctx, d), x_sample.reshape(n_lat, d))
    cvec = jnp.concatenate([c_ctx[None, :], c, jnp.zeros((MOD_ROWS - 1 - dec_batch, d), F32)], axis=0)
    mods_all = _ada_call(cvec, w_ada, b_ada).reshape(depth, MOD_ROWS, N_MODS, d)
    cos_t, sin_t = _rope_tables(dec_seq)

    new_k, new_v = [], []
    for l in range(depth):
        mods = mods_all[l]
        q, k, v, kv32, u, g = _in_call(x, mods, w_in[l].astype(BF16), cos_t, sin_t,
                                       sgu_ln_g[l].reshape(1, SGU_WIDTH), sgu_ln_b[l].reshape(1, SGU_WIDTH), geom)
        new_k.append(kv32[:n_ctx, :KV_WIDTH].reshape(batch, seq, N_KV_HEADS, HEAD_DIM))
        new_v.append(kv32[:n_ctx, KV_WIDTH:].reshape(batch, seq, N_KV_HEADS, HEAD_DIM))
        sink = attn_sink[l]
        a_ctx = _ctx_attn_call(sink, q, k, v, batch, seq)
        a_lat = _lat_attn_call(sink, q, k, v, cache_k[:, l].reshape(dec_batch, past, KV_WIDTH).astype(BF16),
                               cache_v[:, l].reshape(dec_batch, past, KV_WIDTH).astype(BF16), geom)
        bias_full = jnp.repeat(b_s[l].T, SGU_GROUP_DIM, axis=1)
        s = _sgu_call(u, g, w_s[l].astype(BF16), bias_full)
        i = l // 2
        moe = l % 2 == 1
        w_r = None
        if moe:
            w_r = jnp.pad(w_router[i], ((0, 0), (0, LANES - N_EXPERTS)))
        outs = _out_call(a_ctx, a_lat, s, x, mods, w_o[l].astype(BF16), ln1_g[l].reshape(1, d), ln1_b[l].reshape(1, d),
                         w_r, alpha, geom)
        ln_g, ln_b = ln2_g[l].reshape(1, d), ln2_b[l].reshape(1, d)
        if moe:
            x1, h, routing = outs
            plan = _route_plan(routing, n_tok, FFN_TILE)
            eo = _moe_call(h, w_exp_gate[i].astype(BF16), w_exp_up[i].astype(BF16), w_exp_down[i].astype(BF16), plan)
            x = _combine_call(eo, routing, x1, mods, ln_g, ln_b, alpha, geom, split_out=l == depth - 1)
        else:
            x1, h = outs
            x = _ffn_call(h, x1, mods, w_ff_gate[i].astype(BF16), w_ff_up[i].astype(BF16),
                          w_ff_down[i].astype(BF16), ln_g, ln_b, alpha, geom)

    if not isinstance(x, (tuple, list)):
        x = (x[:n_ctx], x[n_ctx:])
    y_prompt = x[0].reshape(batch, seq, d)
    y_sample = x[1].reshape(dec_batch, dec_seq, d)
    return (y_prompt, y_sample, jnp.stack(new_k, axis=1), jnp.stack(new_v, axis=1))
```

```python
import functools

import jax
import jax.numpy as jnp
import numpy as np
from jax import lax
from jax.experimental import pallas as pl
from jax.experimental.pallas import tpu as pltpu

F32 = jnp.float32
BF16 = jnp.bfloat16

D_MODEL = 1024
HEAD_DIM = 64
N_Q_HEADS = 8
N_KV_HEADS = 2
Q_PER_KV = N_Q_HEADS // N_KV_HEADS
ATTN_WIDTH = N_Q_HEADS * HEAD_DIM
KV_WIDTH = N_KV_HEADS * HEAD_DIM
SGU_WIDTH = D_MODEL - ATTN_WIDTH
N_SGU_GROUPS = 8
SGU_GROUP_DIM = SGU_WIDTH // N_SGU_GROUPS
CHUNK = 128
BLOCK = 128
WINDOW = 128
GRID_W = 64
IN_WIDTH = ATTN_WIDTH + 2 * KV_WIDTH + 2 * SGU_WIDTH
OFF_K = ATTN_WIDTH
OFF_V = OFF_K + KV_WIDTH
OFF_U = OFF_V + KV_WIDTH
OFF_G = OFF_U + SGU_WIDTH
N_EXPERTS = 8
TOP_K = 2
ROPE_BASE = 10000.0
ROPE_PAIRS = HEAD_DIM // 4
LN_EPS = 1e-5
ATTN_SCALE = HEAD_DIM ** -0.5
NEG_INF = -1e30
N_MODS = 6

LANES = 128
V7X_MXU_WIDTH = 256
V7X_VMEM_LIMIT_BYTES = 48 * 1024 * 1024

TOKEN_TILE = 256
FFN_TILE = 512
LAT_QUERY_ROWS = 256
SCATTER_DMA_PRIORITY = 1
MOD_ROWS = 8
ADA_COLS = 1536


def _params(sem):
    return pltpu.CompilerParams(dimension_semantics=sem, vmem_limit_bytes=V7X_VMEM_LIMIT_BYTES)


def _gelu_tanh(x):
    return x * (0.5 * (1.0 + jnp.tanh(np.float32(np.sqrt(2.0 / np.pi)) * (x + 0.044715 * (x * x * x)))))


def _sigmoid(x):
    return 1.0 / (1.0 + jnp.exp(-x))


def _layer_norm_rows(t, g, b):
    mu = jnp.mean(t, axis=-1, keepdims=True)
    tc = t - mu
    var = jnp.mean(tc * tc, axis=-1, keepdims=True)
    return tc * lax.rsqrt(var + LN_EPS) * g + b


def _stream_specs(width, ctx_tiles, tm):
    return [pl.BlockSpec((tm, width), lambda i, *_: (jnp.minimum(i, ctx_tiles - 1), 0)),
            pl.BlockSpec((tm, width), lambda i, *_: (jnp.maximum(i - ctx_tiles, 0), 0))]


def _read_stream(ctx_ref, lat_ref, ctx_tiles):
    tile = lax.broadcasted_iota(jnp.int32, ctx_ref.shape, 0) * 0 + pl.program_id(0)
    return jnp.where(tile < ctx_tiles, ctx_ref[...], lat_ref[...])


def _write_stream(ctx_ref, lat_ref, ctx_tiles, value):
    @pl.when(pl.program_id(0) < ctx_tiles)
    def _():
        ctx_ref[...] = value

    @pl.when(pl.program_id(0) >= ctx_tiles)
    def _():
        lat_ref[...] = value


def _ada_kernel(c_ref, w_ref, b_ref, o_ref):
    c = c_ref[...]
    s = (c * _sigmoid(c)).astype(BF16)
    o_ref[...] = jnp.dot(s, w_ref[...].astype(BF16), preferred_element_type=F32) + b_ref[...]


def _ada_call(cvec, w_ada, b_ada):
    depth = w_ada.shape[0]
    n_out = w_ada.shape[2]
    return pl.pallas_call(
        _ada_kernel,
        grid=(depth, n_out // ADA_COLS),
        in_specs=[
            pl.BlockSpec((MOD_ROWS, D_MODEL), lambda l, j: (0, 0)),
            pl.BlockSpec((None, D_MODEL, ADA_COLS), lambda l, j: (l, 0, j)),
            pl.BlockSpec((None, 1, ADA_COLS), lambda l, j: (l, 0, j)),
        ],
        out_specs=pl.BlockSpec((None, MOD_ROWS, ADA_COLS), lambda l, j: (l, 0, j)),
        out_shape=jax.ShapeDtypeStruct((depth, MOD_ROWS, n_out), F32),
        compiler_params=_params(("arbitrary", "arbitrary")),
        name="adaln",
    )(cvec, w_ada, b_ada.reshape(depth, 1, n_out))


def _in_kernel(split_ctx_tiles, *refs):
    if split_ctx_tiles is None:
        x = refs[0][...]
        refs = refs[1:]
    else:
        x = _read_stream(refs[0], refs[1], split_ctx_tiles)
        refs = refs[2:]
    mod_ref, w_ref, cos_ref, sin_ref, lng_ref, lnb_ref, q_ref, k_ref, v_ref, kv_ref, u_ref, g_ref = refs
    tm = x.shape[0]
    h = x * (1.0 + mod_ref[1:2, :]) + mod_ref[0:1, :]
    hb = h.astype(BF16)

    def z_block(start):
        return jnp.dot(hb, w_ref[:, start:start + V7X_MXU_WIDTH], preferred_element_type=F32)

    cos = cos_ref[...]
    sin = sin_ref[...]
    lane = lax.broadcasted_iota(jnp.int32, (tm, LANES), 1)
    first_of_pair = (lane & (2 * ROPE_PAIRS - 1)) < ROPE_PAIRS
    lower_group = lane < SGU_GROUP_DIM

    def rope(t):
        partner = jnp.where(first_of_pair, pltpu.roll(t, LANES - ROPE_PAIRS, 1), pltpu.roll(t, ROPE_PAIRS, 1))
        return t * cos + partner * sin

    halves = (slice(0, LANES), slice(LANES, 2 * LANES))
    inv_n = 1.0 / SGU_GROUP_DIM
    for j in range(SGU_WIDTH // LANES):
        cols = slice(j * LANES, (j + 1) * LANES)
        if j % 2 == 0:
            z = z_block(OFF_G + j * LANES)
        t = _gelu_tanh(z[:, halves[j % 2]])
        s_lo = jnp.sum(jnp.where(lower_group, t, 0.0), axis=-1, keepdims=True)
        s_hi = jnp.sum(jnp.where(lower_group, 0.0, t), axis=-1, keepdims=True)
        tc = t - jnp.where(lower_group, s_lo, s_hi) * inv_n
        sq = tc * tc
        v_lo = jnp.sum(jnp.where(lower_group, sq, 0.0), axis=-1, keepdims=True)
        v_hi = jnp.sum(jnp.where(lower_group, 0.0, sq), axis=-1, keepdims=True)
        var = jnp.where(lower_group, v_lo, v_hi) * inv_n
        g_ref[:, cols] = (tc * lax.rsqrt(var + LN_EPS) * lng_ref[:, cols] + lnb_ref[:, cols]).astype(BF16)

    for start in range(0, ATTN_WIDTH, V7X_MXU_WIDTH):
        z = z_block(start)
        for half in halves:
            q_ref[:, start + half.start:start + half.stop] = (rope(z[:, half]) * ATTN_SCALE).astype(BF16)
    z = z_block(OFF_K)
    k_ref[...] = rope(z[:, halves[0]]).astype(BF16)
    v_ref[...] = z[:, halves[1]].astype(BF16)
    kv_ref[...] = z
    for start in range(0, SGU_WIDTH, V7X_MXU_WIDTH):
        u_ref[:, start:start + V7X_MXU_WIDTH] = _gelu_tanh(z_block(OFF_U + start))


def _in_call(x, mods, w_in, cos_t, sin_t, ln_g, ln_b, geom):
    n = geom.n_tok
    tm = TOKEN_TILE
    mod_idx, rope_idx = geom.mod_index, geom.rope_index
    row = lambda i: (i, 0)
    split = isinstance(x, tuple)
    x_args = list(x) if split else [x]
    x_specs = _stream_specs(D_MODEL, geom.ctx_tiles, tm) if split else [pl.BlockSpec((tm, D_MODEL), row)]
    outs = pl.pallas_call(
        functools.partial(_in_kernel, geom.ctx_tiles if split else None),
        grid=(n // tm,),
        in_specs=x_specs + [
            pl.BlockSpec((None, N_MODS, D_MODEL), lambda i: (mod_idx(i), 0, 0)),
            pl.BlockSpec((D_MODEL, IN_WIDTH), lambda i: (0, 0)),
            pl.BlockSpec((tm, LANES), lambda i: (rope_idx(i), 0)),
            pl.BlockSpec((tm, LANES), lambda i: (rope_idx(i), 0)),
            pl.BlockSpec((1, SGU_WIDTH), lambda i: (0, 0)),
            pl.BlockSpec((1, SGU_WIDTH), lambda i: (0, 0)),
        ],
        out_specs=[
            pl.BlockSpec((tm, ATTN_WIDTH), row),
            pl.BlockSpec((tm, KV_WIDTH), row),
            pl.BlockSpec((tm, KV_WIDTH), row),
            pl.BlockSpec((tm, 2 * KV_WIDTH), row),
            pl.BlockSpec((tm, SGU_WIDTH), row),
            pl.BlockSpec((tm, SGU_WIDTH), row),
        ],
        out_shape=[
            jax.ShapeDtypeStruct((n, ATTN_WIDTH), BF16),
            jax.ShapeDtypeStruct((n, KV_WIDTH), BF16),
            jax.ShapeDtypeStruct((n, KV_WIDTH), BF16),
            jax.ShapeDtypeStruct((n, 2 * KV_WIDTH), F32),
            jax.ShapeDtypeStruct((n, SGU_WIDTH), F32),
            jax.ShapeDtypeStruct((n, SGU_WIDTH), BF16),
        ],
        compiler_params=_params(("arbitrary",)),
        name="in_proj",
    )(*x_args, mods, w_in, cos_t, sin_t, ln_g, ln_b)
    return outs


def _group_attention(q_ref, rows, hk, k_all, v_all, block_masks, sink_ref):
    m_rows = rows.stop - rows.start
    heads = [hk * Q_PER_KV + gq for gq in range(Q_PER_KV)]
    q = jnp.concatenate([q_ref[rows, h * HEAD_DIM:(h + 1) * HEAD_DIM] for h in heads], axis=0)
    s = lax.dot_general(q, k_all, (((1,), (1,)), ((), ())), preferred_element_type=F32)
    n_blocks = k_all.shape[0] // LANES
    blocks = [s[:, b * LANES:(b + 1) * LANES] for b in range(n_blocks)]
    for b, mask in block_masks.items():
        blocks[b] = jnp.where(mask, blocks[b], NEG_INF)
    head_of_row = lax.broadcasted_iota(jnp.int32, (Q_PER_KV * m_rows, 1), 0) // m_rows
    sink = jnp.zeros((Q_PER_KV * m_rows, 1), F32)
    for gq, h in enumerate(heads):
        sink = jnp.where(head_of_row == gq, sink_ref[h], sink)
    m_el = blocks[0]
    for blk in blocks[1:]:
        m_el = jnp.maximum(m_el, blk)
    m = jnp.maximum(jnp.max(m_el, axis=-1, keepdims=True), sink)
    probs = [jnp.exp(blk - m) for blk in blocks]
    l_el = probs[0]
    for p in probs[1:]:
        l_el = l_el + p
    denom = jnp.sum(l_el, axis=-1, keepdims=True) + jnp.exp(sink - m)
    p_all = jnp.concatenate([p.astype(BF16) for p in probs], axis=1)
    o = jnp.dot(p_all, v_all, preferred_element_type=F32) / denom
    return {h: o[gq * m_rows:(gq + 1) * m_rows] for gq, h in enumerate(heads)}


def _store_heads(o_ref, rows, outs):
    for h0 in range(0, N_Q_HEADS, 2):
        pair = jnp.concatenate([outs[h0], outs[h0 + 1]], axis=1)
        o_ref[rows, h0 * HEAD_DIM:(h0 + 2) * HEAD_DIM] = pair.astype(o_ref.dtype)


def _ctx_attn_kernel(sink_ref, q_ref, k_ref, v_ref, o_ref):
    rows = slice(0, q_ref.shape[0])
    outs = {}
    for hk in range(N_KV_HEADS):
        kv_cols = slice(hk * HEAD_DIM, (hk + 1) * HEAD_DIM)
        outs.update(_group_attention(q_ref, rows, hk, k_ref[:, kv_cols], v_ref[:, kv_cols], {}, sink_ref))
    _store_heads(o_ref, rows, outs)


def _ctx_attn_call(sink, q, k, v, n_seq, seq_len):
    n = n_seq * seq_len
    blk = lambda b: (b, 0)
    return pl.pallas_call(
        _ctx_attn_kernel,
        grid=(n_seq,),
        in_specs=[
            pl.BlockSpec(memory_space=pltpu.SMEM),
            pl.BlockSpec((seq_len, ATTN_WIDTH), blk),
            pl.BlockSpec((seq_len, KV_WIDTH), blk),
            pl.BlockSpec((seq_len, KV_WIDTH), blk),
        ],
        out_specs=pl.BlockSpec((seq_len, ATTN_WIDTH), blk),
        out_shape=jax.ShapeDtypeStruct((n, ATTN_WIDTH), BF16),
        compiler_params=_params(("arbitrary",)),
        name="ctx_attention",
    )(sink, q, k, v)


def _lat_attn_kernel(sink_ref, q_ref, k_ref, v_ref, ck_ref, cv_ref, o_ref):
    blocks_per_step = q_ref.shape[0] // BLOCK
    nb = k_ref.shape[0] // BLOCK
    past_blocks = ck_ref.shape[0] // LANES
    r = lax.broadcasted_iota(jnp.int32, (Q_PER_KV * BLOCK, BLOCK), 0) & (BLOCK - 1)
    c = lax.broadcasted_iota(jnp.int32, (Q_PER_KV * BLOCK, BLOCK), 1)
    for sub in range(blocks_per_step):
        j = pl.program_id(1) * blocks_per_step + sub
        rows = slice(sub * BLOCK, (sub + 1) * BLOCK)
        mask_prev = c >= r + jnp.where(j > 0, 0, BLOCK)
        mask_next = c <= r - jnp.where(j < nb - 1, 0, BLOCK)
        prev = pl.ds(pl.multiple_of(jnp.maximum(j - 1, 0) * BLOCK, BLOCK), BLOCK)
        cur = pl.ds(pl.multiple_of(j * BLOCK, BLOCK), BLOCK)
        nxt = pl.ds(pl.multiple_of(jnp.minimum(j + 1, nb - 1) * BLOCK, BLOCK), BLOCK)
        outs = {}
        for hk in range(N_KV_HEADS):
            kv_cols = slice(hk * HEAD_DIM, (hk + 1) * HEAD_DIM)
            k_all = jnp.concatenate([ck_ref[:, kv_cols], k_ref[prev, kv_cols], k_ref[cur, kv_cols],
                                     k_ref[nxt, kv_cols]], axis=0)
            v_all = jnp.concatenate([cv_ref[:, kv_cols], v_ref[prev, kv_cols], v_ref[cur, kv_cols],
                                     v_ref[nxt, kv_cols]], axis=0)
            masks = {past_blocks: mask_prev, past_blocks + 2: mask_next}
            outs.update(_group_attention(q_ref, rows, hk, k_all, v_all, masks, sink_ref))
        _store_heads(o_ref, rows, outs)


def _lat_attn_call(sink, q, k, v, cache_k, cache_v, geom):
    qb = LAT_QUERY_ROWS
    steps = geom.dec_seq // qb
    assert geom.n_ctx % geom.dec_seq == 0 and cache_k.shape[1] % LANES == 0
    seq_base = geom.n_ctx // geom.dec_seq
    past = cache_k.shape[1]
    seq_spec = pl.BlockSpec((geom.dec_seq, KV_WIDTH), lambda b, j: (seq_base + b, 0))
    cache_spec = pl.BlockSpec((None, past, KV_WIDTH), lambda b, j: (b, 0, 0))
    return pl.pallas_call(
        _lat_attn_kernel,
        grid=(geom.dec_batch, steps),
        in_specs=[
            pl.BlockSpec(memory_space=pltpu.SMEM),
            pl.BlockSpec((qb, ATTN_WIDTH), lambda b, j: (geom.n_ctx // qb + b * steps + j, 0)),
            seq_spec, seq_spec,
            cache_spec, cache_spec,
        ],
        out_specs=pl.BlockSpec((qb, ATTN_WIDTH), lambda b, j: (b * steps + j, 0)),
        out_shape=jax.ShapeDtypeStruct((geom.dec_batch * geom.dec_seq, ATTN_WIDTH), BF16),
        compiler_params=_params(("arbitrary", "arbitrary")),
        name="lat_attention",
    )(sink, q, k, v, cache_k, cache_v)


def _spatial_gating(u_ref, g_ref, mix_ref, bias_ref, s_ref):
    tm = u_ref.shape[0]
    lane = lax.broadcasted_iota(jnp.int32, (CHUNK, LANES), 1)
    lower_group = lane < SGU_GROUP_DIM
    for ch in range(tm // CHUNK):
        rows = slice(ch * CHUNK, (ch + 1) * CHUNK)
        for p in range(SGU_WIDTH // LANES):
            cols = slice(p * LANES, (p + 1) * LANES)
            g = g_ref[rows, cols]
            zero = jnp.zeros_like(g)
            mixed = (jnp.dot(mix_ref[2 * p], jnp.where(lower_group, g, zero), preferred_element_type=F32)
                     + jnp.dot(mix_ref[2 * p + 1], jnp.where(lower_group, zero, g), preferred_element_type=F32))
            s_ref[rows, cols] = (u_ref[rows, cols] * (mixed + bias_ref[:, cols])).astype(s_ref.dtype)


def _out_kernel(alpha, with_router, split_x, ctx_tiles, a_ctx_ref, a_lat_ref, u_ref, gg_ref, mix_ref, bias_ref,
                *refs):
    s_ref = refs[-1]
    refs = refs[:-1]
    if split_x:
        x = _read_stream(refs[0], refs[1], ctx_tiles)
        refs = refs[2:]
    else:
        x = refs[0][...]
        refs = refs[1:]
    mod_ref, wa_ref, ws_ref, g_ref, b_ref = refs[:5]
    if with_router:
        wr_ref, x1_ref, h_ref, rt_ref = refs[5:]
    else:
        x1_ref, h_ref = refs[5:]
    a = _read_stream(a_ctx_ref, a_lat_ref, ctx_tiles)
    _spatial_gating(u_ref, gg_ref, mix_ref, bias_ref, s_ref)
    y = (jnp.dot(a, wa_ref[...], preferred_element_type=F32)
         + jnp.dot(s_ref[...], ws_ref[...], preferred_element_type=F32))
    t = alpha * x + mod_ref[2:3, :] * y
    x1 = _layer_norm_rows(t, g_ref[...], b_ref[...])
    x1_ref[...] = x1
    h = x1 * (1.0 + mod_ref[4:5, :]) + mod_ref[3:4, :]
    h_ref[...] = h.astype(h_ref.dtype)
    if with_router:
        tm = h.shape[0]
        wr = wr_ref[...]
        wr_hi = wr.astype(BF16)
        wr_lo = (wr - wr_hi.astype(F32)).astype(BF16)
        h_hi = h.astype(BF16)
        h_lo = (h - h_hi.astype(F32)).astype(BF16)
        logits = (jnp.dot(h_hi, wr_hi, preferred_element_type=F32)
                  + jnp.dot(h_lo, wr_hi, preferred_element_type=F32)
                  + jnp.dot(h_hi, wr_lo, preferred_element_type=F32))
        lane = lax.broadcasted_iota(jnp.int32, (tm, LANES), 1).astype(F32)
        neg = jnp.float32(-jnp.inf)
        lg = jnp.where(lane < N_EXPERTS, logits, neg)
        m1 = jnp.max(lg, axis=-1, keepdims=True)
        i1 = jnp.min(jnp.where(lg == m1, lane, float(LANES)), axis=-1, keepdims=True)
        lg2 = jnp.where(lane == i1, neg, lg)
        m2 = jnp.max(lg2, axis=-1, keepdims=True)
        i2 = jnp.min(jnp.where(lg2 == m2, lane, float(LANES)), axis=-1, keepdims=True)
        e2 = jnp.exp(m2 - m1)
        g1 = 1.0 / (1.0 + e2)
        g2 = e2 / (1.0 + e2)
        rt = jnp.where(lane == 0, i1, jnp.where(lane == 1, i2, jnp.where(lane == 2, g1, jnp.where(lane == 3, g2, 0.0))))
        rt_ref[...] = rt


def _out_call(a_ctx, a_lat, u, gg, w_s, bias_full, x, mods, w_o, ln_g, ln_b, w_router, alpha, geom):
    n = geom.n_tok
    tm = TOKEN_TILE
    ctx_tiles = geom.ctx_tiles
    row = lambda i: (i, 0)
    const = lambda i: (0, 0)
    with_router = w_router is not None
    split_x = isinstance(x, tuple)
    x_args = list(x) if split_x else [x]
    x_specs = _stream_specs(D_MODEL, ctx_tiles, tm) if split_x else [pl.BlockSpec((tm, D_MODEL), row)]
    sgu_specs = [
        pl.BlockSpec((tm, SGU_WIDTH), row),
        pl.BlockSpec((tm, SGU_WIDTH), row),
        pl.BlockSpec((N_SGU_GROUPS, CHUNK, CHUNK), lambda i: (0, 0, 0)),
        pl.BlockSpec((CHUNK, SGU_WIDTH), const),
    ]
    in_specs = _stream_specs(ATTN_WIDTH, ctx_tiles, tm) + sgu_specs + x_specs + [
        pl.BlockSpec((None, N_MODS, D_MODEL), lambda i: (geom.mod_index(i), 0, 0)),
        pl.BlockSpec((ATTN_WIDTH, D_MODEL), const),
        pl.BlockSpec((SGU_WIDTH, D_MODEL), lambda i: (1, 0)),
        pl.BlockSpec((1, D_MODEL), const),
        pl.BlockSpec((1, D_MODEL), const),
    ]
    args = [a_ctx, a_lat, u, gg, w_s, bias_full] + x_args + [mods, w_o, w_o, ln_g, ln_b]
    out_specs = [pl.BlockSpec((tm, D_MODEL), row), pl.BlockSpec((tm, D_MODEL), row)]
    out_shape = [jax.ShapeDtypeStruct((n, D_MODEL), F32),
                 jax.ShapeDtypeStruct((n, D_MODEL), F32 if with_router else BF16)]
    if with_router:
        in_specs.append(pl.BlockSpec((D_MODEL, LANES), const))
        args.append(w_router)
        out_specs.append(pl.BlockSpec((tm, LANES), row))
        out_shape.append(jax.ShapeDtypeStruct((n, LANES), F32))
    return pl.pallas_call(
        functools.partial(_out_kernel, alpha, with_router, split_x, ctx_tiles),
        grid=(n // tm,),
        in_specs=in_specs,
        out_specs=out_specs,
        out_shape=out_shape,
        scratch_shapes=[pltpu.VMEM((tm, SGU_WIDTH), BF16)],
        compiler_params=_params(("arbitrary",)),
        name="out_proj_router" if with_router else "out_proj",
    )(*args)


def _swiglu_blocks(width):
    return [slice(s, min(s + V7X_MXU_WIDTH, width)) for s in range(0, width, V7X_MXU_WIDTH)]


def _swiglu_partial(x, wg_ref, wu_ref, wd_ref, side_work=None):
    out = None
    for n, cols in enumerate(_swiglu_blocks(wg_ref.shape[1])):
        a = jnp.dot(x, wg_ref[:, cols], preferred_element_type=F32)
        if side_work is not None:
            side_work(3 * n)
        b = jnp.dot(x, wu_ref[:, cols], preferred_element_type=F32)
        if side_work is not None:
            side_work(3 * n + 1)
        mid = ((a * _sigmoid(a)) * b).astype(BF16)
        part = jnp.dot(mid, wd_ref[cols, :], preferred_element_type=F32)
        if side_work is not None:
            side_work(3 * n + 2)
        out = part if out is None else out + part
    return out


def _ffn_kernel(alpha, h_ref, x1_ref, mod_ref, wg_ref, wu_ref, wd_ref, g_ref, b_ref, o_ref, acc_ref):
    c = pl.program_id(1)
    part = _swiglu_partial(h_ref[...], wg_ref, wu_ref, wd_ref)

    @pl.when(c == 0)
    def _():
        acc_ref[...] = part

    @pl.when(c > 0)
    def _():
        acc_ref[...] += part

    @pl.when(c == pl.num_programs(1) - 1)
    def _():
        t = alpha * x1_ref[...] + mod_ref[5:6, :] * acc_ref[...]
        o_ref[...] = _layer_norm_rows(t, g_ref[...], b_ref[...])


def _ffn_chunk(d_ff):
    assert d_ff % (2 * LANES) == 0
    return d_ff // 2


def _ffn_call(h, x1, mods, wg, wu, wd, ln_g, ln_b, alpha, geom):
    n = h.shape[0]
    tm = FFN_TILE
    d_ff = wg.shape[1]
    fc = _ffn_chunk(d_ff)
    ratio = tm // TOKEN_TILE
    row = lambda i, c: (i, 0)
    const = lambda i, c: (0, 0)
    return pl.pallas_call(
        functools.partial(_ffn_kernel, alpha),
        grid=(n // tm, d_ff // fc),
        in_specs=[
            pl.BlockSpec((tm, D_MODEL), row),
            pl.BlockSpec((tm, D_MODEL), row),
            pl.BlockSpec((None, N_MODS, D_MODEL), lambda i, c: (geom.mod_index(i * ratio), 0, 0)),
            pl.BlockSpec((D_MODEL, fc), lambda i, c: (0, c)),
            pl.BlockSpec((D_MODEL, fc), lambda i, c: (0, c)),
            pl.BlockSpec((fc, D_MODEL), lambda i, c: (c, 0)),
            pl.BlockSpec((1, D_MODEL), const),
            pl.BlockSpec((1, D_MODEL), const),
        ],
        out_specs=pl.BlockSpec((tm, D_MODEL), row),
        out_shape=jax.ShapeDtypeStruct((n, D_MODEL), F32),
        scratch_shapes=[pltpu.VMEM((tm, D_MODEL), F32)],
        compiler_params=_params(("arbitrary", "arbitrary")),
        name="dense_ffn",
    )(h, x1, mods, wg, wu, wd, ln_g, ln_b)


def _moe_kernel(n_chunks, te_ref, nu_ref, src_ref, dst_ref, h_hbm, wg_ref, wu_ref, wd_ref, o_hbm,
                xbuf, xb, obuf, gsem, ssem):
    del te_ref
    tr = xb.shape[0]
    t = pl.program_id(0)
    c = pl.program_id(1)
    n_tiles = pl.num_programs(0)
    valid = t < nu_ref[0]
    slot = t % 2
    other = 1 - slot
    share = tr // n_chunks
    first_row = c * share
    nxt_base = jnp.minimum(t + 1, n_tiles - 1) * tr
    prv_base = jnp.where(t == 0, n_tiles - 1, t - 1) * tr

    def gather_row(base, dst_slot, r, tok=None):
        tok = src_ref[base + r] if tok is None else tok
        return pltpu.make_async_copy(h_hbm.at[pl.ds(tok, 1), :],
                                     xbuf.at[dst_slot, pl.ds(r, 1), :], gsem.at[dst_slot])

    def scatter_row(base, src_slot, r, dst=None):
        dst = dst_ref[base + r] if dst is None else dst
        return pltpu.make_async_copy(obuf.at[src_slot, pl.ds(r, 1), :],
                                     o_hbm.at[pl.ds(dst, 1), :], ssem.at[src_slot])

    def wait_gather(dst_slot):
        pltpu.make_async_copy(h_hbm.at[pl.ds(0, tr), :], xbuf.at[dst_slot], gsem.at[dst_slot]).wait()

    def wait_scatter(src_slot):
        pltpu.make_async_copy(obuf.at[src_slot], o_hbm.at[pl.ds(0, tr), :], ssem.at[src_slot]).wait()

    def looped(n_rows, start_row):
        def body(r, carry):
            start_row(r)
            return carry
        lax.fori_loop(0, n_rows, body, 0, unroll=8)

    @pl.when((t == 0) & (c == 0))
    def _():
        obuf[1] = jnp.zeros(obuf.shape[1:], obuf.dtype)
        looped(tr, lambda r: gather_row(0, 0, r).start())

    @pl.when(c == 0)
    def _():
        wait_gather(slot)

        @pl.when(t >= 1)
        def _():
            wait_scatter(slot)

    def multiply_tile(slot_s, c_s):
        other_s = 1 - slot_s
        if c_s == 0:
            xb[...] = xbuf[slot_s].astype(BF16)
        n_groups = 3 * len(_swiglu_blocks(wg_ref.shape[1]))

        def row_dmas(i):
            for r in range(c_s * share + share * i // n_groups, c_s * share + share * (i + 1) // n_groups):
                gather_row(nxt_base, other_s, r).start()
                scatter_row(prv_base, other_s, r).start(priority=SCATTER_DMA_PRIORITY)

        part = _swiglu_partial(xb[...], wg_ref, wu_ref, wd_ref, row_dmas)
        if c_s == 0:
            obuf[slot_s] = part
        else:
            obuf[slot_s] += part

    for slot_s in range(2):
        for c_s in range(n_chunks):
            pl.when(valid & (slot == slot_s) & (c == c_s))(functools.partial(multiply_tile, slot_s, c_s))

    @pl.when(jnp.logical_not(valid))
    def _():
        @pl.when(c == 0)
        def _():
            obuf[slot] = jnp.zeros(obuf.shape[1:], obuf.dtype)

        looped(share, lambda r: gather_row(nxt_base, other, first_row + r).start())
        looped(share, lambda r: scatter_row(prv_base, other, first_row + r).start())

    @pl.when((t == n_tiles - 1) & (c == n_chunks - 1))
    def _():
        wait_scatter(other)
        looped(tr, lambda r: scatter_row(t * tr, slot, r).start())
        wait_scatter(slot)
        wait_gather(other)


def _moe_call(h, wg, wu, wd, plan):
    tr = FFN_TILE
    d_ff = wg.shape[2]
    fc = _ffn_chunk(d_ff)
    n_chunks = d_ff // fc
    t_max = plan["tile_expert"].shape[0]

    def chunk(t, c, nu):
        return jnp.where(t < nu[0], c, n_chunks - 1)

    grid_spec = pltpu.PrefetchScalarGridSpec(
        num_scalar_prefetch=4,
        grid=(t_max, n_chunks),
        in_specs=[
            pl.BlockSpec(memory_space=pl.ANY),
            pl.BlockSpec((None, D_MODEL, fc), lambda t, c, te, nu, src, dst: (te[t], 0, chunk(t, c, nu))),
            pl.BlockSpec((None, D_MODEL, fc), lambda t, c, te, nu, src, dst: (te[t], 0, chunk(t, c, nu))),
            pl.BlockSpec((None, fc, D_MODEL), lambda t, c, te, nu, src, dst: (te[t], chunk(t, c, nu), 0)),
        ],
        out_specs=pl.BlockSpec(memory_space=pl.ANY),
        scratch_shapes=[
            pltpu.VMEM((2, tr, D_MODEL), F32),
            pltpu.VMEM((tr, D_MODEL), BF16),
            pltpu.VMEM((2, tr, D_MODEL), F32),
            pltpu.SemaphoreType.DMA((2,)),
            pltpu.SemaphoreType.DMA((2,)),
        ],
    )
    return pl.pallas_call(
        functools.partial(_moe_kernel, n_chunks),
        grid_spec=grid_spec,
        out_shape=jax.ShapeDtypeStruct((t_max * tr, D_MODEL), F32),
        compiler_params=_params(("arbitrary", "arbitrary")),
        name="expert_ffn",
    )(plan["tile_expert"], plan["n_used"], plan["src_token"], plan["dst_row"], h, wg, wu, wd)


def _route_plan(routing, n_tok, tr):
    n_assign = TOP_K * n_tok
    t_max = n_assign // tr + N_EXPERTS
    n_slots = t_max * tr
    n_pad = n_slots - n_assign
    id_bits = (n_slots - 1).bit_length()
    experts = jnp.arange(N_EXPERTS, dtype=jnp.int32)
    e_flat = routing[:, :TOP_K].astype(jnp.int32).T.reshape(-1)
    counts = jnp.sum((e_flat[:, None] == experts[None, :]).astype(jnp.int32), axis=0)
    tiles_e = (counts + tr - 1) // tr
    tile_end = jnp.cumsum(tiles_e)
    n_used = tile_end[-1]
    pad_end = jnp.cumsum(tiles_e * tr - counts)
    pad_ids = jnp.arange(n_pad, dtype=jnp.int32)
    pad_expert = jnp.sum((pad_ids[:, None] >= pad_end[None, :]).astype(jnp.int32), axis=1)
    keys = jnp.concatenate([e_flat * 2, pad_expert * 2 + 1])
    item = jnp.arange(n_slots, dtype=jnp.int32)
    slot_item = jnp.sort((keys << id_bits) | item) & ((1 << id_bits) - 1)
    real = slot_item < n_assign
    src = jnp.where(real, slot_item % n_tok, 0).astype(jnp.int32)
    dst = slot_item.astype(jnp.int32)
    t_ids = jnp.arange(t_max, dtype=jnp.int32)
    te = jnp.sum((t_ids[:, None] >= tile_end[None, :]).astype(jnp.int32), axis=1)
    last_used = jnp.max(jnp.where(tiles_e > 0, experts, 0))
    te = jnp.where(t_ids < n_used, te, last_used).astype(jnp.int32)
    return {"tile_expert": te, "n_used": n_used.reshape(1).astype(jnp.int32), "src_token": src, "dst_row": dst}


def _combine_kernel(alpha, split_ctx_tiles, e1_ref, e2_ref, rt_ref, x1_ref, mod_ref, g_ref, b_ref, *o_refs):
    y = rt_ref[:, 2:3] * e1_ref[...] + rt_ref[:, 3:4] * e2_ref[...]
    t = alpha * x1_ref[...] + mod_ref[5:6, :] * y
    out = _layer_norm_rows(t, g_ref[...], b_ref[...])
    if split_ctx_tiles is None:
        o_refs[0][...] = out
    else:
        _write_stream(o_refs[0], o_refs[1], split_ctx_tiles, out)


def _combine_call(eo, routing, x1, mods, ln_g, ln_b, alpha, geom, split_out):
    n = x1.shape[0]
    tm = TOKEN_TILE
    row = lambda i: (i, 0)
    const = lambda i: (0, 0)
    if split_out:
        out_specs = _stream_specs(D_MODEL, geom.ctx_tiles, tm)
        out_shape = [jax.ShapeDtypeStruct((geom.n_ctx, D_MODEL), F32),
                     jax.ShapeDtypeStruct((n - geom.n_ctx, D_MODEL), F32)]
    else:
        out_specs = pl.BlockSpec((tm, D_MODEL), row)
        out_shape = jax.ShapeDtypeStruct((n, D_MODEL), F32)
    return pl.pallas_call(
        functools.partial(_combine_kernel, alpha, geom.ctx_tiles if split_out else None),
        grid=(n // tm,),
        in_specs=[
            pl.BlockSpec((tm, D_MODEL), row),
            pl.BlockSpec((tm, D_MODEL), lambda i: (i + n // tm, 0)),
            pl.BlockSpec((tm, LANES), row),
            pl.BlockSpec((tm, D_MODEL), row),
            pl.BlockSpec((None, N_MODS, D_MODEL), lambda i: (geom.mod_index(i), 0, 0)),
            pl.BlockSpec((1, D_MODEL), const),
            pl.BlockSpec((1, D_MODEL), const),
        ],
        out_specs=out_specs,
        out_shape=out_shape,
        compiler_params=_params(("arbitrary",)),
        name="expert_combine",
    )(eo, eo, routing, x1, mods, ln_g, ln_b)


class _Geometry:
    def __init__(self, n_ctx, dec_batch, dec_seq):
        self.n_ctx = n_ctx
        self.dec_batch = dec_batch
        self.dec_seq = dec_seq
        self.n_tok = n_ctx + dec_batch * dec_seq
        assert n_ctx % FFN_TILE == 0 and dec_seq % FFN_TILE == 0
        self.ctx_tiles = n_ctx // TOKEN_TILE
        self.lat_tiles = dec_seq // TOKEN_TILE

    def mod_index(self, i):
        return jnp.where(i < self.ctx_tiles, 0, 1 + (i - self.ctx_tiles) // self.lat_tiles)

    def rope_index(self, i):
        return jnp.where(i < self.ctx_tiles, 0, 1 + (i - self.ctx_tiles) % self.lat_tiles)


def _rope_tables(dec_seq):
    pos = jnp.arange(dec_seq, dtype=jnp.int32)
    row = (pos // GRID_W).astype(F32)
    col = (pos % GRID_W).astype(F32)
    inv = ROPE_BASE ** (-jnp.arange(ROPE_PAIRS, dtype=F32) / ROPE_PAIRS)
    ang_r = row[:, None] * inv[None, :]
    ang_c = col[:, None] * inv[None, :]
    cos_h = jnp.concatenate([jnp.cos(ang_r), jnp.cos(ang_r), jnp.cos(ang_c), jnp.cos(ang_c)], axis=-1)
    sin_h = jnp.concatenate([-jnp.sin(ang_r), jnp.sin(ang_r), -jnp.sin(ang_c), jnp.sin(ang_c)], axis=-1)
    reps = LANES // HEAD_DIM
    cos_t = jnp.concatenate([jnp.ones((TOKEN_TILE, LANES), F32), jnp.tile(cos_h, (1, reps))], axis=0)
    sin_t = jnp.concatenate([jnp.zeros((TOKEN_TILE, LANES), F32), jnp.tile(sin_h, (1, reps))], axis=0)
    return cos_t, sin_t


def kernel(x_prompt, x_sample, cache_k, cache_v, c, c_ctx, w_ada, b_ada, w_in, w_o, attn_sink, w_s, b_s, sgu_ln_g, sgu_ln_b, ln1_g, ln1_b, ln2_g, ln2_b, w_ff_gate, w_ff_up, w_ff_down, w_router, w_exp_gate, w_exp_up, w_exp_down):
    batch, seq, d = x_prompt.shape
    dec_batch, dec_seq, _ = x_sample.shape
    depth = w_in.shape[0]
    past = cache_k.shape[2]
    assert d == D_MODEL and dec_batch + 1 <= MOD_ROWS
    n_ctx = batch * seq
    n_lat = dec_batch * dec_seq
    n_tok = n_ctx + n_lat
    geom = _Geometry(n_ctx, dec_batch, dec_seq)
    alpha = float((2 * depth) ** 0.25)

    x = (x_prompt.reshape(n_ctx, d), x_sample.reshape(n_lat, d))
    cvec = jnp.concatenate([c_ctx[None, :], c, jnp.zeros((MOD_ROWS - 1 - dec_batch, d), F32)], axis=0)
    mods_all = _ada_call(cvec, w_ada, b_ada).reshape(depth, MOD_ROWS, N_MODS, d)
    cos_t, sin_t = _rope_tables(dec_seq)

    new_k, new_v = [], []
    for l in range(depth):
        mods = mods_all[l]
        q, k, v, kv32, u, g = _in_call(x, mods, w_in[l].astype(BF16), cos_t, sin_t,
                                       sgu_ln_g[l].reshape(1, SGU_WIDTH), sgu_ln_b[l].reshape(1, SGU_WIDTH), geom)
        new_k.append(kv32[:n_ctx, :KV_WIDTH].reshape(batch, seq, N_KV_HEADS, HEAD_DIM))
        new_v.append(kv32[:n_ctx, KV_WIDTH:].reshape(batch, seq, N_KV_HEADS, HEAD_DIM))
        sink = attn_sink[l]
        a_ctx = _ctx_attn_call(sink, q, k, v, batch, seq)
        a_lat = _lat_attn_call(sink, q, k, v, cache_k[:, l].reshape(dec_batch, past, KV_WIDTH).astype(BF16),
                               cache_v[:, l].reshape(dec_batch, past, KV_WIDTH).astype(BF16), geom)
        bias_full = jnp.repeat(b_s[l].T, SGU_GROUP_DIM, axis=1)
        i = l // 2
        moe = l % 2 == 1
        w_r = None
        if moe:
            w_r = jnp.pad(w_router[i], ((0, 0), (0, LANES - N_EXPERTS)))
        outs = _out_call(a_ctx, a_lat, u, g, w_s[l].astype(BF16), bias_full, x, mods, w_o[l].astype(BF16), ln1_g[l].reshape(1, d), ln1_b[l].reshape(1, d),
                         w_r, alpha, geom)
        ln_g, ln_b = ln2_g[l].reshape(1, d), ln2_b[l].reshape(1, d)
        if moe:
            x1, h, routing = outs
            plan = _route_plan(routing, n_tok, FFN_TILE)
            eo = _moe_call(h, w_exp_gate[i].astype(BF16), w_exp_up[i].astype(BF16), w_exp_down[i].astype(BF16), plan)
            x = _combine_call(eo, routing, x1, mods, ln_g, ln_b, alpha, geom, split_out=l == depth - 1)
        else:
            x1, h = outs
            x = _ffn_call(h, x1, mods, w_ff_gate[i].astype(BF16), w_ff_up[i].astype(BF16),
                          w_ff_down[i].astype(BF16), ln_g, ln_b, alpha, geom)

    if not isinstance(x, (tuple, list)):
        x = (x[:n_ctx], x[n_ctx:])
    y_prompt = x[0].reshape(batch, seq, d)
    y_sample = x[1].reshape(dec_batch, dec_seq, d)
    return (y_prompt, y_sample, jnp.stack(new_k, axis=1), jnp.stack(new_v, axis=1))
```

```python
import functools

import jax
import jax.numpy as jnp
import numpy as np
from jax import lax
from jax.experimental import pallas as pl
from jax.experimental.pallas import tpu as pltpu

F32 = jnp.float32
BF16 = jnp.bfloat16

D_MODEL = 1024
HEAD_DIM = 64
N_Q_HEADS = 8
N_KV_HEADS = 2
Q_PER_KV = N_Q_HEADS // N_KV_HEADS
ATTN_WIDTH = N_Q_HEADS * HEAD_DIM
KV_WIDTH = N_KV_HEADS * HEAD_DIM
SGU_WIDTH = D_MODEL - ATTN_WIDTH
N_SGU_GROUPS = 8
SGU_GROUP_DIM = SGU_WIDTH // N_SGU_GROUPS
CHUNK = 128
BLOCK = 128
WINDOW = 128
GRID_W = 64
IN_WIDTH = ATTN_WIDTH + 2 * KV_WIDTH + 2 * SGU_WIDTH
OFF_K = ATTN_WIDTH
OFF_V = OFF_K + KV_WIDTH
OFF_U = OFF_V + KV_WIDTH
OFF_G = OFF_U + SGU_WIDTH
N_EXPERTS = 8
TOP_K = 2
ROPE_BASE = 10000.0
ROPE_PAIRS = HEAD_DIM // 4
LN_EPS = 1e-5
ATTN_SCALE = HEAD_DIM ** -0.5
NEG_INF = -1e30
N_MODS = 6

LANES = 128
V7X_MXU_WIDTH = 256
V7X_VMEM_LIMIT_BYTES = 48 * 1024 * 1024

TOKEN_TILE = 256
FFN_TILE = 512
LAT_QUERY_ROWS = 256
SCATTER_DMA_PRIORITY = 1
MOD_ROWS = 8
ADA_COLS = 1536


def _params(sem):
    return pltpu.CompilerParams(dimension_semantics=sem, vmem_limit_bytes=V7X_VMEM_LIMIT_BYTES)


def _gelu_tanh(x):
    return x * (0.5 * (1.0 + jnp.tanh(np.float32(np.sqrt(2.0 / np.pi)) * (x + 0.044715 * (x * x * x)))))


def _sigmoid(x):
    return 1.0 / (1.0 + jnp.exp(-x))


def _layer_norm_rows(t, g, b):
    mu = jnp.mean(t, axis=-1, keepdims=True)
    tc = t - mu
    var = jnp.mean(tc * tc, axis=-1, keepdims=True)
    return tc * lax.rsqrt(var + LN_EPS) * g + b


def _rows_to_tiles(x):
    pieces = jnp.stack([x[:, s * LANES:(s + 1) * LANES] for s in range(x.shape[1] // LANES)], axis=0)
    return pltpu.einshape("srl->rsl", pieces)


def _tiles_to_rows(t):
    pieces = pltpu.einshape("rsl->srl", t)
    return jnp.concatenate([pieces[s] for s in range(t.shape[1])], axis=1)


def _stream_specs(width, ctx_tiles, tm):
    return [pl.BlockSpec((tm, width), lambda i, *_: (jnp.minimum(i, ctx_tiles - 1), 0)),
            pl.BlockSpec((tm, width), lambda i, *_: (jnp.maximum(i - ctx_tiles, 0), 0))]


def _read_stream(ctx_ref, lat_ref, ctx_tiles):
    tile = lax.broadcasted_iota(jnp.int32, ctx_ref.shape, 0) * 0 + pl.program_id(0)
    return jnp.where(tile < ctx_tiles, ctx_ref[...], lat_ref[...])


def _write_stream(ctx_ref, lat_ref, ctx_tiles, value):
    @pl.when(pl.program_id(0) < ctx_tiles)
    def _():
        ctx_ref[...] = value

    @pl.when(pl.program_id(0) >= ctx_tiles)
    def _():
        lat_ref[...] = value


def _ada_kernel(c_ref, w_ref, b_ref, o_ref):
    c = c_ref[...]
    s = (c * _sigmoid(c)).astype(BF16)
    o_ref[...] = jnp.dot(s, w_ref[...].astype(BF16), preferred_element_type=F32) + b_ref[...]


def _ada_call(cvec, w_ada, b_ada):
    depth = w_ada.shape[0]
    n_out = w_ada.shape[2]
    return pl.pallas_call(
        _ada_kernel,
        grid=(depth, n_out // ADA_COLS),
        in_specs=[
            pl.BlockSpec((MOD_ROWS, D_MODEL), lambda l, j: (0, 0)),
            pl.BlockSpec((None, D_MODEL, ADA_COLS), lambda l, j: (l, 0, j)),
            pl.BlockSpec((None, 1, ADA_COLS), lambda l, j: (l, 0, j)),
        ],
        out_specs=pl.BlockSpec((None, MOD_ROWS, ADA_COLS), lambda l, j: (l, 0, j)),
        out_shape=jax.ShapeDtypeStruct((depth, MOD_ROWS, n_out), F32),
        compiler_params=_params(("arbitrary", "arbitrary")),
        name="adaln",
    )(cvec, w_ada, b_ada.reshape(depth, 1, n_out))


def _in_kernel(split_ctx_tiles, *refs):
    if split_ctx_tiles is None:
        x = refs[0][...]
        refs = refs[1:]
    else:
        x = _read_stream(refs[0], refs[1], split_ctx_tiles)
        refs = refs[2:]
    mod_ref, w_ref, cos_ref, sin_ref, lng_ref, lnb_ref, q_ref, k_ref, v_ref, kv_ref, u_ref, g_ref = refs
    tm = x.shape[0]
    h = x * (1.0 + mod_ref[1:2, :]) + mod_ref[0:1, :]
    hb = h.astype(BF16)

    def z_block(start):
        return jnp.dot(hb, w_ref[:, start:start + V7X_MXU_WIDTH], preferred_element_type=F32)

    cos = cos_ref[...]
    sin = sin_ref[...]
    lane = lax.broadcasted_iota(jnp.int32, (tm, LANES), 1)
    first_of_pair = (lane & (2 * ROPE_PAIRS - 1)) < ROPE_PAIRS
    lower_group = lane < SGU_GROUP_DIM

    def rope(t):
        partner = jnp.where(first_of_pair, pltpu.roll(t, LANES - ROPE_PAIRS, 1), pltpu.roll(t, ROPE_PAIRS, 1))
        return t * cos + partner * sin

    halves = (slice(0, LANES), slice(LANES, 2 * LANES))
    inv_n = 1.0 / SGU_GROUP_DIM
    for j in range(SGU_WIDTH // LANES):
        cols = slice(j * LANES, (j + 1) * LANES)
        if j % 2 == 0:
            z = z_block(OFF_G + j * LANES)
        t = _gelu_tanh(z[:, halves[j % 2]])
        s_lo = jnp.sum(jnp.where(lower_group, t, 0.0), axis=-1, keepdims=True)
        s_hi = jnp.sum(jnp.where(lower_group, 0.0, t), axis=-1, keepdims=True)
        tc = t - jnp.where(lower_group, s_lo, s_hi) * inv_n
        sq = tc * tc
        v_lo = jnp.sum(jnp.where(lower_group, sq, 0.0), axis=-1, keepdims=True)
        v_hi = jnp.sum(jnp.where(lower_group, 0.0, sq), axis=-1, keepdims=True)
        var = jnp.where(lower_group, v_lo, v_hi) * inv_n
        g_ref[:, cols] = (tc * lax.rsqrt(var + LN_EPS) * lng_ref[:, cols] + lnb_ref[:, cols]).astype(BF16)

    for start in range(0, ATTN_WIDTH, V7X_MXU_WIDTH):
        z = z_block(start)
        for half in halves:
            q_ref[:, start + half.start:start + half.stop] = (rope(z[:, half]) * ATTN_SCALE).astype(BF16)
    z = z_block(OFF_K)
    k_ref[...] = rope(z[:, halves[0]]).astype(BF16)
    v_ref[...] = z[:, halves[1]].astype(BF16)
    kv_ref[...] = z
    for start in range(0, SGU_WIDTH, V7X_MXU_WIDTH):
        u_ref[:, start:start + V7X_MXU_WIDTH] = _gelu_tanh(z_block(OFF_U + start))


def _in_call(x, mods, w_in, cos_t, sin_t, ln_g, ln_b, geom):
    n = geom.n_tok
    tm = TOKEN_TILE
    mod_idx, rope_idx = geom.mod_index, geom.rope_index
    row = lambda i: (i, 0)
    split = isinstance(x, tuple)
    x_args = list(x) if split else [x]
    x_specs = _stream_specs(D_MODEL, geom.ctx_tiles, tm) if split else [pl.BlockSpec((tm, D_MODEL), row)]
    outs = pl.pallas_call(
        functools.partial(_in_kernel, geom.ctx_tiles if split else None),
        grid=(n // tm,),
        in_specs=x_specs + [
            pl.BlockSpec((None, N_MODS, D_MODEL), lambda i: (mod_idx(i), 0, 0)),
            pl.BlockSpec((D_MODEL, IN_WIDTH), lambda i: (0, 0)),
            pl.BlockSpec((tm, LANES), lambda i: (rope_idx(i), 0)),
            pl.BlockSpec((tm, LANES), lambda i: (rope_idx(i), 0)),
            pl.BlockSpec((1, SGU_WIDTH), lambda i: (0, 0)),
            pl.BlockSpec((1, SGU_WIDTH), lambda i: (0, 0)),
        ],
        out_specs=[
            pl.BlockSpec((tm, ATTN_WIDTH), row),
            pl.BlockSpec((tm, KV_WIDTH), row),
            pl.BlockSpec((tm, KV_WIDTH), row),
            pl.BlockSpec((tm, 2 * KV_WIDTH), row),
            pl.BlockSpec((tm, SGU_WIDTH), row),
            pl.BlockSpec((tm, SGU_WIDTH), row),
        ],
        out_shape=[
            jax.ShapeDtypeStruct((n, ATTN_WIDTH), BF16),
            jax.ShapeDtypeStruct((n, KV_WIDTH), BF16),
            jax.ShapeDtypeStruct((n, KV_WIDTH), BF16),
            jax.ShapeDtypeStruct((n, 2 * KV_WIDTH), F32),
            jax.ShapeDtypeStruct((n, SGU_WIDTH), F32),
            jax.ShapeDtypeStruct((n, SGU_WIDTH), BF16),
        ],
        compiler_params=_params(("arbitrary",)),
        name="in_proj",
    )(*x_args, mods, w_in, cos_t, sin_t, ln_g, ln_b)
    return outs


def _group_attention(q_ref, rows, hk, k_all, v_all, block_masks, sink_ref):
    m_rows = rows.stop - rows.start
    heads = [hk * Q_PER_KV + gq for gq in range(Q_PER_KV)]
    q = jnp.concatenate([q_ref[rows, h * HEAD_DIM:(h + 1) * HEAD_DIM] for h in heads], axis=0)
    s = lax.dot_general(q, k_all, (((1,), (1,)), ((), ())), preferred_element_type=F32)
    n_blocks = k_all.shape[0] // LANES
    blocks = [s[:, b * LANES:(b + 1) * LANES] for b in range(n_blocks)]
    for b, mask in block_masks.items():
        blocks[b] = jnp.where(mask, blocks[b], NEG_INF)
    head_of_row = lax.broadcasted_iota(jnp.int32, (Q_PER_KV * m_rows, 1), 0) // m_rows
    sink = jnp.zeros((Q_PER_KV * m_rows, 1), F32)
    for gq, h in enumerate(heads):
        sink = jnp.where(head_of_row == gq, sink_ref[h], sink)
    m_el = blocks[0]
    for blk in blocks[1:]:
        m_el = jnp.maximum(m_el, blk)
    m = jnp.maximum(jnp.max(m_el, axis=-1, keepdims=True), sink)
    probs = [jnp.exp(blk - m) for blk in blocks]
    l_el = probs[0]
    for p in probs[1:]:
        l_el = l_el + p
    denom = jnp.sum(l_el, axis=-1, keepdims=True) + jnp.exp(sink - m)
    p_all = jnp.concatenate([p.astype(BF16) for p in probs], axis=1)
    o = jnp.dot(p_all, v_all, preferred_element_type=F32) / denom
    return {h: o[gq * m_rows:(gq + 1) * m_rows] for gq, h in enumerate(heads)}


def _store_heads(o_ref, rows, outs):
    for h0 in range(0, N_Q_HEADS, 2):
        pair = jnp.concatenate([outs[h0], outs[h0 + 1]], axis=1)
        o_ref[rows, h0 * HEAD_DIM:(h0 + 2) * HEAD_DIM] = pair.astype(o_ref.dtype)


def _ctx_attn_kernel(sink_ref, q_ref, k_ref, v_ref, o_ref):
    rows = slice(0, q_ref.shape[0])
    outs = {}
    for hk in range(N_KV_HEADS):
        kv_cols = slice(hk * HEAD_DIM, (hk + 1) * HEAD_DIM)
        outs.update(_group_attention(q_ref, rows, hk, k_ref[:, kv_cols], v_ref[:, kv_cols], {}, sink_ref))
    _store_heads(o_ref, rows, outs)


def _ctx_attn_call(sink, q, k, v, n_seq, seq_len):
    n = n_seq * seq_len
    blk = lambda b: (b, 0)
    return pl.pallas_call(
        _ctx_attn_kernel,
        grid=(n_seq,),
        in_specs=[
            pl.BlockSpec(memory_space=pltpu.SMEM),
            pl.BlockSpec((seq_len, ATTN_WIDTH), blk),
            pl.BlockSpec((seq_len, KV_WIDTH), blk),
            pl.BlockSpec((seq_len, KV_WIDTH), blk),
        ],
        out_specs=pl.BlockSpec((seq_len, ATTN_WIDTH), blk),
        out_shape=jax.ShapeDtypeStruct((n, ATTN_WIDTH), BF16),
        compiler_params=_params(("arbitrary",)),
        name="ctx_attention",
    )(sink, q, k, v)


def _lat_attn_kernel(sink_ref, q_ref, k_ref, v_ref, ck_ref, cv_ref, o_ref):
    blocks_per_step = q_ref.shape[0] // BLOCK
    nb = k_ref.shape[0] // BLOCK
    past_blocks = ck_ref.shape[0] // LANES
    r = lax.broadcasted_iota(jnp.int32, (Q_PER_KV * BLOCK, BLOCK), 0) & (BLOCK - 1)
    c = lax.broadcasted_iota(jnp.int32, (Q_PER_KV * BLOCK, BLOCK), 1)
    for sub in range(blocks_per_step):
        j = pl.program_id(1) * blocks_per_step + sub
        rows = slice(sub * BLOCK, (sub + 1) * BLOCK)
        mask_prev = c >= r + jnp.where(j > 0, 0, BLOCK)
        mask_next = c <= r - jnp.where(j < nb - 1, 0, BLOCK)
        prev = pl.ds(pl.multiple_of(jnp.maximum(j - 1, 0) * BLOCK, BLOCK), BLOCK)
        cur = pl.ds(pl.multiple_of(j * BLOCK, BLOCK), BLOCK)
        nxt = pl.ds(pl.multiple_of(jnp.minimum(j + 1, nb - 1) * BLOCK, BLOCK), BLOCK)
        outs = {}
        for hk in range(N_KV_HEADS):
            kv_cols = slice(hk * HEAD_DIM, (hk + 1) * HEAD_DIM)
            k_all = jnp.concatenate([ck_ref[:, kv_cols], k_ref[prev, kv_cols], k_ref[cur, kv_cols],
                                     k_ref[nxt, kv_cols]], axis=0)
            v_all = jnp.concatenate([cv_ref[:, kv_cols], v_ref[prev, kv_cols], v_ref[cur, kv_cols],
                                     v_ref[nxt, kv_cols]], axis=0)
            masks = {past_blocks: mask_prev, past_blocks + 2: mask_next}
            outs.update(_group_attention(q_ref, rows, hk, k_all, v_all, masks, sink_ref))
        _store_heads(o_ref, rows, outs)


def _lat_attn_call(sink, q, k, v, cache_k, cache_v, geom):
    qb = LAT_QUERY_ROWS
    steps = geom.dec_seq // qb
    assert geom.n_ctx % geom.dec_seq == 0 and cache_k.shape[1] % LANES == 0
    seq_base = geom.n_ctx // geom.dec_seq
    past = cache_k.shape[1]
    seq_spec = pl.BlockSpec((geom.dec_seq, KV_WIDTH), lambda b, j: (seq_base + b, 0))
    cache_spec = pl.BlockSpec((None, past, KV_WIDTH), lambda b, j: (b, 0, 0))
    return pl.pallas_call(
        _lat_attn_kernel,
        grid=(geom.dec_batch, steps),
        in_specs=[
            pl.BlockSpec(memory_space=pltpu.SMEM),
            pl.BlockSpec((qb, ATTN_WIDTH), lambda b, j: (geom.n_ctx // qb + b * steps + j, 0)),
            seq_spec, seq_spec,
            cache_spec, cache_spec,
        ],
        out_specs=pl.BlockSpec((qb, ATTN_WIDTH), lambda b, j: (b * steps + j, 0)),
        out_shape=jax.ShapeDtypeStruct((geom.dec_batch * geom.dec_seq, ATTN_WIDTH), BF16),
        compiler_params=_params(("arbitrary", "arbitrary")),
        name="lat_attention",
    )(sink, q, k, v, cache_k, cache_v)


def _spatial_gating(u_ref, g_ref, mix_ref, bias_ref, s_ref):
    tm = u_ref.shape[0]
    lane = lax.broadcasted_iota(jnp.int32, (CHUNK, LANES), 1)
    lower_group = lane < SGU_GROUP_DIM
    for ch in range(tm // CHUNK):
        rows = slice(ch * CHUNK, (ch + 1) * CHUNK)
        for p in range(SGU_WIDTH // LANES):
            cols = slice(p * LANES, (p + 1) * LANES)
            g = g_ref[rows, cols]
            zero = jnp.zeros_like(g)
            mixed = (jnp.dot(mix_ref[2 * p], jnp.where(lower_group, g, zero), preferred_element_type=F32)
                     + jnp.dot(mix_ref[2 * p + 1], jnp.where(lower_group, zero, g), preferred_element_type=F32))
            s_ref[rows, cols] = (u_ref[rows, cols] * (mixed + bias_ref[:, cols])).astype(s_ref.dtype)


def _out_kernel(alpha, with_router, split_x, ctx_tiles, a_ctx_ref, a_lat_ref, u_ref, gg_ref, mix_ref, bias_ref,
                *refs):
    s_ref = refs[-1]
    refs = refs[:-1]
    if split_x:
        x = _read_stream(refs[0], refs[1], ctx_tiles)
        refs = refs[2:]
    else:
        x = refs[0][...]
        refs = refs[1:]
    mod_ref, wa_ref, ws_ref, g_ref, b_ref = refs[:5]
    if with_router:
        wr_ref, x1_ref, h_ref, rt_ref = refs[5:]
    else:
        x1_ref, h_ref = refs[5:]
    a = _read_stream(a_ctx_ref, a_lat_ref, ctx_tiles)
    _spatial_gating(u_ref, gg_ref, mix_ref, bias_ref, s_ref)
    y = (jnp.dot(a, wa_ref[...], preferred_element_type=F32)
         + jnp.dot(s_ref[...], ws_ref[...], preferred_element_type=F32))
    t = alpha * x + mod_ref[2:3, :] * y
    x1 = _layer_norm_rows(t, g_ref[...], b_ref[...])
    x1_ref[...] = x1
    h = x1 * (1.0 + mod_ref[4:5, :]) + mod_ref[3:4, :]
    if with_router:
        h_ref[...] = _rows_to_tiles(h)
    else:
        h_ref[...] = h.astype(h_ref.dtype)
    if with_router:
        tm = h.shape[0]
        wr = wr_ref[...]
        wr_hi = wr.astype(BF16)
        wr_lo = (wr - wr_hi.astype(F32)).astype(BF16)
        h_hi = h.astype(BF16)
        h_lo = (h - h_hi.astype(F32)).astype(BF16)
        logits = (jnp.dot(h_hi, wr_hi, preferred_element_type=F32)
                  + jnp.dot(h_lo, wr_hi, preferred_element_type=F32)
                  + jnp.dot(h_hi, wr_lo, preferred_element_type=F32))
        lane = lax.broadcasted_iota(jnp.int32, (tm, LANES), 1).astype(F32)
        neg = jnp.float32(-jnp.inf)
        lg = jnp.where(lane < N_EXPERTS, logits, neg)
        m1 = jnp.max(lg, axis=-1, keepdims=True)
        i1 = jnp.min(jnp.where(lg == m1, lane, float(LANES)), axis=-1, keepdims=True)
        lg2 = jnp.where(lane == i1, neg, lg)
        m2 = jnp.max(lg2, axis=-1, keepdims=True)
        i2 = jnp.min(jnp.where(lg2 == m2, lane, float(LANES)), axis=-1, keepdims=True)
        e2 = jnp.exp(m2 - m1)
        g1 = 1.0 / (1.0 + e2)
        g2 = e2 / (1.0 + e2)
        rt = jnp.where(lane == 0, i1, jnp.where(lane == 1, i2, jnp.where(lane == 2, g1, jnp.where(lane == 3, g2, 0.0))))
        rt_ref[...] = rt


def _out_call(a_ctx, a_lat, u, gg, w_s, bias_full, x, mods, w_o, ln_g, ln_b, w_router, alpha, geom):
    n = geom.n_tok
    tm = TOKEN_TILE
    ctx_tiles = geom.ctx_tiles
    row = lambda i: (i, 0)
    const = lambda i: (0, 0)
    with_router = w_router is not None
    split_x = isinstance(x, tuple)
    x_args = list(x) if split_x else [x]
    x_specs = _stream_specs(D_MODEL, ctx_tiles, tm) if split_x else [pl.BlockSpec((tm, D_MODEL), row)]
    sgu_specs = [
        pl.BlockSpec((tm, SGU_WIDTH), row),
        pl.BlockSpec((tm, SGU_WIDTH), row),
        pl.BlockSpec((N_SGU_GROUPS, CHUNK, CHUNK), lambda i: (0, 0, 0)),
        pl.BlockSpec((CHUNK, SGU_WIDTH), const),
    ]
    in_specs = _stream_specs(ATTN_WIDTH, ctx_tiles, tm) + sgu_specs + x_specs + [
        pl.BlockSpec((None, N_MODS, D_MODEL), lambda i: (geom.mod_index(i), 0, 0)),
        pl.BlockSpec((ATTN_WIDTH, D_MODEL), const),
        pl.BlockSpec((SGU_WIDTH, D_MODEL), lambda i: (1, 0)),
        pl.BlockSpec((1, D_MODEL), const),
        pl.BlockSpec((1, D_MODEL), const),
    ]
    args = [a_ctx, a_lat, u, gg, w_s, bias_full] + x_args + [mods, w_o, w_o, ln_g, ln_b]
    if with_router:
        h_spec = pl.BlockSpec((tm, D_MODEL // LANES, LANES), lambda i: (i, 0, 0))
        h_shape = jax.ShapeDtypeStruct((n, D_MODEL // LANES, LANES), F32)
    else:
        h_spec = pl.BlockSpec((tm, D_MODEL), row)
        h_shape = jax.ShapeDtypeStruct((n, D_MODEL), BF16)
    out_specs = [pl.BlockSpec((tm, D_MODEL), row), h_spec]
    out_shape = [jax.ShapeDtypeStruct((n, D_MODEL), F32), h_shape]
    if with_router:
        in_specs.append(pl.BlockSpec((D_MODEL, LANES), const))
        args.append(w_router)
        out_specs.append(pl.BlockSpec((tm, LANES), row))
        out_shape.append(jax.ShapeDtypeStruct((n, LANES), F32))
    return pl.pallas_call(
        functools.partial(_out_kernel, alpha, with_router, split_x, ctx_tiles),
        grid=(n // tm,),
        in_specs=in_specs,
        out_specs=out_specs,
        out_shape=out_shape,
        scratch_shapes=[pltpu.VMEM((tm, SGU_WIDTH), BF16)],
        compiler_params=_params(("arbitrary",)),
        name="out_proj_router" if with_router else "out_proj",
    )(*args)


def _swiglu_blocks(width):
    return [slice(s, min(s + V7X_MXU_WIDTH, width)) for s in range(0, width, V7X_MXU_WIDTH)]


def _swiglu_partial(x, wg_ref, wu_ref, wd_ref, side_work=None):
    out = None
    for n, cols in enumerate(_swiglu_blocks(wg_ref.shape[1])):
        a = jnp.dot(x, wg_ref[:, cols], preferred_element_type=F32)
        if side_work is not None:
            side_work(3 * n)
        b = jnp.dot(x, wu_ref[:, cols], preferred_element_type=F32)
        if side_work is not None:
            side_work(3 * n + 1)
        mid = ((a * _sigmoid(a)) * b).astype(BF16)
        part = jnp.dot(mid, wd_ref[cols, :], preferred_element_type=F32)
        if side_work is not None:
            side_work(3 * n + 2)
        out = part if out is None else out + part
    return out


def _ffn_kernel(alpha, h_ref, x1_ref, mod_ref, wg_ref, wu_ref, wd_ref, g_ref, b_ref, o_ref, acc_ref):
    c = pl.program_id(1)
    part = _swiglu_partial(h_ref[...], wg_ref, wu_ref, wd_ref)

    @pl.when(c == 0)
    def _():
        acc_ref[...] = part

    @pl.when(c > 0)
    def _():
        acc_ref[...] += part

    @pl.when(c == pl.num_programs(1) - 1)
    def _():
        t = alpha * x1_ref[...] + mod_ref[5:6, :] * acc_ref[...]
        o_ref[...] = _layer_norm_rows(t, g_ref[...], b_ref[...])


def _ffn_chunk(d_ff):
    assert d_ff % (2 * LANES) == 0
    return d_ff // 2


def _ffn_call(h, x1, mods, wg, wu, wd, ln_g, ln_b, alpha, geom):
    n = h.shape[0]
    tm = FFN_TILE
    d_ff = wg.shape[1]
    fc = _ffn_chunk(d_ff)
    ratio = tm // TOKEN_TILE
    row = lambda i, c: (i, 0)
    const = lambda i, c: (0, 0)
    return pl.pallas_call(
        functools.partial(_ffn_kernel, alpha),
        grid=(n // tm, d_ff // fc),
        in_specs=[
            pl.BlockSpec((tm, D_MODEL), row),
            pl.BlockSpec((tm, D_MODEL), row),
            pl.BlockSpec((None, N_MODS, D_MODEL), lambda i, c: (geom.mod_index(i * ratio), 0, 0)),
            pl.BlockSpec((D_MODEL, fc), lambda i, c: (0, c)),
            pl.BlockSpec((D_MODEL, fc), lambda i, c: (0, c)),
            pl.BlockSpec((fc, D_MODEL), lambda i, c: (c, 0)),
            pl.BlockSpec((1, D_MODEL), const),
            pl.BlockSpec((1, D_MODEL), const),
        ],
        out_specs=pl.BlockSpec((tm, D_MODEL), row),
        out_shape=jax.ShapeDtypeStruct((n, D_MODEL), F32),
        scratch_shapes=[pltpu.VMEM((tm, D_MODEL), F32)],
        compiler_params=_params(("arbitrary", "arbitrary")),
        name="dense_ffn",
    )(h, x1, mods, wg, wu, wd, ln_g, ln_b)


def _moe_kernel(n_chunks, te_ref, nu_ref, src_ref, dst_ref, h_hbm, wg_ref, wu_ref, wd_ref, o_hbm,
                xbuf, xb, obuf, gsem, ssem):
    del te_ref
    tr = xb.shape[0]
    t = pl.program_id(0)
    c = pl.program_id(1)
    n_tiles = pl.num_programs(0)
    valid = t < nu_ref[0]
    slot = t % 2
    other = 1 - slot
    share = tr // n_chunks
    first_row = c * share
    nxt_base = jnp.minimum(t + 1, n_tiles - 1) * tr
    prv_base = jnp.where(t == 0, n_tiles - 1, t - 1) * tr

    def gather_row(base, dst_slot, r, tok=None):
        tok = src_ref[base + r] if tok is None else tok
        return pltpu.make_async_copy(h_hbm.at[tok], xbuf.at[dst_slot, r], gsem.at[dst_slot])

    def scatter_row(base, src_slot, r, dst=None):
        dst = dst_ref[base + r] if dst is None else dst
        return pltpu.make_async_copy(obuf.at[src_slot, r], o_hbm.at[dst], ssem.at[src_slot])

    def wait_gather(dst_slot):
        pltpu.make_async_copy(h_hbm.at[pl.ds(0, tr)], xbuf.at[dst_slot], gsem.at[dst_slot]).wait()

    def wait_scatter(src_slot):
        pltpu.make_async_copy(obuf.at[src_slot], o_hbm.at[pl.ds(0, tr)], ssem.at[src_slot]).wait()

    def looped(n_rows, start_row):
        def body(r, carry):
            start_row(r)
            return carry
        lax.fori_loop(0, n_rows, body, 0, unroll=8)

    @pl.when((t == 0) & (c == 0))
    def _():
        obuf[1] = jnp.zeros(obuf.shape[1:], obuf.dtype)
        looped(tr, lambda r: gather_row(0, 0, r).start())

    @pl.when(c == 0)
    def _():
        wait_gather(slot)

        @pl.when(t >= 1)
        def _():
            wait_scatter(slot)

    def multiply_tile(slot_s, c_s):
        other_s = 1 - slot_s
        if c_s == 0:
            xb[...] = _tiles_to_rows(xbuf[slot_s]).astype(BF16)
        n_groups = 3 * len(_swiglu_blocks(wg_ref.shape[1]))

        def row_dmas(i):
            for r in range(c_s * share + share * i // n_groups, c_s * share + share * (i + 1) // n_groups):
                gather_row(nxt_base, other_s, r).start()
                scatter_row(prv_base, other_s, r).start(priority=SCATTER_DMA_PRIORITY)

        part = _swiglu_partial(xb[...], wg_ref, wu_ref, wd_ref, row_dmas)
        if c_s == 0:
            obuf[slot_s] = _rows_to_tiles(part)
        else:
            obuf[slot_s] += _rows_to_tiles(part)

    for slot_s in range(2):
        for c_s in range(n_chunks):
            pl.when(valid & (slot == slot_s) & (c == c_s))(functools.partial(multiply_tile, slot_s, c_s))

    @pl.when(jnp.logical_not(valid))
    def _():
        @pl.when(c == 0)
        def _():
            obuf[slot] = jnp.zeros(obuf.shape[1:], obuf.dtype)

        looped(share, lambda r: gather_row(nxt_base, other, first_row + r).start())
        looped(share, lambda r: scatter_row(prv_base, other, first_row + r).start())

    @pl.when((t == n_tiles - 1) & (c == n_chunks - 1))
    def _():
        wait_scatter(other)
        looped(tr, lambda r: scatter_row(t * tr, slot, r).start())
        wait_scatter(slot)
        wait_gather(other)


def _moe_call(h, wg, wu, wd, plan):
    tr = FFN_TILE
    d_ff = wg.shape[2]
    fc = _ffn_chunk(d_ff)
    n_chunks = d_ff // fc
    t_max = plan["tile_expert"].shape[0]
    row_tile = h.shape[1:]

    def chunk(t, c, nu):
        return jnp.where(t < nu[0], c, n_chunks - 1)

    grid_spec = pltpu.PrefetchScalarGridSpec(
        num_scalar_prefetch=4,
        grid=(t_max, n_chunks),
        in_specs=[
            pl.BlockSpec(memory_space=pl.ANY),
            pl.BlockSpec((None, D_MODEL, fc), lambda t, c, te, nu, src, dst: (te[t], 0, chunk(t, c, nu))),
            pl.BlockSpec((None, D_MODEL, fc), lambda t, c, te, nu, src, dst: (te[t], 0, chunk(t, c, nu))),
            pl.BlockSpec((None, fc, D_MODEL), lambda t, c, te, nu, src, dst: (te[t], chunk(t, c, nu), 0)),
        ],
        out_specs=pl.BlockSpec(memory_space=pl.ANY),
        scratch_shapes=[
            pltpu.VMEM((2, tr) + row_tile, F32),
            pltpu.VMEM((tr, D_MODEL), BF16),
            pltpu.VMEM((2, tr) + row_tile, F32),
            pltpu.SemaphoreType.DMA((2,)),
            pltpu.SemaphoreType.DMA((2,)),
        ],
    )
    return pl.pallas_call(
        functools.partial(_moe_kernel, n_chunks),
        grid_spec=grid_spec,
        out_shape=jax.ShapeDtypeStruct((t_max * tr,) + row_tile, F32),
        compiler_params=_params(("arbitrary", "arbitrary")),
        name="expert_ffn",
    )(plan["tile_expert"], plan["n_used"], plan["src_token"], plan["dst_row"], h, wg, wu, wd)


def _route_plan(routing, n_tok, tr):
    n_assign = TOP_K * n_tok
    t_max = n_assign // tr + N_EXPERTS
    n_slots = t_max * tr
    n_pad = n_slots - n_assign
    id_bits = (n_slots - 1).bit_length()
    experts = jnp.arange(N_EXPERTS, dtype=jnp.int32)
    e_flat = routing[:, :TOP_K].astype(jnp.int32).T.reshape(-1)
    counts = jnp.sum((e_flat[:, None] == experts[None, :]).astype(jnp.int32), axis=0)
    tiles_e = (counts + tr - 1) // tr
    tile_end = jnp.cumsum(tiles_e)
    n_used = tile_end[-1]
    pad_end = jnp.cumsum(tiles_e * tr - counts)
    pad_ids = jnp.arange(n_pad, dtype=jnp.int32)
    pad_expert = jnp.sum((pad_ids[:, None] >= pad_end[None, :]).astype(jnp.int32), axis=1)
    keys = jnp.concatenate([e_flat * 2, pad_expert * 2 + 1])
    item = jnp.arange(n_slots, dtype=jnp.int32)
    slot_item = jnp.sort((keys << id_bits) | item) & ((1 << id_bits) - 1)
    real = slot_item < n_assign
    src = jnp.where(real, slot_item % n_tok, 0).astype(jnp.int32)
    dst = slot_item.astype(jnp.int32)
    t_ids = jnp.arange(t_max, dtype=jnp.int32)
    te = jnp.sum((t_ids[:, None] >= tile_end[None, :]).astype(jnp.int32), axis=1)
    last_used = jnp.max(jnp.where(tiles_e > 0, experts, 0))
    te = jnp.where(t_ids < n_used, te, last_used).astype(jnp.int32)
    return {"tile_expert": te, "n_used": n_used.reshape(1).astype(jnp.int32), "src_token": src, "dst_row": dst}


def _combine_kernel(alpha, split_ctx_tiles, e1_ref, e2_ref, rt_ref, x1_ref, mod_ref, g_ref, b_ref, *o_refs):
    y = rt_ref[:, 2:3] * _tiles_to_rows(e1_ref[...]) + rt_ref[:, 3:4] * _tiles_to_rows(e2_ref[...])
    t = alpha * x1_ref[...] + mod_ref[5:6, :] * y
    out = _layer_norm_rows(t, g_ref[...], b_ref[...])
    if split_ctx_tiles is None:
        o_refs[0][...] = out
    else:
        _write_stream(o_refs[0], o_refs[1], split_ctx_tiles, out)


def _combine_call(eo, routing, x1, mods, ln_g, ln_b, alpha, geom, split_out):
    n = x1.shape[0]
    tm = TOKEN_TILE
    row = lambda i: (i, 0)
    const = lambda i: (0, 0)
    if split_out:
        out_specs = _stream_specs(D_MODEL, geom.ctx_tiles, tm)
        out_shape = [jax.ShapeDtypeStruct((geom.n_ctx, D_MODEL), F32),
                     jax.ShapeDtypeStruct((n - geom.n_ctx, D_MODEL), F32)]
    else:
        out_specs = pl.BlockSpec((tm, D_MODEL), row)
        out_shape = jax.ShapeDtypeStruct((n, D_MODEL), F32)
    return pl.pallas_call(
        functools.partial(_combine_kernel, alpha, geom.ctx_tiles if split_out else None),
        grid=(n // tm,),
        in_specs=[
            pl.BlockSpec((tm,) + eo.shape[1:], lambda i: (i, 0, 0)),
            pl.BlockSpec((tm,) + eo.shape[1:], lambda i: (i + n // tm, 0, 0)),
            pl.BlockSpec((tm, LANES), row),
            pl.BlockSpec((tm, D_MODEL), row),
            pl.BlockSpec((None, N_MODS, D_MODEL), lambda i: (geom.mod_index(i), 0, 0)),
            pl.BlockSpec((1, D_MODEL), const),
            pl.BlockSpec((1, D_MODEL), const),
        ],
        out_specs=out_specs,
        out_shape=out_shape,
        compiler_params=_params(("arbitrary",)),
        name="expert_combine",
    )(eo, eo, routing, x1, mods, ln_g, ln_b)


class _Geometry:
    def __init__(self, n_ctx, dec_batch, dec_seq):
        self.n_ctx = n_ctx
        self.dec_batch = dec_batch
        self.dec_seq = dec_seq
        self.n_tok = n_ctx + dec_batch * dec_seq
        assert n_ctx % FFN_TILE == 0 and dec_seq % FFN_TILE == 0
        self.ctx_tiles = n_ctx // TOKEN_TILE
        self.lat_tiles = dec_seq // TOKEN_TILE

    def mod_index(self, i):
        return jnp.where(i < self.ctx_tiles, 0, 1 + (i - self.ctx_tiles) // self.lat_tiles)

    def rope_index(self, i):
        return jnp.where(i < self.ctx_tiles, 0, 1 + (i - self.ctx_tiles) % self.lat_tiles)


def _rope_tables(dec_seq):
    pos = jnp.arange(dec_seq, dtype=jnp.int32)
    row = (pos // GRID_W).astype(F32)
    col = (pos % GRID_W).astype(F32)
    inv = ROPE_BASE ** (-jnp.arange(ROPE_PAIRS, dtype=F32) / ROPE_PAIRS)
    ang_r = row[:, None] * inv[None, :]
    ang_c = col[:, None] * inv[None, :]
    cos_h = jnp.concatenate([jnp.cos(ang_r), jnp.cos(ang_r), jnp.cos(ang_c), jnp.cos(ang_c)], axis=-1)
    sin_h = jnp.concatenate([-jnp.sin(ang_r), jnp.sin(ang_r), -jnp.sin(ang_c), jnp.sin(ang_c)], axis=-1)
    reps = LANES // HEAD_DIM
    cos_t = jnp.concatenate([jnp.ones((TOKEN_TILE, LANES), F32), jnp.tile(cos_h, (1, reps))], axis=0)
    sin_t = jnp.concatenate([jnp.zeros((TOKEN_TILE, LANES), F32), jnp.tile(sin_h, (1, reps))], axis=0)
    return cos_t, sin_t


def kernel(x_prompt, x_sample, cache_k, cache_v, c, c_ctx, w_ada, b_ada, w_in, w_o, attn_sink, w_s, b_s, sgu_ln_g, sgu_ln_b, ln1_g, ln1_b, ln2_g, ln2_b, w_ff_gate, w_ff_up, w_ff_down, w_router, w_exp_gate, w_exp_up, w_exp_down):
    batch, seq, d = x_prompt.shape
    dec_batch, dec_seq, _ = x_sample.shape
    depth = w_in.shape[0]
    past = cache_k.shape[2]
    assert d == D_MODEL and dec_batch + 1 <= MOD_ROWS
    n_ctx = batch * seq
    n_lat = dec_batch * dec_seq
    n_tok = n_ctx + n_lat
    geom = _Geometry(n_ctx, dec_batch, dec_seq)
    alpha = float((2 * depth) ** 0.25)

    x = (x_prompt.reshape(n_ctx, d), x_sample.reshape(n_lat, d))
    cvec = jnp.concatenate([c_ctx[None, :], c, jnp.zeros((MOD_ROWS - 1 - dec_batch, d), F32)], axis=0)
    mods_all = _ada_call(cvec, w_ada, b_ada).reshape(depth, MOD_ROWS, N_MODS, d)
    cos_t, sin_t = _rope_tables(dec_seq)

    new_k, new_v = [], []
    for l in range(depth):
        mods = mods_all[l]
        q, k, v, kv32, u, g = _in_call(x, mods, w_in[l].astype(BF16), cos_t, sin_t,
                                       sgu_ln_g[l].reshape(1, SGU_WIDTH), sgu_ln_b[l].reshape(1, SGU_WIDTH), geom)
        new_k.append(kv32[:n_ctx, :KV_WIDTH].reshape(batch, seq, N_KV_HEADS, HEAD_DIM))
        new_v.append(kv32[:n_ctx, KV_WIDTH:].reshape(batch, seq, N_KV_HEADS, HEAD_DIM))
        sink = attn_sink[l]
        a_ctx = _ctx_attn_call(sink, q, k, v, batch, seq)
        a_lat = _lat_attn_call(sink, q, k, v, cache_k[:, l].reshape(dec_batch, past, KV_WIDTH).astype(BF16),
                               cache_v[:, l].reshape(dec_batch, past, KV_WIDTH).astype(BF16), geom)
        bias_full = jnp.repeat(b_s[l].T, SGU_GROUP_DIM, axis=1)
        i = l // 2
        moe = l % 2 == 1
        w_r = None
        if moe:
            w_r = jnp.pad(w_router[i], ((0, 0), (0, LANES - N_EXPERTS)))
        outs = _out_call(a_ctx, a_lat, u, g, w_s[l].astype(BF16), bias_full, x, mods, w_o[l].astype(BF16), ln1_g[l].reshape(1, d), ln1_b[l].reshape(1, d),
                         w_r, alpha, geom)
        ln_g, ln_b = ln2_g[l].reshape(1, d), ln2_b[l].reshape(1, d)
        if moe:
            x1, h, routing = outs
            plan = _route_plan(routing, n_tok, FFN_TILE)
            eo = _moe_call(h, w_exp_gate[i].astype(BF16), w_exp_up[i].astype(BF16), w_exp_down[i].astype(BF16), plan)
            x = _combine_call(eo, routing, x1, mods, ln_g, ln_b, alpha, geom, split_out=l == depth - 1)
        else:
            x1, h = outs
            x = _ffn_call(h, x1, mods, w_ff_gate[i].astype(BF16), w_ff_up[i].astype(BF16),
                          w_ff_down[i].astype(BF16), ln_g, ln_b, alpha, geom)

    if not isinstance(x, (tuple, list)):
        x = (x[:n_ctx], x[n_ctx:])
    y_prompt = x[0].reshape(batch, seq, d)
    y_sample = x[1].reshape(dec_batch, dec_seq, d)
    return (y_prompt, y_sample, jnp.stack(new_k, axis=1), jnp.stack(new_v, axis=1))
```

```python
import functools

import jax
import jax.numpy as jnp
import numpy as np
from jax import lax
from jax.experimental import pallas as pl
from jax.experimental.pallas import tpu as pltpu

F32 = jnp.float32
BF16 = jnp.bfloat16

D_MODEL = 1024
HEAD_DIM = 64
N_Q_HEADS = 8
N_KV_HEADS = 2
Q_PER_KV = N_Q_HEADS // N_KV_HEADS
ATTN_WIDTH = N_Q_HEADS * HEAD_DIM
KV_WIDTH = N_KV_HEADS * HEAD_DIM
SGU_WIDTH = D_MODEL - ATTN_WIDTH
N_SGU_GROUPS = 8
SGU_GROUP_DIM = SGU_WIDTH // N_SGU_GROUPS
CHUNK = 128
BLOCK = 128
WINDOW = 128
GRID_W = 64
IN_WIDTH = ATTN_WIDTH + 2 * KV_WIDTH + 2 * SGU_WIDTH
OFF_K = ATTN_WIDTH
OFF_V = OFF_K + KV_WIDTH
OFF_U = OFF_V + KV_WIDTH
OFF_G = OFF_U + SGU_WIDTH
N_EXPERTS = 8
TOP_K = 2
ROPE_BASE = 10000.0
ROPE_PAIRS = HEAD_DIM // 4
LN_EPS = 1e-5
ATTN_SCALE = HEAD_DIM ** -0.5
NEG_INF = -1e30
N_MODS = 6

LANES = 128
V7X_MXU_WIDTH = 256
V7X_VMEM_LIMIT_BYTES = 48 * 1024 * 1024

TOKEN_TILE = 512
FFN_TILE = 512
LAT_QUERY_ROWS = 256
MOD_ROWS = 8
ADA_COLS = 1536


def _params(sem):
    return pltpu.CompilerParams(dimension_semantics=sem, vmem_limit_bytes=V7X_VMEM_LIMIT_BYTES)


def _gelu_tanh(x):
    return x * (0.5 * (1.0 + jnp.tanh(np.float32(np.sqrt(2.0 / np.pi)) * (x + 0.044715 * (x * x * x)))))


def _sigmoid(x):
    return 1.0 / (1.0 + jnp.exp(-x))


def _layer_norm_rows(t, g, b):
    mu = jnp.mean(t, axis=-1, keepdims=True)
    tc = t - mu
    var = jnp.mean(tc * tc, axis=-1, keepdims=True)
    return tc * lax.rsqrt(var + LN_EPS) * g + b


def _rows_to_tiles(x):
    pieces = jnp.stack([x[:, s * LANES:(s + 1) * LANES] for s in range(x.shape[1] // LANES)], axis=0)
    return jnp.swapaxes(pieces, 0, 1)


def _tiles_to_rows(t):
    pieces = jnp.swapaxes(t, 0, 1)
    return jnp.concatenate([pieces[s] for s in range(t.shape[1])], axis=1)


def _stream_specs(width, ctx_tiles, tm):
    return [pl.BlockSpec((tm, width), lambda i, *_: (jnp.minimum(i, ctx_tiles - 1), 0)),
            pl.BlockSpec((tm, width), lambda i, *_: (jnp.maximum(i - ctx_tiles, 0), 0))]


def _read_stream(ctx_ref, lat_ref, ctx_tiles):
    tile = lax.broadcasted_iota(jnp.int32, ctx_ref.shape, 0) * 0 + pl.program_id(0)
    return jnp.where(tile < ctx_tiles, ctx_ref[...], lat_ref[...])


def _write_stream(ctx_ref, lat_ref, ctx_tiles, value):
    @pl.when(pl.program_id(0) < ctx_tiles)
    def _():
        ctx_ref[...] = value

    @pl.when(pl.program_id(0) >= ctx_tiles)
    def _():
        lat_ref[...] = value


def _ada_kernel(c_ref, w_ref, b_ref, o_ref):
    c = c_ref[...]
    s = (c * _sigmoid(c)).astype(BF16)
    o_ref[...] = jnp.dot(s, w_ref[...].astype(BF16), preferred_element_type=F32) + b_ref[...]


def _ada_call(cvec, w_ada, b_ada):
    depth = w_ada.shape[0]
    n_out = w_ada.shape[2]
    return pl.pallas_call(
        _ada_kernel,
        grid=(depth, n_out // ADA_COLS),
        in_specs=[
            pl.BlockSpec((MOD_ROWS, D_MODEL), lambda l, j: (0, 0)),
            pl.BlockSpec((None, D_MODEL, ADA_COLS), lambda l, j: (l, 0, j)),
            pl.BlockSpec((None, 1, ADA_COLS), lambda l, j: (l, 0, j)),
        ],
        out_specs=pl.BlockSpec((None, MOD_ROWS, ADA_COLS), lambda l, j: (l, 0, j)),
        out_shape=jax.ShapeDtypeStruct((depth, MOD_ROWS, n_out), F32),
        compiler_params=_params(("arbitrary", "arbitrary")),
        name="adaln",
    )(cvec, w_ada, b_ada.reshape(depth, 1, n_out))


def _in_kernel(split_ctx_tiles, *refs):
    if split_ctx_tiles is None:
        x = refs[0][...]
        refs = refs[1:]
    else:
        x = _read_stream(refs[0], refs[1], split_ctx_tiles)
        refs = refs[2:]
    mod_ref, w_ref, cos_ref, sin_ref, lng_ref, lnb_ref, q_ref, k_ref, v_ref, kv_ref, u_ref, g_ref = refs
    tm = x.shape[0]
    h = x * (1.0 + mod_ref[1:2, :]) + mod_ref[0:1, :]
    hb = h.astype(BF16)

    def z_block(start):
        return jnp.dot(hb, w_ref[:, start:start + V7X_MXU_WIDTH], preferred_element_type=F32)

    cos = cos_ref[...]
    sin = sin_ref[...]
    lane = lax.broadcasted_iota(jnp.int32, (tm, LANES), 1)
    first_of_pair = (lane & (2 * ROPE_PAIRS - 1)) < ROPE_PAIRS
    lower_group = lane < SGU_GROUP_DIM

    def rope(t):
        partner = jnp.where(first_of_pair, pltpu.roll(t, LANES - ROPE_PAIRS, 1), pltpu.roll(t, ROPE_PAIRS, 1))
        return t * cos + partner * sin

    halves = (slice(0, LANES), slice(LANES, 2 * LANES))
    inv_n = 1.0 / SGU_GROUP_DIM
    for j in range(SGU_WIDTH // LANES):
        cols = slice(j * LANES, (j + 1) * LANES)
        if j % 2 == 0:
            z = z_block(OFF_G + j * LANES)
        t = _gelu_tanh(z[:, halves[j % 2]])
        s_lo = jnp.sum(jnp.where(lower_group, t, 0.0), axis=-1, keepdims=True)
        s_hi = jnp.sum(jnp.where(lower_group, 0.0, t), axis=-1, keepdims=True)
        tc = t - jnp.where(lower_group, s_lo, s_hi) * inv_n
        sq = tc * tc
        v_lo = jnp.sum(jnp.where(lower_group, sq, 0.0), axis=-1, keepdims=True)
        v_hi = jnp.sum(jnp.where(lower_group, 0.0, sq), axis=-1, keepdims=True)
        var = jnp.where(lower_group, v_lo, v_hi) * inv_n
        g_ref[:, cols] = (tc * lax.rsqrt(var + LN_EPS) * lng_ref[:, cols] + lnb_ref[:, cols]).astype(BF16)

    for start in range(0, ATTN_WIDTH, V7X_MXU_WIDTH):
        z = z_block(start)
        for half in halves:
            q_ref[:, start + half.start:start + half.stop] = (rope(z[:, half]) * ATTN_SCALE).astype(BF16)
    z = z_block(OFF_K)
    k_ref[...] = rope(z[:, halves[0]]).astype(BF16)
    v_ref[...] = z[:, halves[1]].astype(BF16)
    kv_ref[...] = z
    for start in range(0, SGU_WIDTH, V7X_MXU_WIDTH):
        u_ref[:, start:start + V7X_MXU_WIDTH] = _gelu_tanh(z_block(OFF_U + start))


def _in_call(x, mods, w_in, cos_t, sin_t, ln_g, ln_b, geom):
    n = geom.n_tok
    tm = TOKEN_TILE
    mod_idx, rope_idx = geom.mod_index, geom.rope_index
    row = lambda i: (i, 0)
    split = isinstance(x, tuple)
    x_args = list(x) if split else [x]
    x_specs = _stream_specs(D_MODEL, geom.ctx_tiles, tm) if split else [pl.BlockSpec((tm, D_MODEL), row)]
    outs = pl.pallas_call(
        functools.partial(_in_kernel, geom.ctx_tiles if split else None),
        grid=(n // tm,),
        in_specs=x_specs + [
            pl.BlockSpec((None, N_MODS, D_MODEL), lambda i: (mod_idx(i), 0, 0)),
            pl.BlockSpec((D_MODEL, IN_WIDTH), lambda i: (0, 0)),
            pl.BlockSpec((tm, LANES), lambda i: (rope_idx(i), 0)),
            pl.BlockSpec((tm, LANES), lambda i: (rope_idx(i), 0)),
            pl.BlockSpec((1, SGU_WIDTH), lambda i: (0, 0)),
            pl.BlockSpec((1, SGU_WIDTH), lambda i: (0, 0)),
        ],
        out_specs=[
            pl.BlockSpec((tm, ATTN_WIDTH), row),
            pl.BlockSpec((tm, KV_WIDTH), row),
            pl.BlockSpec((tm, KV_WIDTH), row),
            pl.BlockSpec((tm, 2 * KV_WIDTH), row),
            pl.BlockSpec((tm, SGU_WIDTH), row),
            pl.BlockSpec((tm, SGU_WIDTH), row),
        ],
        out_shape=[
            jax.ShapeDtypeStruct((n, ATTN_WIDTH), BF16),
            jax.ShapeDtypeStruct((n, KV_WIDTH), BF16),
            jax.ShapeDtypeStruct((n, KV_WIDTH), BF16),
            jax.ShapeDtypeStruct((n, 2 * KV_WIDTH), F32),
            jax.ShapeDtypeStruct((n, SGU_WIDTH), F32),
            jax.ShapeDtypeStruct((n, SGU_WIDTH), BF16),
        ],
        compiler_params=_params(("arbitrary",)),
        name="in_proj",
    )(*x_args, mods, w_in, cos_t, sin_t, ln_g, ln_b)
    return outs


def _group_attention(q_ref, rows, hk, k_all, v_all, block_masks, sink_ref):
    m_rows = rows.stop - rows.start
    heads = [hk * Q_PER_KV + gq for gq in range(Q_PER_KV)]
    q = jnp.concatenate([q_ref[rows, h * HEAD_DIM:(h + 1) * HEAD_DIM] for h in heads], axis=0)
    s = lax.dot_general(q, k_all, (((1,), (1,)), ((), ())), preferred_element_type=F32)
    n_blocks = k_all.shape[0] // LANES
    blocks = [s[:, b * LANES:(b + 1) * LANES] for b in range(n_blocks)]
    for b, mask in block_masks.items():
        blocks[b] = jnp.where(mask, blocks[b], NEG_INF)
    head_of_row = lax.broadcasted_iota(jnp.int32, (Q_PER_KV * m_rows, 1), 0) // m_rows
    sink = jnp.zeros((Q_PER_KV * m_rows, 1), F32)
    for gq, h in enumerate(heads):
        sink = jnp.where(head_of_row == gq, sink_ref[h], sink)
    m_el = blocks[0]
    for blk in blocks[1:]:
        m_el = jnp.maximum(m_el, blk)
    m = jnp.maximum(jnp.max(m_el, axis=-1, keepdims=True), sink)
    probs = [jnp.exp(blk - m) for blk in blocks]
    l_el = probs[0]
    for p in probs[1:]:
        l_el = l_el + p
    denom = jnp.sum(l_el, axis=-1, keepdims=True) + jnp.exp(sink - m)
    p_all = jnp.concatenate([p.astype(BF16) for p in probs], axis=1)
    o = jnp.dot(p_all, v_all, preferred_element_type=F32) / denom
    return {h: o[gq * m_rows:(gq + 1) * m_rows] for gq, h in enumerate(heads)}


def _store_heads(o_ref, rows, outs):
    for h0 in range(0, N_Q_HEADS, 2):
        pair = jnp.concatenate([outs[h0], outs[h0 + 1]], axis=1)
        o_ref[rows, h0 * HEAD_DIM:(h0 + 2) * HEAD_DIM] = pair.astype(o_ref.dtype)


def _ctx_attn_kernel(sink_ref, q_ref, k_ref, v_ref, o_ref):
    rows = slice(0, q_ref.shape[0])
    outs = {}
    for hk in range(N_KV_HEADS):
        kv_cols = slice(hk * HEAD_DIM, (hk + 1) * HEAD_DIM)
        outs.update(_group_attention(q_ref, rows, hk, k_ref[:, kv_cols], v_ref[:, kv_cols], {}, sink_ref))
    _store_heads(o_ref, rows, outs)


def _ctx_attn_call(sink, q, k, v, n_seq, seq_len):
    n = n_seq * seq_len
    blk = lambda b: (b, 0)
    return pl.pallas_call(
        _ctx_attn_kernel,
        grid=(n_seq,),
        in_specs=[
            pl.BlockSpec(memory_space=pltpu.SMEM),
            pl.BlockSpec((seq_len, ATTN_WIDTH), blk),
            pl.BlockSpec((seq_len, KV_WIDTH), blk),
            pl.BlockSpec((seq_len, KV_WIDTH), blk),
        ],
        out_specs=pl.BlockSpec((seq_len, ATTN_WIDTH), blk),
        out_shape=jax.ShapeDtypeStruct((n, ATTN_WIDTH), BF16),
        compiler_params=_params(("arbitrary",)),
        name="ctx_attention",
    )(sink, q, k, v)


def _lat_attn_kernel(sink_ref, q_ref, k_ref, v_ref, ck_ref, cv_ref, o_ref):
    blocks_per_step = q_ref.shape[0] // BLOCK
    nb = k_ref.shape[0] // BLOCK
    past_blocks = ck_ref.shape[0] // LANES
    r = lax.broadcasted_iota(jnp.int32, (Q_PER_KV * BLOCK, BLOCK), 0) & (BLOCK - 1)
    c = lax.broadcasted_iota(jnp.int32, (Q_PER_KV * BLOCK, BLOCK), 1)
    for sub in range(blocks_per_step):
        j = pl.program_id(1) * blocks_per_step + sub
        rows = slice(sub * BLOCK, (sub + 1) * BLOCK)
        mask_prev = c >= r + jnp.where(j > 0, 0, BLOCK)
        mask_next = c <= r - jnp.where(j < nb - 1, 0, BLOCK)
        prev = pl.ds(pl.multiple_of(jnp.maximum(j - 1, 0) * BLOCK, BLOCK), BLOCK)
        cur = pl.ds(pl.multiple_of(j * BLOCK, BLOCK), BLOCK)
        nxt = pl.ds(pl.multiple_of(jnp.minimum(j + 1, nb - 1) * BLOCK, BLOCK), BLOCK)
        outs = {}
        for hk in range(N_KV_HEADS):
            kv_cols = slice(hk * HEAD_DIM, (hk + 1) * HEAD_DIM)
            k_all = jnp.concatenate([ck_ref[:, kv_cols], k_ref[prev, kv_cols], k_ref[cur, kv_cols],
                                     k_ref[nxt, kv_cols]], axis=0)
            v_all = jnp.concatenate([cv_ref[:, kv_cols], v_ref[prev, kv_cols], v_ref[cur, kv_cols],
                                     v_ref[nxt, kv_cols]], axis=0)
            masks = {past_blocks: mask_prev, past_blocks + 2: mask_next}
            outs.update(_group_attention(q_ref, rows, hk, k_all, v_all, masks, sink_ref))
        _store_heads(o_ref, rows, outs)


def _lat_attn_call(sink, q, k, v, cache_k, cache_v, geom):
    qb = LAT_QUERY_ROWS
    steps = geom.dec_seq // qb
    assert geom.n_ctx % geom.dec_seq == 0 and cache_k.shape[1] % LANES == 0
    seq_base = geom.n_ctx // geom.dec_seq
    past = cache_k.shape[1]
    seq_spec = pl.BlockSpec((geom.dec_seq, KV_WIDTH), lambda b, j: (seq_base + b, 0))
    cache_spec = pl.BlockSpec((None, past, KV_WIDTH), lambda b, j: (b, 0, 0))
    return pl.pallas_call(
        _lat_attn_kernel,
        grid=(geom.dec_batch, steps),
        in_specs=[
            pl.BlockSpec(memory_space=pltpu.SMEM),
            pl.BlockSpec((qb, ATTN_WIDTH), lambda b, j: (geom.n_ctx // qb + b * steps + j, 0)),
            seq_spec, seq_spec,
            cache_spec, cache_spec,
        ],
        out_specs=pl.BlockSpec((qb, ATTN_WIDTH), lambda b, j: (b * steps + j, 0)),
        out_shape=jax.ShapeDtypeStruct((geom.dec_batch * geom.dec_seq, ATTN_WIDTH), BF16),
        compiler_params=_params(("arbitrary", "arbitrary")),
        name="lat_attention",
    )(sink, q, k, v, cache_k, cache_v)


def _spatial_gating(u_ref, g_ref, mix_ref, bias_ref, s_ref):
    tm = u_ref.shape[0]
    lane = lax.broadcasted_iota(jnp.int32, (CHUNK, LANES), 1)
    lower_group = lane < SGU_GROUP_DIM
    for ch in range(tm // CHUNK):
        rows = slice(ch * CHUNK, (ch + 1) * CHUNK)
        for p in range(SGU_WIDTH // LANES):
            cols = slice(p * LANES, (p + 1) * LANES)
            g = g_ref[rows, cols]
            zero = jnp.zeros_like(g)
            mixed = (jnp.dot(mix_ref[2 * p], jnp.where(lower_group, g, zero), preferred_element_type=F32)
                     + jnp.dot(mix_ref[2 * p + 1], jnp.where(lower_group, zero, g), preferred_element_type=F32))
            s_ref[rows, cols] = (u_ref[rows, cols] * (mixed + bias_ref[:, cols])).astype(s_ref.dtype)


def _out_kernel(alpha, with_router, split_x, ctx_tiles, a_ctx_ref, a_lat_ref, u_ref, gg_ref, mix_ref, bias_ref,
                *refs):
    s_ref = refs[-1]
    refs = refs[:-1]
    if split_x:
        x = _read_stream(refs[0], refs[1], ctx_tiles)
        refs = refs[2:]
    else:
        x = refs[0][...]
        refs = refs[1:]
    mod_ref, wa_ref, ws_ref, g_ref, b_ref = refs[:5]
    if with_router:
        wr_ref, x1_ref, h_ref, rt_ref = refs[5:]
    else:
        x1_ref, h_ref = refs[5:]
    a = _read_stream(a_ctx_ref, a_lat_ref, ctx_tiles)
    _spatial_gating(u_ref, gg_ref, mix_ref, bias_ref, s_ref)
    y = (jnp.dot(a, wa_ref[...], preferred_element_type=F32)
         + jnp.dot(s_ref[...], ws_ref[...], preferred_element_type=F32))
    t = alpha * x + mod_ref[2:3, :] * y
    x1 = _layer_norm_rows(t, g_ref[...], b_ref[...])
    x1_ref[...] = x1
    h = x1 * (1.0 + mod_ref[4:5, :]) + mod_ref[3:4, :]
    if with_router:
        h_ref[...] = _rows_to_tiles(h)
    else:
        h_ref[...] = h.astype(h_ref.dtype)
    if with_router:
        tm = h.shape[0]
        wr = wr_ref[...]
        wr_hi = wr.astype(BF16)
        wr_lo = (wr - wr_hi.astype(F32)).astype(BF16)
        h_hi = h.astype(BF16)
        h_lo = (h - h_hi.astype(F32)).astype(BF16)
        logits = (jnp.dot(h_hi, wr_hi, preferred_element_type=F32)
                  + jnp.dot(h_lo, wr_hi, preferred_element_type=F32)
                  + jnp.dot(h_hi, wr_lo, preferred_element_type=F32))
        lane = lax.broadcasted_iota(jnp.int32, (tm, LANES), 1).astype(F32)
        neg = jnp.float32(-jnp.inf)
        lg = jnp.where(lane < N_EXPERTS, logits, neg)
        m1 = jnp.max(lg, axis=-1, keepdims=True)
        i1 = jnp.min(jnp.where(lg == m1, lane, float(LANES)), axis=-1, keepdims=True)
        lg2 = jnp.where(lane == i1, neg, lg)
        m2 = jnp.max(lg2, axis=-1, keepdims=True)
        i2 = jnp.min(jnp.where(lg2 == m2, lane, float(LANES)), axis=-1, keepdims=True)
        e2 = jnp.exp(m2 - m1)
        g1 = 1.0 / (1.0 + e2)
        g2 = e2 / (1.0 + e2)
        rt = jnp.where(lane == 0, i1, jnp.where(lane == 1, i2, jnp.where(lane == 2, g1, jnp.where(lane == 3, g2, 0.0))))
        rt_ref[...] = rt


def _out_call(a_ctx, a_lat, u, gg, w_s, bias_full, x, mods, w_o, ln_g, ln_b, w_router, alpha, geom):
    n = geom.n_tok
    tm = TOKEN_TILE
    ctx_tiles = geom.ctx_tiles
    row = lambda i: (i, 0)
    const = lambda i: (0, 0)
    with_router = w_router is not None
    split_x = isinstance(x, tuple)
    x_args = list(x) if split_x else [x]
    x_specs = _stream_specs(D_MODEL, ctx_tiles, tm) if split_x else [pl.BlockSpec((tm, D_MODEL), row)]
    sgu_specs = [
        pl.BlockSpec((tm, SGU_WIDTH), row),
        pl.BlockSpec((tm, SGU_WIDTH), row),
        pl.BlockSpec((N_SGU_GROUPS, CHUNK, CHUNK), lambda i: (0, 0, 0)),
        pl.BlockSpec((CHUNK, SGU_WIDTH), const),
    ]
    in_specs = _stream_specs(ATTN_WIDTH, ctx_tiles, tm) + sgu_specs + x_specs + [
        pl.BlockSpec((None, N_MODS, D_MODEL), lambda i: (geom.mod_index(i), 0, 0)),
        pl.BlockSpec((ATTN_WIDTH, D_MODEL), const),
        pl.BlockSpec((SGU_WIDTH, D_MODEL), lambda i: (1, 0)),
        pl.BlockSpec((1, D_MODEL), const),
        pl.BlockSpec((1, D_MODEL), const),
    ]
    args = [a_ctx, a_lat, u, gg, w_s, bias_full] + x_args + [mods, w_o, w_o, ln_g, ln_b]
    if with_router:
        h_spec = pl.BlockSpec((tm, D_MODEL // LANES, LANES), lambda i: (i, 0, 0))
        h_shape = jax.ShapeDtypeStruct((n, D_MODEL // LANES, LANES), F32)
    else:
        h_spec = pl.BlockSpec((tm, D_MODEL), row)
        h_shape = jax.ShapeDtypeStruct((n, D_MODEL), BF16)
    out_specs = [pl.BlockSpec((tm, D_MODEL), row), h_spec]
    out_shape = [jax.ShapeDtypeStruct((n, D_MODEL), F32), h_shape]
    if with_router:
        in_specs.append(pl.BlockSpec((D_MODEL, LANES), const))
        args.append(w_router)
        out_specs.append(pl.BlockSpec((tm, LANES), row))
        out_shape.append(jax.ShapeDtypeStruct((n, LANES), F32))
    return pl.pallas_call(
        functools.partial(_out_kernel, alpha, with_router, split_x, ctx_tiles),
        grid=(n // tm,),
        in_specs=in_specs,
        out_specs=out_specs,
        out_shape=out_shape,
        scratch_shapes=[pltpu.VMEM((tm, SGU_WIDTH), BF16)],
        compiler_params=_params(("arbitrary",)),
        name="out_proj_router" if with_router else "out_proj",
    )(*args)


def _swiglu_blocks(width):
    return [slice(s, min(s + V7X_MXU_WIDTH, width)) for s in range(0, width, V7X_MXU_WIDTH)]


def _swiglu_partial(x, wg_ref, wu_ref, wd_ref, side_work=None):
    out = None
    for n, cols in enumerate(_swiglu_blocks(wg_ref.shape[1])):
        a = jnp.dot(x, wg_ref[:, cols], preferred_element_type=F32)
        if side_work is not None:
            side_work(3 * n)
        b = jnp.dot(x, wu_ref[:, cols], preferred_element_type=F32)
        if side_work is not None:
            side_work(3 * n + 1)
        mid = ((a * _sigmoid(a)) * b).astype(BF16)
        part = jnp.dot(mid, wd_ref[cols, :], preferred_element_type=F32)
        if side_work is not None:
            side_work(3 * n + 2)
        out = part if out is None else out + part
    return out


def _ffn_kernel(alpha, h_ref, x1_ref, mod_ref, wg_ref, wu_ref, wd_ref, g_ref, b_ref, o_ref, acc_ref):
    c = pl.program_id(1)
    part = _swiglu_partial(h_ref[...], wg_ref, wu_ref, wd_ref)

    @pl.when(c == 0)
    def _():
        acc_ref[...] = part

    @pl.when(c > 0)
    def _():
        acc_ref[...] += part

    @pl.when(c == pl.num_programs(1) - 1)
    def _():
        t = alpha * x1_ref[...] + mod_ref[5:6, :] * acc_ref[...]
        o_ref[...] = _layer_norm_rows(t, g_ref[...], b_ref[...])


def _ffn_chunk(d_ff):
    assert d_ff % (2 * LANES) == 0
    return d_ff // 2


def _ffn_call(h, x1, mods, wg, wu, wd, ln_g, ln_b, alpha, geom):
    n = h.shape[0]
    tm = FFN_TILE
    d_ff = wg.shape[1]
    fc = _ffn_chunk(d_ff)
    ratio = tm // TOKEN_TILE
    row = lambda i, c: (i, 0)
    const = lambda i, c: (0, 0)
    return pl.pallas_call(
        functools.partial(_ffn_kernel, alpha),
        grid=(n // tm, d_ff // fc),
        in_specs=[
            pl.BlockSpec((tm, D_MODEL), row),
            pl.BlockSpec((tm, D_MODEL), row),
            pl.BlockSpec((None, N_MODS, D_MODEL), lambda i, c: (geom.mod_index(i * ratio), 0, 0)),
            pl.BlockSpec((D_MODEL, fc), lambda i, c: (0, c)),
            pl.BlockSpec((D_MODEL, fc), lambda i, c: (0, c)),
            pl.BlockSpec((fc, D_MODEL), lambda i, c: (c, 0)),
            pl.BlockSpec((1, D_MODEL), const),
            pl.BlockSpec((1, D_MODEL), const),
        ],
        out_specs=pl.BlockSpec((tm, D_MODEL), row),
        out_shape=jax.ShapeDtypeStruct((n, D_MODEL), F32),
        scratch_shapes=[pltpu.VMEM((tm, D_MODEL), F32)],
        compiler_params=_params(("arbitrary", "arbitrary")),
        name="dense_ffn",
    )(h, x1, mods, wg, wu, wd, ln_g, ln_b)


def _moe_kernel(n_chunks, te_ref, nu_ref, src_ref, dst_ref, h_hbm, wg_ref, wu_ref, wd_ref, o_hbm,
                xbuf, xb, obuf, gsem, ssem):
    del te_ref
    tr = xb.shape[0]
    t = pl.program_id(0)
    c = pl.program_id(1)
    n_tiles = pl.num_programs(0)
    valid = t < nu_ref[0]
    slot = t % 2
    other = 1 - slot
    share = tr // n_chunks
    first_row = c * share
    nxt_base = jnp.minimum(t + 1, n_tiles - 1) * tr
    prv_base = jnp.where(t == 0, n_tiles - 1, t - 1) * tr

    def gather_row(base, dst_slot, r, tok=None):
        tok = src_ref[base + r] if tok is None else tok
        return pltpu.make_async_copy(h_hbm.at[tok], xbuf.at[dst_slot, r], gsem.at[dst_slot])

    def scatter_row(base, src_slot, r, dst=None):
        dst = dst_ref[base + r] if dst is None else dst
        return pltpu.make_async_copy(obuf.at[src_slot, r], o_hbm.at[dst], ssem.at[src_slot])

    def wait_gather(dst_slot):
        pltpu.make_async_copy(h_hbm.at[pl.ds(0, tr)], xbuf.at[dst_slot], gsem.at[dst_slot]).wait()

    def wait_scatter(src_slot):
        pltpu.make_async_copy(obuf.at[src_slot], o_hbm.at[pl.ds(0, tr)], ssem.at[src_slot]).wait()

    def looped(n_rows, start_row):
        def body(r, carry):
            start_row(r)
            return carry
        lax.fori_loop(0, n_rows, body, 0, unroll=8)

    @pl.when((t == 0) & (c == 0))
    def _():
        obuf[1] = jnp.zeros(obuf.shape[1:], obuf.dtype)
        looped(tr, lambda r: gather_row(0, 0, r).start())

    @pl.when(c == 0)
    def _():
        wait_gather(slot)

        @pl.when(t >= 1)
        def _():
            wait_scatter(slot)

    def multiply_tile(slot_s, c_s):
        other_s = 1 - slot_s
        if c_s == 0:
            xb[...] = _tiles_to_rows(xbuf[slot_s]).astype(BF16)
        n_groups = 3 * len(_swiglu_blocks(wg_ref.shape[1]))

        def row_dmas(i):
            for r in range(c_s * share + share * i // n_groups, c_s * share + share * (i + 1) // n_groups):
                gather_row(nxt_base, other_s, r).start()
                scatter_row(prv_base, other_s, r).start()

        part = _swiglu_partial(xb[...], wg_ref, wu_ref, wd_ref, row_dmas)
        if c_s == 0:
            obuf[slot_s] = _rows_to_tiles(part)
        else:
            obuf[slot_s] += _rows_to_tiles(part)

    for slot_s in range(2):
        for c_s in range(n_chunks):
            pl.when(valid & (slot == slot_s) & (c == c_s))(functools.partial(multiply_tile, slot_s, c_s))

    @pl.when(jnp.logical_not(valid))
    def _():
        @pl.when(c == 0)
        def _():
            obuf[slot] = jnp.zeros(obuf.shape[1:], obuf.dtype)

        looped(share, lambda r: gather_row(nxt_base, other, first_row + r).start())
        looped(share, lambda r: scatter_row(prv_base, other, first_row + r).start())

    @pl.when((t == n_tiles - 1) & (c == n_chunks - 1))
    def _():
        wait_scatter(other)
        looped(tr, lambda r: scatter_row(t * tr, slot, r).start())
        wait_scatter(slot)
        wait_gather(other)


def _moe_call(h, wg, wu, wd, plan):
    tr = FFN_TILE
    d_ff = wg.shape[2]
    fc = _ffn_chunk(d_ff)
    n_chunks = d_ff // fc
    t_max = plan["tile_expert"].shape[0]
    row_tile = h.shape[1:]

    def chunk(t, c, nu):
        return jnp.where(t < nu[0], c, n_chunks - 1)

    grid_spec = pltpu.PrefetchScalarGridSpec(
        num_scalar_prefetch=4,
        grid=(t_max, n_chunks),
        in_specs=[
            pl.BlockSpec(memory_space=pl.ANY),
            pl.BlockSpec((None, D_MODEL, fc), lambda t, c, te, nu, src, dst: (te[t], 0, chunk(t, c, nu))),
            pl.BlockSpec((None, D_MODEL, fc), lambda t, c, te, nu, src, dst: (te[t], 0, chunk(t, c, nu))),
            pl.BlockSpec((None, fc, D_MODEL), lambda t, c, te, nu, src, dst: (te[t], chunk(t, c, nu), 0)),
        ],
        out_specs=pl.BlockSpec(memory_space=pl.ANY),
        scratch_shapes=[
            pltpu.VMEM((2, tr) + row_tile, F32),
            pltpu.VMEM((tr, D_MODEL), BF16),
            pltpu.VMEM((2, tr) + row_tile, F32),
            pltpu.SemaphoreType.DMA((2,)),
            pltpu.SemaphoreType.DMA((2,)),
        ],
    )
    return pl.pallas_call(
        functools.partial(_moe_kernel, n_chunks),
        grid_spec=grid_spec,
        out_shape=jax.ShapeDtypeStruct((t_max * tr,) + row_tile, F32),
        compiler_params=_params(("arbitrary", "arbitrary")),
        name="expert_ffn",
    )(plan["tile_expert"], plan["n_used"], plan["src_token"], plan["dst_row"], h, wg, wu, wd)


def _route_plan(routing, n_tok, tr):
    n_assign = TOP_K * n_tok
    t_max = n_assign // tr + N_EXPERTS
    n_slots = t_max * tr
    n_pad = n_slots - n_assign
    id_bits = (n_slots - 1).bit_length()
    experts = jnp.arange(N_EXPERTS, dtype=jnp.int32)
    e_flat = routing[:, :TOP_K].astype(jnp.int32).T.reshape(-1)
    counts = jnp.sum((e_flat[:, None] == experts[None, :]).astype(jnp.int32), axis=0)
    tiles_e = (counts + tr - 1) // tr
    tile_end = jnp.cumsum(tiles_e)
    n_used = tile_end[-1]
    pad_end = jnp.cumsum(tiles_e * tr - counts)
    pad_ids = jnp.arange(n_pad, dtype=jnp.int32)
    pad_expert = jnp.sum((pad_ids[:, None] >= pad_end[None, :]).astype(jnp.int32), axis=1)
    keys = jnp.concatenate([e_flat * 2, pad_expert * 2 + 1])
    item = jnp.arange(n_slots, dtype=jnp.int32)
    slot_item = jnp.sort((keys << id_bits) | item) & ((1 << id_bits) - 1)
    real = slot_item < n_assign
    src = jnp.where(real, slot_item % n_tok, 0).astype(jnp.int32)
    dst = slot_item.astype(jnp.int32)
    t_ids = jnp.arange(t_max, dtype=jnp.int32)
    te = jnp.sum((t_ids[:, None] >= tile_end[None, :]).astype(jnp.int32), axis=1)
    last_used = jnp.max(jnp.where(tiles_e > 0, experts, 0))
    te = jnp.where(t_ids < n_used, te, last_used).astype(jnp.int32)
    return {"tile_expert": te, "n_used": n_used.reshape(1).astype(jnp.int32), "src_token": src, "dst_row": dst}


def _combine_kernel(alpha, split_ctx_tiles, e1_ref, e2_ref, rt_ref, x1_ref, mod_ref, g_ref, b_ref, *o_refs):
    y = rt_ref[:, 2:3] * _tiles_to_rows(e1_ref[...]) + rt_ref[:, 3:4] * _tiles_to_rows(e2_ref[...])
    t = alpha * x1_ref[...] + mod_ref[5:6, :] * y
    out = _layer_norm_rows(t, g_ref[...], b_ref[...])
    if split_ctx_tiles is None:
        o_refs[0][...] = out
    else:
        _write_stream(o_refs[0], o_refs[1], split_ctx_tiles, out)


def _combine_call(eo, routing, x1, mods, ln_g, ln_b, alpha, geom, split_out):
    n = x1.shape[0]
    tm = TOKEN_TILE
    row = lambda i: (i, 0)
    const = lambda i: (0, 0)
    if split_out:
        out_specs = _stream_specs(D_MODEL, geom.ctx_tiles, tm)
        out_shape = [jax.ShapeDtypeStruct((geom.n_ctx, D_MODEL), F32),
                     jax.ShapeDtypeStruct((n - geom.n_ctx, D_MODEL), F32)]
    else:
        out_specs = pl.BlockSpec((tm, D_MODEL), row)
        out_shape = jax.ShapeDtypeStruct((n, D_MODEL), F32)
    return pl.pallas_call(
        functools.partial(_combine_kernel, alpha, geom.ctx_tiles if split_out else None),
        grid=(n // tm,),
        in_specs=[
            pl.BlockSpec((tm,) + eo.shape[1:], lambda i: (i, 0, 0)),
            pl.BlockSpec((tm,) + eo.shape[1:], lambda i: (i + n // tm, 0, 0)),
            pl.BlockSpec((tm, LANES), row),
            pl.BlockSpec((tm, D_MODEL), row),
            pl.BlockSpec((None, N_MODS, D_MODEL), lambda i: (geom.mod_index(i), 0, 0)),
            pl.BlockSpec((1, D_MODEL), const),
            pl.BlockSpec((1, D_MODEL), const),
        ],
        out_specs=out_specs,
        out_shape=out_shape,
        compiler_params=_params(("arbitrary",)),
        name="expert_combine",
    )(eo, eo, routing, x1, mods, ln_g, ln_b)


class _Geometry:
    def __init__(self, n_ctx, dec_batch, dec_seq):
        self.n_ctx = n_ctx
        self.dec_batch = dec_batch
        self.dec_seq = dec_seq
        self.n_tok = n_ctx + dec_batch * dec_seq
        assert n_ctx % FFN_TILE == 0 and dec_seq % FFN_TILE == 0
        self.ctx_tiles = n_ctx // TOKEN_TILE
        self.lat_tiles = dec_seq // TOKEN_TILE

    def mod_index(self, i):
        return jnp.where(i < self.ctx_tiles, 0, 1 + (i - self.ctx_tiles) // self.lat_tiles)

    def rope_index(self, i):
        return jnp.where(i < self.ctx_tiles, 0, 1 + (i - self.ctx_tiles) % self.lat_tiles)


def _rope_tables(dec_seq):
    pos = jnp.arange(dec_seq, dtype=jnp.int32)
    row = (pos // GRID_W).astype(F32)
    col = (pos % GRID_W).astype(F32)
    inv = ROPE_BASE ** (-jnp.arange(ROPE_PAIRS, dtype=F32) / ROPE_PAIRS)
    ang_r = row[:, None] * inv[None, :]
    ang_c = col[:, None] * inv[None, :]
    cos_h = jnp.concatenate([jnp.cos(ang_r), jnp.cos(ang_r), jnp.cos(ang_c), jnp.cos(ang_c)], axis=-1)
    sin_h = jnp.concatenate([-jnp.sin(ang_r), jnp.sin(ang_r), -jnp.sin(ang_c), jnp.sin(ang_c)], axis=-1)
    reps = LANES // HEAD_DIM
    cos_t = jnp.concatenate([jnp.ones((TOKEN_TILE, LANES), F32), jnp.tile(cos_h, (1, reps))], axis=0)
    sin_t = jnp.concatenate([jnp.zeros((TOKEN_TILE, LANES), F32), jnp.tile(sin_h, (1, reps))], axis=0)
    return cos_t, sin_t


def kernel(x_prompt, x_sample, cache_k, cache_v, c, c_ctx, w_ada, b_ada, w_in, w_o, attn_sink, w_s, b_s, sgu_ln_g, sgu_ln_b, ln1_g, ln1_b, ln2_g, ln2_b, w_ff_gate, w_ff_up, w_ff_down, w_router, w_exp_gate, w_exp_up, w_exp_down):
    batch, seq, d = x_prompt.shape
    dec_batch, dec_seq, _ = x_sample.shape
    depth = w_in.shape[0]
    past = cache_k.shape[2]
    assert d == D_MODEL and dec_batch + 1 <= MOD_ROWS
    n_ctx = batch * seq
    n_lat = dec_batch * dec_seq
    n_tok = n_ctx + n_lat
    geom = _Geometry(n_ctx, dec_batch, dec_seq)
    alpha = float((2 * depth) ** 0.25)

    x = (x_prompt.reshape(n_ctx, d), x_sample.reshape(n_lat, d))
    cvec = jnp.concatenate([c_ctx[None, :], c, jnp.zeros((MOD_ROWS - 1 - dec_batch, d), F32)], axis=0)
    mods_all = _ada_call(cvec, w_ada, b_ada).reshape(depth, MOD_ROWS, N_MODS, d)
    cos_t, sin_t = _rope_tables(dec_seq)

    new_k, new_v = [], []
    for l in range(depth):
        mods = mods_all[l]
        q, k, v, kv32, u, g = _in_call(x, mods, w_in[l].astype(BF16), cos_t, sin_t,
                                       sgu_ln_g[l].reshape(1, SGU_WIDTH), sgu_ln_b[l].reshape(1, SGU_WIDTH), geom)
        new_k.append(kv32[:n_ctx, :KV_WIDTH].reshape(batch, seq, N_KV_HEADS, HEAD_DIM))
        new_v.append(kv32[:n_ctx, KV_WIDTH:].reshape(batch, seq, N_KV_HEADS, HEAD_DIM))
        sink = attn_sink[l]
        a_ctx = _ctx_attn_call(sink, q, k, v, batch, seq)
        a_lat = _lat_attn_call(sink, q, k, v, cache_k[:, l].reshape(dec_batch, past, KV_WIDTH).astype(BF16),
                               cache_v[:, l].reshape(dec_batch, past, KV_WIDTH).astype(BF16), geom)
        bias_full = jnp.repeat(b_s[l].T, SGU_GROUP_DIM, axis=1)
        i = l // 2
        moe = l % 2 == 1
        w_r = None
        if moe:
            w_r = jnp.pad(w_router[i], ((0, 0), (0, LANES - N_EXPERTS)))
        outs = _out_call(a_ctx, a_lat, u, g, w_s[l].astype(BF16), bias_full, x, mods, w_o[l].astype(BF16), ln1_g[l].reshape(1, d), ln1_b[l].reshape(1, d),
                         w_r, alpha, geom)
        ln_g, ln_b = ln2_g[l].reshape(1, d), ln2_b[l].reshape(1, d)
        if moe:
            x1, h, routing = outs
            plan = _route_plan(routing, n_tok, FFN_TILE)
            eo = _moe_call(h, w_exp_gate[i].astype(BF16), w_exp_up[i].astype(BF16), w_exp_down[i].astype(BF16), plan)
            x = _combine_call(eo, routing, x1, mods, ln_g, ln_b, alpha, geom, split_out=l == depth - 1)
        else:
            x1, h = outs
            x = _ffn_call(h, x1, mods, w_ff_gate[i].astype(BF16), w_ff_up[i].astype(BF16),
                          w_ff_down[i].astype(BF16), ln_g, ln_b, alpha, geom)

    if not isinstance(x, (tuple, list)):
        x = (x[:n_ctx], x[n_ctx:])
    y_prompt = x[0].reshape(batch, seq, d)
    y_sample = x[1].reshape(dec_batch, dec_seq, d)
    return (y_prompt, y_sample, jnp.stack(new_k, axis=1), jnp.stack(new_v, axis=1))
```

```python
import functools

import jax
import jax.numpy as jnp
import numpy as np
from jax import lax
from jax.experimental import pallas as pl
from jax.experimental.pallas import tpu as pltpu

F32 = jnp.float32
BF16 = jnp.bfloat16

D_MODEL = 1024
HEAD_DIM = 64
N_Q_HEADS = 8
N_KV_HEADS = 2
Q_PER_KV = N_Q_HEADS // N_KV_HEADS
ATTN_WIDTH = N_Q_HEADS * HEAD_DIM
KV_WIDTH = N_KV_HEADS * HEAD_DIM
SGU_WIDTH = D_MODEL - ATTN_WIDTH
N_SGU_GROUPS = 8
SGU_GROUP_DIM = SGU_WIDTH // N_SGU_GROUPS
CHUNK = 128
BLOCK = 128
WINDOW = 128
GRID_W = 64
IN_WIDTH = ATTN_WIDTH + 2 * KV_WIDTH + 2 * SGU_WIDTH
OFF_K = ATTN_WIDTH
OFF_V = OFF_K + KV_WIDTH
OFF_U = OFF_V + KV_WIDTH
OFF_G = OFF_U + SGU_WIDTH
N_EXPERTS = 8
TOP_K = 2
ROPE_BASE = 10000.0
ROPE_PAIRS = HEAD_DIM // 4
LN_EPS = 1e-5
ATTN_SCALE = HEAD_DIM ** -0.5
NEG_INF = -1e30
N_MODS = 6

LANES = 128
V7X_MXU_WIDTH = 256
V7X_VMEM_LIMIT_BYTES = 48 * 1024 * 1024

IN_TILE = 1024
OUT_TILE = 512
ROUTER_TILE = 256
COMBINE_TILE = 512
FFN_TILE = 512
LAT_QUERY_ROWS = 256
MOD_ROWS = 8
ADA_COLS = 1536


def _params(sem):
    return pltpu.CompilerParams(dimension_semantics=sem, vmem_limit_bytes=V7X_VMEM_LIMIT_BYTES)


def _gelu_tanh(x):
    return x * (0.5 * (1.0 + jnp.tanh(np.float32(np.sqrt(2.0 / np.pi)) * (x + 0.044715 * (x * x * x)))))


def _sigmoid(x):
    return 1.0 / (1.0 + jnp.exp(-x))


def _layer_norm_rows(t, g, b):
    mu = jnp.mean(t, axis=-1, keepdims=True)
    tc = t - mu
    var = jnp.mean(tc * tc, axis=-1, keepdims=True)
    return tc * lax.rsqrt(var + LN_EPS) * g + b


def _rows_to_tiles(x):
    pieces = jnp.stack([x[:, s * LANES:(s + 1) * LANES] for s in range(x.shape[1] // LANES)], axis=0)
    return jnp.swapaxes(pieces, 0, 1)


def _tiles_to_rows(t):
    pieces = jnp.swapaxes(t, 0, 1)
    return jnp.concatenate([pieces[s] for s in range(t.shape[1])], axis=1)


def _stream_specs(width, ctx_tiles, tm):
    return [pl.BlockSpec((tm, width), lambda i, *_: (jnp.minimum(i, ctx_tiles - 1), 0)),
            pl.BlockSpec((tm, width), lambda i, *_: (jnp.maximum(i - ctx_tiles, 0), 0))]


def _read_stream(ctx_ref, lat_ref, ctx_tiles):
    tile = lax.broadcasted_iota(jnp.int32, ctx_ref.shape, 0) * 0 + pl.program_id(0)
    return jnp.where(tile < ctx_tiles, ctx_ref[...], lat_ref[...])


def _write_stream(ctx_ref, lat_ref, ctx_tiles, value):
    @pl.when(pl.program_id(0) < ctx_tiles)
    def _():
        ctx_ref[...] = value

    @pl.when(pl.program_id(0) >= ctx_tiles)
    def _():
        lat_ref[...] = value


def _ada_kernel(c_ref, w_ref, b_ref, o_ref):
    c = c_ref[...]
    s = (c * _sigmoid(c)).astype(BF16)
    o_ref[...] = jnp.dot(s, w_ref[...].astype(BF16), preferred_element_type=F32) + b_ref[...]


def _ada_call(cvec, w_ada, b_ada):
    depth = w_ada.shape[0]
    n_out = w_ada.shape[2]
    return pl.pallas_call(
        _ada_kernel,
        grid=(depth, n_out // ADA_COLS),
        in_specs=[
            pl.BlockSpec((MOD_ROWS, D_MODEL), lambda l, j: (0, 0)),
            pl.BlockSpec((None, D_MODEL, ADA_COLS), lambda l, j: (l, 0, j)),
            pl.BlockSpec((None, 1, ADA_COLS), lambda l, j: (l, 0, j)),
        ],
        out_specs=pl.BlockSpec((None, MOD_ROWS, ADA_COLS), lambda l, j: (l, 0, j)),
        out_shape=jax.ShapeDtypeStruct((depth, MOD_ROWS, n_out), F32),
        compiler_params=_params(("arbitrary", "arbitrary")),
        name="adaln",
    )(cvec, w_ada, b_ada.reshape(depth, 1, n_out))


def _in_kernel(split_ctx_tiles, *refs):
    if split_ctx_tiles is None:
        x = refs[0][...]
        refs = refs[1:]
    else:
        x = _read_stream(refs[0], refs[1], split_ctx_tiles)
        refs = refs[2:]
    mod_ref, w_ref, cos_ref, sin_ref, lng_ref, lnb_ref, q_ref, k_ref, v_ref, kv_ref, u_ref, g_ref = refs
    tm = x.shape[0]
    h = x * (1.0 + mod_ref[1:2, :]) + mod_ref[0:1, :]
    hb = h.astype(BF16)

    def z_block(start):
        return jnp.dot(hb, w_ref[:, start:start + V7X_MXU_WIDTH], preferred_element_type=F32)

    cos = cos_ref[...]
    sin = sin_ref[...]
    lane = lax.broadcasted_iota(jnp.int32, (tm, LANES), 1)
    first_of_pair = (lane & (2 * ROPE_PAIRS - 1)) < ROPE_PAIRS
    lower_group = lane < SGU_GROUP_DIM

    def rope(t):
        partner = jnp.where(first_of_pair, pltpu.roll(t, LANES - ROPE_PAIRS, 1), pltpu.roll(t, ROPE_PAIRS, 1))
        return t * cos + partner * sin

    halves = (slice(0, LANES), slice(LANES, 2 * LANES))
    inv_n = 1.0 / SGU_GROUP_DIM
    for j in range(SGU_WIDTH // LANES):
        cols = slice(j * LANES, (j + 1) * LANES)
        if j % 2 == 0:
            z = z_block(OFF_G + j * LANES)
        t = _gelu_tanh(z[:, halves[j % 2]])
        s_lo = jnp.sum(jnp.where(lower_group, t, 0.0), axis=-1, keepdims=True)
        s_hi = jnp.sum(jnp.where(lower_group, 0.0, t), axis=-1, keepdims=True)
        tc = t - jnp.where(lower_group, s_lo, s_hi) * inv_n
        sq = tc * tc
        v_lo = jnp.sum(jnp.where(lower_group, sq, 0.0), axis=-1, keepdims=True)
        v_hi = jnp.sum(jnp.where(lower_group, 0.0, sq), axis=-1, keepdims=True)
        var = jnp.where(lower_group, v_lo, v_hi) * inv_n
        g_ref[:, cols] = (tc * lax.rsqrt(var + LN_EPS) * lng_ref[:, cols] + lnb_ref[:, cols]).astype(BF16)

    for start in range(0, ATTN_WIDTH, V7X_MXU_WIDTH):
        z = z_block(start)
        for half in halves:
            q_ref[:, start + half.start:start + half.stop] = (rope(z[:, half]) * ATTN_SCALE).astype(BF16)
    z = z_block(OFF_K)
    k_ref[...] = rope(z[:, halves[0]]).astype(BF16)
    v_ref[...] = z[:, halves[1]].astype(BF16)
    kv_ref[...] = z
    for start in range(0, SGU_WIDTH, V7X_MXU_WIDTH):
        u_ref[:, start:start + V7X_MXU_WIDTH] = _gelu_tanh(z_block(OFF_U + start))


def _in_call(x, mods, w_in, cos_t, sin_t, ln_g, ln_b, geom):
    n = geom.n_tok
    tm = IN_TILE
    mod_idx = lambda i: geom.mod_index(i, tm)
    rope_idx = lambda i: geom.rope_index(i, tm)
    row = lambda i: (i, 0)
    split = isinstance(x, tuple)
    x_args = list(x) if split else [x]
    x_specs = _stream_specs(D_MODEL, geom.ctx_tiles(tm), tm) if split else [pl.BlockSpec((tm, D_MODEL), row)]
    outs = pl.pallas_call(
        functools.partial(_in_kernel, geom.ctx_tiles(tm) if split else None),
        grid=(n // tm,),
        in_specs=x_specs + [
            pl.BlockSpec((None, N_MODS, D_MODEL), lambda i: (mod_idx(i), 0, 0)),
            pl.BlockSpec((D_MODEL, IN_WIDTH), lambda i: (0, 0)),
            pl.BlockSpec((tm, LANES), lambda i: (rope_idx(i), 0)),
            pl.BlockSpec((tm, LANES), lambda i: (rope_idx(i), 0)),
            pl.BlockSpec((1, SGU_WIDTH), lambda i: (0, 0)),
            pl.BlockSpec((1, SGU_WIDTH), lambda i: (0, 0)),
        ],
        out_specs=[
            pl.BlockSpec((tm, ATTN_WIDTH), row),
            pl.BlockSpec((tm, KV_WIDTH), row),
            pl.BlockSpec((tm, KV_WIDTH), row),
            pl.BlockSpec((tm, 2 * KV_WIDTH), row),
            pl.BlockSpec((tm, SGU_WIDTH), row),
            pl.BlockSpec((tm, SGU_WIDTH), row),
        ],
        out_shape=[
            jax.ShapeDtypeStruct((n, ATTN_WIDTH), BF16),
            jax.ShapeDtypeStruct((n, KV_WIDTH), BF16),
            jax.ShapeDtypeStruct((n, KV_WIDTH), BF16),
            jax.ShapeDtypeStruct((n, 2 * KV_WIDTH), F32),
            jax.ShapeDtypeStruct((n, SGU_WIDTH), F32),
            jax.ShapeDtypeStruct((n, SGU_WIDTH), BF16),
        ],
        compiler_params=_params(("arbitrary",)),
        name="in_proj",
    )(*x_args, mods, w_in, cos_t, sin_t, ln_g, ln_b)
    return outs


def _group_attention(q_ref, rows, hk, k_all, v_all, block_masks, sink_ref):
    m_rows = rows.stop - rows.start
    heads = [hk * Q_PER_KV + gq for gq in range(Q_PER_KV)]
    q = jnp.concatenate([q_ref[rows, h * HEAD_DIM:(h + 1) * HEAD_DIM] for h in heads], axis=0)
    s = lax.dot_general(q, k_all, (((1,), (1,)), ((), ())), preferred_element_type=F32)
    n_blocks = k_all.shape[0] // LANES
    blocks = [s[:, b * LANES:(b + 1) * LANES] for b in range(n_blocks)]
    for b, mask in block_masks.items():
        blocks[b] = jnp.where(mask, blocks[b], NEG_INF)
    head_of_row = lax.broadcasted_iota(jnp.int32, (Q_PER_KV * m_rows, 1), 0) // m_rows
    sink = jnp.zeros((Q_PER_KV * m_rows, 1), F32)
    for gq, h in enumerate(heads):
        sink = jnp.where(head_of_row == gq, sink_ref[h], sink)
    m_el = blocks[0]
    for blk in blocks[1:]:
        m_el = jnp.maximum(m_el, blk)
    m = jnp.maximum(jnp.max(m_el, axis=-1, keepdims=True), sink)
    probs = [jnp.exp(blk - m) for blk in blocks]
    l_el = probs[0]
    for p in probs[1:]:
        l_el = l_el + p
    denom = jnp.sum(l_el, axis=-1, keepdims=True) + jnp.exp(sink - m)
    p_all = jnp.concatenate([p.astype(BF16) for p in probs], axis=1)
    o = jnp.dot(p_all, v_all, preferred_element_type=F32) / denom
    return {h: o[gq * m_rows:(gq + 1) * m_rows] for gq, h in enumerate(heads)}


def _store_heads(o_ref, rows, outs):
    for h0 in range(0, N_Q_HEADS, 2):
        pair = jnp.concatenate([outs[h0], outs[h0 + 1]], axis=1)
        o_ref[rows, h0 * HEAD_DIM:(h0 + 2) * HEAD_DIM] = pair.astype(o_ref.dtype)


def _ctx_attn_kernel(sink_ref, q_ref, k_ref, v_ref, o_ref):
    rows = slice(0, q_ref.shape[0])
    outs = {}
    for hk in range(N_KV_HEADS):
        kv_cols = slice(hk * HEAD_DIM, (hk + 1) * HEAD_DIM)
        outs.update(_group_attention(q_ref, rows, hk, k_ref[:, kv_cols], v_ref[:, kv_cols], {}, sink_ref))
    _store_heads(o_ref, rows, outs)


def _ctx_attn_call(sink, q, k, v, n_seq, seq_len):
    n = n_seq * seq_len
    blk = lambda b: (b, 0)
    return pl.pallas_call(
        _ctx_attn_kernel,
        grid=(n_seq,),
        in_specs=[
            pl.BlockSpec(memory_space=pltpu.SMEM),
            pl.BlockSpec((seq_len, ATTN_WIDTH), blk),
            pl.BlockSpec((seq_len, KV_WIDTH), blk),
            pl.BlockSpec((seq_len, KV_WIDTH), blk),
        ],
        out_specs=pl.BlockSpec((seq_len, ATTN_WIDTH), blk),
        out_shape=jax.ShapeDtypeStruct((n, ATTN_WIDTH), BF16),
        compiler_params=_params(("arbitrary",)),
        name="ctx_attention",
    )(sink, q, k, v)


def _lat_attn_kernel(sink_ref, q_ref, k_ref, v_ref, ck_ref, cv_ref, o_ref):
    blocks_per_step = q_ref.shape[0] // BLOCK
    nb = k_ref.shape[0] // BLOCK
    past_blocks = ck_ref.shape[0] // LANES
    r = lax.broadcasted_iota(jnp.int32, (Q_PER_KV * BLOCK, BLOCK), 0) & (BLOCK - 1)
    c = lax.broadcasted_iota(jnp.int32, (Q_PER_KV * BLOCK, BLOCK), 1)
    for sub in range(blocks_per_step):
        j = pl.program_id(1) * blocks_per_step + sub
        rows = slice(sub * BLOCK, (sub + 1) * BLOCK)
        mask_prev = c >= r + jnp.where(j > 0, 0, BLOCK)
        mask_next = c <= r - jnp.where(j < nb - 1, 0, BLOCK)
        prev = pl.ds(pl.multiple_of(jnp.maximum(j - 1, 0) * BLOCK, BLOCK), BLOCK)
        cur = pl.ds(pl.multiple_of(j * BLOCK, BLOCK), BLOCK)
        nxt = pl.ds(pl.multiple_of(jnp.minimum(j + 1, nb - 1) * BLOCK, BLOCK), BLOCK)
        outs = {}
        for hk in range(N_KV_HEADS):
            kv_cols = slice(hk * HEAD_DIM, (hk + 1) * HEAD_DIM)
            k_all = jnp.concatenate([ck_ref[:, kv_cols], k_ref[prev, kv_cols], k_ref[cur, kv_cols],
                                     k_ref[nxt, kv_cols]], axis=0)
            v_all = jnp.concatenate([cv_ref[:, kv_cols], v_ref[prev, kv_cols], v_ref[cur, kv_cols],
                                     v_ref[nxt, kv_cols]], axis=0)
            masks = {past_blocks: mask_prev, past_blocks + 2: mask_next}
            outs.update(_group_attention(q_ref, rows, hk, k_all, v_all, masks, sink_ref))
        _store_heads(o_ref, rows, outs)


def _lat_attn_call(sink, q, k, v, cache_k, cache_v, geom):
    qb = LAT_QUERY_ROWS
    steps = geom.dec_seq // qb
    assert geom.n_ctx % geom.dec_seq == 0 and cache_k.shape[1] % LANES == 0
    seq_base = geom.n_ctx // geom.dec_seq
    past = cache_k.shape[1]
    seq_spec = pl.BlockSpec((geom.dec_seq, KV_WIDTH), lambda b, j: (seq_base + b, 0))
    cache_spec = pl.BlockSpec((None, past, KV_WIDTH), lambda b, j: (b, 0, 0))
    return pl.pallas_call(
        _lat_attn_kernel,
        grid=(geom.dec_batch, steps),
        in_specs=[
            pl.BlockSpec(memory_space=pltpu.SMEM),
            pl.BlockSpec((qb, ATTN_WIDTH), lambda b, j: (geom.n_ctx // qb + b * steps + j, 0)),
            seq_spec, seq_spec,
            cache_spec, cache_spec,
        ],
        out_specs=pl.BlockSpec((qb, ATTN_WIDTH), lambda b, j: (b * steps + j, 0)),
        out_shape=jax.ShapeDtypeStruct((geom.dec_batch * geom.dec_seq, ATTN_WIDTH), BF16),
        compiler_params=_params(("arbitrary", "arbitrary")),
        name="lat_attention",
    )(sink, q, k, v, cache_k, cache_v)


def _spatial_gating(u_ref, g_ref, mix_ref, bias_ref, s_ref):
    tm = u_ref.shape[0]
    lane = lax.broadcasted_iota(jnp.int32, (CHUNK, LANES), 1)
    lower_group = lane < SGU_GROUP_DIM
    for ch in range(tm // CHUNK):
        rows = slice(ch * CHUNK, (ch + 1) * CHUNK)
        for p in range(SGU_WIDTH // LANES):
            cols = slice(p * LANES, (p + 1) * LANES)
            g = g_ref[rows, cols]
            zero = jnp.zeros_like(g)
            mixed = (jnp.dot(mix_ref[2 * p], jnp.where(lower_group, g, zero), preferred_element_type=F32)
                     + jnp.dot(mix_ref[2 * p + 1], jnp.where(lower_group, zero, g), preferred_element_type=F32))
            s_ref[rows, cols] = (u_ref[rows, cols] * (mixed + bias_ref[:, cols])).astype(s_ref.dtype)


def _out_kernel(alpha, with_router, split_x, ctx_tiles, a_ctx_ref, a_lat_ref, u_ref, gg_ref, mix_ref, bias_ref,
                *refs):
    s_ref = refs[-1]
    refs = refs[:-1]
    if split_x:
        x = _read_stream(refs[0], refs[1], ctx_tiles)
        refs = refs[2:]
    else:
        x = refs[0][...]
        refs = refs[1:]
    mod_ref, wa_ref, ws_ref, g_ref, b_ref = refs[:5]
    if with_router:
        wr_ref, x1_ref, h_ref, rt_ref = refs[5:]
    else:
        x1_ref, h_ref = refs[5:]
    a = _read_stream(a_ctx_ref, a_lat_ref, ctx_tiles)
    _spatial_gating(u_ref, gg_ref, mix_ref, bias_ref, s_ref)
    y = (jnp.dot(a, wa_ref[...], preferred_element_type=F32)
         + jnp.dot(s_ref[...], ws_ref[...], preferred_element_type=F32))
    t = alpha * x + mod_ref[2:3, :] * y
    x1 = _layer_norm_rows(t, g_ref[...], b_ref[...])
    x1_ref[...] = x1
    h = x1 * (1.0 + mod_ref[4:5, :]) + mod_ref[3:4, :]
    if with_router:
        h_ref[...] = _rows_to_tiles(h)
    else:
        h_ref[...] = h.astype(h_ref.dtype)
    if with_router:
        tm = h.shape[0]
        wr = wr_ref[...]
        wr_hi = wr.astype(BF16)
        wr_lo = (wr - wr_hi.astype(F32)).astype(BF16)
        h_hi = h.astype(BF16)
        h_lo = (h - h_hi.astype(F32)).astype(BF16)
        logits = (jnp.dot(h_hi, wr_hi, preferred_element_type=F32)
                  + jnp.dot(h_lo, wr_hi, preferred_element_type=F32)
                  + jnp.dot(h_hi, wr_lo, preferred_element_type=F32))
        lane = lax.broadcasted_iota(jnp.int32, (tm, LANES), 1).astype(F32)
        neg = jnp.float32(-jnp.inf)
        lg = jnp.where(lane < N_EXPERTS, logits, neg)
        m1 = jnp.max(lg, axis=-1, keepdims=True)
        i1 = jnp.min(jnp.where(lg == m1, lane, float(LANES)), axis=-1, keepdims=True)
        lg2 = jnp.where(lane == i1, neg, lg)
        m2 = jnp.max(lg2, axis=-1, keepdims=True)
        i2 = jnp.min(jnp.where(lg2 == m2, lane, float(LANES)), axis=-1, keepdims=True)
        e2 = jnp.exp(m2 - m1)
        g1 = 1.0 / (1.0 + e2)
        g2 = e2 / (1.0 + e2)
        rt = jnp.where(lane == 0, i1, jnp.where(lane == 1, i2, jnp.where(lane == 2, g1, jnp.where(lane == 3, g2, 0.0))))
        rt_ref[...] = rt


def _out_call(a_ctx, a_lat, u, gg, w_s, bias_full, x, mods, w_o, ln_g, ln_b, w_router, alpha, geom):
    n = geom.n_tok
    with_router = w_router is not None
    tm = ROUTER_TILE if with_router else OUT_TILE
    ctx_tiles = geom.ctx_tiles(tm)
    row = lambda i: (i, 0)
    const = lambda i: (0, 0)
    split_x = isinstance(x, tuple)
    x_args = list(x) if split_x else [x]
    x_specs = _stream_specs(D_MODEL, ctx_tiles, tm) if split_x else [pl.BlockSpec((tm, D_MODEL), row)]
    sgu_specs = [
        pl.BlockSpec((tm, SGU_WIDTH), row),
        pl.BlockSpec((tm, SGU_WIDTH), row),
        pl.BlockSpec((N_SGU_GROUPS, CHUNK, CHUNK), lambda i: (0, 0, 0)),
        pl.BlockSpec((CHUNK, SGU_WIDTH), const),
    ]
    in_specs = _stream_specs(ATTN_WIDTH, ctx_tiles, tm) + sgu_specs + x_specs + [
        pl.BlockSpec((None, N_MODS, D_MODEL), lambda i: (geom.mod_index(i, tm), 0, 0)),
        pl.BlockSpec((ATTN_WIDTH, D_MODEL), const),
        pl.BlockSpec((SGU_WIDTH, D_MODEL), lambda i: (1, 0)),
        pl.BlockSpec((1, D_MODEL), const),
        pl.BlockSpec((1, D_MODEL), const),
    ]
    args = [a_ctx, a_lat, u, gg, w_s, bias_full] + x_args + [mods, w_o, w_o, ln_g, ln_b]
    if with_router:
        h_spec = pl.BlockSpec((tm, D_MODEL // LANES, LANES), lambda i: (i, 0, 0))
        h_shape = jax.ShapeDtypeStruct((n, D_MODEL // LANES, LANES), F32)
    else:
        h_spec = pl.BlockSpec((tm, D_MODEL), row)
        h_shape = jax.ShapeDtypeStruct((n, D_MODEL), BF16)
    out_specs = [pl.BlockSpec((tm, D_MODEL), row), h_spec]
    out_shape = [jax.ShapeDtypeStruct((n, D_MODEL), F32), h_shape]
    if with_router:
        in_specs.append(pl.BlockSpec((D_MODEL, LANES), const))
        args.append(w_router)
        out_specs.append(pl.BlockSpec((tm, LANES), row))
        out_shape.append(jax.ShapeDtypeStruct((n, LANES), F32))
    return pl.pallas_call(
        functools.partial(_out_kernel, alpha, with_router, split_x, ctx_tiles),
        grid=(n // tm,),
        in_specs=in_specs,
        out_specs=out_specs,
        out_shape=out_shape,
        scratch_shapes=[pltpu.VMEM((tm, SGU_WIDTH), BF16)],
        compiler_params=_params(("arbitrary",)),
        name="out_proj_router" if with_router else "out_proj",
    )(*args)


def _swiglu_blocks(width):
    return [slice(s, min(s + V7X_MXU_WIDTH, width)) for s in range(0, width, V7X_MXU_WIDTH)]


def _swiglu_partial(x, wg_ref, wu_ref, wd_ref, side_work=None):
    out = None
    for n, cols in enumerate(_swiglu_blocks(wg_ref.shape[1])):
        a = jnp.dot(x, wg_ref[:, cols], preferred_element_type=F32)
        if side_work is not None:
            side_work(3 * n)
        b = jnp.dot(x, wu_ref[:, cols], preferred_element_type=F32)
        if side_work is not None:
            side_work(3 * n + 1)
        mid = ((a * _sigmoid(a)) * b).astype(BF16)
        part = jnp.dot(mid, wd_ref[cols, :], preferred_element_type=F32)
        if side_work is not None:
            side_work(3 * n + 2)
        out = part if out is None else out + part
    return out


def _ffn_kernel(alpha, h_ref, x1_ref, mod_ref, wg_ref, wu_ref, wd_ref, g_ref, b_ref, o_ref, acc_ref):
    c = pl.program_id(1)
    part = _swiglu_partial(h_ref[...], wg_ref, wu_ref, wd_ref)

    @pl.when(c == 0)
    def _():
        acc_ref[...] = part

    @pl.when(c > 0)
    def _():
        acc_ref[...] += part

    @pl.when(c == pl.num_programs(1) - 1)
    def _():
        t = alpha * x1_ref[...] + mod_ref[5:6, :] * acc_ref[...]
        o_ref[...] = _layer_norm_rows(t, g_ref[...], b_ref[...])


def _ffn_chunk(d_ff):
    assert d_ff % (2 * LANES) == 0
    return d_ff // 2


def _ffn_call(h, x1, mods, wg, wu, wd, ln_g, ln_b, alpha, geom):
    n = h.shape[0]
    tm = FFN_TILE
    d_ff = wg.shape[1]
    fc = _ffn_chunk(d_ff)
    row = lambda i, c: (i, 0)
    const = lambda i, c: (0, 0)
    return pl.pallas_call(
        functools.partial(_ffn_kernel, alpha),
        grid=(n // tm, d_ff // fc),
        in_specs=[
            pl.BlockSpec((tm, D_MODEL), row),
            pl.BlockSpec((tm, D_MODEL), row),
            pl.BlockSpec((None, N_MODS, D_MODEL), lambda i, c: (geom.mod_index(i, tm), 0, 0)),
            pl.BlockSpec((D_MODEL, fc), lambda i, c: (0, c)),
            pl.BlockSpec((D_MODEL, fc), lambda i, c: (0, c)),
            pl.BlockSpec((fc, D_MODEL), lambda i, c: (c, 0)),
            pl.BlockSpec((1, D_MODEL), const),
            pl.BlockSpec((1, D_MODEL), const),
        ],
        out_specs=pl.BlockSpec((tm, D_MODEL), row),
        out_shape=jax.ShapeDtypeStruct((n, D_MODEL), F32),
        scratch_shapes=[pltpu.VMEM((tm, D_MODEL), F32)],
        compiler_params=_params(("arbitrary", "arbitrary")),
        name="dense_ffn",
    )(h, x1, mods, wg, wu, wd, ln_g, ln_b)


def _moe_kernel(n_chunks, te_ref, nu_ref, src_ref, dst_ref, h_hbm, wg_ref, wu_ref, wd_ref, o_hbm,
                xbuf, xb, obuf, gsem, ssem):
    del te_ref
    tr = xb.shape[0]
    t = pl.program_id(0)
    c = pl.program_id(1)
    n_tiles = pl.num_programs(0)
    valid = t < nu_ref[0]
    slot = t % 2
    other = 1 - slot
    share = tr // n_chunks
    first_row = c * share
    nxt_base = jnp.minimum(t + 1, n_tiles - 1) * tr
    prv_base = jnp.where(t == 0, n_tiles - 1, t - 1) * tr

    def gather_row(base, dst_slot, r, tok=None):
        tok = src_ref[base + r] if tok is None else tok
        return pltpu.make_async_copy(h_hbm.at[tok], xbuf.at[dst_slot, r], gsem.at[dst_slot])

    def scatter_row(base, src_slot, r, dst=None):
        dst = dst_ref[base + r] if dst is None else dst
        return pltpu.make_async_copy(obuf.at[src_slot, r], o_hbm.at[dst], ssem.at[src_slot])

    def wait_gather(dst_slot):
        pltpu.make_async_copy(h_hbm.at[pl.ds(0, tr)], xbuf.at[dst_slot], gsem.at[dst_slot]).wait()

    def wait_scatter(src_slot):
        pltpu.make_async_copy(obuf.at[src_slot], o_hbm.at[pl.ds(0, tr)], ssem.at[src_slot]).wait()

    def looped(n_rows, start_row):
        def body(r, carry):
            start_row(r)
            return carry
        lax.fori_loop(0, n_rows, body, 0, unroll=8)

    @pl.when((t == 0) & (c == 0))
    def _():
        obuf[1] = jnp.zeros(obuf.shape[1:], obuf.dtype)
        looped(tr, lambda r: gather_row(0, 0, r).start())

    @pl.when(c == 0)
    def _():
        wait_gather(slot)

        @pl.when(t >= 1)
        def _():
            wait_scatter(slot)

    def multiply_tile(slot_s, c_s):
        other_s = 1 - slot_s
        if c_s == 0:
            xb[...] = _tiles_to_rows(xbuf[slot_s]).astype(BF16)
        n_groups = 3 * len(_swiglu_blocks(wg_ref.shape[1]))

        def row_dmas(i):
            for r in range(c_s * share + share * i // n_groups, c_s * share + share * (i + 1) // n_groups):
                gather_row(nxt_base, other_s, r).start()
                scatter_row(prv_base, other_s, r).start()

        part = _swiglu_partial(xb[...], wg_ref, wu_ref, wd_ref, row_dmas)
        if c_s == 0:
            obuf[slot_s] = _rows_to_tiles(part)
        else:
            obuf[slot_s] += _rows_to_tiles(part)

    for slot_s in range(2):
        for c_s in range(n_chunks):
            pl.when(valid & (slot == slot_s) & (c == c_s))(functools.partial(multiply_tile, slot_s, c_s))

    @pl.when(jnp.logical_not(valid))
    def _():
        @pl.when(c == 0)
        def _():
            obuf[slot] = jnp.zeros(obuf.shape[1:], obuf.dtype)

        looped(share, lambda r: gather_row(nxt_base, other, first_row + r).start())
        looped(share, lambda r: scatter_row(prv_base, other, first_row + r).start())

    @pl.when((t == n_tiles - 1) & (c == n_chunks - 1))
    def _():
        wait_scatter(other)
        looped(tr, lambda r: scatter_row(t * tr, slot, r).start())
        wait_scatter(slot)
        wait_gather(other)


def _moe_call(h, wg, wu, wd, plan):
    tr = FFN_TILE
    d_ff = wg.shape[2]
    fc = _ffn_chunk(d_ff)
    n_chunks = d_ff // fc
    t_max = plan["tile_expert"].shape[0]
    row_tile = h.shape[1:]

    def chunk(t, c, nu):
        return jnp.where(t < nu[0], c, n_chunks - 1)

    grid_spec = pltpu.PrefetchScalarGridSpec(
        num_scalar_prefetch=4,
        grid=(t_max, n_chunks),
        in_specs=[
            pl.BlockSpec(memory_space=pl.ANY),
            pl.BlockSpec((None, D_MODEL, fc), lambda t, c, te, nu, src, dst: (te[t], 0, chunk(t, c, nu))),
            pl.BlockSpec((None, D_MODEL, fc), lambda t, c, te, nu, src, dst: (te[t], 0, chunk(t, c, nu))),
            pl.BlockSpec((None, fc, D_MODEL), lambda t, c, te, nu, src, dst: (te[t], chunk(t, c, nu), 0)),
        ],
        out_specs=pl.BlockSpec(memory_space=pl.ANY),
        scratch_shapes=[
            pltpu.VMEM((2, tr) + row_tile, F32),
            pltpu.VMEM((tr, D_MODEL), BF16),
            pltpu.VMEM((2, tr) + row_tile, F32),
            pltpu.SemaphoreType.DMA((2,)),
            pltpu.SemaphoreType.DMA((2,)),
        ],
    )
    return pl.pallas_call(
        functools.partial(_moe_kernel, n_chunks),
        grid_spec=grid_spec,
        out_shape=jax.ShapeDtypeStruct((t_max * tr,) + row_tile, F32),
        compiler_params=_params(("arbitrary", "arbitrary")),
        name="expert_ffn",
    )(plan["tile_expert"], plan["n_used"], plan["src_token"], plan["dst_row"], h, wg, wu, wd)


def _route_plan(routing, n_tok, tr):
    n_assign = TOP_K * n_tok
    t_max = n_assign // tr + N_EXPERTS
    n_slots = t_max * tr
    n_pad = n_slots - n_assign
    id_bits = (n_slots - 1).bit_length()
    experts = jnp.arange(N_EXPERTS, dtype=jnp.int32)
    e_flat = routing[:, :TOP_K].astype(jnp.int32).T.reshape(-1)
    counts = jnp.sum((e_flat[:, None] == experts[None, :]).astype(jnp.int32), axis=0)
    tiles_e = (counts + tr - 1) // tr
    tile_end = jnp.cumsum(tiles_e)
    n_used = tile_end[-1]
    pad_end = jnp.cumsum(tiles_e * tr - counts)
    pad_ids = jnp.arange(n_pad, dtype=jnp.int32)
    pad_expert = jnp.sum((pad_ids[:, None] >= pad_end[None, :]).astype(jnp.int32), axis=1)
    keys = jnp.concatenate([e_flat * 2, pad_expert * 2 + 1])
    item = jnp.arange(n_slots, dtype=jnp.int32)
    slot_item = jnp.sort((keys << id_bits) | item) & ((1 << id_bits) - 1)
    real = slot_item < n_assign
    src = jnp.where(real, slot_item % n_tok, 0).astype(jnp.int32)
    dst = slot_item.astype(jnp.int32)
    t_ids = jnp.arange(t_max, dtype=jnp.int32)
    te = jnp.sum((t_ids[:, None] >= tile_end[None, :]).astype(jnp.int32), axis=1)
    last_used = jnp.max(jnp.where(tiles_e > 0, experts, 0))
    te = jnp.where(t_ids < n_used, te, last_used).astype(jnp.int32)
    return {"tile_expert": te, "n_used": n_used.reshape(1).astype(jnp.int32), "src_token": src, "dst_row": dst}


def _combine_kernel(alpha, split_ctx_tiles, e1_ref, e2_ref, rt_ref, x1_ref, mod_ref, g_ref, b_ref, *o_refs):
    y = rt_ref[:, 2:3] * _tiles_to_rows(e1_ref[...]) + rt_ref[:, 3:4] * _tiles_to_rows(e2_ref[...])
    t = alpha * x1_ref[...] + mod_ref[5:6, :] * y
    out = _layer_norm_rows(t, g_ref[...], b_ref[...])
    if split_ctx_tiles is None:
        o_refs[0][...] = out
    else:
        _write_stream(o_refs[0], o_refs[1], split_ctx_tiles, out)


def _combine_call(eo, routing, x1, mods, ln_g, ln_b, alpha, geom, split_out):
    n = x1.shape[0]
    tm = COMBINE_TILE
    row = lambda i: (i, 0)
    const = lambda i: (0, 0)
    if split_out:
        out_specs = _stream_specs(D_MODEL, geom.ctx_tiles(tm), tm)
        out_shape = [jax.ShapeDtypeStruct((geom.n_ctx, D_MODEL), F32),
                     jax.ShapeDtypeStruct((n - geom.n_ctx, D_MODEL), F32)]
    else:
        out_specs = pl.BlockSpec((tm, D_MODEL), row)
        out_shape = jax.ShapeDtypeStruct((n, D_MODEL), F32)
    return pl.pallas_call(
        functools.partial(_combine_kernel, alpha, geom.ctx_tiles(tm) if split_out else None),
        grid=(n // tm,),
        in_specs=[
            pl.BlockSpec((tm,) + eo.shape[1:], lambda i: (i, 0, 0)),
            pl.BlockSpec((tm,) + eo.shape[1:], lambda i: (i + n // tm, 0, 0)),
            pl.BlockSpec((tm, LANES), row),
            pl.BlockSpec((tm, D_MODEL), row),
            pl.BlockSpec((None, N_MODS, D_MODEL), lambda i: (geom.mod_index(i, tm), 0, 0)),
            pl.BlockSpec((1, D_MODEL), const),
            pl.BlockSpec((1, D_MODEL), const),
        ],
        out_specs=out_specs,
        out_shape=out_shape,
        compiler_params=_params(("arbitrary",)),
        name="expert_combine",
    )(eo, eo, routing, x1, mods, ln_g, ln_b)


class _Geometry:
    def __init__(self, n_ctx, dec_batch, dec_seq):
        self.n_ctx = n_ctx
        self.dec_batch = dec_batch
        self.dec_seq = dec_seq
        self.n_tok = n_ctx + dec_batch * dec_seq

    def ctx_tiles(self, tm):
        assert self.n_ctx % tm == 0 and self.dec_seq % tm == 0
        return self.n_ctx // tm

    def mod_index(self, i, tm):
        ct = self.ctx_tiles(tm)
        return jnp.where(i < ct, 0, 1 + (i - ct) // (self.dec_seq // tm))

    def rope_index(self, i, tm):
        ct = self.ctx_tiles(tm)
        return jnp.where(i < ct, 0, 1 + (i - ct) % (self.dec_seq // tm))


def _rope_tables(dec_seq):
    pos = jnp.arange(dec_seq, dtype=jnp.int32)
    row = (pos // GRID_W).astype(F32)
    col = (pos % GRID_W).astype(F32)
    inv = ROPE_BASE ** (-jnp.arange(ROPE_PAIRS, dtype=F32) / ROPE_PAIRS)
    ang_r = row[:, None] * inv[None, :]
    ang_c = col[:, None] * inv[None, :]
    cos_h = jnp.concatenate([jnp.cos(ang_r), jnp.cos(ang_r), jnp.cos(ang_c), jnp.cos(ang_c)], axis=-1)
    sin_h = jnp.concatenate([-jnp.sin(ang_r), jnp.sin(ang_r), -jnp.sin(ang_c), jnp.sin(ang_c)], axis=-1)
    reps = LANES // HEAD_DIM
    cos_t = jnp.concatenate([jnp.ones((IN_TILE, LANES), F32), jnp.tile(cos_h, (1, reps))], axis=0)
    sin_t = jnp.concatenate([jnp.zeros((IN_TILE, LANES), F32), jnp.tile(sin_h, (1, reps))], axis=0)
    return cos_t, sin_t


def kernel(x_prompt, x_sample, cache_k, cache_v, c, c_ctx, w_ada, b_ada, w_in, w_o, attn_sink, w_s, b_s, sgu_ln_g, sgu_ln_b, ln1_g, ln1_b, ln2_g, ln2_b, w_ff_gate, w_ff_up, w_ff_down, w_router, w_exp_gate, w_exp_up, w_exp_down):
    batch, seq, d = x_prompt.shape
    dec_batch, dec_seq, _ = x_sample.shape
    depth = w_in.shape[0]
    past = cache_k.shape[2]
    assert d == D_MODEL and dec_batch + 1 <= MOD_ROWS
    n_ctx = batch * seq
    n_lat = dec_batch * dec_seq
    n_tok = n_ctx + n_lat
    geom = _Geometry(n_ctx, dec_batch, dec_seq)
    alpha = float((2 * depth) ** 0.25)

    x = (x_prompt.reshape(n_ctx, d), x_sample.reshape(n_lat, d))
    cvec = jnp.concatenate([c_ctx[None, :], c, jnp.zeros((MOD_ROWS - 1 - dec_batch, d), F32)], axis=0)
    mods_all = _ada_call(cvec, w_ada, b_ada).reshape(depth, MOD_ROWS, N_MODS, d)
    cos_t, sin_t = _rope_tables(dec_seq)

    new_k, new_v = [], []
    for l in range(depth):
        mods = mods_all[l]
        q, k, v, kv32, u, g = _in_call(x, mods, w_in[l].astype(BF16), cos_t, sin_t,
                                       sgu_ln_g[l].reshape(1, SGU_WIDTH), sgu_ln_b[l].reshape(1, SGU_WIDTH), geom)
        new_k.append(kv32[:n_ctx, :KV_WIDTH].reshape(batch, seq, N_KV_HEADS, HEAD_DIM))
        new_v.append(kv32[:n_ctx, KV_WIDTH:].reshape(batch, seq, N_KV_HEADS, HEAD_DIM))
        sink = attn_sink[l]
        a_ctx = _ctx_attn_call(sink, q, k, v, batch, seq)
        a_lat = _lat_attn_call(sink, q, k, v, cache_k[:, l].reshape(dec_batch, past, KV_WIDTH).astype(BF16),
                               cache_v[:, l].reshape(dec_batch, past, KV_WIDTH).astype(BF16), geom)
        bias_full = jnp.repeat(b_s[l].T, SGU_GROUP_DIM, axis=1)
        i = l // 2
        moe = l % 2 == 1
        w_r = None
        if moe:
            w_r = jnp.pad(w_router[i], ((0, 0), (0, LANES - N_EXPERTS)))
        outs = _out_call(a_ctx, a_lat, u, g, w_s[l].astype(BF16), bias_full, x, mods, w_o[l].astype(BF16), ln1_g[l].reshape(1, d), ln1_b[l].reshape(1, d),
                         w_r, alpha, geom)
        ln_g, ln_b = ln2_g[l].reshape(1, d), ln2_b[l].reshape(1, d)
        if moe:
            x1, h, routing = outs
            plan = _route_plan(routing, n_tok, FFN_TILE)
            eo = _moe_call(h, w_exp_gate[i].astype(BF16), w_exp_up[i].astype(BF16), w_exp_down[i].astype(BF16), plan)
            x = _combine_call(eo, routing, x1, mods, ln_g, ln_b, alpha, geom, split_out=l == depth - 1)
        else:
            x1, h = outs
            x = _ffn_call(h, x1, mods, w_ff_gate[i].astype(BF16), w_ff_up[i].astype(BF16),
                          w_ff_down[i].astype(BF16), ln_g, ln_b, alpha, geom)

    if not isinstance(x, (tuple, list)):
        x = (x[:n_ctx], x[n_ctx:])
    y_prompt = x[0].reshape(batch, seq, d)
    y_sample = x[1].reshape(dec_batch, dec_seq, d)
    return (y_prompt, y_sample, jnp.stack(new_k, axis=1), jnp.stack(new_v, axis=1))
```

```python
import functools

import jax
import jax.numpy as jnp
import numpy as np
from jax import lax
from jax.experimental import pallas as pl
from jax.experimental.pallas import tpu as pltpu

F32 = jnp.float32
BF16 = jnp.bfloat16

D_MODEL = 1024
HEAD_DIM = 64
N_Q_HEADS = 8
N_KV_HEADS = 2
Q_PER_KV = N_Q_HEADS // N_KV_HEADS
ATTN_WIDTH = N_Q_HEADS * HEAD_DIM
KV_WIDTH = N_KV_HEADS * HEAD_DIM
SGU_WIDTH = D_MODEL - ATTN_WIDTH
N_SGU_GROUPS = 8
SGU_GROUP_DIM = SGU_WIDTH // N_SGU_GROUPS
CHUNK = 128
BLOCK = 128
WINDOW = 128
GRID_W = 64
IN_WIDTH = ATTN_WIDTH + 2 * KV_WIDTH + 2 * SGU_WIDTH
OFF_K = ATTN_WIDTH
OFF_V = OFF_K + KV_WIDTH
OFF_U = OFF_V + KV_WIDTH
OFF_G = OFF_U + SGU_WIDTH
N_EXPERTS = 8
TOP_K = 2
ROPE_BASE = 10000.0
ROPE_PAIRS = HEAD_DIM // 4
LN_EPS = 1e-5
ATTN_SCALE = HEAD_DIM ** -0.5
NEG_INF = -1e30
N_MODS = 6

LANES = 128
V7X_MXU_WIDTH = 256
V7X_VMEM_LIMIT_BYTES = 56 * 1024 * 1024

IN_TILE = 1024
OUT_TILE = 512
ROUTER_TILE = 256
COMBINE_TILE = 512
DENSE_TILE = 1024
FFN_TILE = 512
LAT_QUERY_ROWS = 512
CTX_SEQS_PER_STEP = 2
MOD_ROWS = 8
ADA_COLS = 1536


def _params(sem):
    return pltpu.CompilerParams(dimension_semantics=sem, vmem_limit_bytes=V7X_VMEM_LIMIT_BYTES)


def _gelu_tanh(x):
    return x * (0.5 * (1.0 + jnp.tanh(np.float32(np.sqrt(2.0 / np.pi)) * (x + 0.044715 * (x * x * x)))))


def _sigmoid(x):
    return 1.0 / (1.0 + jnp.exp(-x))


def _layer_norm_rows(t, g, b):
    mu = jnp.mean(t, axis=-1, keepdims=True)
    tc = t - mu
    var = jnp.mean(tc * tc, axis=-1, keepdims=True)
    return tc * lax.rsqrt(var + LN_EPS) * g + b


def _rows_to_tiles(x):
    pieces = jnp.stack([x[:, s * LANES:(s + 1) * LANES] for s in range(x.shape[1] // LANES)], axis=0)
    return jnp.swapaxes(pieces, 0, 1)


def _tiles_to_rows(t):
    pieces = jnp.swapaxes(t, 0, 1)
    return jnp.concatenate([pieces[s] for s in range(t.shape[1])], axis=1)


def _stream_specs(width, ctx_tiles, tm):
    return [pl.BlockSpec((tm, width), lambda i, *_: (jnp.minimum(i, ctx_tiles - 1), 0)),
            pl.BlockSpec((tm, width), lambda i, *_: (jnp.maximum(i - ctx_tiles, 0), 0))]


def _read_stream(ctx_ref, lat_ref, ctx_tiles):
    tile = lax.broadcasted_iota(jnp.int32, ctx_ref.shape, 0) * 0 + pl.program_id(0)
    return jnp.where(tile < ctx_tiles, ctx_ref[...], lat_ref[...])


def _write_stream(ctx_ref, lat_ref, ctx_tiles, value):
    @pl.when(pl.program_id(0) < ctx_tiles)
    def _():
        ctx_ref[...] = value

    @pl.when(pl.program_id(0) >= ctx_tiles)
    def _():
        lat_ref[...] = value


def _ada_kernel(c_ref, w_ref, b_ref, o_ref):
    c = c_ref[...]
    s = (c * _sigmoid(c)).astype(BF16)
    o_ref[...] = jnp.dot(s, w_ref[...].astype(BF16), preferred_element_type=F32) + b_ref[...]


def _ada_call(cvec, w_ada, b_ada):
    depth = w_ada.shape[0]
    n_out = w_ada.shape[2]
    return pl.pallas_call(
        _ada_kernel,
        grid=(depth, n_out // ADA_COLS),
        in_specs=[
            pl.BlockSpec((MOD_ROWS, D_MODEL), lambda l, j: (0, 0)),
            pl.BlockSpec((None, D_MODEL, ADA_COLS), lambda l, j: (l, 0, j)),
            pl.BlockSpec((None, 1, ADA_COLS), lambda l, j: (l, 0, j)),
        ],
        out_specs=pl.BlockSpec((None, MOD_ROWS, ADA_COLS), lambda l, j: (l, 0, j)),
        out_shape=jax.ShapeDtypeStruct((depth, MOD_ROWS, n_out), F32),
        compiler_params=_params(("arbitrary", "arbitrary")),
        name="adaln",
    )(cvec, w_ada, b_ada.reshape(depth, 1, n_out))


def _in_kernel(split, ctx_tiles, *refs):
    if split:
        x = _read_stream(refs[0], refs[1], ctx_tiles)
        refs = refs[2:]
    else:
        x = refs[0][...]
        refs = refs[1:]
    mod_ref, w_ref, cos_ref, sin_ref, lng_ref, lnb_ref, q_ref, k_ref, v_ref, kv_ref, u_ref, g_ref = refs
    tm = x.shape[0]
    h = x * (1.0 + mod_ref[1:2, :]) + mod_ref[0:1, :]
    hb = h.astype(BF16)

    def z_block(start):
        return jnp.dot(hb, w_ref[:, start:start + V7X_MXU_WIDTH], preferred_element_type=F32)

    cos = cos_ref[...]
    sin = sin_ref[...]
    lane = lax.broadcasted_iota(jnp.int32, (tm, LANES), 1)
    first_of_pair = (lane & (2 * ROPE_PAIRS - 1)) < ROPE_PAIRS
    lower_group = lane < SGU_GROUP_DIM

    def rope(t):
        partner = jnp.where(first_of_pair, pltpu.roll(t, LANES - ROPE_PAIRS, 1), pltpu.roll(t, ROPE_PAIRS, 1))
        return t * cos + partner * sin

    halves = (slice(0, LANES), slice(LANES, 2 * LANES))
    inv_n = 1.0 / SGU_GROUP_DIM
    for j in range(SGU_WIDTH // LANES):
        cols = slice(j * LANES, (j + 1) * LANES)
        if j % 2 == 0:
            z = z_block(OFF_G + j * LANES)
        t = _gelu_tanh(z[:, halves[j % 2]])
        s_lo = jnp.sum(jnp.where(lower_group, t, 0.0), axis=-1, keepdims=True)
        s_hi = jnp.sum(jnp.where(lower_group, 0.0, t), axis=-1, keepdims=True)
        tc = t - jnp.where(lower_group, s_lo, s_hi) * inv_n
        sq = tc * tc
        v_lo = jnp.sum(jnp.where(lower_group, sq, 0.0), axis=-1, keepdims=True)
        v_hi = jnp.sum(jnp.where(lower_group, 0.0, sq), axis=-1, keepdims=True)
        var = jnp.where(lower_group, v_lo, v_hi) * inv_n
        g_ref[:, cols] = (tc * lax.rsqrt(var + LN_EPS) * lng_ref[:, cols] + lnb_ref[:, cols]).astype(BF16)

    for start in range(0, ATTN_WIDTH, V7X_MXU_WIDTH):
        z = z_block(start)
        for half in halves:
            q_ref[:, start + half.start:start + half.stop] = (rope(z[:, half]) * ATTN_SCALE).astype(BF16)
    z = z_block(OFF_K)
    k_ref[...] = rope(z[:, halves[0]]).astype(BF16)
    v_ref[...] = z[:, halves[1]].astype(BF16)
    kv_ref[...] = z
    for start in range(0, SGU_WIDTH, V7X_MXU_WIDTH):
        u_ref[:, start:start + V7X_MXU_WIDTH] = _gelu_tanh(z_block(OFF_U + start))


def _in_call(x, mods, w_in, cos_t, sin_t, ln_g, ln_b, geom):
    n = geom.n_tok
    tm = IN_TILE
    mod_idx = lambda i: geom.mod_index(i, tm)
    rope_idx = lambda i: geom.rope_index(i, tm)
    row = lambda i: (i, 0)
    split = isinstance(x, tuple)
    x_args = list(x) if split else [x]
    ctx_tiles = geom.ctx_tiles(tm)
    x_specs = _stream_specs(D_MODEL, ctx_tiles, tm) if split else [pl.BlockSpec((tm, D_MODEL), row)]
    outs = pl.pallas_call(
        functools.partial(_in_kernel, split, ctx_tiles),
        grid=(n // tm,),
        in_specs=x_specs + [
            pl.BlockSpec((None, N_MODS, D_MODEL), lambda i: (mod_idx(i), 0, 0)),
            pl.BlockSpec((D_MODEL, IN_WIDTH), lambda i: (0, 0)),
            pl.BlockSpec((tm, LANES), lambda i: (rope_idx(i), 0)),
            pl.BlockSpec((tm, LANES), lambda i: (rope_idx(i), 0)),
            pl.BlockSpec((1, SGU_WIDTH), lambda i: (0, 0)),
            pl.BlockSpec((1, SGU_WIDTH), lambda i: (0, 0)),
        ],
        out_specs=[
            pl.BlockSpec((tm, ATTN_WIDTH), row),
            pl.BlockSpec((tm, KV_WIDTH), row),
            pl.BlockSpec((tm, KV_WIDTH), row),
            pl.BlockSpec((tm, 2 * KV_WIDTH), row),
            pl.BlockSpec((tm, SGU_WIDTH), row),
            pl.BlockSpec((tm, SGU_WIDTH), row),
        ],
        out_shape=[
            jax.ShapeDtypeStruct((n, ATTN_WIDTH), BF16),
            jax.ShapeDtypeStruct((n, KV_WIDTH), BF16),
            jax.ShapeDtypeStruct((n, KV_WIDTH), BF16),
            jax.ShapeDtypeStruct((n, 2 * KV_WIDTH), F32),
            jax.ShapeDtypeStruct((n, SGU_WIDTH), F32),
            jax.ShapeDtypeStruct((n, SGU_WIDTH), BF16),
        ],
        compiler_params=_params(("arbitrary",)),
        name="in_proj",
    )(*x_args, mods, w_in, cos_t, sin_t, ln_g, ln_b)
    return outs


def _group_attention(q_ref, rows, hk, k_all, v_all, block_masks, sink_ref):
    m_rows = rows.stop - rows.start
    heads = [hk * Q_PER_KV + gq for gq in range(Q_PER_KV)]
    q = jnp.concatenate([q_ref[rows, h * HEAD_DIM:(h + 1) * HEAD_DIM] for h in heads], axis=0)
    s = lax.dot_general(q, k_all, (((1,), (1,)), ((), ())), preferred_element_type=F32)
    n_blocks = k_all.shape[0] // LANES
    blocks = [s[:, b * LANES:(b + 1) * LANES] for b in range(n_blocks)]
    for b, mask in block_masks.items():
        blocks[b] = jnp.where(mask, blocks[b], NEG_INF)
    head_of_row = lax.broadcasted_iota(jnp.int32, (Q_PER_KV * m_rows, 1), 0) // m_rows
    sink = jnp.zeros((Q_PER_KV * m_rows, 1), F32)
    for gq, h in enumerate(heads):
        sink = jnp.where(head_of_row == gq, sink_ref[h], sink)
    m_el = blocks[0]
    for blk in blocks[1:]:
        m_el = jnp.maximum(m_el, blk)
    m = jnp.maximum(jnp.max(m_el, axis=-1, keepdims=True), sink)
    probs = [jnp.exp(blk - m) for blk in blocks]
    l_el = probs[0]
    for p in probs[1:]:
        l_el = l_el + p
    denom = jnp.sum(l_el, axis=-1, keepdims=True) + jnp.exp(sink - m)
    p_all = jnp.concatenate([p.astype(BF16) for p in probs], axis=1)
    o = jnp.dot(p_all, v_all, preferred_element_type=F32) / denom
    return {h: o[gq * m_rows:(gq + 1) * m_rows] for gq, h in enumerate(heads)}


def _store_heads(o_ref, rows, outs):
    for h0 in range(0, N_Q_HEADS, 2):
        pair = jnp.concatenate([outs[h0], outs[h0 + 1]], axis=1)
        o_ref[rows, h0 * HEAD_DIM:(h0 + 2) * HEAD_DIM] = pair.astype(o_ref.dtype)


def _ctx_attn_kernel(seq_len, sink_ref, q_ref, k_ref, v_ref, o_ref):
    for sq in range(q_ref.shape[0] // seq_len):
        rows = slice(sq * seq_len, (sq + 1) * seq_len)
        outs = {}
        for hk in range(N_KV_HEADS):
            kv_cols = slice(hk * HEAD_DIM, (hk + 1) * HEAD_DIM)
            outs.update(_group_attention(q_ref, rows, hk, k_ref[rows, kv_cols], v_ref[rows, kv_cols], {}, sink_ref))
        _store_heads(o_ref, rows, outs)


def _ctx_attn_call(sink, q, k, v, n_seq, seq_len):
    n = n_seq * seq_len
    per_step = CTX_SEQS_PER_STEP if n_seq % CTX_SEQS_PER_STEP == 0 else 1
    rows = per_step * seq_len
    blk = lambda b: (b, 0)
    return pl.pallas_call(
        functools.partial(_ctx_attn_kernel, seq_len),
        grid=(n_seq // per_step,),
        in_specs=[
            pl.BlockSpec(memory_space=pltpu.SMEM),
            pl.BlockSpec((rows, ATTN_WIDTH), blk),
            pl.BlockSpec((rows, KV_WIDTH), blk),
            pl.BlockSpec((rows, KV_WIDTH), blk),
        ],
        out_specs=pl.BlockSpec((rows, ATTN_WIDTH), blk),
        out_shape=jax.ShapeDtypeStruct((n, ATTN_WIDTH), BF16),
        compiler_params=_params(("arbitrary",)),
        name="ctx_attention",
    )(sink, q, k, v)


def _lat_attn_kernel(sink_ref, q_ref, k_ref, v_ref, ck_ref, cv_ref, o_ref):
    blocks_per_step = q_ref.shape[0] // BLOCK
    nb = k_ref.shape[0] // BLOCK
    past_blocks = ck_ref.shape[0] // LANES
    r = lax.broadcasted_iota(jnp.int32, (Q_PER_KV * BLOCK, BLOCK), 0) & (BLOCK - 1)
    c = lax.broadcasted_iota(jnp.int32, (Q_PER_KV * BLOCK, BLOCK), 1)
    for sub in range(blocks_per_step):
        j = pl.program_id(1) * blocks_per_step + sub
        rows = slice(sub * BLOCK, (sub + 1) * BLOCK)
        mask_prev = c >= r + jnp.where(j > 0, 0, BLOCK)
        mask_next = c <= r - jnp.where(j < nb - 1, 0, BLOCK)
        prev = pl.ds(pl.multiple_of(jnp.maximum(j - 1, 0) * BLOCK, BLOCK), BLOCK)
        cur = pl.ds(pl.multiple_of(j * BLOCK, BLOCK), BLOCK)
        nxt = pl.ds(pl.multiple_of(jnp.minimum(j + 1, nb - 1) * BLOCK, BLOCK), BLOCK)
        outs = {}
        for hk in range(N_KV_HEADS):
            kv_cols = slice(hk * HEAD_DIM, (hk + 1) * HEAD_DIM)
            k_all = jnp.concatenate([ck_ref[:, kv_cols], k_ref[prev, kv_cols], k_ref[cur, kv_cols],
                                     k_ref[nxt, kv_cols]], axis=0)
            v_all = jnp.concatenate([cv_ref[:, kv_cols], v_ref[prev, kv_cols], v_ref[cur, kv_cols],
                                     v_ref[nxt, kv_cols]], axis=0)
            masks = {past_blocks: mask_prev, past_blocks + 2: mask_next}
            outs.update(_group_attention(q_ref, rows, hk, k_all, v_all, masks, sink_ref))
        _store_heads(o_ref, rows, outs)


def _lat_attn_call(sink, q, k, v, cache_k, cache_v, geom):
    qb = LAT_QUERY_ROWS
    steps = geom.dec_seq // qb
    assert geom.n_ctx % geom.dec_seq == 0 and cache_k.shape[1] % LANES == 0
    seq_base = geom.n_ctx // geom.dec_seq
    past = cache_k.shape[1]
    seq_spec = pl.BlockSpec((geom.dec_seq, KV_WIDTH), lambda b, j: (seq_base + b, 0))
    cache_spec = pl.BlockSpec((None, past, KV_WIDTH), lambda b, j: (b, 0, 0))
    return pl.pallas_call(
        _lat_attn_kernel,
        grid=(geom.dec_batch, steps),
        in_specs=[
            pl.BlockSpec(memory_space=pltpu.SMEM),
            pl.BlockSpec((qb, ATTN_WIDTH), lambda b, j: (geom.n_ctx // qb + b * steps + j, 0)),
            seq_spec, seq_spec,
            cache_spec, cache_spec,
        ],
        out_specs=pl.BlockSpec((qb, ATTN_WIDTH), lambda b, j: (b * steps + j, 0)),
        out_shape=jax.ShapeDtypeStruct((geom.dec_batch * geom.dec_seq, ATTN_WIDTH), BF16),
        compiler_params=_params(("arbitrary", "arbitrary")),
        name="lat_attention",
    )(sink, q, k, v, cache_k, cache_v)


def _spatial_gating(u_ref, g_ref, mix_ref, bias_ref, s_ref):
    tm = u_ref.shape[0]
    lane = lax.broadcasted_iota(jnp.int32, (CHUNK, LANES), 1)
    lower_group = lane < SGU_GROUP_DIM
    for ch in range(tm // CHUNK):
        rows = slice(ch * CHUNK, (ch + 1) * CHUNK)
        for p in range(SGU_WIDTH // LANES):
            cols = slice(p * LANES, (p + 1) * LANES)
            g = g_ref[rows, cols]
            zero = jnp.zeros_like(g)
            mixed = (jnp.dot(mix_ref[2 * p], jnp.where(lower_group, g, zero), preferred_element_type=F32)
                     + jnp.dot(mix_ref[2 * p + 1], jnp.where(lower_group, zero, g), preferred_element_type=F32))
            s_ref[rows, cols] = (u_ref[rows, cols] * (mixed + bias_ref[:, cols])).astype(s_ref.dtype)


def _out_kernel(alpha, with_router, split_x, ctx_tiles, a_ctx_ref, a_lat_ref, u_ref, gg_ref, mix_ref, bias_ref,
                *refs):
    s_ref = refs[-1]
    refs = refs[:-1]
    if split_x:
        x = _read_stream(refs[0], refs[1], ctx_tiles)
        refs = refs[2:]
    else:
        x = refs[0][...]
        refs = refs[1:]
    mod_ref, wa_ref, ws_ref, g_ref, b_ref = refs[:5]
    if with_router:
        wr_ref, x1_ref, h_ref, rt_ref = refs[5:]
    else:
        x1_ref, h_ref = refs[5:]
    a = _read_stream(a_ctx_ref, a_lat_ref, ctx_tiles)
    _spatial_gating(u_ref, gg_ref, mix_ref, bias_ref, s_ref)
    y = (jnp.dot(a, wa_ref[...], preferred_element_type=F32)
         + jnp.dot(s_ref[...], ws_ref[...], preferred_element_type=F32))
    t = alpha * x + mod_ref[2:3, :] * y
    x1 = _layer_norm_rows(t, g_ref[...], b_ref[...])
    x1_ref[...] = x1
    h = x1 * (1.0 + mod_ref[4:5, :]) + mod_ref[3:4, :]
    if with_router:
        h_ref[...] = _rows_to_tiles(h)
    else:
        h_ref[...] = h.astype(h_ref.dtype)
    if with_router:
        tm = h.shape[0]
        wr = wr_ref[...]
        wr_hi = wr.astype(BF16)
        wr_lo = (wr - wr_hi.astype(F32)).astype(BF16)
        h_hi = h.astype(BF16)
        h_lo = (h - h_hi.astype(F32)).astype(BF16)
        logits = (jnp.dot(h_hi, wr_hi, preferred_element_type=F32)
                  + jnp.dot(h_lo, wr_hi, preferred_element_type=F32)
                  + jnp.dot(h_hi, wr_lo, preferred_element_type=F32))
        lane = lax.broadcasted_iota(jnp.int32, (tm, LANES), 1).astype(F32)
        neg = jnp.float32(-jnp.inf)
        lg = jnp.where(lane < N_EXPERTS, logits, neg)
        m1 = jnp.max(lg, axis=-1, keepdims=True)
        i1 = jnp.min(jnp.where(lg == m1, lane, float(LANES)), axis=-1, keepdims=True)
        lg2 = jnp.where(lane == i1, neg, lg)
        m2 = jnp.max(lg2, axis=-1, keepdims=True)
        i2 = jnp.min(jnp.where(lg2 == m2, lane, float(LANES)), axis=-1, keepdims=True)
        e2 = jnp.exp(m2 - m1)
        g1 = 1.0 / (1.0 + e2)
        g2 = e2 / (1.0 + e2)
        rt = jnp.where(lane == 0, i1, jnp.where(lane == 1, i2, jnp.where(lane == 2, g1, jnp.where(lane == 3, g2, 0.0))))
        rt_ref[...] = rt


def _out_call(a_ctx, a_lat, u, gg, w_s, bias_full, x, mods, w_o, ln_g, ln_b, w_router, alpha, geom):
    n = geom.n_tok
    with_router = w_router is not None
    tm = ROUTER_TILE if with_router else OUT_TILE
    ctx_tiles = geom.ctx_tiles(tm)
    row = lambda i: (i, 0)
    const = lambda i: (0, 0)
    split_x = isinstance(x, tuple)
    x_args = list(x) if split_x else [x]
    x_specs = _stream_specs(D_MODEL, ctx_tiles, tm) if split_x else [pl.BlockSpec((tm, D_MODEL), row)]
    sgu_specs = [
        pl.BlockSpec((tm, SGU_WIDTH), row),
        pl.BlockSpec((tm, SGU_WIDTH), row),
        pl.BlockSpec((N_SGU_GROUPS, CHUNK, CHUNK), lambda i: (0, 0, 0)),
        pl.BlockSpec((CHUNK, SGU_WIDTH), const),
    ]
    in_specs = _stream_specs(ATTN_WIDTH, ctx_tiles, tm) + sgu_specs + x_specs + [
        pl.BlockSpec((None, N_MODS, D_MODEL), lambda i: (geom.mod_index(i, tm), 0, 0)),
        pl.BlockSpec((ATTN_WIDTH, D_MODEL), const),
        pl.BlockSpec((SGU_WIDTH, D_MODEL), lambda i: (1, 0)),
        pl.BlockSpec((1, D_MODEL), const),
        pl.BlockSpec((1, D_MODEL), const),
    ]
    args = [a_ctx, a_lat, u, gg, w_s, bias_full] + x_args + [mods, w_o, w_o, ln_g, ln_b]
    if with_router:
        h_spec = pl.BlockSpec((tm, D_MODEL // LANES, LANES), lambda i: (i, 0, 0))
        h_shape = jax.ShapeDtypeStruct((n, D_MODEL // LANES, LANES), F32)
    else:
        h_spec = pl.BlockSpec((tm, D_MODEL), row)
        h_shape = jax.ShapeDtypeStruct((n, D_MODEL), BF16)
    out_specs = [pl.BlockSpec((tm, D_MODEL), row), h_spec]
    out_shape = [jax.ShapeDtypeStruct((n, D_MODEL), F32), h_shape]
    if with_router:
        in_specs.append(pl.BlockSpec((D_MODEL, LANES), const))
        args.append(w_router)
        out_specs.append(pl.BlockSpec((tm, LANES), row))
        out_shape.append(jax.ShapeDtypeStruct((n, LANES), F32))
    return pl.pallas_call(
        functools.partial(_out_kernel, alpha, with_router, split_x, ctx_tiles),
        grid=(n // tm,),
        in_specs=in_specs,
        out_specs=out_specs,
        out_shape=out_shape,
        scratch_shapes=[pltpu.VMEM((tm, SGU_WIDTH), BF16)],
        compiler_params=_params(("arbitrary",)),
        name="out_proj_router" if with_router else "out_proj",
    )(*args)


def _swiglu_blocks(width):
    return [slice(s, min(s + V7X_MXU_WIDTH, width)) for s in range(0, width, V7X_MXU_WIDTH)]


def _swiglu_partial(x, wg_ref, wu_ref, wd_ref, side_work=None):
    out = None
    for n, cols in enumerate(_swiglu_blocks(wg_ref.shape[1])):
        a = jnp.dot(x, wg_ref[:, cols], preferred_element_type=F32)
        if side_work is not None:
            side_work(3 * n)
        b = jnp.dot(x, wu_ref[:, cols], preferred_element_type=F32)
        if side_work is not None:
            side_work(3 * n + 1)
        mid = ((a * _sigmoid(a)) * b).astype(BF16)
        part = jnp.dot(mid, wd_ref[cols, :], preferred_element_type=F32)
        if side_work is not None:
            side_work(3 * n + 2)
        out = part if out is None else out + part
    return out


def _ffn_kernel(alpha, h_ref, x1_ref, mod_ref, wg_ref, wu_ref, wd_ref, g_ref, b_ref, o_ref, acc_ref):
    c = pl.program_id(1)
    part = _swiglu_partial(h_ref[...], wg_ref, wu_ref, wd_ref)

    @pl.when(c == 0)
    def _():
        acc_ref[...] = part

    @pl.when(c > 0)
    def _():
        acc_ref[...] += part

    @pl.when(c == pl.num_programs(1) - 1)
    def _():
        t = alpha * x1_ref[...] + mod_ref[5:6, :] * acc_ref[...]
        o_ref[...] = _layer_norm_rows(t, g_ref[...], b_ref[...])


def _ffn_chunk(d_ff):
    assert d_ff % (2 * LANES) == 0
    return d_ff // 2


def _ffn_call(h, x1, mods, wg, wu, wd, ln_g, ln_b, alpha, geom):
    n = h.shape[0]
    tm = DENSE_TILE
    d_ff = wg.shape[1]
    fc = _ffn_chunk(d_ff)
    row = lambda i, c: (i, 0)
    const = lambda i, c: (0, 0)
    return pl.pallas_call(
        functools.partial(_ffn_kernel, alpha),
        grid=(n // tm, d_ff // fc),
        in_specs=[
            pl.BlockSpec((tm, D_MODEL), row),
            pl.BlockSpec((tm, D_MODEL), row),
            pl.BlockSpec((None, N_MODS, D_MODEL), lambda i, c: (geom.mod_index(i, tm), 0, 0)),
            pl.BlockSpec((D_MODEL, fc), lambda i, c: (0, c)),
            pl.BlockSpec((D_MODEL, fc), lambda i, c: (0, c)),
            pl.BlockSpec((fc, D_MODEL), lambda i, c: (c, 0)),
            pl.BlockSpec((1, D_MODEL), const),
            pl.BlockSpec((1, D_MODEL), const),
        ],
        out_specs=pl.BlockSpec((tm, D_MODEL), row),
        out_shape=jax.ShapeDtypeStruct((n, D_MODEL), F32),
        scratch_shapes=[pltpu.VMEM((tm, D_MODEL), F32)],
        compiler_params=_params(("arbitrary", "arbitrary")),
        name="dense_ffn",
    )(h, x1, mods, wg, wu, wd, ln_g, ln_b)


def _moe_kernel(n_chunks, te_ref, nu_ref, src_ref, dst_ref, h_hbm, wg_ref, wu_ref, wd_ref, o_hbm,
                xbuf, xb, obuf, gsem, ssem):
    del te_ref
    tr = xb.shape[0]
    t = pl.program_id(0)
    c = pl.program_id(1)
    n_tiles = pl.num_programs(0)
    valid = t < nu_ref[0]
    slot = t % 2
    other = 1 - slot
    share = tr // n_chunks
    first_row = c * share
    nxt_base = jnp.minimum(t + 1, n_tiles - 1) * tr
    prv_base = jnp.where(t == 0, n_tiles - 1, t - 1) * tr

    def gather_row(base, dst_slot, r, tok=None):
        tok = src_ref[base + r] if tok is None else tok
        return pltpu.make_async_copy(h_hbm.at[tok], xbuf.at[dst_slot, r], gsem.at[dst_slot])

    def scatter_row(base, src_slot, r, dst=None):
        dst = dst_ref[base + r] if dst is None else dst
        return pltpu.make_async_copy(obuf.at[src_slot, r], o_hbm.at[dst], ssem.at[src_slot])

    def wait_gather(dst_slot):
        pltpu.make_async_copy(h_hbm.at[pl.ds(0, tr)], xbuf.at[dst_slot], gsem.at[dst_slot]).wait()

    def wait_scatter(src_slot):
        pltpu.make_async_copy(obuf.at[src_slot], o_hbm.at[pl.ds(0, tr)], ssem.at[src_slot]).wait()

    def looped(n_rows, start_row):
        def body(r, carry):
            start_row(r)
            return carry
        lax.fori_loop(0, n_rows, body, 0, unroll=8)

    @pl.when((t == 0) & (c == 0))
    def _():
        obuf[1] = jnp.zeros(obuf.shape[1:], obuf.dtype)
        looped(tr, lambda r: gather_row(0, 0, r).start())

    @pl.when(c == 0)
    def _():
        wait_gather(slot)

        @pl.when(t >= 1)
        def _():
            wait_scatter(slot)

    def multiply_tile(slot_s, c_s):
        other_s = 1 - slot_s
        if c_s == 0:
            xb[...] = _tiles_to_rows(xbuf[slot_s]).astype(BF16)
        n_groups = 3 * len(_swiglu_blocks(wg_ref.shape[1]))

        def row_dmas(i):
            for r in range(c_s * share + share * i // n_groups, c_s * share + share * (i + 1) // n_groups):
                gather_row(nxt_base, other_s, r).start()
                scatter_row(prv_base, other_s, r).start()

        part = _swiglu_partial(xb[...], wg_ref, wu_ref, wd_ref, row_dmas)
        if c_s == 0:
            obuf[slot_s] = _rows_to_tiles(part)
        else:
            obuf[slot_s] += _rows_to_tiles(part)

    for slot_s in range(2):
        for c_s in range(n_chunks):
            pl.when(valid & (slot == slot_s) & (c == c_s))(functools.partial(multiply_tile, slot_s, c_s))

    @pl.when(jnp.logical_not(valid))
    def _():
        @pl.when(c == 0)
        def _():
            obuf[slot] = jnp.zeros(obuf.shape[1:], obuf.dtype)

        looped(share, lambda r: gather_row(nxt_base, other, first_row + r).start())
        looped(share, lambda r: scatter_row(prv_base, other, first_row + r).start())

    @pl.when((t == n_tiles - 1) & (c == n_chunks - 1))
    def _():
        wait_scatter(other)
        looped(tr, lambda r: scatter_row(t * tr, slot, r).start())
        wait_scatter(slot)
        wait_gather(other)


def _moe_call(h, wg, wu, wd, plan):
    tr = FFN_TILE
    d_ff = wg.shape[2]
    fc = _ffn_chunk(d_ff)
    n_chunks = d_ff // fc
    t_max = plan["tile_expert"].shape[0]
    row_tile = h.shape[1:]

    def chunk(t, c, nu):
        return jnp.where(t < nu[0], c, n_chunks - 1)

    grid_spec = pltpu.PrefetchScalarGridSpec(
        num_scalar_prefetch=4,
        grid=(t_max, n_chunks),
        in_specs=[
            pl.BlockSpec(memory_space=pl.ANY),
            pl.BlockSpec((None, D_MODEL, fc), lambda t, c, te, nu, src, dst: (te[t], 0, chunk(t, c, nu))),
            pl.BlockSpec((None, D_MODEL, fc), lambda t, c, te, nu, src, dst: (te[t], 0, chunk(t, c, nu))),
            pl.BlockSpec((None, fc, D_MODEL), lambda t, c, te, nu, src, dst: (te[t], chunk(t, c, nu), 0)),
        ],
        out_specs=pl.BlockSpec(memory_space=pl.ANY),
        scratch_shapes=[
            pltpu.VMEM((2, tr) + row_tile, F32),
            pltpu.VMEM((tr, D_MODEL), BF16),
            pltpu.VMEM((2, tr) + row_tile, F32),
            pltpu.SemaphoreType.DMA((2,)),
            pltpu.SemaphoreType.DMA((2,)),
        ],
    )
    return pl.pallas_call(
        functools.partial(_moe_kernel, n_chunks),
        grid_spec=grid_spec,
        out_shape=jax.ShapeDtypeStruct((t_max * tr,) + row_tile, F32),
        compiler_params=_params(("arbitrary", "arbitrary")),
        name="expert_ffn",
    )(plan["tile_expert"], plan["n_used"], plan["src_token"], plan["dst_row"], h, wg, wu, wd)


def _route_plan(routing, n_tok, tr):
    n_assign = TOP_K * n_tok
    t_max = n_assign // tr + N_EXPERTS
    n_slots = t_max * tr
    n_pad = n_slots - n_assign
    id_bits = (n_slots - 1).bit_length()
    experts = jnp.arange(N_EXPERTS, dtype=jnp.int32)
    e_flat = routing[:, :TOP_K].astype(jnp.int32).T.reshape(-1)
    counts = jnp.sum((e_flat[:, None] == experts[None, :]).astype(jnp.int32), axis=0)
    tiles_e = (counts + tr - 1) // tr
    tile_end = jnp.cumsum(tiles_e)
    n_used = tile_end[-1]
    pad_end = jnp.cumsum(tiles_e * tr - counts)
    pad_ids = jnp.arange(n_pad, dtype=jnp.int32)
    pad_expert = jnp.sum((pad_ids[:, None] >= pad_end[None, :]).astype(jnp.int32), axis=1)
    keys = jnp.concatenate([e_flat * 2, pad_expert * 2 + 1])
    item = jnp.arange(n_slots, dtype=jnp.int32)
    slot_item = jnp.sort((keys << id_bits) | item) & ((1 << id_bits) - 1)
    real = slot_item < n_assign
    src = jnp.where(real, slot_item % n_tok, 0).astype(jnp.int32)
    dst = slot_item.astype(jnp.int32)
    t_ids = jnp.arange(t_max, dtype=jnp.int32)
    te = jnp.sum((t_ids[:, None] >= tile_end[None, :]).astype(jnp.int32), axis=1)
    last_used = jnp.max(jnp.where(tiles_e > 0, experts, 0))
    te = jnp.where(t_ids < n_used, te, last_used).astype(jnp.int32)
    return {"tile_expert": te, "n_used": n_used.reshape(1).astype(jnp.int32), "src_token": src, "dst_row": dst}


def _combine_kernel(alpha, split_ctx_tiles, e1_ref, e2_ref, rt_ref, x1_ref, mod_ref, g_ref, b_ref, *o_refs):
    y = rt_ref[:, 2:3] * _tiles_to_rows(e1_ref[...]) + rt_ref[:, 3:4] * _tiles_to_rows(e2_ref[...])
    t = alpha * x1_ref[...] + mod_ref[5:6, :] * y
    out = _layer_norm_rows(t, g_ref[...], b_ref[...])
    if split_ctx_tiles is None:
        o_refs[0][...] = out
    else:
        _write_stream(o_refs[0], o_refs[1], split_ctx_tiles, out)


def _combine_call(eo, routing, x1, mods, ln_g, ln_b, alpha, geom, split_out):
    n = x1.shape[0]
    tm = COMBINE_TILE
    row = lambda i: (i, 0)
    const = lambda i: (0, 0)
    if split_out:
        out_specs = _stream_specs(D_MODEL, geom.ctx_tiles(tm), tm)
        out_shape = [jax.ShapeDtypeStruct((geom.n_ctx, D_MODEL), F32),
                     jax.ShapeDtypeStruct((n - geom.n_ctx, D_MODEL), F32)]
    else:
        out_specs = pl.BlockSpec((tm, D_MODEL), row)
        out_shape = jax.ShapeDtypeStruct((n, D_MODEL), F32)
    return pl.pallas_call(
        functools.partial(_combine_kernel, alpha, geom.ctx_tiles(tm) if split_out else None),
        grid=(n // tm,),
        in_specs=[
            pl.BlockSpec((tm,) + eo.shape[1:], lambda i: (i, 0, 0)),
            pl.BlockSpec((tm,) + eo.shape[1:], lambda i: (i + n // tm, 0, 0)),
            pl.BlockSpec((tm, LANES), row),
            pl.BlockSpec((tm, D_MODEL), row),
            pl.BlockSpec((None, N_MODS, D_MODEL), lambda i: (geom.mod_index(i, tm), 0, 0)),
            pl.BlockSpec((1, D_MODEL), const),
            pl.BlockSpec((1, D_MODEL), const),
        ],
        out_specs=out_specs,
        out_shape=out_shape,
        compiler_params=_params(("arbitrary",)),
        name="expert_combine",
    )(eo, eo, routing, x1, mods, ln_g, ln_b)


class _Geometry:
    def __init__(self, n_ctx, dec_batch, dec_seq):
        self.n_ctx = n_ctx
        self.dec_batch = dec_batch
        self.dec_seq = dec_seq
        self.n_tok = n_ctx + dec_batch * dec_seq

    def ctx_tiles(self, tm):
        assert self.n_ctx % tm == 0 and self.dec_seq % tm == 0
        return self.n_ctx // tm

    def mod_index(self, i, tm):
        ct = self.ctx_tiles(tm)
        return jnp.where(i < ct, 0, 1 + (i - ct) // (self.dec_seq // tm))

    def rope_index(self, i, tm):
        ct = self.ctx_tiles(tm)
        return jnp.where(i < ct, 0, 1 + (i - ct) % (self.dec_seq // tm))


def _rope_tables(dec_seq):
    pos = jnp.arange(dec_seq, dtype=jnp.int32)
    row = (pos // GRID_W).astype(F32)
    col = (pos % GRID_W).astype(F32)
    inv = ROPE_BASE ** (-jnp.arange(ROPE_PAIRS, dtype=F32) / ROPE_PAIRS)
    ang_r = row[:, None] * inv[None, :]
    ang_c = col[:, None] * inv[None, :]
    cos_h = jnp.concatenate([jnp.cos(ang_r), jnp.cos(ang_r), jnp.cos(ang_c), jnp.cos(ang_c)], axis=-1)
    sin_h = jnp.concatenate([-jnp.sin(ang_r), jnp.sin(ang_r), -jnp.sin(ang_c), jnp.sin(ang_c)], axis=-1)
    reps = LANES // HEAD_DIM
    cos_t = jnp.concatenate([jnp.ones((IN_TILE, LANES), F32), jnp.tile(cos_h, (1, reps))], axis=0)
    sin_t = jnp.concatenate([jnp.zeros((IN_TILE, LANES), F32), jnp.tile(sin_h, (1, reps))], axis=0)
    return cos_t, sin_t


def kernel(x_prompt, x_sample, cache_k, cache_v, c, c_ctx, w_ada, b_ada, w_in, w_o, attn_sink, w_s, b_s, sgu_ln_g, sgu_ln_b, ln1_g, ln1_b, ln2_g, ln2_b, w_ff_gate, w_ff_up, w_ff_down, w_router, w_exp_gate, w_exp_up, w_exp_down):
    batch, seq, d = x_prompt.shape
    dec_batch, dec_seq, _ = x_sample.shape
    depth = w_in.shape[0]
    past = cache_k.shape[2]
    assert d == D_MODEL and dec_batch + 1 <= MOD_ROWS
    n_ctx = batch * seq
    n_lat = dec_batch * dec_seq
    n_tok = n_ctx + n_lat
    geom = _Geometry(n_ctx, dec_batch, dec_seq)
    alpha = float((2 * depth) ** 0.25)

    x = (x_prompt.reshape(n_ctx, d), x_sample.reshape(n_lat, d))
    cvec = jnp.concatenate([c_ctx[None, :], c, jnp.zeros((MOD_ROWS - 1 - dec_batch, d), F32)], axis=0)
    mods_all = _ada_call(cvec, w_ada, b_ada).reshape(depth, MOD_ROWS, N_MODS, d)
    cos_t, sin_t = _rope_tables(dec_seq)

    new_k, new_v = [], []
    for l in range(depth):
        mods = mods_all[l]
        q, k, v, kv32, u, g = _in_call(x, mods, w_in[l].astype(BF16), cos_t, sin_t,
                                       sgu_ln_g[l].reshape(1, SGU_WIDTH), sgu_ln_b[l].reshape(1, SGU_WIDTH), geom)
        new_k.append(kv32[:n_ctx, :KV_WIDTH].reshape(batch, seq, N_KV_HEADS, HEAD_DIM))
        new_v.append(kv32[:n_ctx, KV_WIDTH:].reshape(batch, seq, N_KV_HEADS, HEAD_DIM))
        sink = attn_sink[l]
        a_ctx = _ctx_attn_call(sink, q, k, v, batch, seq)
        a_lat = _lat_attn_call(sink, q, k, v, cache_k[:, l].reshape(dec_batch, past, KV_WIDTH).astype(BF16),
                               cache_v[:, l].reshape(dec_batch, past, KV_WIDTH).astype(BF16), geom)
        bias_full = jnp.repeat(b_s[l].T, SGU_GROUP_DIM, axis=1)
        i = l // 2
        moe = l % 2 == 1
        w_r = None
        if moe:
            w_r = jnp.pad(w_router[i], ((0, 0), (0, LANES - N_EXPERTS)))
        outs = _out_call(a_ctx, a_lat, u, g, w_s[l].astype(BF16), bias_full, x, mods, w_o[l].astype(BF16), ln1_g[l].reshape(1, d), ln1_b[l].reshape(1, d),
                         w_r, alpha, geom)
        ln_g, ln_b = ln2_g[l].reshape(1, d), ln2_b[l].reshape(1, d)
        if moe:
            x1, h, routing = outs
            plan = _route_plan(routing, n_tok, FFN_TILE)
            eo = _moe_call(h, w_exp_gate[i].astype(BF16), w_exp_up[i].astype(BF16), w_exp_down[i].astype(BF16), plan)
            x = _combine_call(eo, routing, x1, mods, ln_g, ln_b, alpha, geom, split_out=l == depth - 1)
        else:
            x1, h = outs
            x = _ffn_call(h, x1, mods, w_ff_gate[i].astype(BF16), w_ff_up[i].astype(BF16),
                          w_ff_down[i].astype(BF16), ln_g, ln_b, alpha, geom)

    if not isinstance(x, (tuple, list)):
        x = (x[:n_ctx], x[n_ctx:])
    y_prompt = x[0].reshape(batch, seq, d)
    y_sample = x[1].reshape(dec_batch, dec_seq, d)
    return (y_prompt, y_sample, jnp.stack(new_k, axis=1), jnp.stack(new_v, axis=1))
```

```python
import functools

import jax
import jax.numpy as jnp
import numpy as np
from jax import lax
from jax.experimental import pallas as pl
from jax.experimental.pallas import tpu as pltpu

F32 = jnp.float32
BF16 = jnp.bfloat16

D_MODEL = 1024
HEAD_DIM = 64
N_Q_HEADS = 8
N_KV_HEADS = 2
Q_PER_KV = N_Q_HEADS // N_KV_HEADS
ATTN_WIDTH = N_Q_HEADS * HEAD_DIM
KV_WIDTH = N_KV_HEADS * HEAD_DIM
SGU_WIDTH = D_MODEL - ATTN_WIDTH
N_SGU_GROUPS = 8
SGU_GROUP_DIM = SGU_WIDTH // N_SGU_GROUPS
CHUNK = 128
BLOCK = 128
WINDOW = 128
GRID_W = 64
IN_WIDTH = ATTN_WIDTH + 2 * KV_WIDTH + 2 * SGU_WIDTH
OFF_K = ATTN_WIDTH
OFF_V = OFF_K + KV_WIDTH
OFF_U = OFF_V + KV_WIDTH
OFF_G = OFF_U + SGU_WIDTH
N_EXPERTS = 8
TOP_K = 2
ROPE_BASE = 10000.0
ROPE_PAIRS = HEAD_DIM // 4
LN_EPS = 1e-5
ATTN_SCALE = HEAD_DIM ** -0.5
NEG_INF = -1e30
N_MODS = 6

LANES = 128
V7X_MXU_WIDTH = 256
V7X_VMEM_LIMIT_BYTES = 56 * 1024 * 1024

IN_TILE = 1024
OUT_TILE = 512
ROUTER_TILE = 256
COMBINE_TILE = 512
DENSE_TILE = 1024
FFN_TILE = 512
LAT_QUERY_ROWS = 512
CTX_SEQS_PER_STEP = 1
MOD_ROWS = 8
ADA_COLS = 1536


def _params(sem):
    return pltpu.CompilerParams(dimension_semantics=sem, vmem_limit_bytes=V7X_VMEM_LIMIT_BYTES)


def _gelu_tanh(x):
    return x * (0.5 * (1.0 + jnp.tanh(np.float32(np.sqrt(2.0 / np.pi)) * (x + 0.044715 * (x * x * x)))))


def _sigmoid(x):
    return 1.0 / (1.0 + jnp.exp(-x))


def _layer_norm_rows(t, g, b):
    mu = jnp.mean(t, axis=-1, keepdims=True)
    tc = t - mu
    var = jnp.mean(tc * tc, axis=-1, keepdims=True)
    return tc * lax.rsqrt(var + LN_EPS) * g + b


def _rows_to_tiles(x):
    pieces = jnp.stack([x[:, s * LANES:(s + 1) * LANES] for s in range(x.shape[1] // LANES)], axis=0)
    return jnp.swapaxes(pieces, 0, 1)


def _tiles_to_rows(t):
    pieces = jnp.swapaxes(t, 0, 1)
    return jnp.concatenate([pieces[s] for s in range(t.shape[1])], axis=1)


def _stream_specs(width, ctx_tiles, tm):
    return [pl.BlockSpec((tm, width), lambda i, *_: (jnp.minimum(i, ctx_tiles - 1), 0)),
            pl.BlockSpec((tm, width), lambda i, *_: (jnp.maximum(i - ctx_tiles, 0), 0))]


def _read_stream(ctx_ref, lat_ref, ctx_tiles):
    tile = lax.broadcasted_iota(jnp.int32, ctx_ref.shape, 0) * 0 + pl.program_id(0)
    return jnp.where(tile < ctx_tiles, ctx_ref[...], lat_ref[...])


def _write_stream(ctx_ref, lat_ref, ctx_tiles, value):
    @pl.when(pl.program_id(0) < ctx_tiles)
    def _():
        ctx_ref[...] = value

    @pl.when(pl.program_id(0) >= ctx_tiles)
    def _():
        lat_ref[...] = value


def _ada_kernel(c_ref, w_ref, b_ref, o_ref):
    c = c_ref[...]
    s = (c * _sigmoid(c)).astype(BF16)
    o_ref[...] = jnp.dot(s, w_ref[...].astype(BF16), preferred_element_type=F32) + b_ref[...]


def _ada_call(cvec, w_ada, b_ada):
    depth = w_ada.shape[0]
    n_out = w_ada.shape[2]
    return pl.pallas_call(
        _ada_kernel,
        grid=(depth, n_out // ADA_COLS),
        in_specs=[
            pl.BlockSpec((MOD_ROWS, D_MODEL), lambda l, j: (0, 0)),
            pl.BlockSpec((None, D_MODEL, ADA_COLS), lambda l, j: (l, 0, j)),
            pl.BlockSpec((None, 1, ADA_COLS), lambda l, j: (l, 0, j)),
        ],
        out_specs=pl.BlockSpec((None, MOD_ROWS, ADA_COLS), lambda l, j: (l, 0, j)),
        out_shape=jax.ShapeDtypeStruct((depth, MOD_ROWS, n_out), F32),
        compiler_params=_params(("arbitrary", "arbitrary")),
        name="adaln",
    )(cvec, w_ada, b_ada.reshape(depth, 1, n_out))


def _in_kernel(split, ctx_tiles, *refs):
    if split:
        x = _read_stream(refs[0], refs[1], ctx_tiles)
        refs = refs[2:]
    else:
        x = refs[0][...]
        refs = refs[1:]
    mod_ref, w_ref, cos_ref, sin_ref, lng_ref, lnb_ref, q_ref, k_ref, v_ref, kv_ref, u_ref, g_ref = refs
    tm = x.shape[0]
    h = x * (1.0 + mod_ref[1:2, :]) + mod_ref[0:1, :]
    hb = h.astype(BF16)

    def z_block(start):
        return jnp.dot(hb, w_ref[:, start:start + V7X_MXU_WIDTH], preferred_element_type=F32)

    cos = cos_ref[...]
    sin = sin_ref[...]
    lane = lax.broadcasted_iota(jnp.int32, (tm, LANES), 1)
    first_of_pair = (lane & (2 * ROPE_PAIRS - 1)) < ROPE_PAIRS
    lower_group = lane < SGU_GROUP_DIM

    def rope(t):
        partner = jnp.where(first_of_pair, pltpu.roll(t, LANES - ROPE_PAIRS, 1), pltpu.roll(t, ROPE_PAIRS, 1))
        return t * cos + partner * sin

    halves = (slice(0, LANES), slice(LANES, 2 * LANES))
    inv_n = 1.0 / SGU_GROUP_DIM
    for j in range(SGU_WIDTH // LANES):
        cols = slice(j * LANES, (j + 1) * LANES)
        if j % 2 == 0:
            z = z_block(OFF_G + j * LANES)
        t = _gelu_tanh(z[:, halves[j % 2]])
        s_lo = jnp.sum(jnp.where(lower_group, t, 0.0), axis=-1, keepdims=True)
        s_hi = jnp.sum(jnp.where(lower_group, 0.0, t), axis=-1, keepdims=True)
        tc = t - jnp.where(lower_group, s_lo, s_hi) * inv_n
        sq = tc * tc
        v_lo = jnp.sum(jnp.where(lower_group, sq, 0.0), axis=-1, keepdims=True)
        v_hi = jnp.sum(jnp.where(lower_group, 0.0, sq), axis=-1, keepdims=True)
        var = jnp.where(lower_group, v_lo, v_hi) * inv_n
        g_ref[:, cols] = (tc * lax.rsqrt(var + LN_EPS) * lng_ref[:, cols] + lnb_ref[:, cols]).astype(BF16)

    for start in range(0, ATTN_WIDTH, V7X_MXU_WIDTH):
        z = z_block(start)
        for half in halves:
            q_ref[:, start + half.start:start + half.stop] = (rope(z[:, half]) * ATTN_SCALE).astype(BF16)
    z = z_block(OFF_K)
    k_ref[...] = rope(z[:, halves[0]]).astype(BF16)
    v_ref[...] = z[:, halves[1]].astype(BF16)
    kv_ref[...] = z
    for start in range(0, SGU_WIDTH, V7X_MXU_WIDTH):
        u_ref[:, start:start + V7X_MXU_WIDTH] = _gelu_tanh(z_block(OFF_U + start))


def _in_call(x, mods, w_in, cos_t, sin_t, ln_g, ln_b, geom):
    n = geom.n_tok
    tm = IN_TILE
    mod_idx = lambda i: geom.mod_index(i, tm)
    rope_idx = lambda i: geom.rope_index(i, tm)
    row = lambda i: (i, 0)
    split = isinstance(x, tuple)
    x_args = list(x) if split else [x]
    ctx_tiles = geom.ctx_tiles(tm)
    x_specs = _stream_specs(D_MODEL, ctx_tiles, tm) if split else [pl.BlockSpec((tm, D_MODEL), row)]
    outs = pl.pallas_call(
        functools.partial(_in_kernel, split, ctx_tiles),
        grid=(n // tm,),
        in_specs=x_specs + [
            pl.BlockSpec((None, N_MODS, D_MODEL), lambda i: (mod_idx(i), 0, 0)),
            pl.BlockSpec((D_MODEL, IN_WIDTH), lambda i: (0, 0)),
            pl.BlockSpec((tm, LANES), lambda i: (rope_idx(i), 0)),
            pl.BlockSpec((tm, LANES), lambda i: (rope_idx(i), 0)),
            pl.BlockSpec((1, SGU_WIDTH), lambda i: (0, 0)),
            pl.BlockSpec((1, SGU_WIDTH), lambda i: (0, 0)),
        ],
        out_specs=[
            pl.BlockSpec((tm, ATTN_WIDTH), row),
            pl.BlockSpec((tm, KV_WIDTH), row),
            pl.BlockSpec((tm, KV_WIDTH), row),
            pl.BlockSpec((tm, 2 * KV_WIDTH), row),
            pl.BlockSpec((tm, SGU_WIDTH), row),
            pl.BlockSpec((tm, SGU_WIDTH), row),
        ],
        out_shape=[
            jax.ShapeDtypeStruct((n, ATTN_WIDTH), BF16),
            jax.ShapeDtypeStruct((n, KV_WIDTH), BF16),
            jax.ShapeDtypeStruct((n, KV_WIDTH), BF16),
            jax.ShapeDtypeStruct((n, 2 * KV_WIDTH), F32),
            jax.ShapeDtypeStruct((n, SGU_WIDTH), F32),
            jax.ShapeDtypeStruct((n, SGU_WIDTH), BF16),
        ],
        compiler_params=_params(("arbitrary",)),
        name="in_proj",
    )(*x_args, mods, w_in, cos_t, sin_t, ln_g, ln_b)
    return outs


def _group_attention(q_ref, rows, hk, k_all, v_all, block_masks, sink_ref):
    m_rows = rows.stop - rows.start
    heads = [hk * Q_PER_KV + gq for gq in range(Q_PER_KV)]
    q = jnp.concatenate([q_ref[rows, h * HEAD_DIM:(h + 1) * HEAD_DIM] for h in heads], axis=0)
    s = lax.dot_general(q, k_all, (((1,), (1,)), ((), ())), preferred_element_type=F32)
    n_blocks = k_all.shape[0] // LANES
    blocks = [s[:, b * LANES:(b + 1) * LANES] for b in range(n_blocks)]
    for b, mask in block_masks.items():
        blocks[b] = jnp.where(mask, blocks[b], NEG_INF)
    head_of_row = lax.broadcasted_iota(jnp.int32, (Q_PER_KV * m_rows, 1), 0) // m_rows
    sink = jnp.zeros((Q_PER_KV * m_rows, 1), F32)
    for gq, h in enumerate(heads):
        sink = jnp.where(head_of_row == gq, sink_ref[h], sink)
    m_el = blocks[0]
    for blk in blocks[1:]:
        m_el = jnp.maximum(m_el, blk)
    m = jnp.maximum(jnp.max(m_el, axis=-1, keepdims=True), sink)
    probs = [jnp.exp(blk - m) for blk in blocks]
    l_el = probs[0]
    for p in probs[1:]:
        l_el = l_el + p
    denom = jnp.sum(l_el, axis=-1, keepdims=True) + jnp.exp(sink - m)
    p_all = jnp.concatenate([p.astype(BF16) for p in probs], axis=1)
    o = jnp.dot(p_all, v_all, preferred_element_type=F32) / denom
    return {h: o[gq * m_rows:(gq + 1) * m_rows] for gq, h in enumerate(heads)}


def _store_heads(o_ref, rows, outs):
    for h0 in range(0, N_Q_HEADS, 2):
        pair = jnp.concatenate([outs[h0], outs[h0 + 1]], axis=1)
        o_ref[rows, h0 * HEAD_DIM:(h0 + 2) * HEAD_DIM] = pair.astype(o_ref.dtype)


def _ctx_attn_kernel(seq_len, sink_ref, q_ref, k_ref, v_ref, o_ref):
    for sq in range(q_ref.shape[0] // seq_len):
        rows = slice(sq * seq_len, (sq + 1) * seq_len)
        outs = {}
        for hk in range(N_KV_HEADS):
            kv_cols = slice(hk * HEAD_DIM, (hk + 1) * HEAD_DIM)
            outs.update(_group_attention(q_ref, rows, hk, k_ref[rows, kv_cols], v_ref[rows, kv_cols], {}, sink_ref))
        _store_heads(o_ref, rows, outs)


def _ctx_attn_call(sink, q, k, v, n_seq, seq_len):
    n = n_seq * seq_len
    per_step = CTX_SEQS_PER_STEP if n_seq % CTX_SEQS_PER_STEP == 0 else 1
    rows = per_step * seq_len
    blk = lambda b: (b, 0)
    return pl.pallas_call(
        functools.partial(_ctx_attn_kernel, seq_len),
        grid=(n_seq // per_step,),
        in_specs=[
            pl.BlockSpec(memory_space=pltpu.SMEM),
            pl.BlockSpec((rows, ATTN_WIDTH), blk),
            pl.BlockSpec((rows, KV_WIDTH), blk),
            pl.BlockSpec((rows, KV_WIDTH), blk),
        ],
        out_specs=pl.BlockSpec((rows, ATTN_WIDTH), blk),
        out_shape=jax.ShapeDtypeStruct((n, ATTN_WIDTH), BF16),
        compiler_params=_params(("arbitrary",)),
        name="ctx_attention",
    )(sink, q, k, v)


def _lat_attn_kernel(sink_ref, q_ref, k_ref, v_ref, ck_ref, cv_ref, o_ref):
    blocks_per_step = q_ref.shape[0] // BLOCK
    nb = k_ref.shape[0] // BLOCK
    past_blocks = ck_ref.shape[0] // LANES
    r = lax.broadcasted_iota(jnp.int32, (Q_PER_KV * BLOCK, BLOCK), 0) & (BLOCK - 1)
    c = lax.broadcasted_iota(jnp.int32, (Q_PER_KV * BLOCK, BLOCK), 1)
    for sub in range(blocks_per_step):
        j = pl.program_id(1) * blocks_per_step + sub
        rows = slice(sub * BLOCK, (sub + 1) * BLOCK)
        mask_prev = c >= r + jnp.where(j > 0, 0, BLOCK)
        mask_next = c <= r - jnp.where(j < nb - 1, 0, BLOCK)
        prev = pl.ds(pl.multiple_of(jnp.maximum(j - 1, 0) * BLOCK, BLOCK), BLOCK)
        cur = pl.ds(pl.multiple_of(j * BLOCK, BLOCK), BLOCK)
        nxt = pl.ds(pl.multiple_of(jnp.minimum(j + 1, nb - 1) * BLOCK, BLOCK), BLOCK)
        outs = {}
        for hk in range(N_KV_HEADS):
            kv_cols = slice(hk * HEAD_DIM, (hk + 1) * HEAD_DIM)
            k_all = jnp.concatenate([ck_ref[:, kv_cols], k_ref[prev, kv_cols], k_ref[cur, kv_cols],
                                     k_ref[nxt, kv_cols]], axis=0)
            v_all = jnp.concatenate([cv_ref[:, kv_cols], v_ref[prev, kv_cols], v_ref[cur, kv_cols],
                                     v_ref[nxt, kv_cols]], axis=0)
            masks = {past_blocks: mask_prev, past_blocks + 2: mask_next}
            outs.update(_group_attention(q_ref, rows, hk, k_all, v_all, masks, sink_ref))
        _store_heads(o_ref, rows, outs)


def _lat_attn_call(sink, q, k, v, cache_k, cache_v, geom):
    qb = LAT_QUERY_ROWS
    steps = geom.dec_seq // qb
    assert geom.n_ctx % geom.dec_seq == 0 and cache_k.shape[1] % LANES == 0
    seq_base = geom.n_ctx // geom.dec_seq
    past = cache_k.shape[1]
    seq_spec = pl.BlockSpec((geom.dec_seq, KV_WIDTH), lambda b, j: (seq_base + b, 0))
    cache_spec = pl.BlockSpec((None, past, KV_WIDTH), lambda b, j: (b, 0, 0))
    return pl.pallas_call(
        _lat_attn_kernel,
        grid=(geom.dec_batch, steps),
        in_specs=[
            pl.BlockSpec(memory_space=pltpu.SMEM),
            pl.BlockSpec((qb, ATTN_WIDTH), lambda b, j: (geom.n_ctx // qb + b * steps + j, 0)),
            seq_spec, seq_spec,
            cache_spec, cache_spec,
        ],
        out_specs=pl.BlockSpec((qb, ATTN_WIDTH), lambda b, j: (b * steps + j, 0)),
        out_shape=jax.ShapeDtypeStruct((geom.dec_batch * geom.dec_seq, ATTN_WIDTH), BF16),
        compiler_params=_params(("arbitrary", "arbitrary")),
        name="lat_attention",
    )(sink, q, k, v, cache_k, cache_v)


def _spatial_gating(u_ref, g_ref, mix_ref, bias_ref, s_ref):
    tm = u_ref.shape[0]
    lane = lax.broadcasted_iota(jnp.int32, (CHUNK, LANES), 1)
    lower_group = lane < SGU_GROUP_DIM
    for ch in range(tm // CHUNK):
        rows = slice(ch * CHUNK, (ch + 1) * CHUNK)
        for p in range(SGU_WIDTH // LANES):
            cols = slice(p * LANES, (p + 1) * LANES)
            g = g_ref[rows, cols]
            zero = jnp.zeros_like(g)
            mixed = (jnp.dot(mix_ref[2 * p], jnp.where(lower_group, g, zero), preferred_element_type=F32)
                     + jnp.dot(mix_ref[2 * p + 1], jnp.where(lower_group, zero, g), preferred_element_type=F32))
            s_ref[rows, cols] = (u_ref[rows, cols] * (mixed + bias_ref[:, cols])).astype(s_ref.dtype)


def _out_kernel(alpha, with_router, split_x, ctx_tiles, a_ctx_ref, a_lat_ref, u_ref, gg_ref, mix_ref, bias_ref,
                *refs):
    s_ref = refs[-1]
    refs = refs[:-1]
    if split_x:
        x = _read_stream(refs[0], refs[1], ctx_tiles)
        refs = refs[2:]
    else:
        x = refs[0][...]
        refs = refs[1:]
    mod_ref, wa_ref, ws_ref, g_ref, b_ref = refs[:5]
    if with_router:
        wr_ref, x1_ref, h_ref, rt_ref = refs[5:]
    else:
        x1_ref, h_ref = refs[5:]
    a = _read_stream(a_ctx_ref, a_lat_ref, ctx_tiles)
    _spatial_gating(u_ref, gg_ref, mix_ref, bias_ref, s_ref)
    y = (jnp.dot(a, wa_ref[...], preferred_element_type=F32)
         + jnp.dot(s_ref[...], ws_ref[...], preferred_element_type=F32))
    t = alpha * x + mod_ref[2:3, :] * y
    x1 = _layer_norm_rows(t, g_ref[...], b_ref[...])
    x1_ref[...] = x1
    h = x1 * (1.0 + mod_ref[4:5, :]) + mod_ref[3:4, :]
    if with_router:
        h_ref[...] = _rows_to_tiles(h)
    else:
        h_ref[...] = h.astype(h_ref.dtype)
    if with_router:
        tm = h.shape[0]
        wr = wr_ref[...]
        wr_hi = wr.astype(BF16)
        wr_lo = (wr - wr_hi.astype(F32)).astype(BF16)
        h_hi = h.astype(BF16)
        h_lo = (h - h_hi.astype(F32)).astype(BF16)
        logits = (jnp.dot(h_hi, wr_hi, preferred_element_type=F32)
                  + jnp.dot(h_lo, wr_hi, preferred_element_type=F32)
                  + jnp.dot(h_hi, wr_lo, preferred_element_type=F32))
        lane = lax.broadcasted_iota(jnp.int32, (tm, LANES), 1).astype(F32)
        neg = jnp.float32(-jnp.inf)
        lg = jnp.where(lane < N_EXPERTS, logits, neg)
        m1 = jnp.max(lg, axis=-1, keepdims=True)
        i1 = jnp.min(jnp.where(lg == m1, lane, float(LANES)), axis=-1, keepdims=True)
        lg2 = jnp.where(lane == i1, neg, lg)
        m2 = jnp.max(lg2, axis=-1, keepdims=True)
        i2 = jnp.min(jnp.where(lg2 == m2, lane, float(LANES)), axis=-1, keepdims=True)
        e2 = jnp.exp(m2 - m1)
        g1 = 1.0 / (1.0 + e2)
        g2 = e2 / (1.0 + e2)
        rt = jnp.where(lane == 0, i1, jnp.where(lane == 1, i2, jnp.where(lane == 2, g1, jnp.where(lane == 3, g2, 0.0))))
        rt_ref[...] = rt


def _out_call(a_ctx, a_lat, u, gg, w_s, bias_full, x, mods, w_o, ln_g, ln_b, w_router, alpha, geom):
    n = geom.n_tok
    with_router = w_router is not None
    tm = ROUTER_TILE if with_router else OUT_TILE
    ctx_tiles = geom.ctx_tiles(tm)
    row = lambda i: (i, 0)
    const = lambda i: (0, 0)
    split_x = isinstance(x, tuple)
    x_args = list(x) if split_x else [x]
    x_specs = _stream_specs(D_MODEL, ctx_tiles, tm) if split_x else [pl.BlockSpec((tm, D_MODEL), row)]
    sgu_specs = [
        pl.BlockSpec((tm, SGU_WIDTH), row),
        pl.BlockSpec((tm, SGU_WIDTH), row),
        pl.BlockSpec((N_SGU_GROUPS, CHUNK, CHUNK), lambda i: (0, 0, 0)),
        pl.BlockSpec((CHUNK, SGU_WIDTH), const),
    ]
    in_specs = _stream_specs(ATTN_WIDTH, ctx_tiles, tm) + sgu_specs + x_specs + [
        pl.BlockSpec((None, N_MODS, D_MODEL), lambda i: (geom.mod_index(i, tm), 0, 0)),
        pl.BlockSpec((ATTN_WIDTH, D_MODEL), const),
        pl.BlockSpec((SGU_WIDTH, D_MODEL), lambda i: (1, 0)),
        pl.BlockSpec((1, D_MODEL), const),
        pl.BlockSpec((1, D_MODEL), const),
    ]
    args = [a_ctx, a_lat, u, gg, w_s, bias_full] + x_args + [mods, w_o, w_o, ln_g, ln_b]
    if with_router:
        h_spec = pl.BlockSpec((tm, D_MODEL // LANES, LANES), lambda i: (i, 0, 0))
        h_shape = jax.ShapeDtypeStruct((n, D_MODEL // LANES, LANES), F32)
    else:
        h_spec = pl.BlockSpec((tm, D_MODEL), row)
        h_shape = jax.ShapeDtypeStruct((n, D_MODEL), BF16)
    out_specs = [pl.BlockSpec((tm, D_MODEL), row), h_spec]
    out_shape = [jax.ShapeDtypeStruct((n, D_MODEL), F32), h_shape]
    if with_router:
        in_specs.append(pl.BlockSpec((D_MODEL, LANES), const))
        args.append(w_router)
        out_specs.append(pl.BlockSpec((tm, LANES), row))
        out_shape.append(jax.ShapeDtypeStruct((n, LANES), F32))
    return pl.pallas_call(
        functools.partial(_out_kernel, alpha, with_router, split_x, ctx_tiles),
        grid=(n // tm,),
        in_specs=in_specs,
        out_specs=out_specs,
        out_shape=out_shape,
        scratch_shapes=[pltpu.VMEM((tm, SGU_WIDTH), BF16)],
        compiler_params=_params(("arbitrary",)),
        name="out_proj_router" if with_router else "out_proj",
    )(*args)


def _swiglu_blocks(width):
    return [slice(s, min(s + V7X_MXU_WIDTH, width)) for s in range(0, width, V7X_MXU_WIDTH)]


def _swiglu_partial(x, wg_ref, wu_ref, wd_ref, side_work=None):
    out = None
    for n, cols in enumerate(_swiglu_blocks(wg_ref.shape[1])):
        a = jnp.dot(x, wg_ref[:, cols].astype(BF16), preferred_element_type=F32)
        if side_work is not None:
            side_work(3 * n)
        b = jnp.dot(x, wu_ref[:, cols].astype(BF16), preferred_element_type=F32)
        if side_work is not None:
            side_work(3 * n + 1)
        mid = ((a * _sigmoid(a)) * b).astype(BF16)
        part = jnp.dot(mid, wd_ref[cols, :].astype(BF16), preferred_element_type=F32)
        if side_work is not None:
            side_work(3 * n + 2)
        out = part if out is None else out + part
    return out


def _ffn_kernel(alpha, h_ref, x1_ref, mod_ref, wg_ref, wu_ref, wd_ref, g_ref, b_ref, o_ref, acc_ref):
    c = pl.program_id(1)
    part = _swiglu_partial(h_ref[...], wg_ref, wu_ref, wd_ref)

    @pl.when(c == 0)
    def _():
        acc_ref[...] = part

    @pl.when(c > 0)
    def _():
        acc_ref[...] += part

    @pl.when(c == pl.num_programs(1) - 1)
    def _():
        t = alpha * x1_ref[...] + mod_ref[5:6, :] * acc_ref[...]
        o_ref[...] = _layer_norm_rows(t, g_ref[...], b_ref[...])


def _ffn_chunk(d_ff):
    assert d_ff % (2 * LANES) == 0
    return d_ff // 2


def _ffn_call(h, x1, mods, wg, wu, wd, ln_g, ln_b, alpha, geom):
    n = h.shape[0]
    tm = DENSE_TILE
    d_ff = wg.shape[1]
    fc = _ffn_chunk(d_ff)
    row = lambda i, c: (i, 0)
    const = lambda i, c: (0, 0)
    return pl.pallas_call(
        functools.partial(_ffn_kernel, alpha),
        grid=(n // tm, d_ff // fc),
        in_specs=[
            pl.BlockSpec((tm, D_MODEL), row),
            pl.BlockSpec((tm, D_MODEL), row),
            pl.BlockSpec((None, N_MODS, D_MODEL), lambda i, c: (geom.mod_index(i, tm), 0, 0)),
            pl.BlockSpec((D_MODEL, fc), lambda i, c: (0, c)),
            pl.BlockSpec((D_MODEL, fc), lambda i, c: (0, c)),
            pl.BlockSpec((fc, D_MODEL), lambda i, c: (c, 0)),
            pl.BlockSpec((1, D_MODEL), const),
            pl.BlockSpec((1, D_MODEL), const),
        ],
        out_specs=pl.BlockSpec((tm, D_MODEL), row),
        out_shape=jax.ShapeDtypeStruct((n, D_MODEL), F32),
        scratch_shapes=[pltpu.VMEM((tm, D_MODEL), F32)],
        compiler_params=_params(("arbitrary", "arbitrary")),
        name="dense_ffn",
    )(h, x1, mods, wg, wu, wd, ln_g, ln_b)


def _moe_kernel(n_chunks, te_ref, nu_ref, src_ref, dst_ref, h_hbm, wg_ref, wu_ref, wd_ref, o_hbm,
                xbuf, xb, obuf, gsem, ssem):
    del te_ref
    tr = xb.shape[0]
    t = pl.program_id(0)
    c = pl.program_id(1)
    n_tiles = pl.num_programs(0)
    valid = t < nu_ref[0]
    slot = t % 2
    other = 1 - slot
    share = tr // n_chunks
    first_row = c * share
    nxt_base = jnp.minimum(t + 1, n_tiles - 1) * tr
    prv_base = jnp.where(t == 0, n_tiles - 1, t - 1) * tr

    def gather_row(base, dst_slot, r, tok=None):
        tok = src_ref[base + r] if tok is None else tok
        return pltpu.make_async_copy(h_hbm.at[tok], xbuf.at[dst_slot, r], gsem.at[dst_slot])

    def scatter_row(base, src_slot, r, dst=None):
        dst = dst_ref[base + r] if dst is None else dst
        return pltpu.make_async_copy(obuf.at[src_slot, r], o_hbm.at[dst], ssem.at[src_slot])

    def wait_gather(dst_slot):
        pltpu.make_async_copy(h_hbm.at[pl.ds(0, tr)], xbuf.at[dst_slot], gsem.at[dst_slot]).wait()

    def wait_scatter(src_slot):
        pltpu.make_async_copy(obuf.at[src_slot], o_hbm.at[pl.ds(0, tr)], ssem.at[src_slot]).wait()

    def looped(n_rows, start_row):
        def body(r, carry):
            start_row(r)
            return carry
        lax.fori_loop(0, n_rows, body, 0, unroll=8)

    @pl.when((t == 0) & (c == 0))
    def _():
        obuf[1] = jnp.zeros(obuf.shape[1:], obuf.dtype)
        looped(tr, lambda r: gather_row(0, 0, r).start())

    @pl.when(c == 0)
    def _():
        wait_gather(slot)

        @pl.when(t >= 1)
        def _():
            wait_scatter(slot)

    def multiply_tile(slot_s, c_s):
        other_s = 1 - slot_s
        if c_s == 0:
            xb[...] = _tiles_to_rows(xbuf[slot_s]).astype(BF16)
        n_groups = 3 * len(_swiglu_blocks(wg_ref.shape[1]))

        def row_dmas(i):
            for r in range(c_s * share + share * i // n_groups, c_s * share + share * (i + 1) // n_groups):
                gather_row(nxt_base, other_s, r).start()
                scatter_row(prv_base, other_s, r).start()

        part = _swiglu_partial(xb[...], wg_ref, wu_ref, wd_ref, row_dmas)
        if c_s == 0:
            obuf[slot_s] = _rows_to_tiles(part)
        else:
            obuf[slot_s] += _rows_to_tiles(part)

    for slot_s in range(2):
        for c_s in range(n_chunks):
            pl.when(valid & (slot == slot_s) & (c == c_s))(functools.partial(multiply_tile, slot_s, c_s))

    @pl.when(jnp.logical_not(valid))
    def _():
        @pl.when(c == 0)
        def _():
            obuf[slot] = jnp.zeros(obuf.shape[1:], obuf.dtype)

        looped(share, lambda r: gather_row(nxt_base, other, first_row + r).start())
        looped(share, lambda r: scatter_row(prv_base, other, first_row + r).start())

    @pl.when((t == n_tiles - 1) & (c == n_chunks - 1))
    def _():
        wait_scatter(other)
        looped(tr, lambda r: scatter_row(t * tr, slot, r).start())
        wait_scatter(slot)
        wait_gather(other)


def _moe_call(h, wg, wu, wd, plan):
    tr = FFN_TILE
    d_ff = wg.shape[2]
    fc = _ffn_chunk(d_ff)
    n_chunks = d_ff // fc
    t_max = plan["tile_expert"].shape[0]
    row_tile = h.shape[1:]

    def chunk(t, c, nu):
        return jnp.where(t < nu[0], c, n_chunks - 1)

    grid_spec = pltpu.PrefetchScalarGridSpec(
        num_scalar_prefetch=4,
        grid=(t_max, n_chunks),
        in_specs=[
            pl.BlockSpec(memory_space=pl.ANY),
            pl.BlockSpec((None, D_MODEL, fc), lambda t, c, te, nu, src, dst: (te[t], 0, chunk(t, c, nu))),
            pl.BlockSpec((None, D_MODEL, fc), lambda t, c, te, nu, src, dst: (te[t], 0, chunk(t, c, nu))),
            pl.BlockSpec((None, fc, D_MODEL), lambda t, c, te, nu, src, dst: (te[t], chunk(t, c, nu), 0)),
        ],
        out_specs=pl.BlockSpec(memory_space=pl.ANY),
        scratch_shapes=[
            pltpu.VMEM((2, tr) + row_tile, F32),
            pltpu.VMEM((tr, D_MODEL), BF16),
            pltpu.VMEM((2, tr) + row_tile, F32),
            pltpu.SemaphoreType.DMA((2,)),
            pltpu.SemaphoreType.DMA((2,)),
        ],
    )
    return pl.pallas_call(
        functools.partial(_moe_kernel, n_chunks),
        grid_spec=grid_spec,
        out_shape=jax.ShapeDtypeStruct((t_max * tr,) + row_tile, F32),
        compiler_params=_params(("arbitrary", "arbitrary")),
        name="expert_ffn",
    )(plan["tile_expert"], plan["n_used"], plan["src_token"], plan["dst_row"], h, wg, wu, wd)


def _route_plan(routing, n_tok, tr):
    n_assign = TOP_K * n_tok
    t_max = n_assign // tr + N_EXPERTS
    n_slots = t_max * tr
    n_pad = n_slots - n_assign
    id_bits = (n_slots - 1).bit_length()
    experts = jnp.arange(N_EXPERTS, dtype=jnp.int32)
    e_flat = routing[:, :TOP_K].astype(jnp.int32).T.reshape(-1)
    counts = jnp.sum((e_flat[:, None] == experts[None, :]).astype(jnp.int32), axis=0)
    tiles_e = (counts + tr - 1) // tr
    tile_end = jnp.cumsum(tiles_e)
    n_used = tile_end[-1]
    pad_end = jnp.cumsum(tiles_e * tr - counts)
    pad_ids = jnp.arange(n_pad, dtype=jnp.int32)
    pad_expert = jnp.sum((pad_ids[:, None] >= pad_end[None, :]).astype(jnp.int32), axis=1)
    keys = jnp.concatenate([e_flat * 2, pad_expert * 2 + 1])
    item = jnp.arange(n_slots, dtype=jnp.int32)
    slot_item = jnp.sort((keys << id_bits) | item) & ((1 << id_bits) - 1)
    real = slot_item < n_assign
    src = jnp.where(real, slot_item % n_tok, 0).astype(jnp.int32)
    dst = slot_item.astype(jnp.int32)
    t_ids = jnp.arange(t_max, dtype=jnp.int32)
    te = jnp.sum((t_ids[:, None] >= tile_end[None, :]).astype(jnp.int32), axis=1)
    last_used = jnp.max(jnp.where(tiles_e > 0, experts, 0))
    te = jnp.where(t_ids < n_used, te, last_used).astype(jnp.int32)
    return {"tile_expert": te, "n_used": n_used.reshape(1).astype(jnp.int32), "src_token": src, "dst_row": dst}


def _combine_kernel(alpha, split_ctx_tiles, e1_ref, e2_ref, rt_ref, x1_ref, mod_ref, g_ref, b_ref, *o_refs):
    y = rt_ref[:, 2:3] * _tiles_to_rows(e1_ref[...]) + rt_ref[:, 3:4] * _tiles_to_rows(e2_ref[...])
    t = alpha * x1_ref[...] + mod_ref[5:6, :] * y
    out = _layer_norm_rows(t, g_ref[...], b_ref[...])
    if split_ctx_tiles is None:
        o_refs[0][...] = out
    else:
        _write_stream(o_refs[0], o_refs[1], split_ctx_tiles, out)


def _combine_call(eo, routing, x1, mods, ln_g, ln_b, alpha, geom, split_out):
    n = x1.shape[0]
    tm = COMBINE_TILE
    row = lambda i: (i, 0)
    const = lambda i: (0, 0)
    if split_out:
        out_specs = _stream_specs(D_MODEL, geom.ctx_tiles(tm), tm)
        out_shape = [jax.ShapeDtypeStruct((geom.n_ctx, D_MODEL), F32),
                     jax.ShapeDtypeStruct((n - geom.n_ctx, D_MODEL), F32)]
    else:
        out_specs = pl.BlockSpec((tm, D_MODEL), row)
        out_shape = jax.ShapeDtypeStruct((n, D_MODEL), F32)
    return pl.pallas_call(
        functools.partial(_combine_kernel, alpha, geom.ctx_tiles(tm) if split_out else None),
        grid=(n // tm,),
        in_specs=[
            pl.BlockSpec((tm,) + eo.shape[1:], lambda i: (i, 0, 0)),
            pl.BlockSpec((tm,) + eo.shape[1:], lambda i: (i + n // tm, 0, 0)),
            pl.BlockSpec((tm, LANES), row),
            pl.BlockSpec((tm, D_MODEL), row),
            pl.BlockSpec((None, N_MODS, D_MODEL), lambda i: (geom.mod_index(i, tm), 0, 0)),
            pl.BlockSpec((1, D_MODEL), const),
            pl.BlockSpec((1, D_MODEL), const),
        ],
        out_specs=out_specs,
        out_shape=out_shape,
        compiler_params=_params(("arbitrary",)),
        name="expert_combine",
    )(eo, eo, routing, x1, mods, ln_g, ln_b)


class _Geometry:
    def __init__(self, n_ctx, dec_batch, dec_seq):
        self.n_ctx = n_ctx
        self.dec_batch = dec_batch
        self.dec_seq = dec_seq
        self.n_tok = n_ctx + dec_batch * dec_seq

    def ctx_tiles(self, tm):
        assert self.n_ctx % tm == 0 and self.dec_seq % tm == 0
        return self.n_ctx // tm

    def mod_index(self, i, tm):
        ct = self.ctx_tiles(tm)
        return jnp.where(i < ct, 0, 1 + (i - ct) // (self.dec_seq // tm))

    def rope_index(self, i, tm):
        ct = self.ctx_tiles(tm)
        return jnp.where(i < ct, 0, 1 + (i - ct) % (self.dec_seq // tm))


def _rope_tables(dec_seq):
    pos = jnp.arange(dec_seq, dtype=jnp.int32)
    row = (pos // GRID_W).astype(F32)
    col = (pos % GRID_W).astype(F32)
    inv = ROPE_BASE ** (-jnp.arange(ROPE_PAIRS, dtype=F32) / ROPE_PAIRS)
    ang_r = row[:, None] * inv[None, :]
    ang_c = col[:, None] * inv[None, :]
    cos_h = jnp.concatenate([jnp.cos(ang_r), jnp.cos(ang_r), jnp.cos(ang_c), jnp.cos(ang_c)], axis=-1)
    sin_h = jnp.concatenate([-jnp.sin(ang_r), jnp.sin(ang_r), -jnp.sin(ang_c), jnp.sin(ang_c)], axis=-1)
    reps = LANES // HEAD_DIM
    cos_t = jnp.concatenate([jnp.ones((IN_TILE, LANES), F32), jnp.tile(cos_h, (1, reps))], axis=0)
    sin_t = jnp.concatenate([jnp.zeros((IN_TILE, LANES), F32), jnp.tile(sin_h, (1, reps))], axis=0)
    return cos_t, sin_t


def kernel(x_prompt, x_sample, cache_k, cache_v, c, c_ctx, w_ada, b_ada, w_in, w_o, attn_sink, w_s, b_s, sgu_ln_g, sgu_ln_b, ln1_g, ln1_b, ln2_g, ln2_b, w_ff_gate, w_ff_up, w_ff_down, w_router, w_exp_gate, w_exp_up, w_exp_down):
    batch, seq, d = x_prompt.shape
    dec_batch, dec_seq, _ = x_sample.shape
    depth = w_in.shape[0]
    past = cache_k.shape[2]
    assert d == D_MODEL and dec_batch + 1 <= MOD_ROWS
    n_ctx = batch * seq
    n_lat = dec_batch * dec_seq
    n_tok = n_ctx + n_lat
    geom = _Geometry(n_ctx, dec_batch, dec_seq)
    alpha = float((2 * depth) ** 0.25)

    x = (x_prompt.reshape(n_ctx, d), x_sample.reshape(n_lat, d))
    cvec = jnp.concatenate([c_ctx[None, :], c, jnp.zeros((MOD_ROWS - 1 - dec_batch, d), F32)], axis=0)
    mods_all = _ada_call(cvec, w_ada, b_ada).reshape(depth, MOD_ROWS, N_MODS, d)
    cos_t, sin_t = _rope_tables(dec_seq)

    new_k, new_v = [], []
    for l in range(depth):
        mods = mods_all[l]
        q, k, v, kv32, u, g = _in_call(x, mods, w_in[l].astype(BF16), cos_t, sin_t,
                                       sgu_ln_g[l].reshape(1, SGU_WIDTH), sgu_ln_b[l].reshape(1, SGU_WIDTH), geom)
        new_k.append(kv32[:n_ctx, :KV_WIDTH].reshape(batch, seq, N_KV_HEADS, HEAD_DIM))
        new_v.append(kv32[:n_ctx, KV_WIDTH:].reshape(batch, seq, N_KV_HEADS, HEAD_DIM))
        sink = attn_sink[l]
        a_ctx = _ctx_attn_call(sink, q, k, v, batch, seq)
        a_lat = _lat_attn_call(sink, q, k, v, cache_k[:, l].reshape(dec_batch, past, KV_WIDTH).astype(BF16),
                               cache_v[:, l].reshape(dec_batch, past, KV_WIDTH).astype(BF16), geom)
        bias_full = jnp.repeat(b_s[l].T, SGU_GROUP_DIM, axis=1)
        i = l // 2
        moe = l % 2 == 1
        w_r = None
        if moe:
            w_r = jnp.pad(w_router[i], ((0, 0), (0, LANES - N_EXPERTS)))
        outs = _out_call(a_ctx, a_lat, u, g, w_s[l].astype(BF16), bias_full, x, mods, w_o[l].astype(BF16), ln1_g[l].reshape(1, d), ln1_b[l].reshape(1, d),
                         w_r, alpha, geom)
        ln_g, ln_b = ln2_g[l].reshape(1, d), ln2_b[l].reshape(1, d)
        if moe:
            x1, h, routing = outs
            plan = _route_plan(routing, n_tok, FFN_TILE)
            eo = _moe_call(h, w_exp_gate[i], w_exp_up[i], w_exp_down[i], plan)
            x = _combine_call(eo, routing, x1, mods, ln_g, ln_b, alpha, geom, split_out=l == depth - 1)
        else:
            x1, h = outs
            x = _ffn_call(h, x1, mods, w_ff_gate[i].astype(BF16), w_ff_up[i].astype(BF16),
                          w_ff_down[i].astype(BF16), ln_g, ln_b, alpha, geom)

    if not isinstance(x, (tuple, list)):
        x = (x[:n_ctx], x[n_ctx:])
    y_prompt = x[0].reshape(batch, seq, d)
    y_sample = x[1].reshape(dec_batch, dec_seq, d)
    return (y_prompt, y_sample, jnp.stack(new_k, axis=1), jnp.stack(new_v, axis=1))
```

```python
import functools

import jax
import jax.numpy as jnp
import numpy as np
from jax import lax
from jax.experimental import pallas as pl
from jax.experimental.pallas import tpu as pltpu

F32 = jnp.float32
BF16 = jnp.bfloat16

D_MODEL = 1024
HEAD_DIM = 64
N_Q_HEADS = 8
N_KV_HEADS = 2
Q_PER_KV = N_Q_HEADS // N_KV_HEADS
ATTN_WIDTH = N_Q_HEADS * HEAD_DIM
KV_WIDTH = N_KV_HEADS * HEAD_DIM
SGU_WIDTH = D_MODEL - ATTN_WIDTH
N_SGU_GROUPS = 8
SGU_GROUP_DIM = SGU_WIDTH // N_SGU_GROUPS
CHUNK = 128
BLOCK = 128
WINDOW = 128
GRID_W = 64
IN_WIDTH = ATTN_WIDTH + 2 * KV_WIDTH + 2 * SGU_WIDTH
OFF_K = ATTN_WIDTH
OFF_V = OFF_K + KV_WIDTH
OFF_U = OFF_V + KV_WIDTH
OFF_G = OFF_U + SGU_WIDTH
N_EXPERTS = 8
TOP_K = 2
ROPE_BASE = 10000.0
ROPE_PAIRS = HEAD_DIM // 4
LN_EPS = 1e-5
ATTN_SCALE = HEAD_DIM ** -0.5
NEG_INF = -1e30
N_MODS = 6

LANES = 128
V7X_MXU_WIDTH = 256
V7X_VMEM_LIMIT_BYTES = 56 * 1024 * 1024

IN_TILE = 1024
OUT_TILE = 512
ROUTER_TILE = 256
COMBINE_TILE = 512
DENSE_TILE = 1024
FFN_TILE = 512
LAT_QUERY_ROWS = 512
CTX_SEQS_PER_STEP = 1
MOD_ROWS = 8
ADA_COLS = 1536


def _params(sem):
    return pltpu.CompilerParams(dimension_semantics=sem, vmem_limit_bytes=V7X_VMEM_LIMIT_BYTES)


def _gelu_tanh(x):
    return x * (0.5 * (1.0 + jnp.tanh(np.float32(np.sqrt(2.0 / np.pi)) * (x + 0.044715 * (x * x * x)))))


def _sigmoid(x):
    return 1.0 / (1.0 + jnp.exp(-x))


def _layer_norm_rows(t, g, b):
    mu = jnp.mean(t, axis=-1, keepdims=True)
    tc = t - mu
    var = jnp.mean(tc * tc, axis=-1, keepdims=True)
    return tc * lax.rsqrt(var + LN_EPS) * g + b


def _rows_to_tiles(x):
    pieces = jnp.stack([x[:, s * LANES:(s + 1) * LANES] for s in range(x.shape[1] // LANES)], axis=0)
    return jnp.swapaxes(pieces, 0, 1)


def _tiles_to_rows(t):
    pieces = jnp.swapaxes(t, 0, 1)
    return jnp.concatenate([pieces[s] for s in range(t.shape[1])], axis=1)


def _stream_specs(width, ctx_tiles, tm):
    return [pl.BlockSpec((tm, width), lambda i, *_: (jnp.minimum(i, ctx_tiles - 1), 0)),
            pl.BlockSpec((tm, width), lambda i, *_: (jnp.maximum(i - ctx_tiles, 0), 0))]


def _read_stream(ctx_ref, lat_ref, ctx_tiles):
    tile = lax.broadcasted_iota(jnp.int32, ctx_ref.shape, 0) * 0 + pl.program_id(0)
    return jnp.where(tile < ctx_tiles, ctx_ref[...], lat_ref[...])


def _write_stream(ctx_ref, lat_ref, ctx_tiles, value):
    @pl.when(pl.program_id(0) < ctx_tiles)
    def _():
        ctx_ref[...] = value

    @pl.when(pl.program_id(0) >= ctx_tiles)
    def _():
        lat_ref[...] = value


def _ada_kernel(c_ref, w_ref, b_ref, o_ref):
    c = c_ref[...]
    s = (c * _sigmoid(c)).astype(BF16)
    o_ref[...] = jnp.dot(s, w_ref[...].astype(BF16), preferred_element_type=F32) + b_ref[...]


def _ada_call(cvec, w_ada, b_ada):
    depth = w_ada.shape[0]
    n_out = w_ada.shape[2]
    return pl.pallas_call(
        _ada_kernel,
        grid=(depth, n_out // ADA_COLS),
        in_specs=[
            pl.BlockSpec((MOD_ROWS, D_MODEL), lambda l, j: (0, 0)),
            pl.BlockSpec((None, D_MODEL, ADA_COLS), lambda l, j: (l, 0, j)),
            pl.BlockSpec((None, 1, ADA_COLS), lambda l, j: (l, 0, j)),
        ],
        out_specs=pl.BlockSpec((None, MOD_ROWS, ADA_COLS), lambda l, j: (l, 0, j)),
        out_shape=jax.ShapeDtypeStruct((depth, MOD_ROWS, n_out), F32),
        compiler_params=_params(("arbitrary", "arbitrary")),
        name="adaln",
    )(cvec, w_ada, b_ada.reshape(depth, 1, n_out))


def _in_kernel(split, ctx_tiles, *refs):
    if split:
        x = _read_stream(refs[0], refs[1], ctx_tiles)
        refs = refs[2:]
    else:
        x = refs[0][...]
        refs = refs[1:]
    mod_ref, w_ref, cos_ref, sin_ref, lng_ref, lnb_ref, q_ref, k_ref, v_ref, kv_ref, u_ref, g_ref, wb_ref = refs
    tm = x.shape[0]

    @pl.when(pl.program_id(0) == 0)
    def _():
        wb_ref[...] = w_ref[...].astype(BF16)

    h = x * (1.0 + mod_ref[1:2, :]) + mod_ref[0:1, :]
    hb = h.astype(BF16)

    def z_block(start):
        return jnp.dot(hb, wb_ref[:, start:start + V7X_MXU_WIDTH], preferred_element_type=F32)

    cos = cos_ref[...]
    sin = sin_ref[...]
    lane = lax.broadcasted_iota(jnp.int32, (tm, LANES), 1)
    first_of_pair = (lane & (2 * ROPE_PAIRS - 1)) < ROPE_PAIRS
    lower_group = lane < SGU_GROUP_DIM

    def rope(t):
        partner = jnp.where(first_of_pair, pltpu.roll(t, LANES - ROPE_PAIRS, 1), pltpu.roll(t, ROPE_PAIRS, 1))
        return t * cos + partner * sin

    halves = (slice(0, LANES), slice(LANES, 2 * LANES))
    inv_n = 1.0 / SGU_GROUP_DIM
    for j in range(SGU_WIDTH // LANES):
        cols = slice(j * LANES, (j + 1) * LANES)
        if j % 2 == 0:
            z = z_block(OFF_G + j * LANES)
        t = _gelu_tanh(z[:, halves[j % 2]])
        s_lo = jnp.sum(jnp.where(lower_group, t, 0.0), axis=-1, keepdims=True)
        s_hi = jnp.sum(jnp.where(lower_group, 0.0, t), axis=-1, keepdims=True)
        tc = t - jnp.where(lower_group, s_lo, s_hi) * inv_n
        sq = tc * tc
        v_lo = jnp.sum(jnp.where(lower_group, sq, 0.0), axis=-1, keepdims=True)
        v_hi = jnp.sum(jnp.where(lower_group, 0.0, sq), axis=-1, keepdims=True)
        var = jnp.where(lower_group, v_lo, v_hi) * inv_n
        g_ref[:, cols] = (tc * lax.rsqrt(var + LN_EPS) * lng_ref[:, cols] + lnb_ref[:, cols]).astype(BF16)

    for start in range(0, ATTN_WIDTH, V7X_MXU_WIDTH):
        z = z_block(start)
        for half in halves:
            q_ref[:, start + half.start:start + half.stop] = (rope(z[:, half]) * ATTN_SCALE).astype(BF16)
    z = z_block(OFF_K)
    k_ref[...] = rope(z[:, halves[0]]).astype(BF16)
    v_ref[...] = z[:, halves[1]].astype(BF16)
    kv_ref[...] = z
    for start in range(0, SGU_WIDTH, V7X_MXU_WIDTH):
        u_ref[:, start:start + V7X_MXU_WIDTH] = _gelu_tanh(z_block(OFF_U + start))


def _in_call(x, mods, w_in, layer, cos_t, sin_t, ln_g, ln_b, geom):
    n = geom.n_tok
    tm = IN_TILE
    mod_idx = lambda i: geom.mod_index(i, tm)
    rope_idx = lambda i: geom.rope_index(i, tm)
    row = lambda i: (i, 0)
    split = isinstance(x, tuple)
    x_args = list(x) if split else [x]
    ctx_tiles = geom.ctx_tiles(tm)
    x_specs = _stream_specs(D_MODEL, ctx_tiles, tm) if split else [pl.BlockSpec((tm, D_MODEL), row)]
    outs = pl.pallas_call(
        functools.partial(_in_kernel, split, ctx_tiles),
        grid=(n // tm,),
        in_specs=x_specs + [
            pl.BlockSpec((None, N_MODS, D_MODEL), lambda i: (mod_idx(i), 0, 0)),
            pl.BlockSpec((None, D_MODEL, IN_WIDTH), lambda i: (layer, 0, 0)),
            pl.BlockSpec((tm, LANES), lambda i: (rope_idx(i), 0)),
            pl.BlockSpec((tm, LANES), lambda i: (rope_idx(i), 0)),
            pl.BlockSpec((1, SGU_WIDTH), lambda i: (0, 0)),
            pl.BlockSpec((1, SGU_WIDTH), lambda i: (0, 0)),
        ],
        out_specs=[
            pl.BlockSpec((tm, ATTN_WIDTH), row),
            pl.BlockSpec((tm, KV_WIDTH), row),
            pl.BlockSpec((tm, KV_WIDTH), row),
            pl.BlockSpec((tm, 2 * KV_WIDTH), row),
            pl.BlockSpec((tm, SGU_WIDTH), row),
            pl.BlockSpec((tm, SGU_WIDTH), row),
        ],
        out_shape=[
            jax.ShapeDtypeStruct((n, ATTN_WIDTH), BF16),
            jax.ShapeDtypeStruct((n, KV_WIDTH), BF16),
            jax.ShapeDtypeStruct((n, KV_WIDTH), BF16),
            jax.ShapeDtypeStruct((n, 2 * KV_WIDTH), F32),
            jax.ShapeDtypeStruct((n, SGU_WIDTH), F32),
            jax.ShapeDtypeStruct((n, SGU_WIDTH), BF16),
        ],
        scratch_shapes=[pltpu.VMEM((D_MODEL, IN_WIDTH), BF16)],
        compiler_params=_params(("arbitrary",)),
        name="in_proj",
    )(*x_args, mods, w_in, cos_t, sin_t, ln_g, ln_b)
    return outs


def _group_attention(q_ref, rows, hk, k_all, v_all, block_masks, sink_ref):
    m_rows = rows.stop - rows.start
    heads = [hk * Q_PER_KV + gq for gq in range(Q_PER_KV)]
    q = jnp.concatenate([q_ref[rows, h * HEAD_DIM:(h + 1) * HEAD_DIM] for h in heads], axis=0)
    s = lax.dot_general(q, k_all, (((1,), (1,)), ((), ())), preferred_element_type=F32)
    n_blocks = k_all.shape[0] // LANES
    blocks = [s[:, b * LANES:(b + 1) * LANES] for b in range(n_blocks)]
    for b, mask in block_masks.items():
        blocks[b] = jnp.where(mask, blocks[b], NEG_INF)
    head_of_row = lax.broadcasted_iota(jnp.int32, (len(heads) * m_rows, 1), 0) // m_rows
    sink = jnp.zeros((len(heads) * m_rows, 1), F32)
    for gq, h in enumerate(heads):
        sink = jnp.where(head_of_row == gq, sink_ref[h], sink)
    m_el = blocks[0]
    for blk in blocks[1:]:
        m_el = jnp.maximum(m_el, blk)
    m = jnp.maximum(jnp.max(m_el, axis=-1, keepdims=True), sink)
    probs = [jnp.exp(blk - m) for blk in blocks]
    l_el = probs[0]
    for p in probs[1:]:
        l_el = l_el + p
    denom = jnp.sum(l_el, axis=-1, keepdims=True) + jnp.exp(sink - m)
    p_all = jnp.concatenate([p.astype(BF16) for p in probs], axis=1)
    o = jnp.dot(p_all, v_all, preferred_element_type=F32) / denom
    return {h: o[gq * m_rows:(gq + 1) * m_rows] for gq, h in enumerate(heads)}


def _store_heads(o_ref, rows, outs):
    for h0 in range(0, N_Q_HEADS, 2):
        pair = jnp.concatenate([outs[h0], outs[h0 + 1]], axis=1)
        o_ref[rows, h0 * HEAD_DIM:(h0 + 2) * HEAD_DIM] = pair.astype(o_ref.dtype)


def _ctx_attn_kernel(seq_len, sink_ref, q_ref, k_ref, v_ref, o_ref):
    for sq in range(q_ref.shape[0] // seq_len):
        rows = slice(sq * seq_len, (sq + 1) * seq_len)
        outs = {}
        for hk in range(N_KV_HEADS):
            kv_cols = slice(hk * HEAD_DIM, (hk + 1) * HEAD_DIM)
            outs.update(_group_attention(q_ref, rows, hk, k_ref[rows, kv_cols], v_ref[rows, kv_cols], {}, sink_ref))
        _store_heads(o_ref, rows, outs)


def _ctx_attn_call(sink, q, k, v, n_seq, seq_len):
    n = n_seq * seq_len
    per_step = CTX_SEQS_PER_STEP if n_seq % CTX_SEQS_PER_STEP == 0 else 1
    rows = per_step * seq_len
    blk = lambda b: (b, 0)
    return pl.pallas_call(
        functools.partial(_ctx_attn_kernel, seq_len),
        grid=(n_seq // per_step,),
        in_specs=[
            pl.BlockSpec(memory_space=pltpu.SMEM),
            pl.BlockSpec((rows, ATTN_WIDTH), blk),
            pl.BlockSpec((rows, KV_WIDTH), blk),
            pl.BlockSpec((rows, KV_WIDTH), blk),
        ],
        out_specs=pl.BlockSpec((rows, ATTN_WIDTH), blk),
        out_shape=jax.ShapeDtypeStruct((n, ATTN_WIDTH), BF16),
        compiler_params=_params(("arbitrary",)),
        name="ctx_attention",
    )(sink, q, k, v)


def _lat_attn_kernel(sink_ref, q_ref, k_ref, v_ref, ck_ref, cv_ref, o_ref):
    blocks_per_step = q_ref.shape[0] // BLOCK
    nb = k_ref.shape[0] // BLOCK
    past_blocks = ck_ref.shape[0] // LANES
    r = lax.broadcasted_iota(jnp.int32, (Q_PER_KV * BLOCK, BLOCK), 0) & (BLOCK - 1)
    c = lax.broadcasted_iota(jnp.int32, (Q_PER_KV * BLOCK, BLOCK), 1)
    for sub in range(blocks_per_step):
        j = pl.program_id(1) * blocks_per_step + sub
        rows = slice(sub * BLOCK, (sub + 1) * BLOCK)
        mask_prev = c >= r + jnp.where(j > 0, 0, BLOCK)
        mask_next = c <= r - jnp.where(j < nb - 1, 0, BLOCK)
        prev = pl.ds(pl.multiple_of(jnp.maximum(j - 1, 0) * BLOCK, BLOCK), BLOCK)
        cur = pl.ds(pl.multiple_of(j * BLOCK, BLOCK), BLOCK)
        nxt = pl.ds(pl.multiple_of(jnp.minimum(j + 1, nb - 1) * BLOCK, BLOCK), BLOCK)
        outs = {}
        for hk in range(N_KV_HEADS):
            kv_cols = slice(hk * HEAD_DIM, (hk + 1) * HEAD_DIM)
            k_all = jnp.concatenate([ck_ref[:, kv_cols], k_ref[prev, kv_cols], k_ref[cur, kv_cols],
                                     k_ref[nxt, kv_cols]], axis=0)
            v_all = jnp.concatenate([cv_ref[:, kv_cols], v_ref[prev, kv_cols], v_ref[cur, kv_cols],
                                     v_ref[nxt, kv_cols]], axis=0)
            masks = {past_blocks: mask_prev, past_blocks + 2: mask_next}
            outs.update(_group_attention(q_ref, rows, hk, k_all, v_all, masks, sink_ref))
        _store_heads(o_ref, rows, outs)


def _lat_attn_call(sink, q, k, v, cache_k, cache_v, geom):
    qb = LAT_QUERY_ROWS
    steps = geom.dec_seq // qb
    assert geom.n_ctx % geom.dec_seq == 0 and cache_k.shape[1] % LANES == 0
    seq_base = geom.n_ctx // geom.dec_seq
    past = cache_k.shape[1]
    seq_spec = pl.BlockSpec((geom.dec_seq, KV_WIDTH), lambda b, j: (seq_base + b, 0))
    cache_spec = pl.BlockSpec((None, past, KV_WIDTH), lambda b, j: (b, 0, 0))
    return pl.pallas_call(
        _lat_attn_kernel,
        grid=(geom.dec_batch, steps),
        in_specs=[
            pl.BlockSpec(memory_space=pltpu.SMEM),
            pl.BlockSpec((qb, ATTN_WIDTH), lambda b, j: (geom.n_ctx // qb + b * steps + j, 0)),
            seq_spec, seq_spec,
            cache_spec, cache_spec,
        ],
        out_specs=pl.BlockSpec((qb, ATTN_WIDTH), lambda b, j: (b * steps + j, 0)),
        out_shape=jax.ShapeDtypeStruct((geom.dec_batch * geom.dec_seq, ATTN_WIDTH), BF16),
        compiler_params=_params(("arbitrary", "arbitrary")),
        name="lat_attention",
    )(sink, q, k, v, cache_k, cache_v)


def _spatial_gating(u_ref, g_ref, mix_ref, bias_ref, s_ref):
    tm = u_ref.shape[0]
    lane = lax.broadcasted_iota(jnp.int32, (CHUNK, LANES), 1)
    lower_group = lane < SGU_GROUP_DIM
    for ch in range(tm // CHUNK):
        rows = slice(ch * CHUNK, (ch + 1) * CHUNK)
        for p in range(SGU_WIDTH // LANES):
            cols = slice(p * LANES, (p + 1) * LANES)
            g = g_ref[rows, cols]
            zero = jnp.zeros_like(g)
            mixed = (jnp.dot(mix_ref[2 * p], jnp.where(lower_group, g, zero), preferred_element_type=F32)
                     + jnp.dot(mix_ref[2 * p + 1], jnp.where(lower_group, zero, g), preferred_element_type=F32))
            s_ref[rows, cols] = (u_ref[rows, cols] * (mixed + bias_ref[:, cols])).astype(s_ref.dtype)


def _out_kernel(alpha, with_router, split_x, ctx_tiles, a_ctx_ref, a_lat_ref, u_ref, gg_ref, mix_ref, bias_ref,
                *refs):
    s_ref, wab_ref, wsb_ref = refs[-3:]
    refs = refs[:-3]
    if split_x:
        x = _read_stream(refs[0], refs[1], ctx_tiles)
        refs = refs[2:]
    else:
        x = refs[0][...]
        refs = refs[1:]
    mod_ref, wa_ref, ws_ref, g_ref, b_ref = refs[:5]
    if with_router:
        wr_ref, x1_ref, h_ref, rt_ref = refs[5:]
    else:
        x1_ref, h_ref = refs[5:]

    @pl.when(pl.program_id(0) == 0)
    def _():
        wab_ref[...] = wa_ref[...].astype(BF16)
        wsb_ref[...] = ws_ref[...].astype(BF16)

    a = _read_stream(a_ctx_ref, a_lat_ref, ctx_tiles)
    _spatial_gating(u_ref, gg_ref, mix_ref, bias_ref, s_ref)
    y = (jnp.dot(a, wab_ref[...], preferred_element_type=F32)
         + jnp.dot(s_ref[...], wsb_ref[...], preferred_element_type=F32))
    t = alpha * x + mod_ref[2:3, :] * y
    x1 = _layer_norm_rows(t, g_ref[...], b_ref[...])
    x1_ref[...] = x1
    h = x1 * (1.0 + mod_ref[4:5, :]) + mod_ref[3:4, :]
    if with_router:
        h_ref[...] = _rows_to_tiles(h)
    else:
        h_ref[...] = h.astype(h_ref.dtype)
    if with_router:
        tm = h.shape[0]
        wr = wr_ref[...]
        wr_hi = wr.astype(BF16)
        wr_lo = (wr - wr_hi.astype(F32)).astype(BF16)
        h_hi = h.astype(BF16)
        h_lo = (h - h_hi.astype(F32)).astype(BF16)
        logits = (jnp.dot(h_hi, wr_hi, preferred_element_type=F32)
                  + jnp.dot(h_lo, wr_hi, preferred_element_type=F32)
                  + jnp.dot(h_hi, wr_lo, preferred_element_type=F32))
        lane = lax.broadcasted_iota(jnp.int32, (tm, LANES), 1).astype(F32)
        neg = jnp.float32(-jnp.inf)
        lg = jnp.where(lane < N_EXPERTS, logits, neg)
        m1 = jnp.max(lg, axis=-1, keepdims=True)
        i1 = jnp.min(jnp.where(lg == m1, lane, float(LANES)), axis=-1, keepdims=True)
        lg2 = jnp.where(lane == i1, neg, lg)
        m2 = jnp.max(lg2, axis=-1, keepdims=True)
        i2 = jnp.min(jnp.where(lg2 == m2, lane, float(LANES)), axis=-1, keepdims=True)
        e2 = jnp.exp(m2 - m1)
        g1 = 1.0 / (1.0 + e2)
        g2 = e2 / (1.0 + e2)
        rt = jnp.where(lane == 0, i1, jnp.where(lane == 1, i2, jnp.where(lane == 2, g1, jnp.where(lane == 3, g2, 0.0))))
        rt_ref[...] = rt


def _out_call(a_ctx, a_lat, u, gg, w_s, bias_full, x, mods, w_o, layer, ln_g, ln_b, w_router, alpha, geom):
    n = geom.n_tok
    with_router = w_router is not None
    tm = ROUTER_TILE if with_router else OUT_TILE
    ctx_tiles = geom.ctx_tiles(tm)
    row = lambda i: (i, 0)
    const = lambda i: (0, 0)
    split_x = isinstance(x, tuple)
    x_args = list(x) if split_x else [x]
    x_specs = _stream_specs(D_MODEL, ctx_tiles, tm) if split_x else [pl.BlockSpec((tm, D_MODEL), row)]
    sgu_specs = [
        pl.BlockSpec((tm, SGU_WIDTH), row),
        pl.BlockSpec((tm, SGU_WIDTH), row),
        pl.BlockSpec((N_SGU_GROUPS, CHUNK, CHUNK), lambda i: (0, 0, 0)),
        pl.BlockSpec((CHUNK, SGU_WIDTH), const),
    ]
    in_specs = _stream_specs(ATTN_WIDTH, ctx_tiles, tm) + sgu_specs + x_specs + [
        pl.BlockSpec((None, N_MODS, D_MODEL), lambda i: (geom.mod_index(i, tm), 0, 0)),
        pl.BlockSpec((None, ATTN_WIDTH, D_MODEL), lambda i: (layer, 0, 0)),
        pl.BlockSpec((None, SGU_WIDTH, D_MODEL), lambda i: (layer, 1, 0)),
        pl.BlockSpec((1, D_MODEL), const),
        pl.BlockSpec((1, D_MODEL), const),
    ]
    args = [a_ctx, a_lat, u, gg, w_s, bias_full] + x_args + [mods, w_o, w_o, ln_g, ln_b]
    if with_router:
        h_spec = pl.BlockSpec((tm, D_MODEL // LANES, LANES), lambda i: (i, 0, 0))
        h_shape = jax.ShapeDtypeStruct((n, D_MODEL // LANES, LANES), F32)
    else:
        h_spec = pl.BlockSpec((tm, D_MODEL), row)
        h_shape = jax.ShapeDtypeStruct((n, D_MODEL), BF16)
    out_specs = [pl.BlockSpec((tm, D_MODEL), row), h_spec]
    out_shape = [jax.ShapeDtypeStruct((n, D_MODEL), F32), h_shape]
    if with_router:
        in_specs.append(pl.BlockSpec((D_MODEL, LANES), const))
        args.append(w_router)
        out_specs.append(pl.BlockSpec((tm, LANES), row))
        out_shape.append(jax.ShapeDtypeStruct((n, LANES), F32))
    return pl.pallas_call(
        functools.partial(_out_kernel, alpha, with_router, split_x, ctx_tiles),
        grid=(n // tm,),
        in_specs=in_specs,
        out_specs=out_specs,
        out_shape=out_shape,
        scratch_shapes=[pltpu.VMEM((tm, SGU_WIDTH), BF16), pltpu.VMEM((ATTN_WIDTH, D_MODEL), BF16),
                        pltpu.VMEM((SGU_WIDTH, D_MODEL), BF16)],
        compiler_params=_params(("arbitrary",)),
        name="out_proj_router" if with_router else "out_proj",
    )(*args)


def _swiglu_blocks(width):
    return [slice(s, min(s + V7X_MXU_WIDTH, width)) for s in range(0, width, V7X_MXU_WIDTH)]


def _swiglu_partial(x, wg_ref, wu_ref, wd_ref, side_work=None):
    out = None
    for n, cols in enumerate(_swiglu_blocks(wg_ref.shape[1])):
        a = jnp.dot(x, wg_ref[:, cols].astype(BF16), preferred_element_type=F32)
        if side_work is not None:
            side_work(3 * n)
        b = jnp.dot(x, wu_ref[:, cols].astype(BF16), preferred_element_type=F32)
        if side_work is not None:
            side_work(3 * n + 1)
        mid = ((a * _sigmoid(a)) * b).astype(BF16)
        part = jnp.dot(mid, wd_ref[cols, :].astype(BF16), preferred_element_type=F32)
        if side_work is not None:
            side_work(3 * n + 2)
        out = part if out is None else out + part
    return out


def _ffn_kernel(alpha, h_ref, x1_ref, mod_ref, wg_ref, wu_ref, wd_ref, g_ref, b_ref, o_ref, acc_ref):
    c = pl.program_id(1)
    part = _swiglu_partial(h_ref[...], wg_ref, wu_ref, wd_ref)

    @pl.when(c == 0)
    def _():
        acc_ref[...] = part

    @pl.when(c > 0)
    def _():
        acc_ref[...] += part

    @pl.when(c == pl.num_programs(1) - 1)
    def _():
        t = alpha * x1_ref[...] + mod_ref[5:6, :] * acc_ref[...]
        o_ref[...] = _layer_norm_rows(t, g_ref[...], b_ref[...])


def _ffn_chunk(d_ff):
    assert d_ff % (2 * LANES) == 0
    return d_ff // 2


def _ffn_call(h, x1, mods, wg, wu, wd, ln_g, ln_b, alpha, geom):
    n = h.shape[0]
    tm = DENSE_TILE
    d_ff = wg.shape[1]
    fc = _ffn_chunk(d_ff)
    row = lambda i, c: (i, 0)
    const = lambda i, c: (0, 0)
    return pl.pallas_call(
        functools.partial(_ffn_kernel, alpha),
        grid=(n // tm, d_ff // fc),
        in_specs=[
            pl.BlockSpec((tm, D_MODEL), row),
            pl.BlockSpec((tm, D_MODEL), row),
            pl.BlockSpec((None, N_MODS, D_MODEL), lambda i, c: (geom.mod_index(i, tm), 0, 0)),
            pl.BlockSpec((D_MODEL, fc), lambda i, c: (0, c)),
            pl.BlockSpec((D_MODEL, fc), lambda i, c: (0, c)),
            pl.BlockSpec((fc, D_MODEL), lambda i, c: (c, 0)),
            pl.BlockSpec((1, D_MODEL), const),
            pl.BlockSpec((1, D_MODEL), const),
        ],
        out_specs=pl.BlockSpec((tm, D_MODEL), row),
        out_shape=jax.ShapeDtypeStruct((n, D_MODEL), F32),
        scratch_shapes=[pltpu.VMEM((tm, D_MODEL), F32)],
        compiler_params=_params(("arbitrary", "arbitrary")),
        name="dense_ffn",
    )(h, x1, mods, wg, wu, wd, ln_g, ln_b)


def _moe_kernel(n_chunks, te_ref, nu_ref, src_ref, dst_ref, h_hbm, wg_ref, wu_ref, wd_ref, o_hbm,
                xbuf, xb, obuf, gsem, ssem):
    del te_ref
    tr = xb.shape[0]
    t = pl.program_id(0)
    c = pl.program_id(1)
    n_tiles = pl.num_programs(0)
    valid = t < nu_ref[0]
    slot = t % 2
    other = 1 - slot
    share = tr // n_chunks
    first_row = c * share
    nxt_base = jnp.minimum(t + 1, n_tiles - 1) * tr
    prv_base = jnp.where(t == 0, n_tiles - 1, t - 1) * tr

    def gather_row(base, dst_slot, r, tok=None):
        tok = src_ref[base + r] if tok is None else tok
        return pltpu.make_async_copy(h_hbm.at[tok], xbuf.at[dst_slot, r], gsem.at[dst_slot])

    def scatter_row(base, src_slot, r, dst=None):
        dst = dst_ref[base + r] if dst is None else dst
        return pltpu.make_async_copy(obuf.at[src_slot, r], o_hbm.at[dst], ssem.at[src_slot])

    def wait_gather(dst_slot):
        pltpu.make_async_copy(h_hbm.at[pl.ds(0, tr)], xbuf.at[dst_slot], gsem.at[dst_slot]).wait()

    def wait_scatter(src_slot):
        pltpu.make_async_copy(obuf.at[src_slot], o_hbm.at[pl.ds(0, tr)], ssem.at[src_slot]).wait()

    def looped(n_rows, start_row):
        def body(r, carry):
            start_row(r)
            return carry
        lax.fori_loop(0, n_rows, body, 0, unroll=8)

    @pl.when((t == 0) & (c == 0))
    def _():
        obuf[1] = jnp.zeros(obuf.shape[1:], obuf.dtype)
        looped(tr, lambda r: gather_row(0, 0, r).start())

    @pl.when(c == 0)
    def _():
        wait_gather(slot)

        @pl.when(t >= 1)
        def _():
            wait_scatter(slot)

    def multiply_tile(slot_s, c_s):
        other_s = 1 - slot_s
        if c_s == 0:
            xb[...] = _tiles_to_rows(xbuf[slot_s]).astype(BF16)
        n_groups = 3 * len(_swiglu_blocks(wg_ref.shape[1]))

        def row_dmas(i):
            for r in range(c_s * share + share * i // n_groups, c_s * share + share * (i + 1) // n_groups):
                gather_row(nxt_base, other_s, r).start()
                scatter_row(prv_base, other_s, r).start()

        part = _swiglu_partial(xb[...], wg_ref, wu_ref, wd_ref, row_dmas)
        if c_s == 0:
            obuf[slot_s] = _rows_to_tiles(part)
        else:
            obuf[slot_s] += _rows_to_tiles(part)

    for slot_s in range(2):
        for c_s in range(n_chunks):
            pl.when(valid & (slot == slot_s) & (c == c_s))(functools.partial(multiply_tile, slot_s, c_s))

    @pl.when(jnp.logical_not(valid))
    def _():
        @pl.when(c == 0)
        def _():
            obuf[slot] = jnp.zeros(obuf.shape[1:], obuf.dtype)

        looped(share, lambda r: gather_row(nxt_base, other, first_row + r).start())
        looped(share, lambda r: scatter_row(prv_base, other, first_row + r).start())

    @pl.when((t == n_tiles - 1) & (c == n_chunks - 1))
    def _():
        wait_scatter(other)
        looped(tr, lambda r: scatter_row(t * tr, slot, r).start())
        wait_scatter(slot)
        wait_gather(other)


def _moe_call(h, wg, wu, wd, plan):
    tr = FFN_TILE
    d_ff = wg.shape[2]
    fc = _ffn_chunk(d_ff)
    n_chunks = d_ff // fc
    t_max = plan["tile_expert"].shape[0]
    row_tile = h.shape[1:]

    def chunk(t, c, nu):
        return jnp.where(t < nu[0], c, n_chunks - 1)

    grid_spec = pltpu.PrefetchScalarGridSpec(
        num_scalar_prefetch=4,
        grid=(t_max, n_chunks),
        in_specs=[
            pl.BlockSpec(memory_space=pl.ANY),
            pl.BlockSpec((None, D_MODEL, fc), lambda t, c, te, nu, src, dst: (te[t], 0, chunk(t, c, nu))),
            pl.BlockSpec((None, D_MODEL, fc), lambda t, c, te, nu, src, dst: (te[t], 0, chunk(t, c, nu))),
            pl.BlockSpec((None, fc, D_MODEL), lambda t, c, te, nu, src, dst: (te[t], chunk(t, c, nu), 0)),
        ],
        out_specs=pl.BlockSpec(memory_space=pl.ANY),
        scratch_shapes=[
            pltpu.VMEM((2, tr) + row_tile, F32),
            pltpu.VMEM((tr, D_MODEL), BF16),
            pltpu.VMEM((2, tr) + row_tile, F32),
            pltpu.SemaphoreType.DMA((2,)),
            pltpu.SemaphoreType.DMA((2,)),
        ],
    )
    return pl.pallas_call(
        functools.partial(_moe_kernel, n_chunks),
        grid_spec=grid_spec,
        out_shape=jax.ShapeDtypeStruct((t_max * tr,) + row_tile, F32),
        compiler_params=_params(("arbitrary", "arbitrary")),
        name="expert_ffn",
    )(plan["tile_expert"], plan["n_used"], plan["src_token"], plan["dst_row"], h, wg, wu, wd)


def _route_plan(routing, n_tok, tr):
    n_assign = TOP_K * n_tok
    t_max = n_assign // tr + N_EXPERTS
    n_slots = t_max * tr
    n_pad = n_slots - n_assign
    id_bits = (n_slots - 1).bit_length()
    experts = jnp.arange(N_EXPERTS, dtype=jnp.int32)
    e_flat = routing[:, :TOP_K].astype(jnp.int32).T.reshape(-1)
    counts = jnp.sum((e_flat[:, None] == experts[None, :]).astype(jnp.int32), axis=0)
    tiles_e = (counts + tr - 1) // tr
    tile_end = jnp.cumsum(tiles_e)
    n_used = tile_end[-1]
    pad_end = jnp.cumsum(tiles_e * tr - counts)
    pad_ids = jnp.arange(n_pad, dtype=jnp.int32)
    pad_expert = jnp.sum((pad_ids[:, None] >= pad_end[None, :]).astype(jnp.int32), axis=1)
    keys = jnp.concatenate([e_flat * 2, pad_expert * 2 + 1])
    item = jnp.arange(n_slots, dtype=jnp.int32)
    slot_item = jnp.sort((keys << id_bits) | item) & ((1 << id_bits) - 1)
    real = slot_item < n_assign
    src = jnp.where(real, slot_item % n_tok, 0).astype(jnp.int32)
    dst = slot_item.astype(jnp.int32)
    t_ids = jnp.arange(t_max, dtype=jnp.int32)
    te = jnp.sum((t_ids[:, None] >= tile_end[None, :]).astype(jnp.int32), axis=1)
    last_used = jnp.max(jnp.where(tiles_e > 0, experts, 0))
    te = jnp.where(t_ids < n_used, te, last_used).astype(jnp.int32)
    return {"tile_expert": te, "n_used": n_used.reshape(1).astype(jnp.int32), "src_token": src, "dst_row": dst}


def _combine_kernel(alpha, split_ctx_tiles, e1_ref, e2_ref, rt_ref, x1_ref, mod_ref, g_ref, b_ref, *o_refs):
    y = rt_ref[:, 2:3] * _tiles_to_rows(e1_ref[...]) + rt_ref[:, 3:4] * _tiles_to_rows(e2_ref[...])
    t = alpha * x1_ref[...] + mod_ref[5:6, :] * y
    out = _layer_norm_rows(t, g_ref[...], b_ref[...])
    if split_ctx_tiles is None:
        o_refs[0][...] = out
    else:
        _write_stream(o_refs[0], o_refs[1], split_ctx_tiles, out)


def _combine_call(eo, routing, x1, mods, ln_g, ln_b, alpha, geom, split_out):
    n = x1.shape[0]
    tm = COMBINE_TILE
    row = lambda i: (i, 0)
    const = lambda i: (0, 0)
    if split_out:
        out_specs = _stream_specs(D_MODEL, geom.ctx_tiles(tm), tm)
        out_shape = [jax.ShapeDtypeStruct((geom.n_ctx, D_MODEL), F32),
                     jax.ShapeDtypeStruct((n - geom.n_ctx, D_MODEL), F32)]
    else:
        out_specs = pl.BlockSpec((tm, D_MODEL), row)
        out_shape = jax.ShapeDtypeStruct((n, D_MODEL), F32)
    return pl.pallas_call(
        functools.partial(_combine_kernel, alpha, geom.ctx_tiles(tm) if split_out else None),
        grid=(n // tm,),
        in_specs=[
            pl.BlockSpec((tm,) + eo.shape[1:], lambda i: (i, 0, 0)),
            pl.BlockSpec((tm,) + eo.shape[1:], lambda i: (i + n // tm, 0, 0)),
            pl.BlockSpec((tm, LANES), row),
            pl.BlockSpec((tm, D_MODEL), row),
            pl.BlockSpec((None, N_MODS, D_MODEL), lambda i: (geom.mod_index(i, tm), 0, 0)),
            pl.BlockSpec((1, D_MODEL), const),
            pl.BlockSpec((1, D_MODEL), const),
        ],
        out_specs=out_specs,
        out_shape=out_shape,
        compiler_params=_params(("arbitrary",)),
        name="expert_combine",
    )(eo, eo, routing, x1, mods, ln_g, ln_b)


class _Geometry:
    def __init__(self, n_ctx, dec_batch, dec_seq):
        self.n_ctx = n_ctx
        self.dec_batch = dec_batch
        self.dec_seq = dec_seq
        self.n_tok = n_ctx + dec_batch * dec_seq

    def ctx_tiles(self, tm):
        assert self.n_ctx % tm == 0 and self.dec_seq % tm == 0
        return self.n_ctx // tm

    def mod_index(self, i, tm):
        ct = self.ctx_tiles(tm)
        return jnp.where(i < ct, 0, 1 + (i - ct) // (self.dec_seq // tm))

    def rope_index(self, i, tm):
        ct = self.ctx_tiles(tm)
        return jnp.where(i < ct, 0, 1 + (i - ct) % (self.dec_seq // tm))


def _rope_tables(dec_seq):
    pos = jnp.arange(dec_seq, dtype=jnp.int32)
    row = (pos // GRID_W).astype(F32)
    col = (pos % GRID_W).astype(F32)
    inv = ROPE_BASE ** (-jnp.arange(ROPE_PAIRS, dtype=F32) / ROPE_PAIRS)
    ang_r = row[:, None] * inv[None, :]
    ang_c = col[:, None] * inv[None, :]
    cos_h = jnp.concatenate([jnp.cos(ang_r), jnp.cos(ang_r), jnp.cos(ang_c), jnp.cos(ang_c)], axis=-1)
    sin_h = jnp.concatenate([-jnp.sin(ang_r), jnp.sin(ang_r), -jnp.sin(ang_c), jnp.sin(ang_c)], axis=-1)
    reps = LANES // HEAD_DIM
    cos_t = jnp.concatenate([jnp.ones((IN_TILE, LANES), F32), jnp.tile(cos_h, (1, reps))], axis=0)
    sin_t = jnp.concatenate([jnp.zeros((IN_TILE, LANES), F32), jnp.tile(sin_h, (1, reps))], axis=0)
    return cos_t, sin_t


def kernel(x_prompt, x_sample, cache_k, cache_v, c, c_ctx, w_ada, b_ada, w_in, w_o, attn_sink, w_s, b_s, sgu_ln_g, sgu_ln_b, ln1_g, ln1_b, ln2_g, ln2_b, w_ff_gate, w_ff_up, w_ff_down, w_router, w_exp_gate, w_exp_up, w_exp_down):
    batch, seq, d = x_prompt.shape
    dec_batch, dec_seq, _ = x_sample.shape
    depth = w_in.shape[0]
    past = cache_k.shape[2]
    assert d == D_MODEL and dec_batch + 1 <= MOD_ROWS
    n_ctx = batch * seq
    n_lat = dec_batch * dec_seq
    n_tok = n_ctx + n_lat
    geom = _Geometry(n_ctx, dec_batch, dec_seq)
    alpha = float((2 * depth) ** 0.25)

    x = (x_prompt.reshape(n_ctx, d), x_sample.reshape(n_lat, d))
    cvec = jnp.concatenate([c_ctx[None, :], c, jnp.zeros((MOD_ROWS - 1 - dec_batch, d), F32)], axis=0)
    mods_all = _ada_call(cvec, w_ada, b_ada).reshape(depth, MOD_ROWS, N_MODS, d)
    cos_t, sin_t = _rope_tables(dec_seq)

    new_k, new_v = [], []
    for l in range(depth):
        mods = mods_all[l]
        q, k, v, kv32, u, g = _in_call(x, mods, w_in, l, cos_t, sin_t,
                                       sgu_ln_g[l].reshape(1, SGU_WIDTH), sgu_ln_b[l].reshape(1, SGU_WIDTH), geom)
        new_k.append(kv32[:n_ctx, :KV_WIDTH].reshape(batch, seq, N_KV_HEADS, HEAD_DIM))
        new_v.append(kv32[:n_ctx, KV_WIDTH:].reshape(batch, seq, N_KV_HEADS, HEAD_DIM))
        sink = attn_sink[l]
        a_ctx = _ctx_attn_call(sink, q, k, v, batch, seq)
        a_lat = _lat_attn_call(sink, q, k, v, cache_k[:, l].reshape(dec_batch, past, KV_WIDTH).astype(BF16),
                               cache_v[:, l].reshape(dec_batch, past, KV_WIDTH).astype(BF16), geom)
        bias_full = jnp.repeat(b_s[l].T, SGU_GROUP_DIM, axis=1)
        i = l // 2
        moe = l % 2 == 1
        w_r = None
        if moe:
            w_r = jnp.pad(w_router[i], ((0, 0), (0, LANES - N_EXPERTS)))
        outs = _out_call(a_ctx, a_lat, u, g, w_s[l].astype(BF16), bias_full, x, mods, w_o, l, ln1_g[l].reshape(1, d), ln1_b[l].reshape(1, d),
                         w_r, alpha, geom)
        ln_g, ln_b = ln2_g[l].reshape(1, d), ln2_b[l].reshape(1, d)
        if moe:
            x1, h, routing = outs
            plan = _route_plan(routing, n_tok, FFN_TILE)
            eo = _moe_call(h, w_exp_gate[i], w_exp_up[i], w_exp_down[i], plan)
            x = _combine_call(eo, routing, x1, mods, ln_g, ln_b, alpha, geom, split_out=l == depth - 1)
        else:
            x1, h = outs
            x = _ffn_call(h, x1, mods, w_ff_gate[i].astype(BF16), w_ff_up[i].astype(BF16),
                          w_ff_down[i].astype(BF16), ln_g, ln_b, alpha, geom)

    if not isinstance(x, (tuple, list)):
        x = (x[:n_ctx], x[n_ctx:])
    y_prompt = x[0].reshape(batch, seq, d)
    y_sample = x[1].reshape(dec_batch, dec_seq, d)
    return (y_prompt, y_sample, jnp.stack(new_k, axis=1), jnp.stack(new_v, axis=1))
```

```python
import functools

import jax
import jax.numpy as jnp
import numpy as np
from jax import lax
from jax.experimental import pallas as pl
from jax.experimental.pallas import tpu as pltpu

F32 = jnp.float32
BF16 = jnp.bfloat16

D_MODEL = 1024
HEAD_DIM = 64
N_Q_HEADS = 8
N_KV_HEADS = 2
Q_PER_KV = N_Q_HEADS // N_KV_HEADS
ATTN_WIDTH = N_Q_HEADS * HEAD_DIM
KV_WIDTH = N_KV_HEADS * HEAD_DIM
SGU_WIDTH = D_MODEL - ATTN_WIDTH
N_SGU_GROUPS = 8
SGU_GROUP_DIM = SGU_WIDTH // N_SGU_GROUPS
CHUNK = 128
BLOCK = 128
WINDOW = 128
GRID_W = 64
IN_WIDTH = ATTN_WIDTH + 2 * KV_WIDTH + 2 * SGU_WIDTH
OFF_K = ATTN_WIDTH
OFF_V = OFF_K + KV_WIDTH
OFF_U = OFF_V + KV_WIDTH
OFF_G = OFF_U + SGU_WIDTH
N_EXPERTS = 8
TOP_K = 2
ROPE_BASE = 10000.0
ROPE_PAIRS = HEAD_DIM // 4
LN_EPS = 1e-5
ATTN_SCALE = HEAD_DIM ** -0.5
NEG_INF = -1e30
N_MODS = 6

LANES = 128
V7X_MXU_WIDTH = 256
V7X_VMEM_LIMIT_BYTES = 56 * 1024 * 1024

IN_TILE = 1024
OUT_TILE = 512
ROUTER_TILE = 256
COMBINE_TILE = 512
DENSE_TILE = 1024
FFN_TILE = 512
LAT_QUERY_ROWS = 512
CTX_SEQS_PER_STEP = 1
MOD_ROWS = 8
ADA_COLS = 1536


def _params(sem):
    return pltpu.CompilerParams(dimension_semantics=sem, vmem_limit_bytes=V7X_VMEM_LIMIT_BYTES)


def _gelu_tanh(x):
    return x * (0.5 * (1.0 + jnp.tanh(np.float32(np.sqrt(2.0 / np.pi)) * (x + 0.044715 * (x * x * x)))))


def _sigmoid(x):
    return 1.0 / (1.0 + jnp.exp(-x))


def _layer_norm_rows(t, g, b):
    mu = jnp.mean(t, axis=-1, keepdims=True)
    tc = t - mu
    var = jnp.mean(tc * tc, axis=-1, keepdims=True)
    return tc * lax.rsqrt(var + LN_EPS) * g + b


def _rows_to_tiles(x):
    pieces = jnp.stack([x[:, s * LANES:(s + 1) * LANES] for s in range(x.shape[1] // LANES)], axis=0)
    return jnp.swapaxes(pieces, 0, 1)


def _tiles_to_rows(t):
    pieces = jnp.swapaxes(t, 0, 1)
    return jnp.concatenate([pieces[s] for s in range(t.shape[1])], axis=1)


def _stream_specs(width, ctx_tiles, tm):
    return [pl.BlockSpec((tm, width), lambda i, *_: (jnp.minimum(i, ctx_tiles - 1), 0)),
            pl.BlockSpec((tm, width), lambda i, *_: (jnp.maximum(i - ctx_tiles, 0), 0))]


def _read_stream(ctx_ref, lat_ref, ctx_tiles):
    tile = lax.broadcasted_iota(jnp.int32, ctx_ref.shape, 0) * 0 + pl.program_id(0)
    return jnp.where(tile < ctx_tiles, ctx_ref[...], lat_ref[...])


def _write_stream(ctx_ref, lat_ref, ctx_tiles, value):
    @pl.when(pl.program_id(0) < ctx_tiles)
    def _():
        ctx_ref[...] = value

    @pl.when(pl.program_id(0) >= ctx_tiles)
    def _():
        lat_ref[...] = value


def _ada_kernel(c_ref, w_ref, b_ref, o_ref):
    c = c_ref[...]
    s = (c * _sigmoid(c)).astype(BF16)
    o_ref[...] = jnp.dot(s, w_ref[...].astype(BF16), preferred_element_type=F32) + b_ref[...]


def _ada_call(cvec, w_ada, b_ada):
    depth = w_ada.shape[0]
    n_out = w_ada.shape[2]
    return pl.pallas_call(
        _ada_kernel,
        grid=(depth, n_out // ADA_COLS),
        in_specs=[
            pl.BlockSpec((MOD_ROWS, D_MODEL), lambda l, j: (0, 0)),
            pl.BlockSpec((None, D_MODEL, ADA_COLS), lambda l, j: (l, 0, j)),
            pl.BlockSpec((None, 1, ADA_COLS), lambda l, j: (l, 0, j)),
        ],
        out_specs=pl.BlockSpec((None, MOD_ROWS, ADA_COLS), lambda l, j: (l, 0, j)),
        out_shape=jax.ShapeDtypeStruct((depth, MOD_ROWS, n_out), F32),
        compiler_params=_params(("arbitrary", "arbitrary")),
        name="adaln",
    )(cvec, w_ada, b_ada.reshape(depth, 1, n_out))


def _in_kernel(split, ctx_tiles, *refs):
    if split:
        x = _read_stream(refs[0], refs[1], ctx_tiles)
        refs = refs[2:]
    else:
        x = refs[0][...]
        refs = refs[1:]
    mod_ref, w_ref, cos_ref, sin_ref, lng_ref, lnb_ref, q_ref, k_ref, v_ref, kv_ref, u_ref, g_ref = refs
    tm = x.shape[0]
    h = x * (1.0 + mod_ref[1:2, :]) + mod_ref[0:1, :]
    hb = h.astype(BF16)

    def z_block(start):
        return jnp.dot(hb, w_ref[:, start:start + V7X_MXU_WIDTH], preferred_element_type=F32)

    cos = cos_ref[...]
    sin = sin_ref[...]
    lane = lax.broadcasted_iota(jnp.int32, (tm, LANES), 1)
    first_of_pair = (lane & (2 * ROPE_PAIRS - 1)) < ROPE_PAIRS
    lower_group = lane < SGU_GROUP_DIM

    def rope(t):
        partner = jnp.where(first_of_pair, pltpu.roll(t, LANES - ROPE_PAIRS, 1), pltpu.roll(t, ROPE_PAIRS, 1))
        return t * cos + partner * sin

    halves = (slice(0, LANES), slice(LANES, 2 * LANES))
    inv_n = 1.0 / SGU_GROUP_DIM
    for j in range(SGU_WIDTH // LANES):
        cols = slice(j * LANES, (j + 1) * LANES)
        if j % 2 == 0:
            z = z_block(OFF_G + j * LANES)
        t = _gelu_tanh(z[:, halves[j % 2]])
        s_lo = jnp.sum(jnp.where(lower_group, t, 0.0), axis=-1, keepdims=True)
        s_hi = jnp.sum(jnp.where(lower_group, 0.0, t), axis=-1, keepdims=True)
        tc = t - jnp.where(lower_group, s_lo, s_hi) * inv_n
        sq = tc * tc
        v_lo = jnp.sum(jnp.where(lower_group, sq, 0.0), axis=-1, keepdims=True)
        v_hi = jnp.sum(jnp.where(lower_group, 0.0, sq), axis=-1, keepdims=True)
        var = jnp.where(lower_group, v_lo, v_hi) * inv_n
        g_ref[:, cols] = (tc * lax.rsqrt(var + LN_EPS) * lng_ref[:, cols] + lnb_ref[:, cols]).astype(BF16)

    for start in range(0, ATTN_WIDTH, V7X_MXU_WIDTH):
        z = z_block(start)
        for half in halves:
            q_ref[:, start + half.start:start + half.stop] = (rope(z[:, half]) * ATTN_SCALE).astype(BF16)
    z = z_block(OFF_K)
    k_ref[...] = rope(z[:, halves[0]]).astype(BF16)
    v_ref[...] = z[:, halves[1]].astype(BF16)
    kv_ref[...] = z
    for start in range(0, SGU_WIDTH, V7X_MXU_WIDTH):
        u_ref[:, start:start + V7X_MXU_WIDTH] = _gelu_tanh(z_block(OFF_U + start))


def _in_call(x, mods, w_in, cos_t, sin_t, ln_g, ln_b, geom):
    n = geom.n_tok
    tm = IN_TILE
    mod_idx = lambda i: geom.mod_index(i, tm)
    rope_idx = lambda i: geom.rope_index(i, tm)
    row = lambda i: (i, 0)
    split = isinstance(x, tuple)
    x_args = list(x) if split else [x]
    ctx_tiles = geom.ctx_tiles(tm)
    x_specs = _stream_specs(D_MODEL, ctx_tiles, tm) if split else [pl.BlockSpec((tm, D_MODEL), row)]
    outs = pl.pallas_call(
        functools.partial(_in_kernel, split, ctx_tiles),
        grid=(n // tm,),
        in_specs=x_specs + [
            pl.BlockSpec((None, N_MODS, D_MODEL), lambda i: (mod_idx(i), 0, 0)),
            pl.BlockSpec((D_MODEL, IN_WIDTH), lambda i: (0, 0)),
            pl.BlockSpec((tm, LANES), lambda i: (rope_idx(i), 0)),
            pl.BlockSpec((tm, LANES), lambda i: (rope_idx(i), 0)),
            pl.BlockSpec((1, SGU_WIDTH), lambda i: (0, 0)),
            pl.BlockSpec((1, SGU_WIDTH), lambda i: (0, 0)),
        ],
        out_specs=[
            pl.BlockSpec((tm, ATTN_WIDTH), row),
            pl.BlockSpec((tm, KV_WIDTH), row),
            pl.BlockSpec((tm, KV_WIDTH), row),
            pl.BlockSpec((tm, 2 * KV_WIDTH), row),
            pl.BlockSpec((tm, SGU_WIDTH), row),
            pl.BlockSpec((tm, SGU_WIDTH), row),
        ],
        out_shape=[
            jax.ShapeDtypeStruct((n, ATTN_WIDTH), BF16),
            jax.ShapeDtypeStruct((n, KV_WIDTH), BF16),
            jax.ShapeDtypeStruct((n, KV_WIDTH), BF16),
            jax.ShapeDtypeStruct((n, 2 * KV_WIDTH), F32),
            jax.ShapeDtypeStruct((n, SGU_WIDTH), F32),
            jax.ShapeDtypeStruct((n, SGU_WIDTH), BF16),
        ],
        compiler_params=_params(("arbitrary",)),
        name="in_proj",
    )(*x_args, mods, w_in, cos_t, sin_t, ln_g, ln_b)
    return outs


def _group_attention(q_ref, rows, hk, k_all, v_all, block_masks, sink_ref):
    m_rows = rows.stop - rows.start
    heads = [hk * Q_PER_KV + gq for gq in range(Q_PER_KV)]
    q = jnp.concatenate([q_ref[rows, h * HEAD_DIM:(h + 1) * HEAD_DIM] for h in heads], axis=0)
    s = lax.dot_general(q, k_all, (((1,), (1,)), ((), ())), preferred_element_type=F32)
    n_blocks = k_all.shape[0] // LANES
    blocks = [s[:, b * LANES:(b + 1) * LANES] for b in range(n_blocks)]
    for b, mask in block_masks.items():
        blocks[b] = jnp.where(mask, blocks[b], NEG_INF)
    head_of_row = lax.broadcasted_iota(jnp.int32, (Q_PER_KV * m_rows, 1), 0) // m_rows
    sink = jnp.zeros((Q_PER_KV * m_rows, 1), F32)
    for gq, h in enumerate(heads):
        sink = jnp.where(head_of_row == gq, sink_ref[h], sink)
    m_el = blocks[0]
    for blk in blocks[1:]:
        m_el = jnp.maximum(m_el, blk)
    m = jnp.maximum(jnp.max(m_el, axis=-1, keepdims=True), sink)
    probs = [jnp.exp(blk - m) for blk in blocks]
    l_el = probs[0]
    for p in probs[1:]:
        l_el = l_el + p
    denom = jnp.sum(l_el, axis=-1, keepdims=True) + jnp.exp(sink - m)
    p_all = jnp.concatenate([p.astype(BF16) for p in probs], axis=1)
    o = jnp.dot(p_all, v_all, preferred_element_type=F32) / denom
    return {h: o[gq * m_rows:(gq + 1) * m_rows] for gq, h in enumerate(heads)}


def _store_heads(o_ref, rows, outs):
    for h0 in range(0, N_Q_HEADS, 2):
        pair = jnp.concatenate([outs[h0], outs[h0 + 1]], axis=1)
        o_ref[rows, h0 * HEAD_DIM:(h0 + 2) * HEAD_DIM] = pair.astype(o_ref.dtype)


def _ctx_attn_kernel(seq_len, sink_ref, q_ref, k_ref, v_ref, o_ref):
    for sq in range(q_ref.shape[0] // seq_len):
        rows = slice(sq * seq_len, (sq + 1) * seq_len)
        outs = {}
        for hk in range(N_KV_HEADS):
            kv_cols = slice(hk * HEAD_DIM, (hk + 1) * HEAD_DIM)
            outs.update(_group_attention(q_ref, rows, hk, k_ref[rows, kv_cols], v_ref[rows, kv_cols], {}, sink_ref))
        _store_heads(o_ref, rows, outs)


def _ctx_attn_call(sink, q, k, v, n_seq, seq_len):
    n = n_seq * seq_len
    per_step = CTX_SEQS_PER_STEP if n_seq % CTX_SEQS_PER_STEP == 0 else 1
    rows = per_step * seq_len
    blk = lambda b: (b, 0)
    return pl.pallas_call(
        functools.partial(_ctx_attn_kernel, seq_len),
        grid=(n_seq // per_step,),
        in_specs=[
            pl.BlockSpec(memory_space=pltpu.SMEM),
            pl.BlockSpec((rows, ATTN_WIDTH), blk),
            pl.BlockSpec((rows, KV_WIDTH), blk),
            pl.BlockSpec((rows, KV_WIDTH), blk),
        ],
        out_specs=pl.BlockSpec((rows, ATTN_WIDTH), blk),
        out_shape=jax.ShapeDtypeStruct((n, ATTN_WIDTH), BF16),
        compiler_params=_params(("arbitrary",)),
        name="ctx_attention",
    )(sink, q, k, v)


def _lat_attn_kernel(sink_ref, q_ref, k_ref, v_ref, ck_ref, cv_ref, o_ref):
    blocks_per_step = q_ref.shape[0] // BLOCK
    nb = k_ref.shape[0] // BLOCK
    past_blocks = ck_ref.shape[0] // LANES
    r = lax.broadcasted_iota(jnp.int32, (Q_PER_KV * BLOCK, BLOCK), 0) & (BLOCK - 1)
    c = lax.broadcasted_iota(jnp.int32, (Q_PER_KV * BLOCK, BLOCK), 1)
    for sub in range(blocks_per_step):
        j = pl.program_id(1) * blocks_per_step + sub
        rows = slice(sub * BLOCK, (sub + 1) * BLOCK)
        mask_prev = c >= r + jnp.where(j > 0, 0, BLOCK)
        mask_next = c <= r - jnp.where(j < nb - 1, 0, BLOCK)
        prev = pl.ds(pl.multiple_of(jnp.maximum(j - 1, 0) * BLOCK, BLOCK), BLOCK)
        cur = pl.ds(pl.multiple_of(j * BLOCK, BLOCK), BLOCK)
        nxt = pl.ds(pl.multiple_of(jnp.minimum(j + 1, nb - 1) * BLOCK, BLOCK), BLOCK)
        outs = {}
        for hk in range(N_KV_HEADS):
            kv_cols = slice(hk * HEAD_DIM, (hk + 1) * HEAD_DIM)
            k_all = jnp.concatenate([ck_ref[:, kv_cols], k_ref[prev, kv_cols], k_ref[cur, kv_cols],
                                     k_ref[nxt, kv_cols]], axis=0)
            v_all = jnp.concatenate([cv_ref[:, kv_cols], v_ref[prev, kv_cols], v_ref[cur, kv_cols],
                                     v_ref[nxt, kv_cols]], axis=0)
            masks = {past_blocks: mask_prev, past_blocks + 2: mask_next}
            outs.update(_group_attention(q_ref, rows, hk, k_all, v_all, masks, sink_ref))
        _store_heads(o_ref, rows, outs)


def _lat_attn_call(sink, q, k, v, cache_k, cache_v, geom):
    qb = LAT_QUERY_ROWS
    steps = geom.dec_seq // qb
    assert geom.n_ctx % geom.dec_seq == 0 and cache_k.shape[1] % LANES == 0
    seq_base = geom.n_ctx // geom.dec_seq
    past = cache_k.shape[1]
    seq_spec = pl.BlockSpec((geom.dec_seq, KV_WIDTH), lambda b, j: (seq_base + b, 0))
    cache_spec = pl.BlockSpec((None, past, KV_WIDTH), lambda b, j: (b, 0, 0))
    return pl.pallas_call(
        _lat_attn_kernel,
        grid=(geom.dec_batch, steps),
        in_specs=[
            pl.BlockSpec(memory_space=pltpu.SMEM),
            pl.BlockSpec((qb, ATTN_WIDTH), lambda b, j: (geom.n_ctx // qb + b * steps + j, 0)),
            seq_spec, seq_spec,
            cache_spec, cache_spec,
        ],
        out_specs=pl.BlockSpec((qb, ATTN_WIDTH), lambda b, j: (b * steps + j, 0)),
        out_shape=jax.ShapeDtypeStruct((geom.dec_batch * geom.dec_seq, ATTN_WIDTH), BF16),
        compiler_params=_params(("arbitrary", "arbitrary")),
        name="lat_attention",
    )(sink, q, k, v, cache_k, cache_v)


def _spatial_gating(u_ref, g_ref, mix_ref, bias_ref, s_ref):
    tm = u_ref.shape[0]
    lane = lax.broadcasted_iota(jnp.int32, (CHUNK, LANES), 1)
    lower_group = lane < SGU_GROUP_DIM
    for ch in range(tm // CHUNK):
        rows = slice(ch * CHUNK, (ch + 1) * CHUNK)
        for p in range(SGU_WIDTH // LANES):
            cols = slice(p * LANES, (p + 1) * LANES)
            g = g_ref[rows, cols]
            zero = jnp.zeros_like(g)
            mixed = (jnp.dot(mix_ref[2 * p], jnp.where(lower_group, g, zero), preferred_element_type=F32)
                     + jnp.dot(mix_ref[2 * p + 1], jnp.where(lower_group, zero, g), preferred_element_type=F32))
            s_ref[rows, cols] = (u_ref[rows, cols] * (mixed + bias_ref[:, cols])).astype(s_ref.dtype)


def _out_kernel(alpha, with_router, split_x, ctx_tiles, a_ctx_ref, a_lat_ref, u_ref, gg_ref, mix_ref, bias_ref,
                *refs):
    s_ref = refs[-1]
    refs = refs[:-1]
    if split_x:
        x = _read_stream(refs[0], refs[1], ctx_tiles)
        refs = refs[2:]
    else:
        x = refs[0][...]
        refs = refs[1:]
    mod_ref, wa_ref, ws_ref, g_ref, b_ref = refs[:5]
    if with_router:
        wr_ref, x1_ref, h_ref, rt_ref = refs[5:]
    else:
        x1_ref, h_ref = refs[5:]
    a = _read_stream(a_ctx_ref, a_lat_ref, ctx_tiles)
    _spatial_gating(u_ref, gg_ref, mix_ref, bias_ref, s_ref)
    y = (jnp.dot(a, wa_ref[...], preferred_element_type=F32)
         + jnp.dot(s_ref[...], ws_ref[...], preferred_element_type=F32))
    t = alpha * x + mod_ref[2:3, :] * y
    x1 = _layer_norm_rows(t, g_ref[...], b_ref[...])
    x1_ref[...] = x1
    h = x1 * (1.0 + mod_ref[4:5, :]) + mod_ref[3:4, :]
    if with_router:
        h_ref[...] = _rows_to_tiles(h)
    else:
        h_ref[...] = h.astype(h_ref.dtype)
    if with_router:
        tm = h.shape[0]
        wr = wr_ref[...]
        wr_hi = wr.astype(BF16)
        wr_lo = (wr - wr_hi.astype(F32)).astype(BF16)
        h_hi = h.astype(BF16)
        h_lo = (h - h_hi.astype(F32)).astype(BF16)
        logits = (jnp.dot(h_hi, wr_hi, preferred_element_type=F32)
                  + jnp.dot(h_lo, wr_hi, preferred_element_type=F32)
                  + jnp.dot(h_hi, wr_lo, preferred_element_type=F32))
        lane = lax.broadcasted_iota(jnp.int32, (tm, LANES), 1).astype(F32)
        neg = jnp.float32(-jnp.inf)
        lg = jnp.where(lane < N_EXPERTS, logits, neg)
        m1 = jnp.max(lg, axis=-1, keepdims=True)
        i1 = jnp.min(jnp.where(lg == m1, lane, float(LANES)), axis=-1, keepdims=True)
        lg2 = jnp.where(lane == i1, neg, lg)
        m2 = jnp.max(lg2, axis=-1, keepdims=True)
        i2 = jnp.min(jnp.where(lg2 == m2, lane, float(LANES)), axis=-1, keepdims=True)
        e2 = jnp.exp(m2 - m1)
        g1 = 1.0 / (1.0 + e2)
        g2 = e2 / (1.0 + e2)
        rt = jnp.where(lane == 0, i1, jnp.where(lane == 1, i2, jnp.where(lane == 2, g1, jnp.where(lane == 3, g2, 0.0))))
        rt_ref[...] = rt


def _out_call(a_ctx, a_lat, u, gg, w_s, bias_full, x, mods, w_o, ln_g, ln_b, w_router, alpha, geom):
    n = geom.n_tok
    with_router = w_router is not None
    tm = ROUTER_TILE if with_router else OUT_TILE
    ctx_tiles = geom.ctx_tiles(tm)
    row = lambda i: (i, 0)
    const = lambda i: (0, 0)
    split_x = isinstance(x, tuple)
    x_args = list(x) if split_x else [x]
    x_specs = _stream_specs(D_MODEL, ctx_tiles, tm) if split_x else [pl.BlockSpec((tm, D_MODEL), row)]
    sgu_specs = [
        pl.BlockSpec((tm, SGU_WIDTH), row),
        pl.BlockSpec((tm, SGU_WIDTH), row),
        pl.BlockSpec((N_SGU_GROUPS, CHUNK, CHUNK), lambda i: (0, 0, 0)),
        pl.BlockSpec((CHUNK, SGU_WIDTH), const),
    ]
    in_specs = _stream_specs(ATTN_WIDTH, ctx_tiles, tm) + sgu_specs + x_specs + [
        pl.BlockSpec((None, N_MODS, D_MODEL), lambda i: (geom.mod_index(i, tm), 0, 0)),
        pl.BlockSpec((ATTN_WIDTH, D_MODEL), const),
        pl.BlockSpec((SGU_WIDTH, D_MODEL), lambda i: (1, 0)),
        pl.BlockSpec((1, D_MODEL), const),
        pl.BlockSpec((1, D_MODEL), const),
    ]
    args = [a_ctx, a_lat, u, gg, w_s, bias_full] + x_args + [mods, w_o, w_o, ln_g, ln_b]
    if with_router:
        h_spec = pl.BlockSpec((tm, D_MODEL // LANES, LANES), lambda i: (i, 0, 0))
        h_shape = jax.ShapeDtypeStruct((n, D_MODEL // LANES, LANES), F32)
    else:
        h_spec = pl.BlockSpec((tm, D_MODEL), row)
        h_shape = jax.ShapeDtypeStruct((n, D_MODEL), BF16)
    out_specs = [pl.BlockSpec((tm, D_MODEL), row), h_spec]
    out_shape = [jax.ShapeDtypeStruct((n, D_MODEL), F32), h_shape]
    if with_router:
        in_specs.append(pl.BlockSpec((D_MODEL, LANES), const))
        args.append(w_router)
        out_specs.append(pl.BlockSpec((tm, LANES), row))
        out_shape.append(jax.ShapeDtypeStruct((n, LANES), F32))
    return pl.pallas_call(
        functools.partial(_out_kernel, alpha, with_router, split_x, ctx_tiles),
        grid=(n // tm,),
        in_specs=in_specs,
        out_specs=out_specs,
        out_shape=out_shape,
        scratch_shapes=[pltpu.VMEM((tm, SGU_WIDTH), BF16)],
        compiler_params=_params(("arbitrary",)),
        name="out_proj_router" if with_router else "out_proj",
    )(*args)


def _swiglu_blocks(width):
    return [slice(s, min(s + V7X_MXU_WIDTH, width)) for s in range(0, width, V7X_MXU_WIDTH)]


def _swiglu_partial(x, wg_ref, wu_ref, wd_ref, side_work=None):
    out = None
    for n, cols in enumerate(_swiglu_blocks(wg_ref.shape[1])):
        a = jnp.dot(x, wg_ref[:, cols].astype(BF16), preferred_element_type=F32)
        if side_work is not None:
            side_work(3 * n)
        b = jnp.dot(x, wu_ref[:, cols].astype(BF16), preferred_element_type=F32)
        if side_work is not None:
            side_work(3 * n + 1)
        mid = ((a * _sigmoid(a)) * b).astype(BF16)
        part = jnp.dot(mid, wd_ref[cols, :].astype(BF16), preferred_element_type=F32)
        if side_work is not None:
            side_work(3 * n + 2)
        out = part if out is None else out + part
    return out


def _ffn_kernel(alpha, h_ref, x1_ref, mod_ref, wg_ref, wu_ref, wd_ref, g_ref, b_ref, o_ref, acc_ref):
    c = pl.program_id(1)
    part = _swiglu_partial(h_ref[...], wg_ref, wu_ref, wd_ref)

    @pl.when(c == 0)
    def _():
        acc_ref[...] = part

    @pl.when(c > 0)
    def _():
        acc_ref[...] += part

    @pl.when(c == pl.num_programs(1) - 1)
    def _():
        t = alpha * x1_ref[...] + mod_ref[5:6, :] * acc_ref[...]
        o_ref[...] = _layer_norm_rows(t, g_ref[...], b_ref[...])


def _ffn_chunk(d_ff):
    assert d_ff % (2 * LANES) == 0
    return d_ff // 2


def _ffn_call(h, x1, mods, wg, wu, wd, ln_g, ln_b, alpha, geom):
    n = h.shape[0]
    tm = DENSE_TILE
    d_ff = wg.shape[1]
    fc = _ffn_chunk(d_ff)
    row = lambda i, c: (i, 0)
    const = lambda i, c: (0, 0)
    return pl.pallas_call(
        functools.partial(_ffn_kernel, alpha),
        grid=(n // tm, d_ff // fc),
        in_specs=[
            pl.BlockSpec((tm, D_MODEL), row),
            pl.BlockSpec((tm, D_MODEL), row),
            pl.BlockSpec((None, N_MODS, D_MODEL), lambda i, c: (geom.mod_index(i, tm), 0, 0)),
            pl.BlockSpec((D_MODEL, fc), lambda i, c: (0, c)),
            pl.BlockSpec((D_MODEL, fc), lambda i, c: (0, c)),
            pl.BlockSpec((fc, D_MODEL), lambda i, c: (c, 0)),
            pl.BlockSpec((1, D_MODEL), const),
            pl.BlockSpec((1, D_MODEL), const),
        ],
        out_specs=pl.BlockSpec((tm, D_MODEL), row),
        out_shape=jax.ShapeDtypeStruct((n, D_MODEL), F32),
        scratch_shapes=[pltpu.VMEM((tm, D_MODEL), F32)],
        compiler_params=_params(("arbitrary", "arbitrary")),
        name="dense_ffn",
    )(h, x1, mods, wg, wu, wd, ln_g, ln_b)


def _moe_kernel(n_chunks, te_ref, nu_ref, src_ref, dst_ref, h_hbm, wg_ref, wu_ref, wd_ref, o_hbm,
                xbuf, xb, obuf, gsem, ssem):
    del te_ref
    tr = xb.shape[0]
    t = pl.program_id(0)
    c = pl.program_id(1)
    n_tiles = pl.num_programs(0)
    valid = t < nu_ref[0]
    slot = t % 2
    other = 1 - slot
    share = tr // n_chunks
    first_row = c * share
    nxt_base = jnp.minimum(t + 1, n_tiles - 1) * tr
    prv_base = jnp.where(t == 0, n_tiles - 1, t - 1) * tr

    def gather_row(base, dst_slot, r, tok=None):
        tok = src_ref[base + r] if tok is None else tok
        return pltpu.make_async_copy(h_hbm.at[tok], xbuf.at[dst_slot, r], gsem.at[dst_slot])

    def scatter_row(base, src_slot, r, dst=None):
        dst = dst_ref[base + r] if dst is None else dst
        return pltpu.make_async_copy(obuf.at[src_slot, r], o_hbm.at[dst], ssem.at[src_slot])

    def wait_gather(dst_slot):
        pltpu.make_async_copy(h_hbm.at[pl.ds(0, tr)], xbuf.at[dst_slot], gsem.at[dst_slot]).wait()

    def wait_scatter(src_slot):
        pltpu.make_async_copy(obuf.at[src_slot], o_hbm.at[pl.ds(0, tr)], ssem.at[src_slot]).wait()

    def looped(n_rows, start_row):
        def body(r, carry):
            start_row(r)
            return carry
        lax.fori_loop(0, n_rows, body, 0, unroll=8)

    @pl.when((t == 0) & (c == 0))
    def _():
        obuf[1] = jnp.zeros(obuf.shape[1:], obuf.dtype)
        looped(tr, lambda r: gather_row(0, 0, r).start())

    n_used = nu_ref[0]

    @pl.when(c == 0)
    def _():
        @pl.when(t <= n_used)
        def _():
            wait_gather(slot)

        @pl.when(t >= 1)
        def _():
            wait_scatter(slot)

    def tile_scatter(base, src_slot):
        return pltpu.make_async_copy(obuf.at[src_slot], o_hbm.at[pl.ds(dst_ref[base], tr)], ssem.at[src_slot])

    def multiply_tile(slot_s, c_s):
        other_s = 1 - slot_s
        if c_s == 0:
            xb[...] = _tiles_to_rows(xbuf[slot_s]).astype(BF16)
        n_groups = 3 * len(_swiglu_blocks(wg_ref.shape[1]))

        def row_dmas(i):
            for r in range(c_s * share + share * i // n_groups, c_s * share + share * (i + 1) // n_groups):
                gather_row(nxt_base, other_s, r).start()
                scatter_row(prv_base, other_s, r).start()

        part = _swiglu_partial(xb[...], wg_ref, wu_ref, wd_ref, row_dmas)
        if c_s == 0:
            obuf[slot_s] = _rows_to_tiles(part)
        else:
            obuf[slot_s] += _rows_to_tiles(part)

    for slot_s in range(2):
        for c_s in range(n_chunks):
            pl.when(valid & (slot == slot_s) & (c == c_s))(functools.partial(multiply_tile, slot_s, c_s))

    @pl.when(jnp.logical_not(valid))
    def _():
        @pl.when(c == 0)
        def _():
            obuf[slot] = jnp.zeros(obuf.shape[1:], obuf.dtype)

        @pl.when(t == n_used)
        def _():
            looped(share, lambda r: scatter_row(prv_base, other, first_row + r).start())

        @pl.when((t > n_used) & (c == 0))
        def _():
            tile_scatter(prv_base, other).start()

    @pl.when((t == n_tiles - 1) & (c == n_chunks - 1))
    def _():
        wait_scatter(other)
        tile_scatter(t * tr, slot).start()
        wait_scatter(slot)


def _moe_call(h, wg, wu, wd, plan):
    tr = FFN_TILE
    d_ff = wg.shape[2]
    fc = _ffn_chunk(d_ff)
    n_chunks = d_ff // fc
    t_max = plan["tile_expert"].shape[0]
    row_tile = h.shape[1:]

    def chunk(t, c, nu):
        return jnp.where(t < nu[0], c, n_chunks - 1)

    grid_spec = pltpu.PrefetchScalarGridSpec(
        num_scalar_prefetch=4,
        grid=(t_max, n_chunks),
        in_specs=[
            pl.BlockSpec(memory_space=pl.ANY),
            pl.BlockSpec((None, D_MODEL, fc), lambda t, c, te, nu, src, dst: (te[t], 0, chunk(t, c, nu))),
            pl.BlockSpec((None, D_MODEL, fc), lambda t, c, te, nu, src, dst: (te[t], 0, chunk(t, c, nu))),
            pl.BlockSpec((None, fc, D_MODEL), lambda t, c, te, nu, src, dst: (te[t], chunk(t, c, nu), 0)),
        ],
        out_specs=pl.BlockSpec(memory_space=pl.ANY),
        scratch_shapes=[
            pltpu.VMEM((2, tr) + row_tile, F32),
            pltpu.VMEM((tr, D_MODEL), BF16),
            pltpu.VMEM((2, tr) + row_tile, F32),
            pltpu.SemaphoreType.DMA((2,)),
            pltpu.SemaphoreType.DMA((2,)),
        ],
    )
    return pl.pallas_call(
        functools.partial(_moe_kernel, n_chunks),
        grid_spec=grid_spec,
        out_shape=jax.ShapeDtypeStruct((t_max * tr,) + row_tile, F32),
        compiler_params=_params(("arbitrary", "arbitrary")),
        name="expert_ffn",
    )(plan["tile_expert"], plan["n_used"], plan["src_token"], plan["dst_row"], h, wg, wu, wd)


def _route_plan(routing, n_tok, tr):
    n_assign = TOP_K * n_tok
    t_max = n_assign // tr + N_EXPERTS
    n_slots = t_max * tr
    n_pad = n_slots - n_assign
    id_bits = (n_slots - 1).bit_length()
    experts = jnp.arange(N_EXPERTS, dtype=jnp.int32)
    e_flat = routing[:, :TOP_K].astype(jnp.int32).T.reshape(-1)
    counts = jnp.sum((e_flat[:, None] == experts[None, :]).astype(jnp.int32), axis=0)
    tiles_e = (counts + tr - 1) // tr
    tile_end = jnp.cumsum(tiles_e)
    n_used = tile_end[-1]
    pad_end = jnp.cumsum(tiles_e * tr - counts)
    pad_ids = jnp.arange(n_pad, dtype=jnp.int32)
    pad_expert = jnp.sum((pad_ids[:, None] >= pad_end[None, :]).astype(jnp.int32), axis=1)
    keys = jnp.concatenate([e_flat * 2, pad_expert * 2 + 1])
    item = jnp.arange(n_slots, dtype=jnp.int32)
    slot_item = jnp.sort((keys << id_bits) | item) & ((1 << id_bits) - 1)
    real = slot_item < n_assign
    src = jnp.where(real, slot_item % n_tok, 0).astype(jnp.int32)
    dst = slot_item.astype(jnp.int32)
    t_ids = jnp.arange(t_max, dtype=jnp.int32)
    te = jnp.sum((t_ids[:, None] >= tile_end[None, :]).astype(jnp.int32), axis=1)
    last_used = jnp.max(jnp.where(tiles_e > 0, experts, 0))
    te = jnp.where(t_ids < n_used, te, last_used).astype(jnp.int32)
    return {"tile_expert": te, "n_used": n_used.reshape(1).astype(jnp.int32), "src_token": src, "dst_row": dst}


def _combine_kernel(alpha, split_ctx_tiles, e1_ref, e2_ref, rt_ref, x1_ref, mod_ref, g_ref, b_ref, *o_refs):
    y = rt_ref[:, 2:3] * _tiles_to_rows(e1_ref[...]) + rt_ref[:, 3:4] * _tiles_to_rows(e2_ref[...])
    t = alpha * x1_ref[...] + mod_ref[5:6, :] * y
    out = _layer_norm_rows(t, g_ref[...], b_ref[...])
    if split_ctx_tiles is None:
        o_refs[0][...] = out
    else:
        _write_stream(o_refs[0], o_refs[1], split_ctx_tiles, out)


def _combine_call(eo, routing, x1, mods, ln_g, ln_b, alpha, geom, split_out):
    n = x1.shape[0]
    tm = COMBINE_TILE
    row = lambda i: (i, 0)
    const = lambda i: (0, 0)
    if split_out:
        out_specs = _stream_specs(D_MODEL, geom.ctx_tiles(tm), tm)
        out_shape = [jax.ShapeDtypeStruct((geom.n_ctx, D_MODEL), F32),
                     jax.ShapeDtypeStruct((n - geom.n_ctx, D_MODEL), F32)]
    else:
        out_specs = pl.BlockSpec((tm, D_MODEL), row)
        out_shape = jax.ShapeDtypeStruct((n, D_MODEL), F32)
    return pl.pallas_call(
        functools.partial(_combine_kernel, alpha, geom.ctx_tiles(tm) if split_out else None),
        grid=(n // tm,),
        in_specs=[
            pl.BlockSpec((tm,) + eo.shape[1:], lambda i: (i, 0, 0)),
            pl.BlockSpec((tm,) + eo.shape[1:], lambda i: (i + n // tm, 0, 0)),
            pl.BlockSpec((tm, LANES), row),
            pl.BlockSpec((tm, D_MODEL), row),
            pl.BlockSpec((None, N_MODS, D_MODEL), lambda i: (geom.mod_index(i, tm), 0, 0)),
            pl.BlockSpec((1, D_MODEL), const),
            pl.BlockSpec((1, D_MODEL), const),
        ],
        out_specs=out_specs,
        out_shape=out_shape,
        compiler_params=_params(("arbitrary",)),
        name="expert_combine",
    )(eo, eo, routing, x1, mods, ln_g, ln_b)


class _Geometry:
    def __init__(self, n_ctx, dec_batch, dec_seq):
        self.n_ctx = n_ctx
        self.dec_batch = dec_batch
        self.dec_seq = dec_seq
        self.n_tok = n_ctx + dec_batch * dec_seq

    def ctx_tiles(self, tm):
        assert self.n_ctx % tm == 0 and self.dec_seq % tm == 0
        return self.n_ctx // tm

    def mod_index(self, i, tm):
        ct = self.ctx_tiles(tm)
        return jnp.where(i < ct, 0, 1 + (i - ct) // (self.dec_seq // tm))

    def rope_index(self, i, tm):
        ct = self.ctx_tiles(tm)
        return jnp.where(i < ct, 0, 1 + (i - ct) % (self.dec_seq // tm))


def _rope_tables(dec_seq):
    pos = jnp.arange(dec_seq, dtype=jnp.int32)
    row = (pos // GRID_W).astype(F32)
    col = (pos % GRID_W).astype(F32)
    inv = ROPE_BASE ** (-jnp.arange(ROPE_PAIRS, dtype=F32) / ROPE_PAIRS)
    ang_r = row[:, None] * inv[None, :]
    ang_c = col[:, None] * inv[None, :]
    cos_h = jnp.concatenate([jnp.cos(ang_r), jnp.cos(ang_r), jnp.cos(ang_c), jnp.cos(ang_c)], axis=-1)
    sin_h = jnp.concatenate([-jnp.sin(ang_r), jnp.sin(ang_r), -jnp.sin(ang_c), jnp.sin(ang_c)], axis=-1)
    reps = LANES // HEAD_DIM
    cos_t = jnp.concatenate([jnp.ones((IN_TILE, LANES), F32), jnp.tile(cos_h, (1, reps))], axis=0)
    sin_t = jnp.concatenate([jnp.zeros((IN_TILE, LANES), F32), jnp.tile(sin_h, (1, reps))], axis=0)
    return cos_t, sin_t


def kernel(x_prompt, x_sample, cache_k, cache_v, c, c_ctx, w_ada, b_ada, w_in, w_o, attn_sink, w_s, b_s, sgu_ln_g, sgu_ln_b, ln1_g, ln1_b, ln2_g, ln2_b, w_ff_gate, w_ff_up, w_ff_down, w_router, w_exp_gate, w_exp_up, w_exp_down):
    batch, seq, d = x_prompt.shape
    dec_batch, dec_seq, _ = x_sample.shape
    depth = w_in.shape[0]
    past = cache_k.shape[2]
    assert d == D_MODEL and dec_batch + 1 <= MOD_ROWS
    n_ctx = batch * seq
    n_lat = dec_batch * dec_seq
    n_tok = n_ctx + n_lat
    geom = _Geometry(n_ctx, dec_batch, dec_seq)
    alpha = float((2 * depth) ** 0.25)

    x = (x_prompt.reshape(n_ctx, d), x_sample.reshape(n_lat, d))
    cvec = jnp.concatenate([c_ctx[None, :], c, jnp.zeros((MOD_ROWS - 1 - dec_batch, d), F32)], axis=0)
    mods_all = _ada_call(cvec, w_ada, b_ada).reshape(depth, MOD_ROWS, N_MODS, d)
    cos_t, sin_t = _rope_tables(dec_seq)

    new_k, new_v = [], []
    for l in range(depth):
        mods = mods_all[l]
        q, k, v, kv32, u, g = _in_call(x, mods, w_in[l].astype(BF16), cos_t, sin_t,
                                       sgu_ln_g[l].reshape(1, SGU_WIDTH), sgu_ln_b[l].reshape(1, SGU_WIDTH), geom)
        new_k.append(kv32[:n_ctx, :KV_WIDTH].reshape(batch, seq, N_KV_HEADS, HEAD_DIM))
        new_v.append(kv32[:n_ctx, KV_WIDTH:].reshape(batch, seq, N_KV_HEADS, HEAD_DIM))
        sink = attn_sink[l]
        a_ctx = _ctx_attn_call(sink, q, k, v, batch, seq)
        a_lat = _lat_attn_call(sink, q, k, v, cache_k[:, l].reshape(dec_batch, past, KV_WIDTH).astype(BF16),
                               cache_v[:, l].reshape(dec_batch, past, KV_WIDTH).astype(BF16), geom)
        bias_full = jnp.repeat(b_s[l].T, SGU_GROUP_DIM, axis=1)
        i = l // 2
        moe = l % 2 == 1
        w_r = None
        if moe:
            w_r = jnp.pad(w_router[i], ((0, 0), (0, LANES - N_EXPERTS)))
        outs = _out_call(a_ctx, a_lat, u, g, w_s[l].astype(BF16), bias_full, x, mods, w_o[l].astype(BF16), ln1_g[l].reshape(1, d), ln1_b[l].reshape(1, d),
                         w_r, alpha, geom)
        ln_g, ln_b = ln2_g[l].reshape(1, d), ln2_b[l].reshape(1, d)
        if moe:
            x1, h, routing = outs
            plan = _route_plan(routing, n_tok, FFN_TILE)
            eo = _moe_call(h, w_exp_gate[i], w_exp_up[i], w_exp_down[i], plan)
            x = _combine_call(eo, routing, x1, mods, ln_g, ln_b, alpha, geom, split_out=l == depth - 1)
        else:
            x1, h = outs
            x = _ffn_call(h, x1, mods, w_ff_gate[i].astype(BF16), w_ff_up[i].astype(BF16),
                          w_ff_down[i].astype(BF16), ln_g, ln_b, alpha, geom)

    if not isinstance(x, (tuple, list)):
        x = (x[:n_ctx], x[n_ctx:])
    y_prompt = x[0].reshape(batch, seq, d)
    y_sample = x[1].reshape(dec_batch, dec_seq, d)
    return (y_prompt, y_sample, jnp.stack(new_k, axis=1), jnp.stack(new_v, axis=1))
```

```python
import functools

import jax
import jax.numpy as jnp
import numpy as np
from jax import lax
from jax.experimental import pallas as pl
from jax.experimental.pallas import tpu as pltpu

F32 = jnp.float32
BF16 = jnp.bfloat16

D_MODEL = 1024
HEAD_DIM = 64
N_Q_HEADS = 8
N_KV_HEADS = 2
Q_PER_KV = N_Q_HEADS // N_KV_HEADS
ATTN_WIDTH = N_Q_HEADS * HEAD_DIM
KV_WIDTH = N_KV_HEADS * HEAD_DIM
SGU_WIDTH = D_MODEL - ATTN_WIDTH
N_SGU_GROUPS = 8
SGU_GROUP_DIM = SGU_WIDTH // N_SGU_GROUPS
CHUNK = 128
BLOCK = 128
WINDOW = 128
GRID_W = 64
IN_WIDTH = ATTN_WIDTH + 2 * KV_WIDTH + 2 * SGU_WIDTH
OFF_K = ATTN_WIDTH
OFF_V = OFF_K + KV_WIDTH
OFF_U = OFF_V + KV_WIDTH
OFF_G = OFF_U + SGU_WIDTH
N_EXPERTS = 8
TOP_K = 2
ROPE_BASE = 10000.0
ROPE_PAIRS = HEAD_DIM // 4
LN_EPS = 1e-5
ATTN_SCALE = HEAD_DIM ** -0.5
NEG_INF = -1e30
N_MODS = 6

LANES = 128
V7X_MXU_WIDTH = 256
V7X_VMEM_LIMIT_BYTES = 56 * 1024 * 1024

IN_TILE = 1024
OUT_TILE = 512
ROUTER_TILE = 256
COMBINE_TILE = 512
DENSE_TILE = 1024
FFN_TILE = 512
LAT_QUERY_ROWS = 512
CTX_SEQS_PER_STEP = 1
MOD_ROWS = 8
ADA_COLS = 1536


def _params(sem):
    return pltpu.CompilerParams(dimension_semantics=sem, vmem_limit_bytes=V7X_VMEM_LIMIT_BYTES)


def _gelu_tanh(x):
    return x * (0.5 * (1.0 + jnp.tanh(np.float32(np.sqrt(2.0 / np.pi)) * (x + 0.044715 * (x * x * x)))))


def _sigmoid(x):
    return 1.0 / (1.0 + jnp.exp(-x))


def _layer_norm_rows(t, g, b):
    mu = jnp.mean(t, axis=-1, keepdims=True)
    tc = t - mu
    var = jnp.mean(tc * tc, axis=-1, keepdims=True)
    return tc * lax.rsqrt(var + LN_EPS) * g + b


def _rows_to_tiles(x):
    pieces = jnp.stack([x[:, s * LANES:(s + 1) * LANES] for s in range(x.shape[1] // LANES)], axis=0)
    return jnp.swapaxes(pieces, 0, 1)


def _tiles_to_rows(t):
    pieces = jnp.swapaxes(t, 0, 1)
    return jnp.concatenate([pieces[s] for s in range(t.shape[1])], axis=1)


def _stream_specs(width, ctx_tiles, tm):
    return [pl.BlockSpec((tm, width), lambda i, *_: (jnp.minimum(i, ctx_tiles - 1), 0)),
            pl.BlockSpec((tm, width), lambda i, *_: (jnp.maximum(i - ctx_tiles, 0), 0))]


def _read_stream(ctx_ref, lat_ref, ctx_tiles):
    tile = lax.broadcasted_iota(jnp.int32, ctx_ref.shape, 0) * 0 + pl.program_id(0)
    return jnp.where(tile < ctx_tiles, ctx_ref[...], lat_ref[...])


def _write_stream(ctx_ref, lat_ref, ctx_tiles, value):
    @pl.when(pl.program_id(0) < ctx_tiles)
    def _():
        ctx_ref[...] = value

    @pl.when(pl.program_id(0) >= ctx_tiles)
    def _():
        lat_ref[...] = value


def _ada_kernel(c_ref, w_ref, b_ref, o_ref):
    c = c_ref[...]
    s = (c * _sigmoid(c)).astype(BF16)
    o_ref[...] = jnp.dot(s, w_ref[...].astype(BF16), preferred_element_type=F32) + b_ref[...]


def _ada_call(cvec, w_ada, b_ada):
    depth = w_ada.shape[0]
    n_out = w_ada.shape[2]
    return pl.pallas_call(
        _ada_kernel,
        grid=(depth, n_out // ADA_COLS),
        in_specs=[
            pl.BlockSpec((MOD_ROWS, D_MODEL), lambda l, j: (0, 0)),
            pl.BlockSpec((None, D_MODEL, ADA_COLS), lambda l, j: (l, 0, j)),
            pl.BlockSpec((None, 1, ADA_COLS), lambda l, j: (l, 0, j)),
        ],
        out_specs=pl.BlockSpec((None, MOD_ROWS, ADA_COLS), lambda l, j: (l, 0, j)),
        out_shape=jax.ShapeDtypeStruct((depth, MOD_ROWS, n_out), F32),
        compiler_params=_params(("arbitrary", "arbitrary")),
        name="adaln",
    )(cvec, w_ada, b_ada.reshape(depth, 1, n_out))


def _in_kernel(split, ctx_tiles, *refs):
    if split:
        x = _read_stream(refs[0], refs[1], ctx_tiles)
        refs = refs[2:]
    else:
        x = refs[0][...]
        refs = refs[1:]
    mod_ref, w_ref, cos_ref, sin_ref, lng_ref, lnb_ref, q_ref, k_ref, v_ref, kv_ref, u_ref, g_ref = refs
    tm = x.shape[0]
    h = x * (1.0 + mod_ref[1:2, :]) + mod_ref[0:1, :]
    hb = h.astype(BF16)

    def z_block(start):
        return jnp.dot(hb, w_ref[:, start:start + V7X_MXU_WIDTH], preferred_element_type=F32)

    cos = cos_ref[...]
    sin = sin_ref[...]
    lane = lax.broadcasted_iota(jnp.int32, (tm, LANES), 1)
    first_of_pair = (lane & (2 * ROPE_PAIRS - 1)) < ROPE_PAIRS
    lower_group = lane < SGU_GROUP_DIM

    def rope(t):
        partner = jnp.where(first_of_pair, pltpu.roll(t, LANES - ROPE_PAIRS, 1), pltpu.roll(t, ROPE_PAIRS, 1))
        return t * cos + partner * sin

    halves = (slice(0, LANES), slice(LANES, 2 * LANES))
    inv_n = 1.0 / SGU_GROUP_DIM
    for j in range(SGU_WIDTH // LANES):
        cols = slice(j * LANES, (j + 1) * LANES)
        if j % 2 == 0:
            z = z_block(OFF_G + j * LANES)
        t = _gelu_tanh(z[:, halves[j % 2]])
        s_lo = jnp.sum(jnp.where(lower_group, t, 0.0), axis=-1, keepdims=True)
        s_hi = jnp.sum(jnp.where(lower_group, 0.0, t), axis=-1, keepdims=True)
        tc = t - jnp.where(lower_group, s_lo, s_hi) * inv_n
        sq = tc * tc
        v_lo = jnp.sum(jnp.where(lower_group, sq, 0.0), axis=-1, keepdims=True)
        v_hi = jnp.sum(jnp.where(lower_group, 0.0, sq), axis=-1, keepdims=True)
        var = jnp.where(lower_group, v_lo, v_hi) * inv_n
        g_ref[:, cols] = (tc * lax.rsqrt(var + LN_EPS) * lng_ref[:, cols] + lnb_ref[:, cols]).astype(BF16)

    for start in range(0, ATTN_WIDTH, V7X_MXU_WIDTH):
        z = z_block(start)
        for half in halves:
            q_ref[:, start + half.start:start + half.stop] = (rope(z[:, half]) * ATTN_SCALE).astype(BF16)
    z = z_block(OFF_K)
    k_ref[...] = rope(z[:, halves[0]]).astype(BF16)
    v_ref[...] = z[:, halves[1]].astype(BF16)
    kv_ref[...] = z
    for start in range(0, SGU_WIDTH, V7X_MXU_WIDTH):
        u_ref[:, start:start + V7X_MXU_WIDTH] = _gelu_tanh(z_block(OFF_U + start))


def _in_call(x, mods, w_in, cos_t, sin_t, ln_g, ln_b, geom):
    n = geom.n_tok
    tm = IN_TILE
    mod_idx = lambda i: geom.mod_index(i, tm)
    rope_idx = lambda i: geom.rope_index(i, tm)
    row = lambda i: (i, 0)
    split = isinstance(x, tuple)
    x_args = list(x) if split else [x]
    ctx_tiles = geom.ctx_tiles(tm)
    x_specs = _stream_specs(D_MODEL, ctx_tiles, tm) if split else [pl.BlockSpec((tm, D_MODEL), row)]
    outs = pl.pallas_call(
        functools.partial(_in_kernel, split, ctx_tiles),
        grid=(n // tm,),
        in_specs=x_specs + [
            pl.BlockSpec((None, N_MODS, D_MODEL), lambda i: (mod_idx(i), 0, 0)),
            pl.BlockSpec((D_MODEL, IN_WIDTH), lambda i: (0, 0)),
            pl.BlockSpec((tm, LANES), lambda i: (rope_idx(i), 0)),
            pl.BlockSpec((tm, LANES), lambda i: (rope_idx(i), 0)),
            pl.BlockSpec((1, SGU_WIDTH), lambda i: (0, 0)),
            pl.BlockSpec((1, SGU_WIDTH), lambda i: (0, 0)),
        ],
        out_specs=[
            pl.BlockSpec((tm, ATTN_WIDTH), row),
            pl.BlockSpec((tm, KV_WIDTH), row),
            pl.BlockSpec((tm, KV_WIDTH), row),
            pl.BlockSpec((tm, 2 * KV_WIDTH), row),
            pl.BlockSpec((tm, SGU_WIDTH), row),
            pl.BlockSpec((tm, SGU_WIDTH), row),
        ],
        out_shape=[
            jax.ShapeDtypeStruct((n, ATTN_WIDTH), BF16),
            jax.ShapeDtypeStruct((n, KV_WIDTH), BF16),
            jax.ShapeDtypeStruct((n, KV_WIDTH), BF16),
            jax.ShapeDtypeStruct((n, 2 * KV_WIDTH), F32),
            jax.ShapeDtypeStruct((n, SGU_WIDTH), F32),
            jax.ShapeDtypeStruct((n, SGU_WIDTH), BF16),
        ],
        compiler_params=_params(("arbitrary",)),
        name="in_proj",
    )(*x_args, mods, w_in, cos_t, sin_t, ln_g, ln_b)
    return outs


def _group_attention(q_ref, rows, hk, k_all, v_all, block_masks, sink_ref):
    m_rows = rows.stop - rows.start
    heads = [hk * Q_PER_KV + gq for gq in range(Q_PER_KV)]
    q = jnp.concatenate([q_ref[rows, h * HEAD_DIM:(h + 1) * HEAD_DIM] for h in heads], axis=0)
    s = lax.dot_general(q, k_all, (((1,), (1,)), ((), ())), preferred_element_type=F32)
    n_blocks = k_all.shape[0] // LANES
    blocks = [s[:, b * LANES:(b + 1) * LANES] for b in range(n_blocks)]
    for b, mask in block_masks.items():
        blocks[b] = jnp.where(mask, blocks[b], NEG_INF)
    head_of_row = lax.broadcasted_iota(jnp.int32, (Q_PER_KV * m_rows, 1), 0) // m_rows
    sink = jnp.zeros((Q_PER_KV * m_rows, 1), F32)
    for gq, h in enumerate(heads):
        sink = jnp.where(head_of_row == gq, sink_ref[h], sink)
    m_el = blocks[0]
    for blk in blocks[1:]:
        m_el = jnp.maximum(m_el, blk)
    m = jnp.maximum(jnp.max(m_el, axis=-1, keepdims=True), sink)
    probs = [jnp.exp(blk - m) for blk in blocks]
    l_el = probs[0]
    for p in probs[1:]:
        l_el = l_el + p
    denom = jnp.sum(l_el, axis=-1, keepdims=True) + jnp.exp(sink - m)
    p_all = jnp.concatenate([p.astype(BF16) for p in probs], axis=1)
    o = jnp.dot(p_all, v_all, preferred_element_type=F32) / denom
    return {h: o[gq * m_rows:(gq + 1) * m_rows] for gq, h in enumerate(heads)}


def _store_heads(o_ref, rows, outs):
    for h0 in range(0, N_Q_HEADS, 2):
        pair = jnp.concatenate([outs[h0], outs[h0 + 1]], axis=1)
        o_ref[rows, h0 * HEAD_DIM:(h0 + 2) * HEAD_DIM] = pair.astype(o_ref.dtype)


def _ctx_attn_kernel(seq_len, sink_ref, q_ref, k_ref, v_ref, o_ref):
    for sq in range(q_ref.shape[0] // seq_len):
        rows = slice(sq * seq_len, (sq + 1) * seq_len)
        outs = {}
        for hk in range(N_KV_HEADS):
            kv_cols = slice(hk * HEAD_DIM, (hk + 1) * HEAD_DIM)
            outs.update(_group_attention(q_ref, rows, hk, k_ref[rows, kv_cols], v_ref[rows, kv_cols], {}, sink_ref))
        _store_heads(o_ref, rows, outs)


def _ctx_attn_call(sink, q, k, v, n_seq, seq_len):
    n = n_seq * seq_len
    per_step = CTX_SEQS_PER_STEP if n_seq % CTX_SEQS_PER_STEP == 0 else 1
    rows = per_step * seq_len
    blk = lambda b: (b, 0)
    return pl.pallas_call(
        functools.partial(_ctx_attn_kernel, seq_len),
        grid=(n_seq // per_step,),
        in_specs=[
            pl.BlockSpec(memory_space=pltpu.SMEM),
            pl.BlockSpec((rows, ATTN_WIDTH), blk),
            pl.BlockSpec((rows, KV_WIDTH), blk),
            pl.BlockSpec((rows, KV_WIDTH), blk),
        ],
        out_specs=pl.BlockSpec((rows, ATTN_WIDTH), blk),
        out_shape=jax.ShapeDtypeStruct((n, ATTN_WIDTH), BF16),
        compiler_params=_params(("arbitrary",)),
        name="ctx_attention",
    )(sink, q, k, v)


def _lat_attn_kernel(sink_ref, q_ref, k_ref, v_ref, ck_ref, cv_ref, o_ref):
    blocks_per_step = q_ref.shape[0] // BLOCK
    nb = k_ref.shape[0] // BLOCK
    past_blocks = ck_ref.shape[0] // LANES
    r = lax.broadcasted_iota(jnp.int32, (Q_PER_KV * BLOCK, BLOCK), 0) & (BLOCK - 1)
    c = lax.broadcasted_iota(jnp.int32, (Q_PER_KV * BLOCK, BLOCK), 1)
    for sub in range(blocks_per_step):
        j = pl.program_id(1) * blocks_per_step + sub
        rows = slice(sub * BLOCK, (sub + 1) * BLOCK)
        mask_prev = c >= r + jnp.where(j > 0, 0, BLOCK)
        mask_next = c <= r - jnp.where(j < nb - 1, 0, BLOCK)
        prev = pl.ds(pl.multiple_of(jnp.maximum(j - 1, 0) * BLOCK, BLOCK), BLOCK)
        cur = pl.ds(pl.multiple_of(j * BLOCK, BLOCK), BLOCK)
        nxt = pl.ds(pl.multiple_of(jnp.minimum(j + 1, nb - 1) * BLOCK, BLOCK), BLOCK)
        outs = {}
        for hk in range(N_KV_HEADS):
            kv_cols = slice(hk * HEAD_DIM, (hk + 1) * HEAD_DIM)
            k_all = jnp.concatenate([ck_ref[:, kv_cols], k_ref[prev, kv_cols], k_ref[cur, kv_cols],
                                     k_ref[nxt, kv_cols]], axis=0)
            v_all = jnp.concatenate([cv_ref[:, kv_cols], v_ref[prev, kv_cols], v_ref[cur, kv_cols],
                                     v_ref[nxt, kv_cols]], axis=0)
            masks = {past_blocks: mask_prev, past_blocks + 2: mask_next}
            outs.update(_group_attention(q_ref, rows, hk, k_all, v_all, masks, sink_ref))
        _store_heads(o_ref, rows, outs)


def _lat_attn_call(sink, q, k, v, cache_k, cache_v, geom):
    qb = LAT_QUERY_ROWS
    steps = geom.dec_seq // qb
    assert geom.n_ctx % geom.dec_seq == 0 and cache_k.shape[1] % LANES == 0
    seq_base = geom.n_ctx // geom.dec_seq
    past = cache_k.shape[1]
    seq_spec = pl.BlockSpec((geom.dec_seq, KV_WIDTH), lambda b, j: (seq_base + b, 0))
    cache_spec = pl.BlockSpec((None, past, KV_WIDTH), lambda b, j: (b, 0, 0))
    return pl.pallas_call(
        _lat_attn_kernel,
        grid=(geom.dec_batch, steps),
        in_specs=[
            pl.BlockSpec(memory_space=pltpu.SMEM),
            pl.BlockSpec((qb, ATTN_WIDTH), lambda b, j: (geom.n_ctx // qb + b * steps + j, 0)),
            seq_spec, seq_spec,
            cache_spec, cache_spec,
        ],
        out_specs=pl.BlockSpec((qb, ATTN_WIDTH), lambda b, j: (b * steps + j, 0)),
        out_shape=jax.ShapeDtypeStruct((geom.dec_batch * geom.dec_seq, ATTN_WIDTH), BF16),
        compiler_params=_params(("arbitrary", "arbitrary")),
        name="lat_attention",
    )(sink, q, k, v, cache_k, cache_v)


def _spatial_gating(u_ref, g_ref, mix_ref, bias_ref, s_ref):
    tm = u_ref.shape[0]
    lane = lax.broadcasted_iota(jnp.int32, (CHUNK, LANES), 1)
    lower_group = lane < SGU_GROUP_DIM
    for ch in range(tm // CHUNK):
        rows = slice(ch * CHUNK, (ch + 1) * CHUNK)
        for p in range(SGU_WIDTH // LANES):
            cols = slice(p * LANES, (p + 1) * LANES)
            g = g_ref[rows, cols]
            zero = jnp.zeros_like(g)
            mixed = (jnp.dot(mix_ref[2 * p], jnp.where(lower_group, g, zero), preferred_element_type=F32)
                     + jnp.dot(mix_ref[2 * p + 1], jnp.where(lower_group, zero, g), preferred_element_type=F32))
            s_ref[rows, cols] = (u_ref[rows, cols] * (mixed + bias_ref[:, cols])).astype(s_ref.dtype)


def _out_kernel(alpha, with_router, split_x, ctx_tiles, a_ctx_ref, a_lat_ref, u_ref, gg_ref, mix_ref, bias_ref,
                *refs):
    s_ref = refs[-1]
    refs = refs[:-1]
    if split_x:
        x = _read_stream(refs[0], refs[1], ctx_tiles)
        refs = refs[2:]
    else:
        x = refs[0][...]
        refs = refs[1:]
    mod_ref, wa_ref, ws_ref, g_ref, b_ref = refs[:5]
    if with_router:
        wr_ref, x1_ref, h_ref, rt_ref = refs[5:]
    else:
        x1_ref, h_ref = refs[5:]
    a = _read_stream(a_ctx_ref, a_lat_ref, ctx_tiles)
    _spatial_gating(u_ref, gg_ref, mix_ref, bias_ref, s_ref)
    y = (jnp.dot(a, wa_ref[...], preferred_element_type=F32)
         + jnp.dot(s_ref[...], ws_ref[...], preferred_element_type=F32))
    t = alpha * x + mod_ref[2:3, :] * y
    x1 = _layer_norm_rows(t, g_ref[...], b_ref[...])
    x1_ref[...] = x1
    h = x1 * (1.0 + mod_ref[4:5, :]) + mod_ref[3:4, :]
    if with_router:
        h_ref[...] = _rows_to_tiles(h)
    else:
        h_ref[...] = h.astype(h_ref.dtype)
    if with_router:
        tm = h.shape[0]
        wr = wr_ref[...]
        wr_hi = wr.astype(BF16)
        wr_lo = (wr - wr_hi.astype(F32)).astype(BF16)
        h_hi = h.astype(BF16)
        h_lo = (h - h_hi.astype(F32)).astype(BF16)
        logits = (jnp.dot(h_hi, wr_hi, preferred_element_type=F32)
                  + jnp.dot(h_lo, wr_hi, preferred_element_type=F32)
                  + jnp.dot(h_hi, wr_lo, preferred_element_type=F32))
        lane = lax.broadcasted_iota(jnp.int32, (tm, LANES), 1).astype(F32)
        neg = jnp.float32(-jnp.inf)
        lg = jnp.where(lane < N_EXPERTS, logits, neg)
        m1 = jnp.max(lg, axis=-1, keepdims=True)
        i1 = jnp.min(jnp.where(lg == m1, lane, float(LANES)), axis=-1, keepdims=True)
        lg2 = jnp.where(lane == i1, neg, lg)
        m2 = jnp.max(lg2, axis=-1, keepdims=True)
        i2 = jnp.min(jnp.where(lg2 == m2, lane, float(LANES)), axis=-1, keepdims=True)
        e2 = jnp.exp(m2 - m1)
        g1 = 1.0 / (1.0 + e2)
        g2 = e2 / (1.0 + e2)
        rt = jnp.where(lane == 0, i1, jnp.where(lane == 1, i2, jnp.where(lane == 2, g1, jnp.where(lane == 3, g2, 0.0))))
        rt_ref[...] = rt


def _out_call(a_ctx, a_lat, u, gg, w_s, bias_full, x, mods, w_o, ln_g, ln_b, w_router, alpha, geom):
    n = geom.n_tok
    with_router = w_router is not None
    tm = ROUTER_TILE if with_router else OUT_TILE
    ctx_tiles = geom.ctx_tiles(tm)
    row = lambda i: (i, 0)
    const = lambda i: (0, 0)
    split_x = isinstance(x, tuple)
    x_args = list(x) if split_x else [x]
    x_specs = _stream_specs(D_MODEL, ctx_tiles, tm) if split_x else [pl.BlockSpec((tm, D_MODEL), row)]
    sgu_specs = [
        pl.BlockSpec((tm, SGU_WIDTH), row),
        pl.BlockSpec((tm, SGU_WIDTH), row),
        pl.BlockSpec((N_SGU_GROUPS, CHUNK, CHUNK), lambda i: (0, 0, 0)),
        pl.BlockSpec((CHUNK, SGU_WIDTH), const),
    ]
    in_specs = _stream_specs(ATTN_WIDTH, ctx_tiles, tm) + sgu_specs + x_specs + [
        pl.BlockSpec((None, N_MODS, D_MODEL), lambda i: (geom.mod_index(i, tm), 0, 0)),
        pl.BlockSpec((ATTN_WIDTH, D_MODEL), const),
        pl.BlockSpec((SGU_WIDTH, D_MODEL), lambda i: (1, 0)),
        pl.BlockSpec((1, D_MODEL), const),
        pl.BlockSpec((1, D_MODEL), const),
    ]
    args = [a_ctx, a_lat, u, gg, w_s, bias_full] + x_args + [mods, w_o, w_o, ln_g, ln_b]
    if with_router:
        h_spec = pl.BlockSpec((tm, D_MODEL // LANES, LANES), lambda i: (i, 0, 0))
        h_shape = jax.ShapeDtypeStruct((n, D_MODEL // LANES, LANES), F32)
    else:
        h_spec = pl.BlockSpec((tm, D_MODEL), row)
        h_shape = jax.ShapeDtypeStruct((n, D_MODEL), BF16)
    out_specs = [pl.BlockSpec((tm, D_MODEL), row), h_spec]
    out_shape = [jax.ShapeDtypeStruct((n, D_MODEL), F32), h_shape]
    if with_router:
        in_specs.append(pl.BlockSpec((D_MODEL, LANES), const))
        args.append(w_router)
        out_specs.append(pl.BlockSpec((tm, LANES), row))
        out_shape.append(jax.ShapeDtypeStruct((n, LANES), F32))
    return pl.pallas_call(
        functools.partial(_out_kernel, alpha, with_router, split_x, ctx_tiles),
        grid=(n // tm,),
        in_specs=in_specs,
        out_specs=out_specs,
        out_shape=out_shape,
        scratch_shapes=[pltpu.VMEM((tm, SGU_WIDTH), BF16)],
        compiler_params=_params(("arbitrary",)),
        name="out_proj_router" if with_router else "out_proj",
    )(*args)


def _swiglu_blocks(width):
    return [slice(s, min(s + V7X_MXU_WIDTH, width)) for s in range(0, width, V7X_MXU_WIDTH)]


def _swiglu_partial(x, wg_ref, wu_ref, wd_ref, side_work=None):
    out = None
    for n, cols in enumerate(_swiglu_blocks(wg_ref.shape[1])):
        a = jnp.dot(x, wg_ref[:, cols].astype(BF16), preferred_element_type=F32)
        if side_work is not None:
            side_work(3 * n)
        b = jnp.dot(x, wu_ref[:, cols].astype(BF16), preferred_element_type=F32)
        if side_work is not None:
            side_work(3 * n + 1)
        mid = ((a * _sigmoid(a)) * b).astype(BF16)
        part = jnp.dot(mid, wd_ref[cols, :].astype(BF16), preferred_element_type=F32)
        if side_work is not None:
            side_work(3 * n + 2)
        out = part if out is None else out + part
    return out


def _ffn_kernel(alpha, h_ref, x1_ref, mod_ref, wg_ref, wu_ref, wd_ref, g_ref, b_ref, o_ref):
    y = _swiglu_partial(h_ref[...], wg_ref, wu_ref, wd_ref)
    t = alpha * x1_ref[...] + mod_ref[5:6, :] * y
    o_ref[...] = _layer_norm_rows(t, g_ref[...], b_ref[...])


def _ffn_chunk(d_ff):
    assert d_ff % (2 * LANES) == 0
    return d_ff // 2


def _ffn_call(h, x1, mods, wg, wu, wd, ln_g, ln_b, alpha, geom):
    n = h.shape[0]
    tm = DENSE_TILE
    d_ff = wg.shape[1]
    row = lambda i: (i, 0)
    const = lambda i: (0, 0)
    resident = pl.Buffered(1)
    return pl.pallas_call(
        functools.partial(_ffn_kernel, alpha),
        grid=(n // tm,),
        in_specs=[
            pl.BlockSpec((tm, D_MODEL), row),
            pl.BlockSpec((tm, D_MODEL), row),
            pl.BlockSpec((None, N_MODS, D_MODEL), lambda i: (geom.mod_index(i, tm), 0, 0)),
            pl.BlockSpec((D_MODEL, d_ff), const, pipeline_mode=resident),
            pl.BlockSpec((D_MODEL, d_ff), const, pipeline_mode=resident),
            pl.BlockSpec((d_ff, D_MODEL), const, pipeline_mode=resident),
            pl.BlockSpec((1, D_MODEL), const),
            pl.BlockSpec((1, D_MODEL), const),
        ],
        out_specs=pl.BlockSpec((tm, D_MODEL), row),
        out_shape=jax.ShapeDtypeStruct((n, D_MODEL), F32),
        compiler_params=_params(("arbitrary",)),
        name="dense_ffn",
    )(h, x1, mods, wg, wu, wd, ln_g, ln_b)


def _moe_kernel(n_chunks, te_ref, nu_ref, src_ref, dst_ref, h_hbm, wg_ref, wu_ref, wd_ref, o_hbm,
                xbuf, xb, obuf, gsem, ssem):
    del te_ref
    tr = xb.shape[0]
    t = pl.program_id(0)
    c = pl.program_id(1)
    n_tiles = pl.num_programs(0)
    valid = t < nu_ref[0]
    slot = t % 2
    other = 1 - slot
    share = tr // n_chunks
    first_row = c * share
    nxt_base = jnp.minimum(t + 1, n_tiles - 1) * tr
    prv_base = jnp.where(t == 0, n_tiles - 1, t - 1) * tr

    def gather_row(base, dst_slot, r, tok=None):
        tok = src_ref[base + r] if tok is None else tok
        return pltpu.make_async_copy(h_hbm.at[tok], xbuf.at[dst_slot, r], gsem.at[dst_slot])

    def scatter_row(base, src_slot, r, dst=None):
        dst = dst_ref[base + r] if dst is None else dst
        return pltpu.make_async_copy(obuf.at[src_slot, r], o_hbm.at[dst], ssem.at[src_slot])

    def wait_gather(dst_slot):
        pltpu.make_async_copy(h_hbm.at[pl.ds(0, tr)], xbuf.at[dst_slot], gsem.at[dst_slot]).wait()

    def wait_scatter(src_slot):
        pltpu.make_async_copy(obuf.at[src_slot], o_hbm.at[pl.ds(0, tr)], ssem.at[src_slot]).wait()

    def looped(n_rows, start_row):
        def body(r, carry):
            start_row(r)
            return carry
        lax.fori_loop(0, n_rows, body, 0, unroll=8)

    @pl.when((t == 0) & (c == 0))
    def _():
        obuf[1] = jnp.zeros(obuf.shape[1:], obuf.dtype)
        looped(tr, lambda r: gather_row(0, 0, r).start())

    n_used = nu_ref[0]

    @pl.when(c == 0)
    def _():
        @pl.when(t <= n_used)
        def _():
            wait_gather(slot)

        @pl.when(t >= 1)
        def _():
            wait_scatter(slot)

    def tile_scatter(base, src_slot):
        return pltpu.make_async_copy(obuf.at[src_slot], o_hbm.at[pl.ds(dst_ref[base], tr)], ssem.at[src_slot])

    def multiply_tile(slot_s, c_s):
        other_s = 1 - slot_s
        if c_s == 0:
            xb[...] = _tiles_to_rows(xbuf[slot_s]).astype(BF16)
        n_groups = 3 * len(_swiglu_blocks(wg_ref.shape[1]))

        def row_dmas(i):
            for r in range(c_s * share + share * i // n_groups, c_s * share + share * (i + 1) // n_groups):
                gather_row(nxt_base, other_s, r).start()
                scatter_row(prv_base, other_s, r).start()

        part = _swiglu_partial(xb[...], wg_ref, wu_ref, wd_ref, row_dmas)
        if c_s == 0:
            obuf[slot_s] = _rows_to_tiles(part)
        else:
            obuf[slot_s] += _rows_to_tiles(part)

    for slot_s in range(2):
        for c_s in range(n_chunks):
            pl.when(valid & (slot == slot_s) & (c == c_s))(functools.partial(multiply_tile, slot_s, c_s))

    @pl.when(jnp.logical_not(valid))
    def _():
        @pl.when(c == 0)
        def _():
            obuf[slot] = jnp.zeros(obuf.shape[1:], obuf.dtype)

        @pl.when(t == n_used)
        def _():
            looped(share, lambda r: scatter_row(prv_base, other, first_row + r).start())

        @pl.when((t > n_used) & (c == 0))
        def _():
            tile_scatter(prv_base, other).start()

    @pl.when((t == n_tiles - 1) & (c == n_chunks - 1))
    def _():
        wait_scatter(other)
        tile_scatter(t * tr, slot).start()
        wait_scatter(slot)


def _moe_call(h, wg, wu, wd, plan):
    tr = FFN_TILE
    d_ff = wg.shape[2]
    fc = _ffn_chunk(d_ff)
    n_chunks = d_ff // fc
    t_max = plan["tile_expert"].shape[0]
    row_tile = h.shape[1:]

    def chunk(t, c, nu):
        return jnp.where(t < nu[0], c, n_chunks - 1)

    grid_spec = pltpu.PrefetchScalarGridSpec(
        num_scalar_prefetch=4,
        grid=(t_max, n_chunks),
        in_specs=[
            pl.BlockSpec(memory_space=pl.ANY),
            pl.BlockSpec((None, D_MODEL, fc), lambda t, c, te, nu, src, dst: (te[t], 0, chunk(t, c, nu))),
            pl.BlockSpec((None, D_MODEL, fc), lambda t, c, te, nu, src, dst: (te[t], 0, chunk(t, c, nu))),
            pl.BlockSpec((None, fc, D_MODEL), lambda t, c, te, nu, src, dst: (te[t], chunk(t, c, nu), 0)),
        ],
        out_specs=pl.BlockSpec(memory_space=pl.ANY),
        scratch_shapes=[
            pltpu.VMEM((2, tr) + row_tile, F32),
            pltpu.VMEM((tr, D_MODEL), BF16),
            pltpu.VMEM((2, tr) + row_tile, F32),
            pltpu.SemaphoreType.DMA((2,)),
            pltpu.SemaphoreType.DMA((2,)),
        ],
    )
    return pl.pallas_call(
        functools.partial(_moe_kernel, n_chunks),
        grid_spec=grid_spec,
        out_shape=jax.ShapeDtypeStruct((t_max * tr,) + row_tile, F32),
        compiler_params=_params(("arbitrary", "arbitrary")),
        name="expert_ffn",
    )(plan["tile_expert"], plan["n_used"], plan["src_token"], plan["dst_row"], h, wg, wu, wd)


def _route_plan(routing, n_tok, tr):
    n_assign = TOP_K * n_tok
    t_max = n_assign // tr + N_EXPERTS
    n_slots = t_max * tr
    n_pad = n_slots - n_assign
    id_bits = (n_slots - 1).bit_length()
    experts = jnp.arange(N_EXPERTS, dtype=jnp.int32)
    e_flat = routing[:, :TOP_K].astype(jnp.int32).T.reshape(-1)
    counts = jnp.sum((e_flat[:, None] == experts[None, :]).astype(jnp.int32), axis=0)
    tiles_e = (counts + tr - 1) // tr
    tile_end = jnp.cumsum(tiles_e)
    n_used = tile_end[-1]
    pad_end = jnp.cumsum(tiles_e * tr - counts)
    pad_ids = jnp.arange(n_pad, dtype=jnp.int32)
    pad_expert = jnp.sum((pad_ids[:, None] >= pad_end[None, :]).astype(jnp.int32), axis=1)
    keys = jnp.concatenate([e_flat * 2, pad_expert * 2 + 1])
    item = jnp.arange(n_slots, dtype=jnp.int32)
    slot_item = jnp.sort((keys << id_bits) | item) & ((1 << id_bits) - 1)
    real = slot_item < n_assign
    src = jnp.where(real, slot_item % n_tok, 0).astype(jnp.int32)
    dst = slot_item.astype(jnp.int32)
    t_ids = jnp.arange(t_max, dtype=jnp.int32)
    te = jnp.sum((t_ids[:, None] >= tile_end[None, :]).astype(jnp.int32), axis=1)
    last_used = jnp.max(jnp.where(tiles_e > 0, experts, 0))
    te = jnp.where(t_ids < n_used, te, last_used).astype(jnp.int32)
    return {"tile_expert": te, "n_used": n_used.reshape(1).astype(jnp.int32), "src_token": src, "dst_row": dst}


def _combine_kernel(alpha, split_ctx_tiles, e1_ref, e2_ref, rt_ref, x1_ref, mod_ref, g_ref, b_ref, *o_refs):
    y = rt_ref[:, 2:3] * _tiles_to_rows(e1_ref[...]) + rt_ref[:, 3:4] * _tiles_to_rows(e2_ref[...])
    t = alpha * x1_ref[...] + mod_ref[5:6, :] * y
    out = _layer_norm_rows(t, g_ref[...], b_ref[...])
    if split_ctx_tiles is None:
        o_refs[0][...] = out
    else:
        _write_stream(o_refs[0], o_refs[1], split_ctx_tiles, out)


def _combine_call(eo, routing, x1, mods, ln_g, ln_b, alpha, geom, split_out):
    n = x1.shape[0]
    tm = COMBINE_TILE
    row = lambda i: (i, 0)
    const = lambda i: (0, 0)
    if split_out:
        out_specs = _stream_specs(D_MODEL, geom.ctx_tiles(tm), tm)
        out_shape = [jax.ShapeDtypeStruct((geom.n_ctx, D_MODEL), F32),
                     jax.ShapeDtypeStruct((n - geom.n_ctx, D_MODEL), F32)]
    else:
        out_specs = pl.BlockSpec((tm, D_MODEL), row)
        out_shape = jax.ShapeDtypeStruct((n, D_MODEL), F32)
    return pl.pallas_call(
        functools.partial(_combine_kernel, alpha, geom.ctx_tiles(tm) if split_out else None),
        grid=(n // tm,),
        in_specs=[
            pl.BlockSpec((tm,) + eo.shape[1:], lambda i: (i, 0, 0)),
            pl.BlockSpec((tm,) + eo.shape[1:], lambda i: (i + n // tm, 0, 0)),
            pl.BlockSpec((tm, LANES), row),
            pl.BlockSpec((tm, D_MODEL), row),
            pl.BlockSpec((None, N_MODS, D_MODEL), lambda i: (geom.mod_index(i, tm), 0, 0)),
            pl.BlockSpec((1, D_MODEL), const),
            pl.BlockSpec((1, D_MODEL), const),
        ],
        out_specs=out_specs,
        out_shape=out_shape,
        compiler_params=_params(("arbitrary",)),
        name="expert_combine",
    )(eo, eo, routing, x1, mods, ln_g, ln_b)


class _Geometry:
    def __init__(self, n_ctx, dec_batch, dec_seq):
        self.n_ctx = n_ctx
        self.dec_batch = dec_batch
        self.dec_seq = dec_seq
        self.n_tok = n_ctx + dec_batch * dec_seq

    def ctx_tiles(self, tm):
        assert self.n_ctx % tm == 0 and self.dec_seq % tm == 0
        return self.n_ctx // tm

    def mod_index(self, i, tm):
        ct = self.ctx_tiles(tm)
        return jnp.where(i < ct, 0, 1 + (i - ct) // (self.dec_seq // tm))

    def rope_index(self, i, tm):
        ct = self.ctx_tiles(tm)
        return jnp.where(i < ct, 0, 1 + (i - ct) % (self.dec_seq // tm))


def _rope_tables(dec_seq):
    pos = jnp.arange(dec_seq, dtype=jnp.int32)
    row = (pos // GRID_W).astype(F32)
    col = (pos % GRID_W).astype(F32)
    inv = ROPE_BASE ** (-jnp.arange(ROPE_PAIRS, dtype=F32) / ROPE_PAIRS)
    ang_r = row[:, None] * inv[None, :]
    ang_c = col[:, None] * inv[None, :]
    cos_h = jnp.concatenate([jnp.cos(ang_r), jnp.cos(ang_r), jnp.cos(ang_c), jnp.cos(ang_c)], axis=-1)
    sin_h = jnp.concatenate([-jnp.sin(ang_r), jnp.sin(ang_r), -jnp.sin(ang_c), jnp.sin(ang_c)], axis=-1)
    reps = LANES // HEAD_DIM
    cos_t = jnp.concatenate([jnp.ones((IN_TILE, LANES), F32), jnp.tile(cos_h, (1, reps))], axis=0)
    sin_t = jnp.concatenate([jnp.zeros((IN_TILE, LANES), F32), jnp.tile(sin_h, (1, reps))], axis=0)
    return cos_t, sin_t


def kernel(x_prompt, x_sample, cache_k, cache_v, c, c_ctx, w_ada, b_ada, w_in, w_o, attn_sink, w_s, b_s, sgu_ln_g, sgu_ln_b, ln1_g, ln1_b, ln2_g, ln2_b, w_ff_gate, w_ff_up, w_ff_down, w_router, w_exp_gate, w_exp_up, w_exp_down):
    batch, seq, d = x_prompt.shape
    dec_batch, dec_seq, _ = x_sample.shape
    depth = w_in.shape[0]
    past = cache_k.shape[2]
    assert d == D_MODEL and dec_batch + 1 <= MOD_ROWS
    n_ctx = batch * seq
    n_lat = dec_batch * dec_seq
    n_tok = n_ctx + n_lat
    geom = _Geometry(n_ctx, dec_batch, dec_seq)
    alpha = float((2 * depth) ** 0.25)

    x = (x_prompt.reshape(n_ctx, d), x_sample.reshape(n_lat, d))
    cvec = jnp.concatenate([c_ctx[None, :], c, jnp.zeros((MOD_ROWS - 1 - dec_batch, d), F32)], axis=0)
    mods_all = _ada_call(cvec, w_ada, b_ada).reshape(depth, MOD_ROWS, N_MODS, d)
    cos_t, sin_t = _rope_tables(dec_seq)

    new_k, new_v = [], []
    for l in range(depth):
        mods = mods_all[l]
        q, k, v, kv32, u, g = _in_call(x, mods, w_in[l].astype(BF16), cos_t, sin_t,
                                       sgu_ln_g[l].reshape(1, SGU_WIDTH), sgu_ln_b[l].reshape(1, SGU_WIDTH), geom)
        new_k.append(kv32[:n_ctx, :KV_WIDTH].reshape(batch, seq, N_KV_HEADS, HEAD_DIM))
        new_v.append(kv32[:n_ctx, KV_WIDTH:].reshape(batch, seq, N_KV_HEADS, HEAD_DIM))
        sink = attn_sink[l]
        a_ctx = _ctx_attn_call(sink, q, k, v, batch, seq)
        a_lat = _lat_attn_call(sink, q, k, v, cache_k[:, l].reshape(dec_batch, past, KV_WIDTH).astype(BF16),
                               cache_v[:, l].reshape(dec_batch, past, KV_WIDTH).astype(BF16), geom)
        bias_full = jnp.repeat(b_s[l].T, SGU_GROUP_DIM, axis=1)
        i = l // 2
        moe = l % 2 == 1
        w_r = None
        if moe:
            w_r = jnp.pad(w_router[i], ((0, 0), (0, LANES - N_EXPERTS)))
        outs = _out_call(a_ctx, a_lat, u, g, w_s[l].astype(BF16), bias_full, x, mods, w_o[l].astype(BF16), ln1_g[l].reshape(1, d), ln1_b[l].reshape(1, d),
                         w_r, alpha, geom)
        ln_g, ln_b = ln2_g[l].reshape(1, d), ln2_b[l].reshape(1, d)
        if moe:
            x1, h, routing = outs
            plan = _route_plan(routing, n_tok, FFN_TILE)
            eo = _moe_call(h, w_exp_gate[i], w_exp_up[i], w_exp_down[i], plan)
            x = _combine_call(eo, routing, x1, mods, ln_g, ln_b, alpha, geom, split_out=l == depth - 1)
        else:
            x1, h = outs
            x = _ffn_call(h, x1, mods, w_ff_gate[i].astype(BF16), w_ff_up[i].astype(BF16),
                          w_ff_down[i].astype(BF16), ln_g, ln_b, alpha, geom)

    if not isinstance(x, (tuple, list)):
        x = (x[:n_ctx], x[n_ctx:])
    y_prompt = x[0].reshape(batch, seq, d)
    y_sample = x[1].reshape(dec_batch, dec_seq, d)
    return (y_prompt, y_sample, jnp.stack(new_k, axis=1), jnp.stack(new_v, axis=1))
```

```python
import functools

import jax
import jax.numpy as jnp
import numpy as np
from jax import lax
from jax.experimental import pallas as pl
from jax.experimental.pallas import tpu as pltpu

F32 = jnp.float32
BF16 = jnp.bfloat16

D_MODEL = 1024
HEAD_DIM = 64
N_Q_HEADS = 8
N_KV_HEADS = 2
Q_PER_KV = N_Q_HEADS // N_KV_HEADS
ATTN_WIDTH = N_Q_HEADS * HEAD_DIM
KV_WIDTH = N_KV_HEADS * HEAD_DIM
SGU_WIDTH = D_MODEL - ATTN_WIDTH
N_SGU_GROUPS = 8
SGU_GROUP_DIM = SGU_WIDTH // N_SGU_GROUPS
CHUNK = 128
BLOCK = 128
WINDOW = 128
GRID_W = 64
IN_WIDTH = ATTN_WIDTH + 2 * KV_WIDTH + 2 * SGU_WIDTH
OFF_K = ATTN_WIDTH
OFF_V = OFF_K + KV_WIDTH
OFF_U = OFF_V + KV_WIDTH
OFF_G = OFF_U + SGU_WIDTH
N_EXPERTS = 8
TOP_K = 2
ROPE_BASE = 10000.0
ROPE_PAIRS = HEAD_DIM // 4
LN_EPS = 1e-5
ATTN_SCALE = HEAD_DIM ** -0.5
NEG_INF = -1e30
N_MODS = 6

LANES = 128
V7X_MXU_WIDTH = 256
V7X_VMEM_LIMIT_BYTES = 56 * 1024 * 1024

IN_TILE = 1024
OUT_TILE = 512
ROUTER_TILE = 256
COMBINE_TILE = 512
DENSE_TILE = 512
FFN_TILE = 512
LAT_QUERY_ROWS = 512
CTX_SEQS_PER_STEP = 1
MOD_ROWS = 8
ADA_COLS = 1536


def _params(sem):
    return pltpu.CompilerParams(dimension_semantics=sem, vmem_limit_bytes=V7X_VMEM_LIMIT_BYTES)


def _gelu_tanh(x):
    return x * (0.5 * (1.0 + jnp.tanh(np.float32(np.sqrt(2.0 / np.pi)) * (x + 0.044715 * (x * x * x)))))


def _sigmoid(x):
    return 1.0 / (1.0 + jnp.exp(-x))


def _layer_norm_rows(t, g, b):
    mu = jnp.mean(t, axis=-1, keepdims=True)
    tc = t - mu
    var = jnp.mean(tc * tc, axis=-1, keepdims=True)
    return tc * lax.rsqrt(var + LN_EPS) * g + b


def _rows_to_tiles(x):
    pieces = jnp.stack([x[:, s * LANES:(s + 1) * LANES] for s in range(x.shape[1] // LANES)], axis=0)
    return jnp.swapaxes(pieces, 0, 1)


def _tiles_to_rows(t):
    pieces = jnp.swapaxes(t, 0, 1)
    return jnp.concatenate([pieces[s] for s in range(t.shape[1])], axis=1)


def _stream_specs(width, ctx_tiles, tm):
    return [pl.BlockSpec((tm, width), lambda i, *_: (jnp.minimum(i, ctx_tiles - 1), 0)),
            pl.BlockSpec((tm, width), lambda i, *_: (jnp.maximum(i - ctx_tiles, 0), 0))]


def _read_stream(ctx_ref, lat_ref, ctx_tiles):
    tile = lax.broadcasted_iota(jnp.int32, ctx_ref.shape, 0) * 0 + pl.program_id(0)
    return jnp.where(tile < ctx_tiles, ctx_ref[...], lat_ref[...])


def _write_stream(ctx_ref, lat_ref, ctx_tiles, value):
    @pl.when(pl.program_id(0) < ctx_tiles)
    def _():
        ctx_ref[...] = value

    @pl.when(pl.program_id(0) >= ctx_tiles)
    def _():
        lat_ref[...] = value


def _ada_kernel(c_ref, w_ref, b_ref, o_ref):
    c = c_ref[...]
    s = (c * _sigmoid(c)).astype(BF16)
    o_ref[...] = jnp.dot(s, w_ref[...].astype(BF16), preferred_element_type=F32) + b_ref[...]


def _ada_call(cvec, w_ada, b_ada):
    depth = w_ada.shape[0]
    n_out = w_ada.shape[2]
    return pl.pallas_call(
        _ada_kernel,
        grid=(depth, n_out // ADA_COLS),
        in_specs=[
            pl.BlockSpec((MOD_ROWS, D_MODEL), lambda l, j: (0, 0)),
            pl.BlockSpec((None, D_MODEL, ADA_COLS), lambda l, j: (l, 0, j)),
            pl.BlockSpec((None, 1, ADA_COLS), lambda l, j: (l, 0, j)),
        ],
        out_specs=pl.BlockSpec((None, MOD_ROWS, ADA_COLS), lambda l, j: (l, 0, j)),
        out_shape=jax.ShapeDtypeStruct((depth, MOD_ROWS, n_out), F32),
        compiler_params=_params(("arbitrary", "arbitrary")),
        name="adaln",
    )(cvec, w_ada, b_ada.reshape(depth, 1, n_out))


def _in_kernel(split, ctx_tiles, *refs):
    if split:
        x = _read_stream(refs[0], refs[1], ctx_tiles)
        refs = refs[2:]
    else:
        x = refs[0][...]
        refs = refs[1:]
    mod_ref, w_ref, cos_ref, sin_ref, lng_ref, lnb_ref, q_ref, k_ref, v_ref, kv_ref, u_ref, g_ref = refs
    tm = x.shape[0]
    h = x * (1.0 + mod_ref[1:2, :]) + mod_ref[0:1, :]
    hb = h.astype(BF16)

    def z_block(start):
        return jnp.dot(hb, w_ref[:, start:start + V7X_MXU_WIDTH], preferred_element_type=F32)

    cos = cos_ref[...]
    sin = sin_ref[...]
    lane = lax.broadcasted_iota(jnp.int32, (tm, LANES), 1)
    first_of_pair = (lane & (2 * ROPE_PAIRS - 1)) < ROPE_PAIRS
    lower_group = lane < SGU_GROUP_DIM

    def rope(t):
        partner = jnp.where(first_of_pair, pltpu.roll(t, LANES - ROPE_PAIRS, 1), pltpu.roll(t, ROPE_PAIRS, 1))
        return t * cos + partner * sin

    halves = (slice(0, LANES), slice(LANES, 2 * LANES))
    inv_n = 1.0 / SGU_GROUP_DIM
    for j in range(SGU_WIDTH // LANES):
        cols = slice(j * LANES, (j + 1) * LANES)
        if j % 2 == 0:
            z = z_block(OFF_G + j * LANES)
        t = _gelu_tanh(z[:, halves[j % 2]])
        s_lo = jnp.sum(jnp.where(lower_group, t, 0.0), axis=-1, keepdims=True)
        s_hi = jnp.sum(jnp.where(lower_group, 0.0, t), axis=-1, keepdims=True)
        tc = t - jnp.where(lower_group, s_lo, s_hi) * inv_n
        sq = tc * tc
        v_lo = jnp.sum(jnp.where(lower_group, sq, 0.0), axis=-1, keepdims=True)
        v_hi = jnp.sum(jnp.where(lower_group, 0.0, sq), axis=-1, keepdims=True)
        var = jnp.where(lower_group, v_lo, v_hi) * inv_n
        g_ref[:, cols] = (tc * lax.rsqrt(var + LN_EPS) * lng_ref[:, cols] + lnb_ref[:, cols]).astype(BF16)

    for start in range(0, ATTN_WIDTH, V7X_MXU_WIDTH):
        z = z_block(start)
        for half in halves:
            q_ref[:, start + half.start:start + half.stop] = (rope(z[:, half]) * ATTN_SCALE).astype(BF16)
    z = z_block(OFF_K)
    k_ref[...] = rope(z[:, halves[0]]).astype(BF16)
    v_ref[...] = z[:, halves[1]].astype(BF16)
    kv_ref[...] = z
    for start in range(0, SGU_WIDTH, V7X_MXU_WIDTH):
        u_ref[:, start:start + V7X_MXU_WIDTH] = _gelu_tanh(z_block(OFF_U + start))


def _in_call(x, mods, w_in, cos_t, sin_t, ln_g, ln_b, geom):
    n = geom.n_tok
    tm = IN_TILE
    mod_idx = lambda i: geom.mod_index(i, tm)
    rope_idx = lambda i: geom.rope_index(i, tm)
    row = lambda i: (i, 0)
    split = isinstance(x, tuple)
    x_args = list(x) if split else [x]
    ctx_tiles = geom.ctx_tiles(tm)
    x_specs = _stream_specs(D_MODEL, ctx_tiles, tm) if split else [pl.BlockSpec((tm, D_MODEL), row)]
    outs = pl.pallas_call(
        functools.partial(_in_kernel, split, ctx_tiles),
        grid=(n // tm,),
        in_specs=x_specs + [
            pl.BlockSpec((None, N_MODS, D_MODEL), lambda i: (mod_idx(i), 0, 0)),
            pl.BlockSpec((D_MODEL, IN_WIDTH), lambda i: (0, 0)),
            pl.BlockSpec((tm, LANES), lambda i: (rope_idx(i), 0)),
            pl.BlockSpec((tm, LANES), lambda i: (rope_idx(i), 0)),
            pl.BlockSpec((1, SGU_WIDTH), lambda i: (0, 0)),
            pl.BlockSpec((1, SGU_WIDTH), lambda i: (0, 0)),
        ],
        out_specs=[
            pl.BlockSpec((tm, ATTN_WIDTH), row),
            pl.BlockSpec((tm, KV_WIDTH), row),
            pl.BlockSpec((tm, KV_WIDTH), row),
            pl.BlockSpec((tm, 2 * KV_WIDTH), row),
            pl.BlockSpec((tm, SGU_WIDTH), row),
            pl.BlockSpec((tm, SGU_WIDTH), row),
        ],
        out_shape=[
            jax.ShapeDtypeStruct((n, ATTN_WIDTH), BF16),
            jax.ShapeDtypeStruct((n, KV_WIDTH), BF16),
            jax.ShapeDtypeStruct((n, KV_WIDTH), BF16),
            jax.ShapeDtypeStruct((n, 2 * KV_WIDTH), F32),
            jax.ShapeDtypeStruct((n, SGU_WIDTH), F32),
            jax.ShapeDtypeStruct((n, SGU_WIDTH), BF16),
        ],
        compiler_params=_params(("arbitrary",)),
        name="in_proj",
    )(*x_args, mods, w_in, cos_t, sin_t, ln_g, ln_b)
    return outs


def _group_attention(q_ref, rows, hk, k_all, v_all, block_masks, sink_ref):
    m_rows = rows.stop - rows.start
    heads = [hk * Q_PER_KV + gq for gq in range(Q_PER_KV)]
    q = jnp.concatenate([q_ref[rows, h * HEAD_DIM:(h + 1) * HEAD_DIM] for h in heads], axis=0)
    s = lax.dot_general(q, k_all, (((1,), (1,)), ((), ())), preferred_element_type=F32)
    n_blocks = k_all.shape[0] // LANES
    blocks = [s[:, b * LANES:(b + 1) * LANES] for b in range(n_blocks)]
    for b, mask in block_masks.items():
        blocks[b] = jnp.where(mask, blocks[b], NEG_INF)
    head_of_row = lax.broadcasted_iota(jnp.int32, (Q_PER_KV * m_rows, 1), 0) // m_rows
    sink = jnp.zeros((Q_PER_KV * m_rows, 1), F32)
    for gq, h in enumerate(heads):
        sink = jnp.where(head_of_row == gq, sink_ref[h], sink)
    m_el = blocks[0]
    for blk in blocks[1:]:
        m_el = jnp.maximum(m_el, blk)
    m = jnp.maximum(jnp.max(m_el, axis=-1, keepdims=True), sink)
    probs = [jnp.exp(blk - m) for blk in blocks]
    l_el = probs[0]
    for p in probs[1:]:
        l_el = l_el + p
    denom = jnp.sum(l_el, axis=-1, keepdims=True) + jnp.exp(sink - m)
    p_all = jnp.concatenate([p.astype(BF16) for p in probs], axis=1)
    o = jnp.dot(p_all, v_all, preferred_element_type=F32) / denom
    return {h: o[gq * m_rows:(gq + 1) * m_rows] for gq, h in enumerate(heads)}


def _store_heads(o_ref, rows, outs):
    for h0 in range(0, N_Q_HEADS, 2):
        pair = jnp.concatenate([outs[h0], outs[h0 + 1]], axis=1)
        o_ref[rows, h0 * HEAD_DIM:(h0 + 2) * HEAD_DIM] = pair.astype(o_ref.dtype)


def _ctx_attn_kernel(seq_len, sink_ref, q_ref, k_ref, v_ref, o_ref):
    for sq in range(q_ref.shape[0] // seq_len):
        rows = slice(sq * seq_len, (sq + 1) * seq_len)
        outs = {}
        for hk in range(N_KV_HEADS):
            kv_cols = slice(hk * HEAD_DIM, (hk + 1) * HEAD_DIM)
            outs.update(_group_attention(q_ref, rows, hk, k_ref[rows, kv_cols], v_ref[rows, kv_cols], {}, sink_ref))
        _store_heads(o_ref, rows, outs)


def _ctx_attn_call(sink, q, k, v, n_seq, seq_len):
    n = n_seq * seq_len
    per_step = CTX_SEQS_PER_STEP if n_seq % CTX_SEQS_PER_STEP == 0 else 1
    rows = per_step * seq_len
    blk = lambda b: (b, 0)
    return pl.pallas_call(
        functools.partial(_ctx_attn_kernel, seq_len),
        grid=(n_seq // per_step,),
        in_specs=[
            pl.BlockSpec(memory_space=pltpu.SMEM),
            pl.BlockSpec((rows, ATTN_WIDTH), blk),
            pl.BlockSpec((rows, KV_WIDTH), blk),
            pl.BlockSpec((rows, KV_WIDTH), blk),
        ],
        out_specs=pl.BlockSpec((rows, ATTN_WIDTH), blk),
        out_shape=jax.ShapeDtypeStruct((n, ATTN_WIDTH), BF16),
        compiler_params=_params(("arbitrary",)),
        name="ctx_attention",
    )(sink, q, k, v)


def _lat_attn_kernel(sink_ref, q_ref, k_ref, v_ref, ck_ref, cv_ref, o_ref):
    blocks_per_step = q_ref.shape[0] // BLOCK
    nb = k_ref.shape[0] // BLOCK
    past_blocks = ck_ref.shape[0] // LANES
    r = lax.broadcasted_iota(jnp.int32, (Q_PER_KV * BLOCK, BLOCK), 0) & (BLOCK - 1)
    c = lax.broadcasted_iota(jnp.int32, (Q_PER_KV * BLOCK, BLOCK), 1)
    for sub in range(blocks_per_step):
        j = pl.program_id(1) * blocks_per_step + sub
        rows = slice(sub * BLOCK, (sub + 1) * BLOCK)
        mask_prev = c >= r + jnp.where(j > 0, 0, BLOCK)
        mask_next = c <= r - jnp.where(j < nb - 1, 0, BLOCK)
        prev = pl.ds(pl.multiple_of(jnp.maximum(j - 1, 0) * BLOCK, BLOCK), BLOCK)
        cur = pl.ds(pl.multiple_of(j * BLOCK, BLOCK), BLOCK)
        nxt = pl.ds(pl.multiple_of(jnp.minimum(j + 1, nb - 1) * BLOCK, BLOCK), BLOCK)
        outs = {}
        for hk in range(N_KV_HEADS):
            kv_cols = slice(hk * HEAD_DIM, (hk + 1) * HEAD_DIM)
            k_all = jnp.concatenate([ck_ref[:, kv_cols], k_ref[prev, kv_cols], k_ref[cur, kv_cols],
                                     k_ref[nxt, kv_cols]], axis=0)
            v_all = jnp.concatenate([cv_ref[:, kv_cols], v_ref[prev, kv_cols], v_ref[cur, kv_cols],
                                     v_ref[nxt, kv_cols]], axis=0)
            masks = {past_blocks: mask_prev, past_blocks + 2: mask_next}
            outs.update(_group_attention(q_ref, rows, hk, k_all, v_all, masks, sink_ref))
        _store_heads(o_ref, rows, outs)


def _lat_attn_call(sink, q, k, v, cache_k, cache_v, geom):
    qb = LAT_QUERY_ROWS
    steps = geom.dec_seq // qb
    assert geom.n_ctx % geom.dec_seq == 0 and cache_k.shape[1] % LANES == 0
    seq_base = geom.n_ctx // geom.dec_seq
    past = cache_k.shape[1]
    seq_spec = pl.BlockSpec((geom.dec_seq, KV_WIDTH), lambda b, j: (seq_base + b, 0))
    cache_spec = pl.BlockSpec((None, past, KV_WIDTH), lambda b, j: (b, 0, 0))
    return pl.pallas_call(
        _lat_attn_kernel,
        grid=(geom.dec_batch, steps),
        in_specs=[
            pl.BlockSpec(memory_space=pltpu.SMEM),
            pl.BlockSpec((qb, ATTN_WIDTH), lambda b, j: (geom.n_ctx // qb + b * steps + j, 0)),
            seq_spec, seq_spec,
            cache_spec, cache_spec,
        ],
        out_specs=pl.BlockSpec((qb, ATTN_WIDTH), lambda b, j: (b * steps + j, 0)),
        out_shape=jax.ShapeDtypeStruct((geom.dec_batch * geom.dec_seq, ATTN_WIDTH), BF16),
        compiler_params=_params(("arbitrary", "arbitrary")),
        name="lat_attention",
    )(sink, q, k, v, cache_k, cache_v)


def _spatial_gating(u_ref, g_ref, mix_ref, bias_ref, s_ref):
    tm = u_ref.shape[0]
    lane = lax.broadcasted_iota(jnp.int32, (CHUNK, LANES), 1)
    lower_group = lane < SGU_GROUP_DIM
    for ch in range(tm // CHUNK):
        rows = slice(ch * CHUNK, (ch + 1) * CHUNK)
        for p in range(SGU_WIDTH // LANES):
            cols = slice(p * LANES, (p + 1) * LANES)
            g = g_ref[rows, cols]
            zero = jnp.zeros_like(g)
            mixed = (jnp.dot(mix_ref[2 * p], jnp.where(lower_group, g, zero), preferred_element_type=F32)
                     + jnp.dot(mix_ref[2 * p + 1], jnp.where(lower_group, zero, g), preferred_element_type=F32))
            s_ref[rows, cols] = (u_ref[rows, cols] * (mixed + bias_ref[:, cols])).astype(s_ref.dtype)


def _out_kernel(alpha, with_router, split_x, ctx_tiles, a_ctx_ref, a_lat_ref, u_ref, gg_ref, mix_ref, bias_ref,
                *refs):
    s_ref = refs[-1]
    refs = refs[:-1]
    if split_x:
        x = _read_stream(refs[0], refs[1], ctx_tiles)
        refs = refs[2:]
    else:
        x = refs[0][...]
        refs = refs[1:]
    mod_ref, wa_ref, ws_ref, g_ref, b_ref = refs[:5]
    if with_router:
        wr_ref, x1_ref, h_ref, rt_ref = refs[5:]
    else:
        x1_ref, h_ref = refs[5:]
    a = _read_stream(a_ctx_ref, a_lat_ref, ctx_tiles)
    _spatial_gating(u_ref, gg_ref, mix_ref, bias_ref, s_ref)
    y = (jnp.dot(a, wa_ref[...], preferred_element_type=F32)
         + jnp.dot(s_ref[...], ws_ref[...], preferred_element_type=F32))
    t = alpha * x + mod_ref[2:3, :] * y
    x1 = _layer_norm_rows(t, g_ref[...], b_ref[...])
    x1_ref[...] = x1
    h = x1 * (1.0 + mod_ref[4:5, :]) + mod_ref[3:4, :]
    if with_router:
        h_ref[...] = _rows_to_tiles(h)
    else:
        h_ref[...] = h.astype(h_ref.dtype)
    if with_router:
        tm = h.shape[0]
        wr = wr_ref[...]
        wr_hi = wr.astype(BF16)
        wr_lo = (wr - wr_hi.astype(F32)).astype(BF16)
        h_hi = h.astype(BF16)
        h_lo = (h - h_hi.astype(F32)).astype(BF16)
        logits = (jnp.dot(h_hi, wr_hi, preferred_element_type=F32)
                  + jnp.dot(h_lo, wr_hi, preferred_element_type=F32)
                  + jnp.dot(h_hi, wr_lo, preferred_element_type=F32))
        lane = lax.broadcasted_iota(jnp.int32, (tm, LANES), 1).astype(F32)
        neg = jnp.float32(-jnp.inf)
        lg = jnp.where(lane < N_EXPERTS, logits, neg)
        m1 = jnp.max(lg, axis=-1, keepdims=True)
        i1 = jnp.min(jnp.where(lg == m1, lane, float(LANES)), axis=-1, keepdims=True)
        lg2 = jnp.where(lane == i1, neg, lg)
        m2 = jnp.max(lg2, axis=-1, keepdims=True)
        i2 = jnp.min(jnp.where(lg2 == m2, lane, float(LANES)), axis=-1, keepdims=True)
        e2 = jnp.exp(m2 - m1)
        g1 = 1.0 / (1.0 + e2)
        g2 = e2 / (1.0 + e2)
        rt = jnp.where(lane == 0, i1, jnp.where(lane == 1, i2, jnp.where(lane == 2, g1, jnp.where(lane == 3, g2, 0.0))))
        rt_ref[...] = rt


def _out_call(a_ctx, a_lat, u, gg, w_s, bias_full, x, mods, w_o, ln_g, ln_b, w_router, alpha, geom):
    n = geom.n_tok
    with_router = w_router is not None
    tm = ROUTER_TILE if with_router else OUT_TILE
    ctx_tiles = geom.ctx_tiles(tm)
    row = lambda i: (i, 0)
    const = lambda i: (0, 0)
    split_x = isinstance(x, tuple)
    x_args = list(x) if split_x else [x]
    x_specs = _stream_specs(D_MODEL, ctx_tiles, tm) if split_x else [pl.BlockSpec((tm, D_MODEL), row)]
    sgu_specs = [
        pl.BlockSpec((tm, SGU_WIDTH), row),
        pl.BlockSpec((tm, SGU_WIDTH), row),
        pl.BlockSpec((N_SGU_GROUPS, CHUNK, CHUNK), lambda i: (0, 0, 0)),
        pl.BlockSpec((CHUNK, SGU_WIDTH), const),
    ]
    in_specs = _stream_specs(ATTN_WIDTH, ctx_tiles, tm) + sgu_specs + x_specs + [
        pl.BlockSpec((None, N_MODS, D_MODEL), lambda i: (geom.mod_index(i, tm), 0, 0)),
        pl.BlockSpec((ATTN_WIDTH, D_MODEL), const),
        pl.BlockSpec((SGU_WIDTH, D_MODEL), lambda i: (1, 0)),
        pl.BlockSpec((1, D_MODEL), const),
        pl.BlockSpec((1, D_MODEL), const),
    ]
    args = [a_ctx, a_lat, u, gg, w_s, bias_full] + x_args + [mods, w_o, w_o, ln_g, ln_b]
    if with_router:
        h_spec = pl.BlockSpec((tm, D_MODEL // LANES, LANES), lambda i: (i, 0, 0))
        h_shape = jax.ShapeDtypeStruct((n, D_MODEL // LANES, LANES), F32)
    else:
        h_spec = pl.BlockSpec((tm, D_MODEL), row)
        h_shape = jax.ShapeDtypeStruct((n, D_MODEL), BF16)
    out_specs = [pl.BlockSpec((tm, D_MODEL), row), h_spec]
    out_shape = [jax.ShapeDtypeStruct((n, D_MODEL), F32), h_shape]
    if with_router:
        in_specs.append(pl.BlockSpec((D_MODEL, LANES), const))
        args.append(w_router)
        out_specs.append(pl.BlockSpec((tm, LANES), row))
        out_shape.append(jax.ShapeDtypeStruct((n, LANES), F32))
    return pl.pallas_call(
        functools.partial(_out_kernel, alpha, with_router, split_x, ctx_tiles),
        grid=(n // tm,),
        in_specs=in_specs,
        out_specs=out_specs,
        out_shape=out_shape,
        scratch_shapes=[pltpu.VMEM((tm, SGU_WIDTH), BF16)],
        compiler_params=_params(("arbitrary",)),
        name="out_proj_router" if with_router else "out_proj",
    )(*args)


def _swiglu_blocks(width):
    return [slice(s, min(s + V7X_MXU_WIDTH, width)) for s in range(0, width, V7X_MXU_WIDTH)]


def _swiglu_partial(x, wg_ref, wu_ref, wd_ref, side_work=None):
    out = None
    for n, cols in enumerate(_swiglu_blocks(wg_ref.shape[1])):
        a = jnp.dot(x, wg_ref[:, cols].astype(BF16), preferred_element_type=F32)
        if side_work is not None:
            side_work(3 * n)
        b = jnp.dot(x, wu_ref[:, cols].astype(BF16), preferred_element_type=F32)
        if side_work is not None:
            side_work(3 * n + 1)
        mid = ((a * _sigmoid(a)) * b).astype(BF16)
        part = jnp.dot(mid, wd_ref[cols, :].astype(BF16), preferred_element_type=F32)
        if side_work is not None:
            side_work(3 * n + 2)
        out = part if out is None else out + part
    return out


def _ffn_kernel(alpha, h_ref, x1_ref, mod_ref, wg_ref, wu_ref, wd_ref, g_ref, b_ref, o_ref):
    y = _swiglu_partial(h_ref[...], wg_ref, wu_ref, wd_ref)
    t = alpha * x1_ref[...] + mod_ref[5:6, :] * y
    o_ref[...] = _layer_norm_rows(t, g_ref[...], b_ref[...])


def _ffn_chunk(d_ff):
    assert d_ff % (2 * LANES) == 0
    return d_ff // 2


def _ffn_call(h, x1, mods, wg, wu, wd, ln_g, ln_b, alpha, geom):
    n = h.shape[0]
    tm = DENSE_TILE
    d_ff = wg.shape[1]
    row = lambda i: (i, 0)
    const = lambda i: (0, 0)
    resident = pl.Buffered(1)
    return pl.pallas_call(
        functools.partial(_ffn_kernel, alpha),
        grid=(n // tm,),
        in_specs=[
            pl.BlockSpec((tm, D_MODEL), row),
            pl.BlockSpec((tm, D_MODEL), row),
            pl.BlockSpec((None, N_MODS, D_MODEL), lambda i: (geom.mod_index(i, tm), 0, 0)),
            pl.BlockSpec((D_MODEL, d_ff), const, pipeline_mode=resident),
            pl.BlockSpec((D_MODEL, d_ff), const, pipeline_mode=resident),
            pl.BlockSpec((d_ff, D_MODEL), const, pipeline_mode=resident),
            pl.BlockSpec((1, D_MODEL), const),
            pl.BlockSpec((1, D_MODEL), const),
        ],
        out_specs=pl.BlockSpec((tm, D_MODEL), row),
        out_shape=jax.ShapeDtypeStruct((n, D_MODEL), F32),
        compiler_params=_params(("arbitrary",)),
        name="dense_ffn",
    )(h, x1, mods, wg, wu, wd, ln_g, ln_b)


def _moe_kernel(n_chunks, te_ref, nu_ref, src_ref, dst_ref, h_hbm, wg_ref, wu_ref, wd_ref, o_hbm,
                xbuf, xb, obuf, gsem, ssem):
    del te_ref
    tr = xb.shape[0]
    t = pl.program_id(0)
    c = pl.program_id(1)
    n_tiles = pl.num_programs(0)
    valid = t < nu_ref[0]
    slot = t % 2
    other = 1 - slot
    share = tr // n_chunks
    first_row = c * share
    nxt_base = jnp.minimum(t + 1, n_tiles - 1) * tr
    prv_base = jnp.where(t == 0, n_tiles - 1, t - 1) * tr

    def gather_row(base, dst_slot, r, tok=None):
        tok = src_ref[base + r] if tok is None else tok
        return pltpu.make_async_copy(h_hbm.at[tok], xbuf.at[dst_slot, r], gsem.at[dst_slot])

    def scatter_row(base, src_slot, r, dst=None):
        dst = dst_ref[base + r] if dst is None else dst
        return pltpu.make_async_copy(obuf.at[src_slot, r], o_hbm.at[dst], ssem.at[src_slot])

    def wait_gather(dst_slot):
        pltpu.make_async_copy(h_hbm.at[pl.ds(0, tr)], xbuf.at[dst_slot], gsem.at[dst_slot]).wait()

    def wait_scatter(src_slot):
        pltpu.make_async_copy(obuf.at[src_slot], o_hbm.at[pl.ds(0, tr)], ssem.at[src_slot]).wait()

    def looped(n_rows, start_row):
        def body(r, carry):
            start_row(r)
            return carry
        lax.fori_loop(0, n_rows, body, 0, unroll=8)

    @pl.when((t == 0) & (c == 0))
    def _():
        obuf[1] = jnp.zeros(obuf.shape[1:], obuf.dtype)
        looped(tr, lambda r: gather_row(0, 0, r).start())

    n_used = nu_ref[0]

    @pl.when(c == 0)
    def _():
        @pl.when(t <= n_used)
        def _():
            wait_gather(slot)

        @pl.when(t >= 1)
        def _():
            wait_scatter(slot)

    def tile_scatter(base, src_slot):
        return pltpu.make_async_copy(obuf.at[src_slot], o_hbm.at[pl.ds(dst_ref[base], tr)], ssem.at[src_slot])

    def multiply_tile(slot_s, c_s):
        other_s = 1 - slot_s
        if c_s == 0:
            xb[...] = _tiles_to_rows(xbuf[slot_s]).astype(BF16)
        n_groups = 3 * len(_swiglu_blocks(wg_ref.shape[1]))

        def row_dmas(i):
            for r in range(c_s * share + share * i // n_groups, c_s * share + share * (i + 1) // n_groups):
                gather_row(nxt_base, other_s, r).start()
                scatter_row(prv_base, other_s, r).start()

        part = _swiglu_partial(xb[...], wg_ref, wu_ref, wd_ref, row_dmas)
        if c_s == 0:
            obuf[slot_s] = _rows_to_tiles(part)
        else:
            obuf[slot_s] += _rows_to_tiles(part)

    for slot_s in range(2):
        for c_s in range(n_chunks):
            pl.when(valid & (slot == slot_s) & (c == c_s))(functools.partial(multiply_tile, slot_s, c_s))

    @pl.when(jnp.logical_not(valid))
    def _():
        @pl.when(c == 0)
        def _():
            obuf[slot] = jnp.zeros(obuf.shape[1:], obuf.dtype)

        @pl.when(t == n_used)
        def _():
            looped(share, lambda r: scatter_row(prv_base, other, first_row + r).start())

        @pl.when((t > n_used) & (c == 0))
        def _():
            tile_scatter(prv_base, other).start()

    @pl.when((t == n_tiles - 1) & (c == n_chunks - 1))
    def _():
        wait_scatter(other)
        tile_scatter(t * tr, slot).start()
        wait_scatter(slot)


def _moe_call(h, wg, wu, wd, plan):
    tr = FFN_TILE
    d_ff = wg.shape[2]
    fc = _ffn_chunk(d_ff)
    n_chunks = d_ff // fc
    t_max = plan["tile_expert"].shape[0]
    row_tile = h.shape[1:]

    def chunk(t, c, nu):
        return jnp.where(t < nu[0], c, n_chunks - 1)

    grid_spec = pltpu.PrefetchScalarGridSpec(
        num_scalar_prefetch=4,
        grid=(t_max, n_chunks),
        in_specs=[
            pl.BlockSpec(memory_space=pl.ANY),
            pl.BlockSpec((None, D_MODEL, fc), lambda t, c, te, nu, src, dst: (te[t], 0, chunk(t, c, nu))),
            pl.BlockSpec((None, D_MODEL, fc), lambda t, c, te, nu, src, dst: (te[t], 0, chunk(t, c, nu))),
            pl.BlockSpec((None, fc, D_MODEL), lambda t, c, te, nu, src, dst: (te[t], chunk(t, c, nu), 0)),
        ],
        out_specs=pl.BlockSpec(memory_space=pl.ANY),
        scratch_shapes=[
            pltpu.VMEM((2, tr) + row_tile, F32),
            pltpu.VMEM((tr, D_MODEL), BF16),
            pltpu.VMEM((2, tr) + row_tile, F32),
            pltpu.SemaphoreType.DMA((2,)),
            pltpu.SemaphoreType.DMA((2,)),
        ],
    )
    return pl.pallas_call(
        functools.partial(_moe_kernel, n_chunks),
        grid_spec=grid_spec,
        out_shape=jax.ShapeDtypeStruct((t_max * tr,) + row_tile, F32),
        compiler_params=_params(("arbitrary", "arbitrary")),
        name="expert_ffn",
    )(plan["tile_expert"], plan["n_used"], plan["src_token"], plan["dst_row"], h, wg, wu, wd)


def _route_plan(routing, n_tok, tr):
    n_assign = TOP_K * n_tok
    t_max = n_assign // tr + N_EXPERTS
    n_slots = t_max * tr
    n_pad = n_slots - n_assign
    id_bits = (n_slots - 1).bit_length()
    experts = jnp.arange(N_EXPERTS, dtype=jnp.int32)
    e_flat = routing[:, :TOP_K].astype(jnp.int32).T.reshape(-1)
    counts = jnp.sum((e_flat[:, None] == experts[None, :]).astype(jnp.int32), axis=0)
    tiles_e = (counts + tr - 1) // tr
    tile_end = jnp.cumsum(tiles_e)
    n_used = tile_end[-1]
    pad_end = jnp.cumsum(tiles_e * tr - counts)
    pad_ids = jnp.arange(n_pad, dtype=jnp.int32)
    pad_expert = jnp.sum((pad_ids[:, None] >= pad_end[None, :]).astype(jnp.int32), axis=1)
    keys = jnp.concatenate([e_flat * 2, pad_expert * 2 + 1])
    item = jnp.arange(n_slots, dtype=jnp.int32)
    slot_item = jnp.sort((keys << id_bits) | item) & ((1 << id_bits) - 1)
    real = slot_item < n_assign
    src = jnp.where(real, slot_item % n_tok, 0).astype(jnp.int32)
    dst = slot_item.astype(jnp.int32)
    t_ids = jnp.arange(t_max, dtype=jnp.int32)
    te = jnp.sum((t_ids[:, None] >= tile_end[None, :]).astype(jnp.int32), axis=1)
    last_used = jnp.max(jnp.where(tiles_e > 0, experts, 0))
    te = jnp.where(t_ids < n_used, te, last_used).astype(jnp.int32)
    return {"tile_expert": te, "n_used": n_used.reshape(1).astype(jnp.int32), "src_token": src, "dst_row": dst}


def _combine_kernel(alpha, split_ctx_tiles, e1_ref, e2_ref, rt_ref, x1_ref, mod_ref, g_ref, b_ref, *o_refs):
    y = rt_ref[:, 2:3] * _tiles_to_rows(e1_ref[...]) + rt_ref[:, 3:4] * _tiles_to_rows(e2_ref[...])
    t = alpha * x1_ref[...] + mod_ref[5:6, :] * y
    out = _layer_norm_rows(t, g_ref[...], b_ref[...])
    if split_ctx_tiles is None:
        o_refs[0][...] = out
    else:
        _write_stream(o_refs[0], o_refs[1], split_ctx_tiles, out)


def _combine_call(eo, routing, x1, mods, ln_g, ln_b, alpha, geom, split_out):
    n = x1.shape[0]
    tm = COMBINE_TILE
    row = lambda i: (i, 0)
    const = lambda i: (0, 0)
    if split_out:
        out_specs = _stream_specs(D_MODEL, geom.ctx_tiles(tm), tm)
        out_shape = [jax.ShapeDtypeStruct((geom.n_ctx, D_MODEL), F32),
                     jax.ShapeDtypeStruct((n - geom.n_ctx, D_MODEL), F32)]
    else:
        out_specs = pl.BlockSpec((tm, D_MODEL), row)
        out_shape = jax.ShapeDtypeStruct((n, D_MODEL), F32)
    return pl.pallas_call(
        functools.partial(_combine_kernel, alpha, geom.ctx_tiles(tm) if split_out else None),
        grid=(n // tm,),
        in_specs=[
            pl.BlockSpec((tm,) + eo.shape[1:], lambda i: (i, 0, 0)),
            pl.BlockSpec((tm,) + eo.shape[1:], lambda i: (i + n // tm, 0, 0)),
            pl.BlockSpec((tm, LANES), row),
            pl.BlockSpec((tm, D_MODEL), row),
            pl.BlockSpec((None, N_MODS, D_MODEL), lambda i: (geom.mod_index(i, tm), 0, 0)),
            pl.BlockSpec((1, D_MODEL), const),
            pl.BlockSpec((1, D_MODEL), const),
        ],
        out_specs=out_specs,
        out_shape=out_shape,
        compiler_params=_params(("arbitrary",)),
        name="expert_combine",
    )(eo, eo, routing, x1, mods, ln_g, ln_b)


class _Geometry:
    def __init__(self, n_ctx, dec_batch, dec_seq):
        self.n_ctx = n_ctx
        self.dec_batch = dec_batch
        self.dec_seq = dec_seq
        self.n_tok = n_ctx + dec_batch * dec_seq

    def ctx_tiles(self, tm):
        assert self.n_ctx % tm == 0 and self.dec_seq % tm == 0
        return self.n_ctx // tm

    def mod_index(self, i, tm):
        ct = self.ctx_tiles(tm)
        return jnp.where(i < ct, 0, 1 + (i - ct) // (self.dec_seq // tm))

    def rope_index(self, i, tm):
        ct = self.ctx_tiles(tm)
        return jnp.where(i < ct, 0, 1 + (i - ct) % (self.dec_seq // tm))


def _rope_tables(dec_seq):
    pos = jnp.arange(dec_seq, dtype=jnp.int32)
    row = (pos // GRID_W).astype(F32)
    col = (pos % GRID_W).astype(F32)
    inv = ROPE_BASE ** (-jnp.arange(ROPE_PAIRS, dtype=F32) / ROPE_PAIRS)
    ang_r = row[:, None] * inv[None, :]
    ang_c = col[:, None] * inv[None, :]
    cos_h = jnp.concatenate([jnp.cos(ang_r), jnp.cos(ang_r), jnp.cos(ang_c), jnp.cos(ang_c)], axis=-1)
    sin_h = jnp.concatenate([-jnp.sin(ang_r), jnp.sin(ang_r), -jnp.sin(ang_c), jnp.sin(ang_c)], axis=-1)
    reps = LANES // HEAD_DIM
    cos_t = jnp.concatenate([jnp.ones((IN_TILE, LANES), F32), jnp.tile(cos_h, (1, reps))], axis=0)
    sin_t = jnp.concatenate([jnp.zeros((IN_TILE, LANES), F32), jnp.tile(sin_h, (1, reps))], axis=0)
    return cos_t, sin_t


def kernel(x_prompt, x_sample, cache_k, cache_v, c, c_ctx, w_ada, b_ada, w_in, w_o, attn_sink, w_s, b_s, sgu_ln_g, sgu_ln_b, ln1_g, ln1_b, ln2_g, ln2_b, w_ff_gate, w_ff_up, w_ff_down, w_router, w_exp_gate, w_exp_up, w_exp_down):
    batch, seq, d = x_prompt.shape
    dec_batch, dec_seq, _ = x_sample.shape
    depth = w_in.shape[0]
    past = cache_k.shape[2]
    assert d == D_MODEL and dec_batch + 1 <= MOD_ROWS
    n_ctx = batch * seq
    n_lat = dec_batch * dec_seq
    n_tok = n_ctx + n_lat
    geom = _Geometry(n_ctx, dec_batch, dec_seq)
    alpha = float((2 * depth) ** 0.25)

    x = (x_prompt.reshape(n_ctx, d), x_sample.reshape(n_lat, d))
    cvec = jnp.concatenate([c_ctx[None, :], c, jnp.zeros((MOD_ROWS - 1 - dec_batch, d), F32)], axis=0)
    mods_all = _ada_call(cvec, w_ada, b_ada).reshape(depth, MOD_ROWS, N_MODS, d)
    cos_t, sin_t = _rope_tables(dec_seq)

    new_k, new_v = [], []
    for l in range(depth):
        mods = mods_all[l]
        q, k, v, kv32, u, g = _in_call(x, mods, w_in[l].astype(BF16), cos_t, sin_t,
                                       sgu_ln_g[l].reshape(1, SGU_WIDTH), sgu_ln_b[l].reshape(1, SGU_WIDTH), geom)
        new_k.append(kv32[:n_ctx, :KV_WIDTH].reshape(batch, seq, N_KV_HEADS, HEAD_DIM))
        new_v.append(kv32[:n_ctx, KV_WIDTH:].reshape(batch, seq, N_KV_HEADS, HEAD_DIM))
        sink = attn_sink[l]
        a_ctx = _ctx_attn_call(sink, q, k, v, batch, seq)
        a_lat = _lat_attn_call(sink, q, k, v, cache_k[:, l].reshape(dec_batch, past, KV_WIDTH).astype(BF16),
                               cache_v[:, l].reshape(dec_batch, past, KV_WIDTH).astype(BF16), geom)
        bias_full = jnp.repeat(b_s[l].T, SGU_GROUP_DIM, axis=1)
        i = l // 2
        moe = l % 2 == 1
        w_r = None
        if moe:
            w_r = jnp.pad(w_router[i], ((0, 0), (0, LANES - N_EXPERTS)))
        outs = _out_call(a_ctx, a_lat, u, g, w_s[l].astype(BF16), bias_full, x, mods, w_o[l].astype(BF16), ln1_g[l].reshape(1, d), ln1_b[l].reshape(1, d),
                         w_r, alpha, geom)
        ln_g, ln_b = ln2_g[l].reshape(1, d), ln2_b[l].reshape(1, d)
        if moe:
            x1, h, routing = outs
            plan = _route_plan(routing, n_tok, FFN_TILE)
            eo = _moe_call(h, w_exp_gate[i], w_exp_up[i], w_exp_down[i], plan)
            x = _combine_call(eo, routing, x1, mods, ln_g, ln_b, alpha, geom, split_out=l == depth - 1)
        else:
            x1, h = outs
            x = _ffn_call(h, x1, mods, w_ff_gate[i], w_ff_up[i], w_ff_down[i], ln_g, ln_b, alpha, geom)

    if not isinstance(x, (tuple, list)):
        x = (x[:n_ctx], x[n_ctx:])
    y_prompt = x[0].reshape(batch, seq, d)
    y_sample = x[1].reshape(dec_batch, dec_seq, d)
    return (y_prompt, y_sample, jnp.stack(new_k, axis=1), jnp.stack(new_v, axis=1))
```

```python
import functools

import jax
import jax.numpy as jnp
import numpy as np
from jax import lax
from jax.experimental import pallas as pl
from jax.experimental.pallas import tpu as pltpu

F32 = jnp.float32
BF16 = jnp.bfloat16

D_MODEL = 1024
HEAD_DIM = 64
N_Q_HEADS = 8
N_KV_HEADS = 2
Q_PER_KV = N_Q_HEADS // N_KV_HEADS
ATTN_WIDTH = N_Q_HEADS * HEAD_DIM
KV_WIDTH = N_KV_HEADS * HEAD_DIM
SGU_WIDTH = D_MODEL - ATTN_WIDTH
N_SGU_GROUPS = 8
SGU_GROUP_DIM = SGU_WIDTH // N_SGU_GROUPS
CHUNK = 128
BLOCK = 128
WINDOW = 128
GRID_W = 64
IN_WIDTH = ATTN_WIDTH + 2 * KV_WIDTH + 2 * SGU_WIDTH
OFF_K = ATTN_WIDTH
OFF_V = OFF_K + KV_WIDTH
OFF_U = OFF_V + KV_WIDTH
OFF_G = OFF_U + SGU_WIDTH
N_EXPERTS = 8
TOP_K = 2
ROPE_BASE = 10000.0
ROPE_PAIRS = HEAD_DIM // 4
LN_EPS = 1e-5
ATTN_SCALE = HEAD_DIM ** -0.5
NEG_INF = -1e30
N_MODS = 6

LANES = 128
V7X_MXU_WIDTH = 256
V7X_VMEM_LIMIT_BYTES = 56 * 1024 * 1024

IN_TILE = 1024
OUT_TILE = 512
ROUTER_TILE = 256
COMBINE_TILE = 512
DENSE_TILE = 512
FFN_TILE = 512
LAT_QUERY_ROWS = 1024
CTX_SEQS_PER_STEP = 1
MOD_ROWS = 8
ADA_COLS = 1536


def _params(sem):
    return pltpu.CompilerParams(dimension_semantics=sem, vmem_limit_bytes=V7X_VMEM_LIMIT_BYTES)


def _gelu_tanh(x):
    return x * (0.5 * (1.0 + jnp.tanh(np.float32(np.sqrt(2.0 / np.pi)) * (x + 0.044715 * (x * x * x)))))


def _sigmoid(x):
    return 1.0 / (1.0 + jnp.exp(-x))


def _layer_norm_rows(t, g, b):
    mu = jnp.mean(t, axis=-1, keepdims=True)
    tc = t - mu
    var = jnp.mean(tc * tc, axis=-1, keepdims=True)
    return tc * lax.rsqrt(var + LN_EPS) * g + b


def _rows_to_tiles(x):
    pieces = jnp.stack([x[:, s * LANES:(s + 1) * LANES] for s in range(x.shape[1] // LANES)], axis=0)
    return jnp.swapaxes(pieces, 0, 1)


def _tiles_to_rows(t):
    pieces = jnp.swapaxes(t, 0, 1)
    return jnp.concatenate([pieces[s] for s in range(t.shape[1])], axis=1)


def _stream_specs(width, ctx_tiles, tm):
    return [pl.BlockSpec((tm, width), lambda i, *_: (jnp.minimum(i, ctx_tiles - 1), 0)),
            pl.BlockSpec((tm, width), lambda i, *_: (jnp.maximum(i - ctx_tiles, 0), 0))]


def _read_stream(ctx_ref, lat_ref, ctx_tiles):
    tile = lax.broadcasted_iota(jnp.int32, ctx_ref.shape, 0) * 0 + pl.program_id(0)
    return jnp.where(tile < ctx_tiles, ctx_ref[...], lat_ref[...])


def _write_stream(ctx_ref, lat_ref, ctx_tiles, value):
    @pl.when(pl.program_id(0) < ctx_tiles)
    def _():
        ctx_ref[...] = value

    @pl.when(pl.program_id(0) >= ctx_tiles)
    def _():
        lat_ref[...] = value


def _ada_kernel(c_ref, w_ref, b_ref, o_ref):
    c = c_ref[...]
    s = (c * _sigmoid(c)).astype(BF16)
    o_ref[...] = jnp.dot(s, w_ref[...].astype(BF16), preferred_element_type=F32) + b_ref[...]


def _ada_call(cvec, w_ada, b_ada):
    depth = w_ada.shape[0]
    n_out = w_ada.shape[2]
    return pl.pallas_call(
        _ada_kernel,
        grid=(depth, n_out // ADA_COLS),
        in_specs=[
            pl.BlockSpec((MOD_ROWS, D_MODEL), lambda l, j: (0, 0)),
            pl.BlockSpec((None, D_MODEL, ADA_COLS), lambda l, j: (l, 0, j)),
            pl.BlockSpec((None, 1, ADA_COLS), lambda l, j: (l, 0, j)),
        ],
        out_specs=pl.BlockSpec((None, MOD_ROWS, ADA_COLS), lambda l, j: (l, 0, j)),
        out_shape=jax.ShapeDtypeStruct((depth, MOD_ROWS, n_out), F32),
        compiler_params=_params(("arbitrary", "arbitrary")),
        name="adaln",
    )(cvec, w_ada, b_ada.reshape(depth, 1, n_out))


def _in_kernel(split, ctx_tiles, *refs):
    if split:
        x = _read_stream(refs[0], refs[1], ctx_tiles)
        refs = refs[2:]
    else:
        x = refs[0][...]
        refs = refs[1:]
    mod_ref, w_ref, cos_ref, sin_ref, lng_ref, lnb_ref, q_ref, k_ref, v_ref, kv_ref, u_ref, g_ref = refs
    tm = x.shape[0]
    h = x * (1.0 + mod_ref[1:2, :]) + mod_ref[0:1, :]
    hb = h.astype(BF16)

    def z_block(start):
        return jnp.dot(hb, w_ref[:, start:start + V7X_MXU_WIDTH], preferred_element_type=F32)

    cos = cos_ref[...]
    sin = sin_ref[...]
    lane = lax.broadcasted_iota(jnp.int32, (tm, LANES), 1)
    first_of_pair = (lane & (2 * ROPE_PAIRS - 1)) < ROPE_PAIRS
    lower_group = lane < SGU_GROUP_DIM

    def rope(t):
        partner = jnp.where(first_of_pair, pltpu.roll(t, LANES - ROPE_PAIRS, 1), pltpu.roll(t, ROPE_PAIRS, 1))
        return t * cos + partner * sin

    halves = (slice(0, LANES), slice(LANES, 2 * LANES))
    inv_n = 1.0 / SGU_GROUP_DIM
    for j in range(SGU_WIDTH // LANES):
        cols = slice(j * LANES, (j + 1) * LANES)
        if j % 2 == 0:
            z = z_block(OFF_G + j * LANES)
        t = _gelu_tanh(z[:, halves[j % 2]])
        s_lo = jnp.sum(jnp.where(lower_group, t, 0.0), axis=-1, keepdims=True)
        s_hi = jnp.sum(jnp.where(lower_group, 0.0, t), axis=-1, keepdims=True)
        tc = t - jnp.where(lower_group, s_lo, s_hi) * inv_n
        sq = tc * tc
        v_lo = jnp.sum(jnp.where(lower_group, sq, 0.0), axis=-1, keepdims=True)
        v_hi = jnp.sum(jnp.where(lower_group, 0.0, sq), axis=-1, keepdims=True)
        var = jnp.where(lower_group, v_lo, v_hi) * inv_n
        g_ref[:, cols] = (tc * lax.rsqrt(var + LN_EPS) * lng_ref[:, cols] + lnb_ref[:, cols]).astype(BF16)

    for start in range(0, ATTN_WIDTH, V7X_MXU_WIDTH):
        z = z_block(start)
        for half in halves:
            q_ref[:, start + half.start:start + half.stop] = (rope(z[:, half]) * ATTN_SCALE).astype(BF16)
    z = z_block(OFF_K)
    k_ref[...] = rope(z[:, halves[0]]).astype(BF16)
    v_ref[...] = z[:, halves[1]].astype(BF16)
    kv_ref[...] = z
    for start in range(0, SGU_WIDTH, V7X_MXU_WIDTH):
        u_ref[:, start:start + V7X_MXU_WIDTH] = _gelu_tanh(z_block(OFF_U + start))


def _in_call(x, mods, w_in, cos_t, sin_t, ln_g, ln_b, geom):
    n = geom.n_tok
    tm = IN_TILE
    mod_idx = lambda i: geom.mod_index(i, tm)
    rope_idx = lambda i: geom.rope_index(i, tm)
    row = lambda i: (i, 0)
    split = isinstance(x, tuple)
    x_args = list(x) if split else [x]
    ctx_tiles = geom.ctx_tiles(tm)
    x_specs = _stream_specs(D_MODEL, ctx_tiles, tm) if split else [pl.BlockSpec((tm, D_MODEL), row)]
    outs = pl.pallas_call(
        functools.partial(_in_kernel, split, ctx_tiles),
        grid=(n // tm,),
        in_specs=x_specs + [
            pl.BlockSpec((None, N_MODS, D_MODEL), lambda i: (mod_idx(i), 0, 0)),
            pl.BlockSpec((D_MODEL, IN_WIDTH), lambda i: (0, 0)),
            pl.BlockSpec((tm, LANES), lambda i: (rope_idx(i), 0)),
            pl.BlockSpec((tm, LANES), lambda i: (rope_idx(i), 0)),
            pl.BlockSpec((1, SGU_WIDTH), lambda i: (0, 0)),
            pl.BlockSpec((1, SGU_WIDTH), lambda i: (0, 0)),
        ],
        out_specs=[
            pl.BlockSpec((tm, ATTN_WIDTH), row),
            pl.BlockSpec((tm, KV_WIDTH), row),
            pl.BlockSpec((tm, KV_WIDTH), row),
            pl.BlockSpec((tm, 2 * KV_WIDTH), row),
            pl.BlockSpec((tm, SGU_WIDTH), row),
            pl.BlockSpec((tm, SGU_WIDTH), row),
        ],
        out_shape=[
            jax.ShapeDtypeStruct((n, ATTN_WIDTH), BF16),
            jax.ShapeDtypeStruct((n, KV_WIDTH), BF16),
            jax.ShapeDtypeStruct((n, KV_WIDTH), BF16),
            jax.ShapeDtypeStruct((n, 2 * KV_WIDTH), F32),
            jax.ShapeDtypeStruct((n, SGU_WIDTH), F32),
            jax.ShapeDtypeStruct((n, SGU_WIDTH), BF16),
        ],
        compiler_params=_params(("arbitrary",)),
        name="in_proj",
    )(*x_args, mods, w_in, cos_t, sin_t, ln_g, ln_b)
    return outs


def _group_attention(q_ref, rows, hk, k_all, v_all, block_masks, sink_ref):
    m_rows = rows.stop - rows.start
    heads = [hk * Q_PER_KV + gq for gq in range(Q_PER_KV)]
    q = jnp.concatenate([q_ref[rows, h * HEAD_DIM:(h + 1) * HEAD_DIM] for h in heads], axis=0)
    s = lax.dot_general(q, k_all, (((1,), (1,)), ((), ())), preferred_element_type=F32)
    n_blocks = k_all.shape[0] // LANES
    blocks = [s[:, b * LANES:(b + 1) * LANES] for b in range(n_blocks)]
    for b, mask in block_masks.items():
        blocks[b] = jnp.where(mask, blocks[b], NEG_INF)
    head_of_row = lax.broadcasted_iota(jnp.int32, (Q_PER_KV * m_rows, 1), 0) // m_rows
    sink = jnp.zeros((Q_PER_KV * m_rows, 1), F32)
    for gq, h in enumerate(heads):
        sink = jnp.where(head_of_row == gq, sink_ref[h], sink)
    m_el = blocks[0]
    for blk in blocks[1:]:
        m_el = jnp.maximum(m_el, blk)
    m = jnp.maximum(jnp.max(m_el, axis=-1, keepdims=True), sink)
    probs = [jnp.exp(blk - m) for blk in blocks]
    l_el = probs[0]
    for p in probs[1:]:
        l_el = l_el + p
    denom = jnp.sum(l_el, axis=-1, keepdims=True) + jnp.exp(sink - m)
    p_all = jnp.concatenate([p.astype(BF16) for p in probs], axis=1)
    o = jnp.dot(p_all, v_all, preferred_element_type=F32) / denom
    return {h: o[gq * m_rows:(gq + 1) * m_rows] for gq, h in enumerate(heads)}


def _store_heads(o_ref, rows, outs):
    for h0 in range(0, N_Q_HEADS, 2):
        pair = jnp.concatenate([outs[h0], outs[h0 + 1]], axis=1)
        o_ref[rows, h0 * HEAD_DIM:(h0 + 2) * HEAD_DIM] = pair.astype(o_ref.dtype)


def _ctx_attn_kernel(seq_len, sink_ref, q_ref, k_ref, v_ref, o_ref):
    for sq in range(q_ref.shape[0] // seq_len):
        rows = slice(sq * seq_len, (sq + 1) * seq_len)
        outs = {}
        for hk in range(N_KV_HEADS):
            kv_cols = slice(hk * HEAD_DIM, (hk + 1) * HEAD_DIM)
            outs.update(_group_attention(q_ref, rows, hk, k_ref[rows, kv_cols], v_ref[rows, kv_cols], {}, sink_ref))
        _store_heads(o_ref, rows, outs)


def _ctx_attn_call(sink, q, k, v, n_seq, seq_len):
    n = n_seq * seq_len
    per_step = CTX_SEQS_PER_STEP if n_seq % CTX_SEQS_PER_STEP == 0 else 1
    rows = per_step * seq_len
    blk = lambda b: (b, 0)
    return pl.pallas_call(
        functools.partial(_ctx_attn_kernel, seq_len),
        grid=(n_seq // per_step,),
        in_specs=[
            pl.BlockSpec(memory_space=pltpu.SMEM),
            pl.BlockSpec((rows, ATTN_WIDTH), blk),
            pl.BlockSpec((rows, KV_WIDTH), blk),
            pl.BlockSpec((rows, KV_WIDTH), blk),
        ],
        out_specs=pl.BlockSpec((rows, ATTN_WIDTH), blk),
        out_shape=jax.ShapeDtypeStruct((n, ATTN_WIDTH), BF16),
        compiler_params=_params(("arbitrary",)),
        name="ctx_attention",
    )(sink, q, k, v)


def _lat_attn_kernel(sink_ref, q_ref, k_ref, v_ref, ck_ref, cv_ref, o_ref):
    blocks_per_step = q_ref.shape[0] // BLOCK
    nb = k_ref.shape[0] // BLOCK
    past_blocks = ck_ref.shape[0] // LANES
    r = lax.broadcasted_iota(jnp.int32, (Q_PER_KV * BLOCK, BLOCK), 0) & (BLOCK - 1)
    c = lax.broadcasted_iota(jnp.int32, (Q_PER_KV * BLOCK, BLOCK), 1)
    for sub in range(blocks_per_step):
        j = pl.program_id(1) * blocks_per_step + sub
        rows = slice(sub * BLOCK, (sub + 1) * BLOCK)
        mask_prev = c >= r + jnp.where(j > 0, 0, BLOCK)
        mask_next = c <= r - jnp.where(j < nb - 1, 0, BLOCK)
        prev = pl.ds(pl.multiple_of(jnp.maximum(j - 1, 0) * BLOCK, BLOCK), BLOCK)
        cur = pl.ds(pl.multiple_of(j * BLOCK, BLOCK), BLOCK)
        nxt = pl.ds(pl.multiple_of(jnp.minimum(j + 1, nb - 1) * BLOCK, BLOCK), BLOCK)
        outs = {}
        for hk in range(N_KV_HEADS):
            kv_cols = slice(hk * HEAD_DIM, (hk + 1) * HEAD_DIM)
            k_all = jnp.concatenate([ck_ref[:, kv_cols], k_ref[prev, kv_cols], k_ref[cur, kv_cols],
                                     k_ref[nxt, kv_cols]], axis=0)
            v_all = jnp.concatenate([cv_ref[:, kv_cols], v_ref[prev, kv_cols], v_ref[cur, kv_cols],
                                     v_ref[nxt, kv_cols]], axis=0)
            masks = {past_blocks: mask_prev, past_blocks + 2: mask_next}
            outs.update(_group_attention(q_ref, rows, hk, k_all, v_all, masks, sink_ref))
        _store_heads(o_ref, rows, outs)


def _lat_attn_call(sink, q, k, v, cache_k, cache_v, geom):
    qb = LAT_QUERY_ROWS
    steps = geom.dec_seq // qb
    assert WINDOW == BLOCK
    assert geom.n_ctx % geom.dec_seq == 0 and cache_k.shape[1] % LANES == 0
    seq_base = geom.n_ctx // geom.dec_seq
    past = cache_k.shape[1]
    seq_spec = pl.BlockSpec((geom.dec_seq, KV_WIDTH), lambda b, j: (seq_base + b, 0))
    cache_spec = pl.BlockSpec((None, past, KV_WIDTH), lambda b, j: (b, 0, 0))
    return pl.pallas_call(
        _lat_attn_kernel,
        grid=(geom.dec_batch, steps),
        in_specs=[
            pl.BlockSpec(memory_space=pltpu.SMEM),
            pl.BlockSpec((qb, ATTN_WIDTH), lambda b, j: (geom.n_ctx // qb + b * steps + j, 0)),
            seq_spec, seq_spec,
            cache_spec, cache_spec,
        ],
        out_specs=pl.BlockSpec((qb, ATTN_WIDTH), lambda b, j: (b * steps + j, 0)),
        out_shape=jax.ShapeDtypeStruct((geom.dec_batch * geom.dec_seq, ATTN_WIDTH), BF16),
        compiler_params=_params(("arbitrary", "arbitrary")),
        name="lat_attention",
    )(sink, q, k, v, cache_k, cache_v)


def _spatial_gating(u_ref, g_ref, mix_ref, bias_ref, s_ref):
    tm = u_ref.shape[0]
    lane = lax.broadcasted_iota(jnp.int32, (CHUNK, LANES), 1)
    lower_group = lane < SGU_GROUP_DIM
    for ch in range(tm // CHUNK):
        rows = slice(ch * CHUNK, (ch + 1) * CHUNK)
        for p in range(SGU_WIDTH // LANES):
            cols = slice(p * LANES, (p + 1) * LANES)
            g = g_ref[rows, cols]
            zero = jnp.zeros_like(g)
            mixed = (jnp.dot(mix_ref[2 * p], jnp.where(lower_group, g, zero), preferred_element_type=F32)
                     + jnp.dot(mix_ref[2 * p + 1], jnp.where(lower_group, zero, g), preferred_element_type=F32))
            s_ref[rows, cols] = (u_ref[rows, cols] * (mixed + bias_ref[:, cols])).astype(s_ref.dtype)


def _out_kernel(alpha, with_router, split_x, ctx_tiles, a_ctx_ref, a_lat_ref, u_ref, gg_ref, mix_ref, bias_ref,
                *refs):
    s_ref = refs[-1]
    refs = refs[:-1]
    if split_x:
        x = _read_stream(refs[0], refs[1], ctx_tiles)
        refs = refs[2:]
    else:
        x = refs[0][...]
        refs = refs[1:]
    mod_ref, wa_ref, ws_ref, g_ref, b_ref = refs[:5]
    if with_router:
        wr_ref, x1_ref, h_ref, rt_ref = refs[5:]
    else:
        x1_ref, h_ref = refs[5:]
    a = _read_stream(a_ctx_ref, a_lat_ref, ctx_tiles)
    _spatial_gating(u_ref, gg_ref, mix_ref, bias_ref, s_ref)
    y = (jnp.dot(a, wa_ref[...], preferred_element_type=F32)
         + jnp.dot(s_ref[...], ws_ref[...], preferred_element_type=F32))
    t = alpha * x + mod_ref[2:3, :] * y
    x1 = _layer_norm_rows(t, g_ref[...], b_ref[...])
    x1_ref[...] = x1
    h = x1 * (1.0 + mod_ref[4:5, :]) + mod_ref[3:4, :]
    if with_router:
        h_ref[...] = _rows_to_tiles(h)
    else:
        h_ref[...] = h.astype(h_ref.dtype)
    if with_router:
        tm = h.shape[0]
        wr = wr_ref[...]
        wr_hi = wr.astype(BF16)
        wr_lo = (wr - wr_hi.astype(F32)).astype(BF16)
        h_hi = h.astype(BF16)
        h_lo = (h - h_hi.astype(F32)).astype(BF16)
        logits = (jnp.dot(h_hi, wr_hi, preferred_element_type=F32)
                  + jnp.dot(h_lo, wr_hi, preferred_element_type=F32)
                  + jnp.dot(h_hi, wr_lo, preferred_element_type=F32))
        lane = lax.broadcasted_iota(jnp.int32, (tm, LANES), 1).astype(F32)
        neg = jnp.float32(-jnp.inf)
        lg = jnp.where(lane < N_EXPERTS, logits, neg)
        m1 = jnp.max(lg, axis=-1, keepdims=True)
        i1 = jnp.min(jnp.where(lg == m1, lane, float(LANES)), axis=-1, keepdims=True)
        lg2 = jnp.where(lane == i1, neg, lg)
        m2 = jnp.max(lg2, axis=-1, keepdims=True)
        i2 = jnp.min(jnp.where(lg2 == m2, lane, float(LANES)), axis=-1, keepdims=True)
        e2 = jnp.exp(m2 - m1)
        g1 = 1.0 / (1.0 + e2)
        g2 = e2 / (1.0 + e2)
        rt = jnp.where(lane == 0, i1, jnp.where(lane == 1, i2, jnp.where(lane == 2, g1, jnp.where(lane == 3, g2, 0.0))))
        rt_ref[...] = rt


def _out_call(a_ctx, a_lat, u, gg, w_s, bias_full, x, mods, w_o, ln_g, ln_b, w_router, alpha, geom):
    n = geom.n_tok
    with_router = w_router is not None
    tm = ROUTER_TILE if with_router else OUT_TILE
    ctx_tiles = geom.ctx_tiles(tm)
    row = lambda i: (i, 0)
    const = lambda i: (0, 0)
    split_x = isinstance(x, tuple)
    x_args = list(x) if split_x else [x]
    x_specs = _stream_specs(D_MODEL, ctx_tiles, tm) if split_x else [pl.BlockSpec((tm, D_MODEL), row)]
    sgu_specs = [
        pl.BlockSpec((tm, SGU_WIDTH), row),
        pl.BlockSpec((tm, SGU_WIDTH), row),
        pl.BlockSpec((N_SGU_GROUPS, CHUNK, CHUNK), lambda i: (0, 0, 0)),
        pl.BlockSpec((CHUNK, SGU_WIDTH), const),
    ]
    in_specs = _stream_specs(ATTN_WIDTH, ctx_tiles, tm) + sgu_specs + x_specs + [
        pl.BlockSpec((None, N_MODS, D_MODEL), lambda i: (geom.mod_index(i, tm), 0, 0)),
        pl.BlockSpec((ATTN_WIDTH, D_MODEL), const),
        pl.BlockSpec((SGU_WIDTH, D_MODEL), lambda i: (1, 0)),
        pl.BlockSpec((1, D_MODEL), const),
        pl.BlockSpec((1, D_MODEL), const),
    ]
    args = [a_ctx, a_lat, u, gg, w_s, bias_full] + x_args + [mods, w_o, w_o, ln_g, ln_b]
    if with_router:
        h_spec = pl.BlockSpec((tm, D_MODEL // LANES, LANES), lambda i: (i, 0, 0))
        h_shape = jax.ShapeDtypeStruct((n, D_MODEL // LANES, LANES), F32)
    else:
        h_spec = pl.BlockSpec((tm, D_MODEL), row)
        h_shape = jax.ShapeDtypeStruct((n, D_MODEL), BF16)
    out_specs = [pl.BlockSpec((tm, D_MODEL), row), h_spec]
    out_shape = [jax.ShapeDtypeStruct((n, D_MODEL), F32), h_shape]
    if with_router:
        in_specs.append(pl.BlockSpec((D_MODEL, LANES), const))
        args.append(w_router)
        out_specs.append(pl.BlockSpec((tm, LANES), row))
        out_shape.append(jax.ShapeDtypeStruct((n, LANES), F32))
    return pl.pallas_call(
        functools.partial(_out_kernel, alpha, with_router, split_x, ctx_tiles),
        grid=(n // tm,),
        in_specs=in_specs,
        out_specs=out_specs,
        out_shape=out_shape,
        scratch_shapes=[pltpu.VMEM((tm, SGU_WIDTH), BF16)],
        compiler_params=_params(("arbitrary",)),
        name="out_proj_router" if with_router else "out_proj",
    )(*args)


def _swiglu_blocks(width):
    return [slice(s, min(s + V7X_MXU_WIDTH, width)) for s in range(0, width, V7X_MXU_WIDTH)]


def _swiglu_partial(x, wg_ref, wu_ref, wd_ref, side_work=None):
    out = None
    for n, cols in enumerate(_swiglu_blocks(wg_ref.shape[1])):
        a = jnp.dot(x, wg_ref[:, cols].astype(BF16), preferred_element_type=F32)
        if side_work is not None:
            side_work(3 * n)
        b = jnp.dot(x, wu_ref[:, cols].astype(BF16), preferred_element_type=F32)
        if side_work is not None:
            side_work(3 * n + 1)
        mid = ((a * _sigmoid(a)) * b).astype(BF16)
        part = jnp.dot(mid, wd_ref[cols, :].astype(BF16), preferred_element_type=F32)
        if side_work is not None:
            side_work(3 * n + 2)
        out = part if out is None else out + part
    return out


def _ffn_kernel(alpha, h_ref, x1_ref, mod_ref, wg_ref, wu_ref, wd_ref, g_ref, b_ref, o_ref):
    y = _swiglu_partial(h_ref[...], wg_ref, wu_ref, wd_ref)
    t = alpha * x1_ref[...] + mod_ref[5:6, :] * y
    o_ref[...] = _layer_norm_rows(t, g_ref[...], b_ref[...])


def _ffn_chunk(d_ff):
    assert d_ff % (2 * LANES) == 0
    return d_ff // 2


def _ffn_call(h, x1, mods, wg, wu, wd, ln_g, ln_b, alpha, geom):
    n = h.shape[0]
    tm = DENSE_TILE
    d_ff = wg.shape[1]
    row = lambda i: (i, 0)
    const = lambda i: (0, 0)
    resident = pl.Buffered(1)
    return pl.pallas_call(
        functools.partial(_ffn_kernel, alpha),
        grid=(n // tm,),
        in_specs=[
            pl.BlockSpec((tm, D_MODEL), row),
            pl.BlockSpec((tm, D_MODEL), row),
            pl.BlockSpec((None, N_MODS, D_MODEL), lambda i: (geom.mod_index(i, tm), 0, 0)),
            pl.BlockSpec((D_MODEL, d_ff), const, pipeline_mode=resident),
            pl.BlockSpec((D_MODEL, d_ff), const, pipeline_mode=resident),
            pl.BlockSpec((d_ff, D_MODEL), const, pipeline_mode=resident),
            pl.BlockSpec((1, D_MODEL), const),
            pl.BlockSpec((1, D_MODEL), const),
        ],
        out_specs=pl.BlockSpec((tm, D_MODEL), row),
        out_shape=jax.ShapeDtypeStruct((n, D_MODEL), F32),
        compiler_params=_params(("arbitrary",)),
        name="dense_ffn",
    )(h, x1, mods, wg, wu, wd, ln_g, ln_b)


def _moe_kernel(n_chunks, te_ref, nu_ref, src_ref, dst_ref, h_hbm, wg_ref, wu_ref, wd_ref, o_hbm,
                xbuf, xb, obuf, gsem, ssem):
    del te_ref
    tr = xb.shape[0]
    t = pl.program_id(0)
    c = pl.program_id(1)
    n_tiles = pl.num_programs(0)
    valid = t < nu_ref[0]
    slot = t % 2
    other = 1 - slot
    share = tr // n_chunks
    first_row = c * share
    nxt_base = jnp.minimum(t + 1, n_tiles - 1) * tr
    prv_base = jnp.where(t == 0, n_tiles - 1, t - 1) * tr

    def gather_row(base, dst_slot, r, tok=None):
        tok = src_ref[base + r] if tok is None else tok
        return pltpu.make_async_copy(h_hbm.at[tok], xbuf.at[dst_slot, r], gsem.at[dst_slot])

    def scatter_row(base, src_slot, r, dst=None):
        dst = dst_ref[base + r] if dst is None else dst
        return pltpu.make_async_copy(obuf.at[src_slot, r], o_hbm.at[dst], ssem.at[src_slot])

    def wait_gather(dst_slot):
        pltpu.make_async_copy(h_hbm.at[pl.ds(0, tr)], xbuf.at[dst_slot], gsem.at[dst_slot]).wait()

    def wait_scatter(src_slot):
        pltpu.make_async_copy(obuf.at[src_slot], o_hbm.at[pl.ds(0, tr)], ssem.at[src_slot]).wait()

    def looped(n_rows, start_row):
        def body(r, carry):
            start_row(r)
            return carry
        lax.fori_loop(0, n_rows, body, 0, unroll=8)

    @pl.when((t == 0) & (c == 0))
    def _():
        obuf[1] = jnp.zeros(obuf.shape[1:], obuf.dtype)
        looped(tr, lambda r: gather_row(0, 0, r).start())

    n_used = nu_ref[0]

    @pl.when(c == 0)
    def _():
        @pl.when(t <= n_used)
        def _():
            wait_gather(slot)

        @pl.when(t >= 1)
        def _():
            wait_scatter(slot)

    def tile_scatter(base, src_slot):
        return pltpu.make_async_copy(obuf.at[src_slot], o_hbm.at[pl.ds(dst_ref[base], tr)], ssem.at[src_slot])

    def multiply_tile(slot_s, c_s):
        other_s = 1 - slot_s
        if c_s == 0:
            xb[...] = _tiles_to_rows(xbuf[slot_s]).astype(BF16)
        n_groups = 3 * len(_swiglu_blocks(wg_ref.shape[1]))

        def row_dmas(i):
            for r in range(c_s * share + share * i // n_groups, c_s * share + share * (i + 1) // n_groups):
                gather_row(nxt_base, other_s, r).start()
                scatter_row(prv_base, other_s, r).start()

        part = _swiglu_partial(xb[...], wg_ref, wu_ref, wd_ref, row_dmas)
        if c_s == 0:
            obuf[slot_s] = _rows_to_tiles(part)
        else:
            obuf[slot_s] += _rows_to_tiles(part)

    for slot_s in range(2):
        for c_s in range(n_chunks):
            pl.when(valid & (slot == slot_s) & (c == c_s))(functools.partial(multiply_tile, slot_s, c_s))

    @pl.when(jnp.logical_not(valid))
    def _():
        @pl.when(c == 0)
        def _():
            obuf[slot] = jnp.zeros(obuf.shape[1:], obuf.dtype)

        @pl.when(t == n_used)
        def _():
            looped(share, lambda r: scatter_row(prv_base, other, first_row + r).start())

        @pl.when((t > n_used) & (c == 0))
        def _():
            tile_scatter(prv_base, other).start()

    @pl.when((t == n_tiles - 1) & (c == n_chunks - 1))
    def _():
        wait_scatter(other)
        tile_scatter(t * tr, slot).start()
        wait_scatter(slot)


def _moe_call(h, wg, wu, wd, plan):
    tr = FFN_TILE
    d_ff = wg.shape[2]
    fc = _ffn_chunk(d_ff)
    n_chunks = d_ff // fc
    t_max = plan["tile_expert"].shape[0]
    row_tile = h.shape[1:]

    def chunk(t, c, nu):
        return jnp.where(t < nu[0], c, n_chunks - 1)

    grid_spec = pltpu.PrefetchScalarGridSpec(
        num_scalar_prefetch=4,
        grid=(t_max, n_chunks),
        in_specs=[
            pl.BlockSpec(memory_space=pl.ANY),
            pl.BlockSpec((None, D_MODEL, fc), lambda t, c, te, nu, src, dst: (te[t], 0, chunk(t, c, nu))),
            pl.BlockSpec((None, D_MODEL, fc), lambda t, c, te, nu, src, dst: (te[t], 0, chunk(t, c, nu))),
            pl.BlockSpec((None, fc, D_MODEL), lambda t, c, te, nu, src, dst: (te[t], chunk(t, c, nu), 0)),
        ],
        out_specs=pl.BlockSpec(memory_space=pl.ANY),
        scratch_shapes=[
            pltpu.VMEM((2, tr) + row_tile, F32),
            pltpu.VMEM((tr, D_MODEL), BF16),
            pltpu.VMEM((2, tr) + row_tile, F32),
            pltpu.SemaphoreType.DMA((2,)),
            pltpu.SemaphoreType.DMA((2,)),
        ],
    )
    return pl.pallas_call(
        functools.partial(_moe_kernel, n_chunks),
        grid_spec=grid_spec,
        out_shape=jax.ShapeDtypeStruct((t_max * tr,) + row_tile, F32),
        compiler_params=_params(("arbitrary", "arbitrary")),
        name="expert_ffn",
    )(plan["tile_expert"], plan["n_used"], plan["src_token"], plan["dst_row"], h, wg, wu, wd)


def _route_plan(routing, n_tok, tr):
    n_assign = TOP_K * n_tok
    t_max = n_assign // tr + N_EXPERTS
    n_slots = t_max * tr
    n_pad = n_slots - n_assign
    id_bits = (n_slots - 1).bit_length()
    experts = jnp.arange(N_EXPERTS, dtype=jnp.int32)
    e_flat = routing[:, :TOP_K].astype(jnp.int32).T.reshape(-1)
    counts = jnp.sum((e_flat[:, None] == experts[None, :]).astype(jnp.int32), axis=0)
    tiles_e = (counts + tr - 1) // tr
    tile_end = jnp.cumsum(tiles_e)
    n_used = tile_end[-1]
    pad_end = jnp.cumsum(tiles_e * tr - counts)
    pad_ids = jnp.arange(n_pad, dtype=jnp.int32)
    pad_expert = jnp.sum((pad_ids[:, None] >= pad_end[None, :]).astype(jnp.int32), axis=1)
    keys = jnp.concatenate([e_flat * 2, pad_expert * 2 + 1])
    item = jnp.arange(n_slots, dtype=jnp.int32)
    slot_item = jnp.sort((keys << id_bits) | item) & ((1 << id_bits) - 1)
    real = slot_item < n_assign
    src = jnp.where(real, slot_item % n_tok, 0).astype(jnp.int32)
    dst = slot_item.astype(jnp.int32)
    t_ids = jnp.arange(t_max, dtype=jnp.int32)
    te = jnp.sum((t_ids[:, None] >= tile_end[None, :]).astype(jnp.int32), axis=1)
    last_used = jnp.max(jnp.where(tiles_e > 0, experts, 0))
    te = jnp.where(t_ids < n_used, te, last_used).astype(jnp.int32)
    return {"tile_expert": te, "n_used": n_used.reshape(1).astype(jnp.int32), "src_token": src, "dst_row": dst}


def _combine_kernel(alpha, split_ctx_tiles, e1_ref, e2_ref, rt_ref, x1_ref, mod_ref, g_ref, b_ref, *o_refs):
    y = rt_ref[:, 2:3] * _tiles_to_rows(e1_ref[...]) + rt_ref[:, 3:4] * _tiles_to_rows(e2_ref[...])
    t = alpha * x1_ref[...] + mod_ref[5:6, :] * y
    out = _layer_norm_rows(t, g_ref[...], b_ref[...])
    if split_ctx_tiles is None:
        o_refs[0][...] = out
    else:
        _write_stream(o_refs[0], o_refs[1], split_ctx_tiles, out)


def _combine_call(eo, routing, x1, mods, ln_g, ln_b, alpha, geom, split_out):
    n = x1.shape[0]
    tm = COMBINE_TILE
    row = lambda i: (i, 0)
    const = lambda i: (0, 0)
    if split_out:
        out_specs = _stream_specs(D_MODEL, geom.ctx_tiles(tm), tm)
        out_shape = [jax.ShapeDtypeStruct((geom.n_ctx, D_MODEL), F32),
                     jax.ShapeDtypeStruct((n - geom.n_ctx, D_MODEL), F32)]
    else:
        out_specs = pl.BlockSpec((tm, D_MODEL), row)
        out_shape = jax.ShapeDtypeStruct((n, D_MODEL), F32)
    return pl.pallas_call(
        functools.partial(_combine_kernel, alpha, geom.ctx_tiles(tm) if split_out else None),
        grid=(n // tm,),
        in_specs=[
            pl.BlockSpec((tm,) + eo.shape[1:], lambda i: (i, 0, 0)),
            pl.BlockSpec((tm,) + eo.shape[1:], lambda i: (i + n // tm, 0, 0)),
            pl.BlockSpec((tm, LANES), row),
            pl.BlockSpec((tm, D_MODEL), row),
            pl.BlockSpec((None, N_MODS, D_MODEL), lambda i: (geom.mod_index(i, tm), 0, 0)),
            pl.BlockSpec((1, D_MODEL), const),
            pl.BlockSpec((1, D_MODEL), const),
        ],
        out_specs=out_specs,
        out_shape=out_shape,
        compiler_params=_params(("arbitrary",)),
        name="expert_combine",
    )(eo, eo, routing, x1, mods, ln_g, ln_b)


class _Geometry:
    def __init__(self, n_ctx, dec_batch, dec_seq):
        self.n_ctx = n_ctx
        self.dec_batch = dec_batch
        self.dec_seq = dec_seq
        self.n_tok = n_ctx + dec_batch * dec_seq

    def ctx_tiles(self, tm):
        assert self.n_ctx % tm == 0 and self.dec_seq % tm == 0
        return self.n_ctx // tm

    def mod_index(self, i, tm):
        ct = self.ctx_tiles(tm)
        return jnp.where(i < ct, 0, 1 + (i - ct) // (self.dec_seq // tm))

    def rope_index(self, i, tm):
        ct = self.ctx_tiles(tm)
        return jnp.where(i < ct, 0, 1 + (i - ct) % (self.dec_seq // tm))


def _rope_tables(dec_seq):
    pos = jnp.arange(dec_seq, dtype=jnp.int32)
    row = (pos // GRID_W).astype(F32)
    col = (pos % GRID_W).astype(F32)
    inv = ROPE_BASE ** (-jnp.arange(ROPE_PAIRS, dtype=F32) / ROPE_PAIRS)
    ang_r = row[:, None] * inv[None, :]
    ang_c = col[:, None] * inv[None, :]
    cos_h = jnp.concatenate([jnp.cos(ang_r), jnp.cos(ang_r), jnp.cos(ang_c), jnp.cos(ang_c)], axis=-1)
    sin_h = jnp.concatenate([-jnp.sin(ang_r), jnp.sin(ang_r), -jnp.sin(ang_c), jnp.sin(ang_c)], axis=-1)
    reps = LANES // HEAD_DIM
    cos_t = jnp.concatenate([jnp.ones((IN_TILE, LANES), F32), jnp.tile(cos_h, (1, reps))], axis=0)
    sin_t = jnp.concatenate([jnp.zeros((IN_TILE, LANES), F32), jnp.tile(sin_h, (1, reps))], axis=0)
    return cos_t, sin_t


def kernel(x_prompt, x_sample, cache_k, cache_v, c, c_ctx, w_ada, b_ada, w_in, w_o, attn_sink, w_s, b_s, sgu_ln_g, sgu_ln_b, ln1_g, ln1_b, ln2_g, ln2_b, w_ff_gate, w_ff_up, w_ff_down, w_router, w_exp_gate, w_exp_up, w_exp_down):
    batch, seq, d = x_prompt.shape
    dec_batch, dec_seq, _ = x_sample.shape
    depth = w_in.shape[0]
    past = cache_k.shape[2]
    assert d == D_MODEL and dec_batch + 1 <= MOD_ROWS
    n_ctx = batch * seq
    n_lat = dec_batch * dec_seq
    n_tok = n_ctx + n_lat
    geom = _Geometry(n_ctx, dec_batch, dec_seq)
    alpha = float((2 * depth) ** 0.25)

    x = (x_prompt.reshape(n_ctx, d), x_sample.reshape(n_lat, d))
    cvec = jnp.concatenate([c_ctx[None, :], c, jnp.zeros((MOD_ROWS - 1 - dec_batch, d), F32)], axis=0)
    mods_all = _ada_call(cvec, w_ada, b_ada).reshape(depth, MOD_ROWS, N_MODS, d)
    cos_t, sin_t = _rope_tables(dec_seq)

    new_k, new_v = [], []
    for l in range(depth):
        mods = mods_all[l]
        q, k, v, kv32, u, g = _in_call(x, mods, w_in[l].astype(BF16), cos_t, sin_t,
                                       sgu_ln_g[l].reshape(1, SGU_WIDTH), sgu_ln_b[l].reshape(1, SGU_WIDTH), geom)
        new_k.append(kv32[:n_ctx, :KV_WIDTH].reshape(batch, seq, N_KV_HEADS, HEAD_DIM))
        new_v.append(kv32[:n_ctx, KV_WIDTH:].reshape(batch, seq, N_KV_HEADS, HEAD_DIM))
        sink = attn_sink[l]
        a_ctx = _ctx_attn_call(sink, q, k, v, batch, seq)
        a_lat = _lat_attn_call(sink, q, k, v, cache_k[:, l].reshape(dec_batch, past, KV_WIDTH).astype(BF16),
                               cache_v[:, l].reshape(dec_batch, past, KV_WIDTH).astype(BF16), geom)
        bias_full = jnp.repeat(b_s[l].T, SGU_GROUP_DIM, axis=1)
        i = l // 2
        moe = l % 2 == 1
        w_r = None
        if moe:
            w_r = jnp.pad(w_router[i], ((0, 0), (0, LANES - N_EXPERTS)))
        outs = _out_call(a_ctx, a_lat, u, g, w_s[l].astype(BF16), bias_full, x, mods, w_o[l].astype(BF16), ln1_g[l].reshape(1, d), ln1_b[l].reshape(1, d),
                         w_r, alpha, geom)
        ln_g, ln_b = ln2_g[l].reshape(1, d), ln2_b[l].reshape(1, d)
        if moe:
            x1, h, routing = outs
            plan = _route_plan(routing, n_tok, FFN_TILE)
            eo = _moe_call(h, w_exp_gate[i], w_exp_up[i], w_exp_down[i], plan)
            x = _combine_call(eo, routing, x1, mods, ln_g, ln_b, alpha, geom, split_out=l == depth - 1)
        else:
            x1, h = outs
            x = _ffn_call(h, x1, mods, w_ff_gate[i], w_ff_up[i], w_ff_down[i], ln_g, ln_b, alpha, geom)

    if not isinstance(x, (tuple, list)):
        x = (x[:n_ctx], x[n_ctx:])
    y_prompt = x[0].reshape(batch, seq, d)
    y_sample = x[1].reshape(dec_batch, dec_seq, d)
    return (y_prompt, y_sample, jnp.stack(new_k, axis=1), jnp.stack(new_v, axis=1))
```

```python
import functools

import jax
import jax.numpy as jnp
import numpy as np
from jax import lax
from jax.experimental import pallas as pl
from jax.experimental.pallas import tpu as pltpu

F32 = jnp.float32
BF16 = jnp.bfloat16

D_MODEL = 1024
HEAD_DIM = 64
N_Q_HEADS = 8
N_KV_HEADS = 2
Q_PER_KV = N_Q_HEADS // N_KV_HEADS
ATTN_WIDTH = N_Q_HEADS * HEAD_DIM
KV_WIDTH = N_KV_HEADS * HEAD_DIM
SGU_WIDTH = D_MODEL - ATTN_WIDTH
N_SGU_GROUPS = 8
SGU_GROUP_DIM = SGU_WIDTH // N_SGU_GROUPS
CHUNK = 128
BLOCK = 128
WINDOW = 128
GRID_W = 64
IN_WIDTH = ATTN_WIDTH + 2 * KV_WIDTH + 2 * SGU_WIDTH
OFF_K = ATTN_WIDTH
OFF_V = OFF_K + KV_WIDTH
OFF_U = OFF_V + KV_WIDTH
OFF_G = OFF_U + SGU_WIDTH
N_EXPERTS = 8
TOP_K = 2
ROPE_BASE = 10000.0
ROPE_PAIRS = HEAD_DIM // 4
LN_EPS = 1e-5
ATTN_SCALE = HEAD_DIM ** -0.5
NEG_INF = -1e30
N_MODS = 6

LANES = 128
V7X_MXU_WIDTH = 256
V7X_VMEM_LIMIT_BYTES = 56 * 1024 * 1024

IN_TILE = 1024
OUT_TILE = 512
ROUTER_TILE = 256
COMBINE_TILE = 1024
DENSE_TILE = 512
FFN_TILE = 512
LAT_QUERY_ROWS = 1024
CTX_SEQS_PER_STEP = 1
MOD_ROWS = 8
ADA_COLS = 1536


def _params(sem):
    return pltpu.CompilerParams(dimension_semantics=sem, vmem_limit_bytes=V7X_VMEM_LIMIT_BYTES)


def _gelu_tanh(x):
    return x * (0.5 * (1.0 + jnp.tanh(np.float32(np.sqrt(2.0 / np.pi)) * (x + 0.044715 * (x * x * x)))))


def _sigmoid(x):
    return 1.0 / (1.0 + jnp.exp(-x))


def _layer_norm_rows(t, g, b):
    mu = jnp.mean(t, axis=-1, keepdims=True)
    tc = t - mu
    var = jnp.mean(tc * tc, axis=-1, keepdims=True)
    return tc * lax.rsqrt(var + LN_EPS) * g + b


def _rows_to_tiles(x):
    pieces = jnp.stack([x[:, s * LANES:(s + 1) * LANES] for s in range(x.shape[1] // LANES)], axis=0)
    return jnp.swapaxes(pieces, 0, 1)


def _tiles_to_rows(t):
    pieces = jnp.swapaxes(t, 0, 1)
    return jnp.concatenate([pieces[s] for s in range(t.shape[1])], axis=1)


def _stream_specs(width, ctx_tiles, tm):
    return [pl.BlockSpec((tm, width), lambda i, *_: (jnp.minimum(i, ctx_tiles - 1), 0)),
            pl.BlockSpec((tm, width), lambda i, *_: (jnp.maximum(i - ctx_tiles, 0), 0))]


def _read_stream(ctx_ref, lat_ref, ctx_tiles):
    tile = lax.broadcasted_iota(jnp.int32, ctx_ref.shape, 0) * 0 + pl.program_id(0)
    return jnp.where(tile < ctx_tiles, ctx_ref[...], lat_ref[...])


def _write_stream(ctx_ref, lat_ref, ctx_tiles, value):
    @pl.when(pl.program_id(0) < ctx_tiles)
    def _():
        ctx_ref[...] = value

    @pl.when(pl.program_id(0) >= ctx_tiles)
    def _():
        lat_ref[...] = value


def _ada_kernel(c_ref, w_ref, b_ref, o_ref):
    c = c_ref[...]
    s = (c * _sigmoid(c)).astype(BF16)
    o_ref[...] = jnp.dot(s, w_ref[...].astype(BF16), preferred_element_type=F32) + b_ref[...]


def _ada_call(cvec, w_ada, b_ada):
    depth = w_ada.shape[0]
    n_out = w_ada.shape[2]
    return pl.pallas_call(
        _ada_kernel,
        grid=(depth, n_out // ADA_COLS),
        in_specs=[
            pl.BlockSpec((MOD_ROWS, D_MODEL), lambda l, j: (0, 0)),
            pl.BlockSpec((None, D_MODEL, ADA_COLS), lambda l, j: (l, 0, j)),
            pl.BlockSpec((None, 1, ADA_COLS), lambda l, j: (l, 0, j)),
        ],
        out_specs=pl.BlockSpec((None, MOD_ROWS, ADA_COLS), lambda l, j: (l, 0, j)),
        out_shape=jax.ShapeDtypeStruct((depth, MOD_ROWS, n_out), F32),
        compiler_params=_params(("arbitrary", "arbitrary")),
        name="adaln",
    )(cvec, w_ada, b_ada.reshape(depth, 1, n_out))


def _in_kernel(split, ctx_tiles, *refs):
    if split:
        x = _read_stream(refs[0], refs[1], ctx_tiles)
        refs = refs[2:]
    else:
        x = refs[0][...]
        refs = refs[1:]
    mod_ref, w_ref, cos_ref, sin_ref, lng_ref, lnb_ref, q_ref, k_ref, v_ref, kv_ref, u_ref, g_ref = refs
    tm = x.shape[0]
    h = x * (1.0 + mod_ref[1:2, :]) + mod_ref[0:1, :]
    hb = h.astype(BF16)

    def z_block(start):
        return jnp.dot(hb, w_ref[:, start:start + V7X_MXU_WIDTH], preferred_element_type=F32)

    cos = cos_ref[...]
    sin = sin_ref[...]
    lane = lax.broadcasted_iota(jnp.int32, (tm, LANES), 1)
    first_of_pair = (lane & (2 * ROPE_PAIRS - 1)) < ROPE_PAIRS
    lower_group = lane < SGU_GROUP_DIM

    def rope(t):
        partner = jnp.where(first_of_pair, pltpu.roll(t, LANES - ROPE_PAIRS, 1), pltpu.roll(t, ROPE_PAIRS, 1))
        return t * cos + partner * sin

    halves = (slice(0, LANES), slice(LANES, 2 * LANES))
    inv_n = 1.0 / SGU_GROUP_DIM
    for j in range(SGU_WIDTH // LANES):
        cols = slice(j * LANES, (j + 1) * LANES)
        if j % 2 == 0:
            z = z_block(OFF_G + j * LANES)
        t = _gelu_tanh(z[:, halves[j % 2]])
        s_lo = jnp.sum(jnp.where(lower_group, t, 0.0), axis=-1, keepdims=True)
        s_hi = jnp.sum(jnp.where(lower_group, 0.0, t), axis=-1, keepdims=True)
        tc = t - jnp.where(lower_group, s_lo, s_hi) * inv_n
        sq = tc * tc
        v_lo = jnp.sum(jnp.where(lower_group, sq, 0.0), axis=-1, keepdims=True)
        v_hi = jnp.sum(jnp.where(lower_group, 0.0, sq), axis=-1, keepdims=True)
        var = jnp.where(lower_group, v_lo, v_hi) * inv_n
        g_ref[:, cols] = (tc * lax.rsqrt(var + LN_EPS) * lng_ref[:, cols] + lnb_ref[:, cols]).astype(BF16)

    for start in range(0, ATTN_WIDTH, V7X_MXU_WIDTH):
        z = z_block(start)
        for half in halves:
            q_ref[:, start + half.start:start + half.stop] = (rope(z[:, half]) * ATTN_SCALE).astype(BF16)
    z = z_block(OFF_K)
    k_ref[...] = rope(z[:, halves[0]]).astype(BF16)
    v_ref[...] = z[:, halves[1]].astype(BF16)
    kv_ref[...] = z
    for start in range(0, SGU_WIDTH, V7X_MXU_WIDTH):
        u_ref[:, start:start + V7X_MXU_WIDTH] = _gelu_tanh(z_block(OFF_U + start))


def _in_call(x, mods, w_in, cos_t, sin_t, ln_g, ln_b, geom):
    n = geom.n_tok
    tm = IN_TILE
    mod_idx = lambda i: geom.mod_index(i, tm)
    rope_idx = lambda i: geom.rope_index(i, tm)
    row = lambda i: (i, 0)
    split = isinstance(x, tuple)
    x_args = list(x) if split else [x]
    ctx_tiles = geom.ctx_tiles(tm)
    x_specs = _stream_specs(D_MODEL, ctx_tiles, tm) if split else [pl.BlockSpec((tm, D_MODEL), row)]
    outs = pl.pallas_call(
        functools.partial(_in_kernel, split, ctx_tiles),
        grid=(n // tm,),
        in_specs=x_specs + [
            pl.BlockSpec((None, N_MODS, D_MODEL), lambda i: (mod_idx(i), 0, 0)),
            pl.BlockSpec((D_MODEL, IN_WIDTH), lambda i: (0, 0)),
            pl.BlockSpec((tm, LANES), lambda i: (rope_idx(i), 0)),
            pl.BlockSpec((tm, LANES), lambda i: (rope_idx(i), 0)),
            pl.BlockSpec((1, SGU_WIDTH), lambda i: (0, 0)),
            pl.BlockSpec((1, SGU_WIDTH), lambda i: (0, 0)),
        ],
        out_specs=[
            pl.BlockSpec((tm, ATTN_WIDTH), row),
            pl.BlockSpec((tm, KV_WIDTH), row),
            pl.BlockSpec((tm, KV_WIDTH), row),
            pl.BlockSpec((tm, 2 * KV_WIDTH), row),
            pl.BlockSpec((tm, SGU_WIDTH), row),
            pl.BlockSpec((tm, SGU_WIDTH), row),
        ],
        out_shape=[
            jax.ShapeDtypeStruct((n, ATTN_WIDTH), BF16),
            jax.ShapeDtypeStruct((n, KV_WIDTH), BF16),
            jax.ShapeDtypeStruct((n, KV_WIDTH), BF16),
            jax.ShapeDtypeStruct((n, 2 * KV_WIDTH), F32),
            jax.ShapeDtypeStruct((n, SGU_WIDTH), F32),
            jax.ShapeDtypeStruct((n, SGU_WIDTH), BF16),
        ],
        compiler_params=_params(("arbitrary",)),
        name="in_proj",
    )(*x_args, mods, w_in, cos_t, sin_t, ln_g, ln_b)
    return outs


def _group_attention(q_ref, rows, hk, k_all, v_all, block_masks, sink_ref):
    m_rows = rows.stop - rows.start
    heads = [hk * Q_PER_KV + gq for gq in range(Q_PER_KV)]
    q = jnp.concatenate([q_ref[rows, h * HEAD_DIM:(h + 1) * HEAD_DIM] for h in heads], axis=0)
    s = lax.dot_general(q, k_all, (((1,), (1,)), ((), ())), preferred_element_type=F32)
    n_blocks = k_all.shape[0] // LANES
    blocks = [s[:, b * LANES:(b + 1) * LANES] for b in range(n_blocks)]
    for b, mask in block_masks.items():
        blocks[b] = jnp.where(mask, blocks[b], NEG_INF)
    head_of_row = lax.broadcasted_iota(jnp.int32, (Q_PER_KV * m_rows, 1), 0) // m_rows
    sink = jnp.zeros((Q_PER_KV * m_rows, 1), F32)
    for gq, h in enumerate(heads):
        sink = jnp.where(head_of_row == gq, sink_ref[h], sink)
    m_el = blocks[0]
    for blk in blocks[1:]:
        m_el = jnp.maximum(m_el, blk)
    m = jnp.maximum(jnp.max(m_el, axis=-1, keepdims=True), sink)
    probs = [jnp.exp(blk - m) for blk in blocks]
    l_el = probs[0]
    for p in probs[1:]:
        l_el = l_el + p
    denom = jnp.sum(l_el, axis=-1, keepdims=True) + jnp.exp(sink - m)
    p_all = jnp.concatenate([p.astype(BF16) for p in probs], axis=1)
    o = jnp.dot(p_all, v_all, preferred_element_type=F32) / denom
    return {h: o[gq * m_rows:(gq + 1) * m_rows] for gq, h in enumerate(heads)}


def _store_heads(o_ref, rows, outs):
    for h0 in range(0, N_Q_HEADS, 2):
        pair = jnp.concatenate([outs[h0], outs[h0 + 1]], axis=1)
        o_ref[rows, h0 * HEAD_DIM:(h0 + 2) * HEAD_DIM] = pair.astype(o_ref.dtype)


def _ctx_attn_kernel(seq_len, sink_ref, q_ref, k_ref, v_ref, o_ref):
    for sq in range(q_ref.shape[0] // seq_len):
        rows = slice(sq * seq_len, (sq + 1) * seq_len)
        outs = {}
        for hk in range(N_KV_HEADS):
            kv_cols = slice(hk * HEAD_DIM, (hk + 1) * HEAD_DIM)
            outs.update(_group_attention(q_ref, rows, hk, k_ref[rows, kv_cols], v_ref[rows, kv_cols], {}, sink_ref))
        _store_heads(o_ref, rows, outs)


def _ctx_attn_call(sink, q, k, v, n_seq, seq_len):
    n = n_seq * seq_len
    per_step = CTX_SEQS_PER_STEP if n_seq % CTX_SEQS_PER_STEP == 0 else 1
    rows = per_step * seq_len
    blk = lambda b: (b, 0)
    return pl.pallas_call(
        functools.partial(_ctx_attn_kernel, seq_len),
        grid=(n_seq // per_step,),
        in_specs=[
            pl.BlockSpec(memory_space=pltpu.SMEM),
            pl.BlockSpec((rows, ATTN_WIDTH), blk),
            pl.BlockSpec((rows, KV_WIDTH), blk),
            pl.BlockSpec((rows, KV_WIDTH), blk),
        ],
        out_specs=pl.BlockSpec((rows, ATTN_WIDTH), blk),
        out_shape=jax.ShapeDtypeStruct((n, ATTN_WIDTH), BF16),
        compiler_params=_params(("arbitrary",)),
        name="ctx_attention",
    )(sink, q, k, v)


def _lat_attn_kernel(sink_ref, q_ref, k_ref, v_ref, ck_ref, cv_ref, o_ref):
    blocks_per_step = q_ref.shape[0] // BLOCK
    nb = k_ref.shape[0] // BLOCK
    past_blocks = ck_ref.shape[0] // LANES
    r = lax.broadcasted_iota(jnp.int32, (Q_PER_KV * BLOCK, BLOCK), 0) & (BLOCK - 1)
    c = lax.broadcasted_iota(jnp.int32, (Q_PER_KV * BLOCK, BLOCK), 1)
    for sub in range(blocks_per_step):
        j = pl.program_id(1) * blocks_per_step + sub
        rows = slice(sub * BLOCK, (sub + 1) * BLOCK)
        mask_prev = c >= r + jnp.where(j > 0, 0, BLOCK)
        mask_next = c <= r - jnp.where(j < nb - 1, 0, BLOCK)
        prev = pl.ds(pl.multiple_of(jnp.maximum(j - 1, 0) * BLOCK, BLOCK), BLOCK)
        cur = pl.ds(pl.multiple_of(j * BLOCK, BLOCK), BLOCK)
        nxt = pl.ds(pl.multiple_of(jnp.minimum(j + 1, nb - 1) * BLOCK, BLOCK), BLOCK)
        outs = {}
        for hk in range(N_KV_HEADS):
            kv_cols = slice(hk * HEAD_DIM, (hk + 1) * HEAD_DIM)
            k_all = jnp.concatenate([ck_ref[:, kv_cols], k_ref[prev, kv_cols], k_ref[cur, kv_cols],
                                     k_ref[nxt, kv_cols]], axis=0)
            v_all = jnp.concatenate([cv_ref[:, kv_cols], v_ref[prev, kv_cols], v_ref[cur, kv_cols],
                                     v_ref[nxt, kv_cols]], axis=0)
            masks = {past_blocks: mask_prev, past_blocks + 2: mask_next}
            outs.update(_group_attention(q_ref, rows, hk, k_all, v_all, masks, sink_ref))
        _store_heads(o_ref, rows, outs)


def _lat_attn_call(sink, q, k, v, cache_k, cache_v, geom):
    qb = LAT_QUERY_ROWS
    steps = geom.dec_seq // qb
    assert WINDOW == BLOCK
    assert geom.n_ctx % geom.dec_seq == 0 and cache_k.shape[1] % LANES == 0
    seq_base = geom.n_ctx // geom.dec_seq
    past = cache_k.shape[1]
    seq_spec = pl.BlockSpec((geom.dec_seq, KV_WIDTH), lambda b, j: (seq_base + b, 0))
    cache_spec = pl.BlockSpec((None, past, KV_WIDTH), lambda b, j: (b, 0, 0))
    return pl.pallas_call(
        _lat_attn_kernel,
        grid=(geom.dec_batch, steps),
        in_specs=[
            pl.BlockSpec(memory_space=pltpu.SMEM),
            pl.BlockSpec((qb, ATTN_WIDTH), lambda b, j: (geom.n_ctx // qb + b * steps + j, 0)),
            seq_spec, seq_spec,
            cache_spec, cache_spec,
        ],
        out_specs=pl.BlockSpec((qb, ATTN_WIDTH), lambda b, j: (b * steps + j, 0)),
        out_shape=jax.ShapeDtypeStruct((geom.dec_batch * geom.dec_seq, ATTN_WIDTH), BF16),
        compiler_params=_params(("arbitrary", "arbitrary")),
        name="lat_attention",
    )(sink, q, k, v, cache_k, cache_v)


def _spatial_gating(u_ref, g_ref, mix_ref, bias_ref, s_ref):
    tm = u_ref.shape[0]
    lane = lax.broadcasted_iota(jnp.int32, (CHUNK, LANES), 1)
    lower_group = lane < SGU_GROUP_DIM
    for ch in range(tm // CHUNK):
        rows = slice(ch * CHUNK, (ch + 1) * CHUNK)
        for p in range(SGU_WIDTH // LANES):
            cols = slice(p * LANES, (p + 1) * LANES)
            g = g_ref[rows, cols]
            zero = jnp.zeros_like(g)
            mixed = (jnp.dot(mix_ref[2 * p], jnp.where(lower_group, g, zero), preferred_element_type=F32)
                     + jnp.dot(mix_ref[2 * p + 1], jnp.where(lower_group, zero, g), preferred_element_type=F32))
            s_ref[rows, cols] = (u_ref[rows, cols] * (mixed + bias_ref[:, cols])).astype(s_ref.dtype)


def _out_kernel(alpha, with_router, split_x, ctx_tiles, a_ctx_ref, a_lat_ref, u_ref, gg_ref, mix_ref, bias_ref,
                *refs):
    s_ref = refs[-1]
    refs = refs[:-1]
    if split_x:
        x = _read_stream(refs[0], refs[1], ctx_tiles)
        refs = refs[2:]
    else:
        x = refs[0][...]
        refs = refs[1:]
    mod_ref, wa_ref, ws_ref, g_ref, b_ref = refs[:5]
    if with_router:
        wr_ref, x1_ref, h_ref, rt_ref = refs[5:]
    else:
        x1_ref, h_ref = refs[5:]
    a = _read_stream(a_ctx_ref, a_lat_ref, ctx_tiles)
    _spatial_gating(u_ref, gg_ref, mix_ref, bias_ref, s_ref)
    y = (jnp.dot(a, wa_ref[...], preferred_element_type=F32)
         + jnp.dot(s_ref[...], ws_ref[...], preferred_element_type=F32))
    t = alpha * x + mod_ref[2:3, :] * y
    x1 = _layer_norm_rows(t, g_ref[...], b_ref[...])
    x1_ref[...] = x1
    h = x1 * (1.0 + mod_ref[4:5, :]) + mod_ref[3:4, :]
    if with_router:
        h_ref[...] = _rows_to_tiles(h)
    else:
        h_ref[...] = h.astype(h_ref.dtype)
    if with_router:
        tm = h.shape[0]
        wr = wr_ref[...]
        wr_hi = wr.astype(BF16)
        wr_lo = (wr - wr_hi.astype(F32)).astype(BF16)
        h_hi = h.astype(BF16)
        h_lo = (h - h_hi.astype(F32)).astype(BF16)
        logits = (jnp.dot(h_hi, wr_hi, preferred_element_type=F32)
                  + jnp.dot(h_lo, wr_hi, preferred_element_type=F32)
                  + jnp.dot(h_hi, wr_lo, preferred_element_type=F32))
        lane = lax.broadcasted_iota(jnp.int32, (tm, LANES), 1).astype(F32)
        neg = jnp.float32(-jnp.inf)
        lg = jnp.where(lane < N_EXPERTS, logits, neg)
        m1 = jnp.max(lg, axis=-1, keepdims=True)
        i1 = jnp.min(jnp.where(lg == m1, lane, float(LANES)), axis=-1, keepdims=True)
        lg2 = jnp.where(lane == i1, neg, lg)
        m2 = jnp.max(lg2, axis=-1, keepdims=True)
        i2 = jnp.min(jnp.where(lg2 == m2, lane, float(LANES)), axis=-1, keepdims=True)
        e2 = jnp.exp(m2 - m1)
        g1 = 1.0 / (1.0 + e2)
        g2 = e2 / (1.0 + e2)
        rt = jnp.where(lane == 0, i1, jnp.where(lane == 1, i2, jnp.where(lane == 2, g1, jnp.where(lane == 3, g2, 0.0))))
        rt_ref[...] = rt


def _out_call(a_ctx, a_lat, u, gg, w_s, bias_full, x, mods, w_o, ln_g, ln_b, w_router, alpha, geom):
    n = geom.n_tok
    with_router = w_router is not None
    tm = ROUTER_TILE if with_router else OUT_TILE
    ctx_tiles = geom.ctx_tiles(tm)
    row = lambda i: (i, 0)
    const = lambda i: (0, 0)
    split_x = isinstance(x, tuple)
    x_args = list(x) if split_x else [x]
    x_specs = _stream_specs(D_MODEL, ctx_tiles, tm) if split_x else [pl.BlockSpec((tm, D_MODEL), row)]
    sgu_specs = [
        pl.BlockSpec((tm, SGU_WIDTH), row),
        pl.BlockSpec((tm, SGU_WIDTH), row),
        pl.BlockSpec((N_SGU_GROUPS, CHUNK, CHUNK), lambda i: (0, 0, 0)),
        pl.BlockSpec((CHUNK, SGU_WIDTH), const),
    ]
    in_specs = _stream_specs(ATTN_WIDTH, ctx_tiles, tm) + sgu_specs + x_specs + [
        pl.BlockSpec((None, N_MODS, D_MODEL), lambda i: (geom.mod_index(i, tm), 0, 0)),
        pl.BlockSpec((ATTN_WIDTH, D_MODEL), const),
        pl.BlockSpec((SGU_WIDTH, D_MODEL), lambda i: (1, 0)),
        pl.BlockSpec((1, D_MODEL), const),
        pl.BlockSpec((1, D_MODEL), const),
    ]
    args = [a_ctx, a_lat, u, gg, w_s, bias_full] + x_args + [mods, w_o, w_o, ln_g, ln_b]
    if with_router:
        h_spec = pl.BlockSpec((tm, D_MODEL // LANES, LANES), lambda i: (i, 0, 0))
        h_shape = jax.ShapeDtypeStruct((n, D_MODEL // LANES, LANES), F32)
    else:
        h_spec = pl.BlockSpec((tm, D_MODEL), row)
        h_shape = jax.ShapeDtypeStruct((n, D_MODEL), BF16)
    out_specs = [pl.BlockSpec((tm, D_MODEL), row), h_spec]
    out_shape = [jax.ShapeDtypeStruct((n, D_MODEL), F32), h_shape]
    if with_router:
        in_specs.append(pl.BlockSpec((D_MODEL, LANES), const))
        args.append(w_router)
        out_specs.append(pl.BlockSpec((tm, LANES), row))
        out_shape.append(jax.ShapeDtypeStruct((n, LANES), F32))
    return pl.pallas_call(
        functools.partial(_out_kernel, alpha, with_router, split_x, ctx_tiles),
        grid=(n // tm,),
        in_specs=in_specs,
        out_specs=out_specs,
        out_shape=out_shape,
        scratch_shapes=[pltpu.VMEM((tm, SGU_WIDTH), BF16)],
        compiler_params=_params(("arbitrary",)),
        name="out_proj_router" if with_router else "out_proj",
    )(*args)


def _swiglu_blocks(width):
    return [slice(s, min(s + V7X_MXU_WIDTH, width)) for s in range(0, width, V7X_MXU_WIDTH)]


def _swiglu_partial(x, wg_ref, wu_ref, wd_ref, side_work=None):
    out = None
    for n, cols in enumerate(_swiglu_blocks(wg_ref.shape[1])):
        a = jnp.dot(x, wg_ref[:, cols].astype(BF16), preferred_element_type=F32)
        if side_work is not None:
            side_work(3 * n)
        b = jnp.dot(x, wu_ref[:, cols].astype(BF16), preferred_element_type=F32)
        if side_work is not None:
            side_work(3 * n + 1)
        mid = ((a * _sigmoid(a)) * b).astype(BF16)
        part = jnp.dot(mid, wd_ref[cols, :].astype(BF16), preferred_element_type=F32)
        if side_work is not None:
            side_work(3 * n + 2)
        out = part if out is None else out + part
    return out


def _ffn_kernel(alpha, h_ref, x1_ref, mod_ref, wg_ref, wu_ref, wd_ref, g_ref, b_ref, o_ref):
    y = _swiglu_partial(h_ref[...], wg_ref, wu_ref, wd_ref)
    t = alpha * x1_ref[...] + mod_ref[5:6, :] * y
    o_ref[...] = _layer_norm_rows(t, g_ref[...], b_ref[...])


def _ffn_chunk(d_ff):
    assert d_ff % (2 * LANES) == 0
    return d_ff // 2


def _ffn_call(h, x1, mods, wg, wu, wd, ln_g, ln_b, alpha, geom):
    n = h.shape[0]
    tm = DENSE_TILE
    d_ff = wg.shape[1]
    row = lambda i: (i, 0)
    const = lambda i: (0, 0)
    resident = pl.Buffered(1)
    return pl.pallas_call(
        functools.partial(_ffn_kernel, alpha),
        grid=(n // tm,),
        in_specs=[
            pl.BlockSpec((tm, D_MODEL), row),
            pl.BlockSpec((tm, D_MODEL), row),
            pl.BlockSpec((None, N_MODS, D_MODEL), lambda i: (geom.mod_index(i, tm), 0, 0)),
            pl.BlockSpec((D_MODEL, d_ff), const, pipeline_mode=resident),
            pl.BlockSpec((D_MODEL, d_ff), const, pipeline_mode=resident),
            pl.BlockSpec((d_ff, D_MODEL), const, pipeline_mode=resident),
            pl.BlockSpec((1, D_MODEL), const),
            pl.BlockSpec((1, D_MODEL), const),
        ],
        out_specs=pl.BlockSpec((tm, D_MODEL), row),
        out_shape=jax.ShapeDtypeStruct((n, D_MODEL), F32),
        compiler_params=_params(("arbitrary",)),
        name="dense_ffn",
    )(h, x1, mods, wg, wu, wd, ln_g, ln_b)


def _moe_kernel(n_chunks, te_ref, nu_ref, src_ref, dst_ref, h_hbm, wg_ref, wu_ref, wd_ref, o_hbm,
                xbuf, xb, obuf, gsem, ssem):
    del te_ref
    tr = xb.shape[0]
    t = pl.program_id(0)
    c = pl.program_id(1)
    n_tiles = pl.num_programs(0)
    valid = t < nu_ref[0]
    slot = t % 2
    other = 1 - slot
    share = tr // n_chunks
    first_row = c * share
    nxt_base = jnp.minimum(t + 1, n_tiles - 1) * tr
    prv_base = jnp.where(t == 0, n_tiles - 1, t - 1) * tr

    def gather_row(base, dst_slot, r, tok=None):
        tok = src_ref[base + r] if tok is None else tok
        return pltpu.make_async_copy(h_hbm.at[tok], xbuf.at[dst_slot, r], gsem.at[dst_slot])

    def scatter_row(base, src_slot, r, dst=None):
        dst = dst_ref[base + r] if dst is None else dst
        return pltpu.make_async_copy(obuf.at[src_slot, r], o_hbm.at[dst], ssem.at[src_slot])

    def wait_gather(dst_slot):
        pltpu.make_async_copy(h_hbm.at[pl.ds(0, tr)], xbuf.at[dst_slot], gsem.at[dst_slot]).wait()

    def wait_scatter(src_slot):
        pltpu.make_async_copy(obuf.at[src_slot], o_hbm.at[pl.ds(0, tr)], ssem.at[src_slot]).wait()

    def looped(n_rows, start_row):
        def body(r, carry):
            start_row(r)
            return carry
        lax.fori_loop(0, n_rows, body, 0, unroll=8)

    @pl.when((t == 0) & (c == 0))
    def _():
        obuf[1] = jnp.zeros(obuf.shape[1:], obuf.dtype)
        looped(tr, lambda r: gather_row(0, 0, r).start())

    n_used = nu_ref[0]

    @pl.when(c == 0)
    def _():
        @pl.when(t <= n_used)
        def _():
            wait_gather(slot)

        @pl.when(t >= 1)
        def _():
            wait_scatter(slot)

    def tile_scatter(base, src_slot):
        return pltpu.make_async_copy(obuf.at[src_slot], o_hbm.at[pl.ds(dst_ref[base], tr)], ssem.at[src_slot])

    def multiply_tile(slot_s, c_s):
        other_s = 1 - slot_s
        if c_s == 0:
            xb[...] = _tiles_to_rows(xbuf[slot_s]).astype(BF16)
        n_groups = 3 * len(_swiglu_blocks(wg_ref.shape[1]))

        def row_dmas(i):
            for r in range(c_s * share + share * i // n_groups, c_s * share + share * (i + 1) // n_groups):
                gather_row(nxt_base, other_s, r).start()
                scatter_row(prv_base, other_s, r).start()

        part = _swiglu_partial(xb[...], wg_ref, wu_ref, wd_ref, row_dmas)
        if c_s == 0:
            obuf[slot_s] = _rows_to_tiles(part)
        else:
            obuf[slot_s] += _rows_to_tiles(part)

    for slot_s in range(2):
        for c_s in range(n_chunks):
            pl.when(valid & (slot == slot_s) & (c == c_s))(functools.partial(multiply_tile, slot_s, c_s))

    @pl.when(jnp.logical_not(valid))
    def _():
        @pl.when(c == 0)
        def _():
            obuf[slot] = jnp.zeros(obuf.shape[1:], obuf.dtype)

        @pl.when(t == n_used)
        def _():
            looped(share, lambda r: scatter_row(prv_base, other, first_row + r).start())

        @pl.when((t > n_used) & (c == 0))
        def _():
            tile_scatter(prv_base, other).start()

    @pl.when((t == n_tiles - 1) & (c == n_chunks - 1))
    def _():
        wait_scatter(other)
        tile_scatter(t * tr, slot).start()
        wait_scatter(slot)


def _moe_call(h, wg, wu, wd, plan):
    tr = FFN_TILE
    d_ff = wg.shape[2]
    fc = _ffn_chunk(d_ff)
    n_chunks = d_ff // fc
    t_max = plan["tile_expert"].shape[0]
    row_tile = h.shape[1:]

    def chunk(t, c, nu):
        return jnp.where(t < nu[0], c, n_chunks - 1)

    grid_spec = pltpu.PrefetchScalarGridSpec(
        num_scalar_prefetch=4,
        grid=(t_max, n_chunks),
        in_specs=[
            pl.BlockSpec(memory_space=pl.ANY),
            pl.BlockSpec((None, D_MODEL, fc), lambda t, c, te, nu, src, dst: (te[t], 0, chunk(t, c, nu))),
            pl.BlockSpec((None, D_MODEL, fc), lambda t, c, te, nu, src, dst: (te[t], 0, chunk(t, c, nu))),
            pl.BlockSpec((None, fc, D_MODEL), lambda t, c, te, nu, src, dst: (te[t], chunk(t, c, nu), 0)),
        ],
        out_specs=pl.BlockSpec(memory_space=pl.ANY),
        scratch_shapes=[
            pltpu.VMEM((2, tr) + row_tile, F32),
            pltpu.VMEM((tr, D_MODEL), BF16),
            pltpu.VMEM((2, tr) + row_tile, F32),
            pltpu.SemaphoreType.DMA((2,)),
            pltpu.SemaphoreType.DMA((2,)),
        ],
    )
    return pl.pallas_call(
        functools.partial(_moe_kernel, n_chunks),
        grid_spec=grid_spec,
        out_shape=jax.ShapeDtypeStruct((t_max * tr,) + row_tile, F32),
        compiler_params=_params(("arbitrary", "arbitrary")),
        name="expert_ffn",
    )(plan["tile_expert"], plan["n_used"], plan["src_token"], plan["dst_row"], h, wg, wu, wd)


def _route_plan(routing, n_tok, tr):
    n_assign = TOP_K * n_tok
    t_max = n_assign // tr + N_EXPERTS
    n_slots = t_max * tr
    n_pad = n_slots - n_assign
    id_bits = (n_slots - 1).bit_length()
    experts = jnp.arange(N_EXPERTS, dtype=jnp.int32)
    e_flat = routing[:, :TOP_K].astype(jnp.int32).T.reshape(-1)
    counts = jnp.sum((e_flat[:, None] == experts[None, :]).astype(jnp.int32), axis=0)
    tiles_e = (counts + tr - 1) // tr
    tile_end = jnp.cumsum(tiles_e)
    n_used = tile_end[-1]
    pad_end = jnp.cumsum(tiles_e * tr - counts)
    pad_ids = jnp.arange(n_pad, dtype=jnp.int32)
    pad_expert = jnp.sum((pad_ids[:, None] >= pad_end[None, :]).astype(jnp.int32), axis=1)
    keys = jnp.concatenate([e_flat * 2, pad_expert * 2 + 1])
    item = jnp.arange(n_slots, dtype=jnp.int32)
    slot_item = jnp.sort((keys << id_bits) | item) & ((1 << id_bits) - 1)
    real = slot_item < n_assign
    src = jnp.where(real, slot_item % n_tok, 0).astype(jnp.int32)
    dst = slot_item.astype(jnp.int32)
    t_ids = jnp.arange(t_max, dtype=jnp.int32)
    te = jnp.sum((t_ids[:, None] >= tile_end[None, :]).astype(jnp.int32), axis=1)
    last_used = jnp.max(jnp.where(tiles_e > 0, experts, 0))
    te = jnp.where(t_ids < n_used, te, last_used).astype(jnp.int32)
    return {"tile_expert": te, "n_used": n_used.reshape(1).astype(jnp.int32), "src_token": src, "dst_row": dst}


def _combine_kernel(alpha, split_ctx_tiles, e1_ref, e2_ref, rt_ref, x1_ref, mod_ref, g_ref, b_ref, *o_refs):
    y = rt_ref[:, 2:3] * _tiles_to_rows(e1_ref[...]) + rt_ref[:, 3:4] * _tiles_to_rows(e2_ref[...])
    t = alpha * x1_ref[...] + mod_ref[5:6, :] * y
    out = _layer_norm_rows(t, g_ref[...], b_ref[...])
    if split_ctx_tiles is None:
        o_refs[0][...] = out
    else:
        _write_stream(o_refs[0], o_refs[1], split_ctx_tiles, out)


def _combine_call(eo, routing, x1, mods, ln_g, ln_b, alpha, geom, split_out):
    n = x1.shape[0]
    tm = COMBINE_TILE
    row = lambda i: (i, 0)
    const = lambda i: (0, 0)
    if split_out:
        out_specs = _stream_specs(D_MODEL, geom.ctx_tiles(tm), tm)
        out_shape = [jax.ShapeDtypeStruct((geom.n_ctx, D_MODEL), F32),
                     jax.ShapeDtypeStruct((n - geom.n_ctx, D_MODEL), F32)]
    else:
        out_specs = pl.BlockSpec((tm, D_MODEL), row)
        out_shape = jax.ShapeDtypeStruct((n, D_MODEL), F32)
    return pl.pallas_call(
        functools.partial(_combine_kernel, alpha, geom.ctx_tiles(tm) if split_out else None),
        grid=(n // tm,),
        in_specs=[
            pl.BlockSpec((tm,) + eo.shape[1:], lambda i: (i, 0, 0)),
            pl.BlockSpec((tm,) + eo.shape[1:], lambda i: (i + n // tm, 0, 0)),
            pl.BlockSpec((tm, LANES), row),
            pl.BlockSpec((tm, D_MODEL), row),
            pl.BlockSpec((None, N_MODS, D_MODEL), lambda i: (geom.mod_index(i, tm), 0, 0)),
            pl.BlockSpec((1, D_MODEL), const),
            pl.BlockSpec((1, D_MODEL), const),
        ],
        out_specs=out_specs,
        out_shape=out_shape,
        compiler_params=_params(("arbitrary",)),
        name="expert_combine",
    )(eo, eo, routing, x1, mods, ln_g, ln_b)


class _Geometry:
    def __init__(self, n_ctx, dec_batch, dec_seq):
        self.n_ctx = n_ctx
        self.dec_batch = dec_batch
        self.dec_seq = dec_seq
        self.n_tok = n_ctx + dec_batch * dec_seq

    def ctx_tiles(self, tm):
        assert self.n_ctx % tm == 0 and self.dec_seq % tm == 0
        return self.n_ctx // tm

    def mod_index(self, i, tm):
        ct = self.ctx_tiles(tm)
        return jnp.where(i < ct, 0, 1 + (i - ct) // (self.dec_seq // tm))

    def rope_index(self, i, tm):
        ct = self.ctx_tiles(tm)
        return jnp.where(i < ct, 0, 1 + (i - ct) % (self.dec_seq // tm))


def _rope_tables(dec_seq):
    pos = jnp.arange(dec_seq, dtype=jnp.int32)
    row = (pos // GRID_W).astype(F32)
    col = (pos % GRID_W).astype(F32)
    inv = ROPE_BASE ** (-jnp.arange(ROPE_PAIRS, dtype=F32) / ROPE_PAIRS)
    ang_r = row[:, None] * inv[None, :]
    ang_c = col[:, None] * inv[None, :]
    cos_h = jnp.concatenate([jnp.cos(ang_r), jnp.cos(ang_r), jnp.cos(ang_c), jnp.cos(ang_c)], axis=-1)
    sin_h = jnp.concatenate([-jnp.sin(ang_r), jnp.sin(ang_r), -jnp.sin(ang_c), jnp.sin(ang_c)], axis=-1)
    reps = LANES // HEAD_DIM
    cos_t = jnp.concatenate([jnp.ones((IN_TILE, LANES), F32), jnp.tile(cos_h, (1, reps))], axis=0)
    sin_t = jnp.concatenate([jnp.zeros((IN_TILE, LANES), F32), jnp.tile(sin_h, (1, reps))], axis=0)
    return cos_t, sin_t


def kernel(x_prompt, x_sample, cache_k, cache_v, c, c_ctx, w_ada, b_ada, w_in, w_o, attn_sink, w_s, b_s, sgu_ln_g, sgu_ln_b, ln1_g, ln1_b, ln2_g, ln2_b, w_ff_gate, w_ff_up, w_ff_down, w_router, w_exp_gate, w_exp_up, w_exp_down):
    batch, seq, d = x_prompt.shape
    dec_batch, dec_seq, _ = x_sample.shape
    depth = w_in.shape[0]
    past = cache_k.shape[2]
    assert d == D_MODEL and dec_batch + 1 <= MOD_ROWS
    n_ctx = batch * seq
    n_lat = dec_batch * dec_seq
    n_tok = n_ctx + n_lat
    geom = _Geometry(n_ctx, dec_batch, dec_seq)
    alpha = float((2 * depth) ** 0.25)

    x = (x_prompt.reshape(n_ctx, d), x_sample.reshape(n_lat, d))
    cvec = jnp.concatenate([c_ctx[None, :], c, jnp.zeros((MOD_ROWS - 1 - dec_batch, d), F32)], axis=0)
    mods_all = _ada_call(cvec, w_ada, b_ada).reshape(depth, MOD_ROWS, N_MODS, d)
    cos_t, sin_t = _rope_tables(dec_seq)

    new_k, new_v = [], []
    for l in range(depth):
        mods = mods_all[l]
        q, k, v, kv32, u, g = _in_call(x, mods, w_in[l].astype(BF16), cos_t, sin_t,
                                       sgu_ln_g[l].reshape(1, SGU_WIDTH), sgu_ln_b[l].reshape(1, SGU_WIDTH), geom)
        new_k.append(kv32[:n_ctx, :KV_WIDTH].reshape(batch, seq, N_KV_HEADS, HEAD_DIM))
        new_v.append(kv32[:n_ctx, KV_WIDTH:].reshape(batch, seq, N_KV_HEADS, HEAD_DIM))
        sink = attn_sink[l]
        a_ctx = _ctx_attn_call(sink, q, k, v, batch, seq)
        a_lat = _lat_attn_call(sink, q, k, v, cache_k[:, l].reshape(dec_batch, past, KV_WIDTH).astype(BF16),
                               cache_v[:, l].reshape(dec_batch, past, KV_WIDTH).astype(BF16), geom)
        bias_full = jnp.repeat(b_s[l].T, SGU_GROUP_DIM, axis=1)
        i = l // 2
        moe = l % 2 == 1
        w_r = None
        if moe:
            w_r = jnp.pad(w_router[i], ((0, 0), (0, LANES - N_EXPERTS)))
        outs = _out_call(a_ctx, a_lat, u, g, w_s[l].astype(BF16), bias_full, x, mods, w_o[l].astype(BF16), ln1_g[l].reshape(1, d), ln1_b[l].reshape(1, d),
                         w_r, alpha, geom)
        ln_g, ln_b = ln2_g[l].reshape(1, d), ln2_b[l].reshape(1, d)
        if moe:
            x1, h, routing = outs
            plan = _route_plan(routing, n_tok, FFN_TILE)
            eo = _moe_call(h, w_exp_gate[i], w_exp_up[i], w_exp_down[i], plan)
            x = _combine_call(eo, routing, x1, mods, ln_g, ln_b, alpha, geom, split_out=l == depth - 1)
        else:
            x1, h = outs
            x = _ffn_call(h, x1, mods, w_ff_gate[i], w_ff_up[i], w_ff_down[i], ln_g, ln_b, alpha, geom)

    if not isinstance(x, (tuple, list)):
        x = (x[:n_ctx], x[n_ctx:])
    y_prompt = x[0].reshape(batch, seq, d)
    y_sample = x[1].reshape(dec_batch, dec_seq, d)
    return (y_prompt, y_sample, jnp.stack(new_k, axis=1), jnp.stack(new_v, axis=1))
```

```python
import functools

import jax
import jax.numpy as jnp
import numpy as np
from jax import lax
from jax.experimental import pallas as pl
from jax.experimental.pallas import tpu as pltpu

F32 = jnp.float32
BF16 = jnp.bfloat16

D_MODEL = 1024
HEAD_DIM = 64
N_Q_HEADS = 8
N_KV_HEADS = 2
Q_PER_KV = N_Q_HEADS // N_KV_HEADS
ATTN_WIDTH = N_Q_HEADS * HEAD_DIM
KV_WIDTH = N_KV_HEADS * HEAD_DIM
SGU_WIDTH = D_MODEL - ATTN_WIDTH
N_SGU_GROUPS = 8
SGU_GROUP_DIM = SGU_WIDTH // N_SGU_GROUPS
CHUNK = 128
BLOCK = 128
WINDOW = 128
GRID_W = 64
IN_WIDTH = ATTN_WIDTH + 2 * KV_WIDTH + 2 * SGU_WIDTH
OFF_K = ATTN_WIDTH
OFF_V = OFF_K + KV_WIDTH
OFF_U = OFF_V + KV_WIDTH
OFF_G = OFF_U + SGU_WIDTH
N_EXPERTS = 8
TOP_K = 2
ROPE_BASE = 10000.0
ROPE_PAIRS = HEAD_DIM // 4
LN_EPS = 1e-5
ATTN_SCALE = HEAD_DIM ** -0.5
NEG_INF = -1e30
N_MODS = 6

LANES = 128
V7X_MXU_WIDTH = 256
V7X_VMEM_LIMIT_BYTES = 56 * 1024 * 1024

IN_TILE = 1024
OUT_TILE = 512
ROUTER_TILE = 256
COMBINE_TILE = 1024
DENSE_TILE = 512
FFN_TILE = 512
LAT_QUERY_ROWS = 1024
CTX_SEQS_PER_STEP = 1
MOD_ROWS = 8
ADA_COLS = 1536


def _params(sem):
    return pltpu.CompilerParams(dimension_semantics=sem, vmem_limit_bytes=V7X_VMEM_LIMIT_BYTES)


def _gelu_tanh(x):
    return x * (0.5 * (1.0 + jnp.tanh(np.float32(np.sqrt(2.0 / np.pi)) * (x + 0.044715 * (x * x * x)))))


def _sigmoid(x):
    return 1.0 / (1.0 + jnp.exp(-x))


def _layer_norm_rows(t, g, b):
    mu = jnp.mean(t, axis=-1, keepdims=True)
    tc = t - mu
    var = jnp.mean(tc * tc, axis=-1, keepdims=True)
    return tc * lax.rsqrt(var + LN_EPS) * g + b


def _rows_to_tiles(x):
    pieces = jnp.stack([x[:, s * LANES:(s + 1) * LANES] for s in range(x.shape[1] // LANES)], axis=0)
    return jnp.swapaxes(pieces, 0, 1)


def _tiles_to_rows(t):
    pieces = jnp.swapaxes(t, 0, 1)
    return jnp.concatenate([pieces[s] for s in range(t.shape[1])], axis=1)


def _stream_specs(width, ctx_tiles, tm):
    return [pl.BlockSpec((tm, width), lambda i, *_: (jnp.minimum(i, ctx_tiles - 1), 0)),
            pl.BlockSpec((tm, width), lambda i, *_: (jnp.maximum(i - ctx_tiles, 0), 0))]


def _read_stream(ctx_ref, lat_ref, ctx_tiles):
    tile = lax.broadcasted_iota(jnp.int32, ctx_ref.shape, 0) * 0 + pl.program_id(0)
    return jnp.where(tile < ctx_tiles, ctx_ref[...], lat_ref[...])


def _write_stream(ctx_ref, lat_ref, ctx_tiles, value):
    @pl.when(pl.program_id(0) < ctx_tiles)
    def _():
        ctx_ref[...] = value

    @pl.when(pl.program_id(0) >= ctx_tiles)
    def _():
        lat_ref[...] = value


def _ada_kernel(c_ref, w_ref, b_ref, o_ref):
    c = c_ref[...]
    s = (c * _sigmoid(c)).astype(BF16)
    o_ref[...] = jnp.dot(s, w_ref[...].astype(BF16), preferred_element_type=F32) + b_ref[...]


def _ada_call(cvec, w_ada, b_ada):
    depth = w_ada.shape[0]
    n_out = w_ada.shape[2]
    return pl.pallas_call(
        _ada_kernel,
        grid=(depth, n_out // ADA_COLS),
        in_specs=[
            pl.BlockSpec((MOD_ROWS, D_MODEL), lambda l, j: (0, 0)),
            pl.BlockSpec((None, D_MODEL, ADA_COLS), lambda l, j: (l, 0, j)),
            pl.BlockSpec((None, 1, ADA_COLS), lambda l, j: (l, 0, j)),
        ],
        out_specs=pl.BlockSpec((None, MOD_ROWS, ADA_COLS), lambda l, j: (l, 0, j)),
        out_shape=jax.ShapeDtypeStruct((depth, MOD_ROWS, n_out), F32),
        compiler_params=_params(("arbitrary", "arbitrary")),
        name="adaln",
    )(cvec, w_ada, b_ada.reshape(depth, 1, n_out))


def _in_kernel(split, ctx_tiles, *refs):
    if split:
        x = _read_stream(refs[0], refs[1], ctx_tiles)
        refs = refs[2:]
    else:
        x = refs[0][...]
        refs = refs[1:]
    mod_ref, w_ref, cos_ref, sin_ref, lng_ref, lnb_ref, q_ref, k_ref, v_ref, kv_ref, u_ref, g_ref = refs
    tm = x.shape[0]
    h = x * (1.0 + mod_ref[1:2, :]) + mod_ref[0:1, :]
    hb = h.astype(BF16)

    def z_block(start):
        return jnp.dot(hb, w_ref[:, start:start + V7X_MXU_WIDTH], preferred_element_type=F32)

    cos = cos_ref[...]
    sin = sin_ref[...]
    lane = lax.broadcasted_iota(jnp.int32, (tm, LANES), 1)
    first_of_pair = (lane & (2 * ROPE_PAIRS - 1)) < ROPE_PAIRS
    lower_group = lane < SGU_GROUP_DIM

    def rope(t):
        partner = jnp.where(first_of_pair, pltpu.roll(t, LANES - ROPE_PAIRS, 1), pltpu.roll(t, ROPE_PAIRS, 1))
        return t * cos + partner * sin

    halves = (slice(0, LANES), slice(LANES, 2 * LANES))
    inv_n = 1.0 / SGU_GROUP_DIM
    for j in range(SGU_WIDTH // LANES):
        cols = slice(j * LANES, (j + 1) * LANES)
        if j % 2 == 0:
            z = z_block(OFF_G + j * LANES)
        t = _gelu_tanh(z[:, halves[j % 2]])
        s_lo = jnp.sum(jnp.where(lower_group, t, 0.0), axis=-1, keepdims=True)
        s_hi = jnp.sum(jnp.where(lower_group, 0.0, t), axis=-1, keepdims=True)
        tc = t - jnp.where(lower_group, s_lo, s_hi) * inv_n
        sq = tc * tc
        v_lo = jnp.sum(jnp.where(lower_group, sq, 0.0), axis=-1, keepdims=True)
        v_hi = jnp.sum(jnp.where(lower_group, 0.0, sq), axis=-1, keepdims=True)
        var = jnp.where(lower_group, v_lo, v_hi) * inv_n
        g_ref[:, cols] = (tc * lax.rsqrt(var + LN_EPS) * lng_ref[:, cols] + lnb_ref[:, cols]).astype(BF16)

    for start in range(0, ATTN_WIDTH, V7X_MXU_WIDTH):
        z = z_block(start)
        for half in halves:
            q_ref[:, start + half.start:start + half.stop] = (rope(z[:, half]) * ATTN_SCALE).astype(BF16)
    z = z_block(OFF_K)
    k_ref[...] = rope(z[:, halves[0]]).astype(BF16)
    v_ref[...] = z[:, halves[1]].astype(BF16)
    kv_ref[...] = z
    for start in range(0, SGU_WIDTH, V7X_MXU_WIDTH):
        u_ref[:, start:start + V7X_MXU_WIDTH] = _gelu_tanh(z_block(OFF_U + start))


def _in_call(x, mods, w_in, cos_t, sin_t, ln_g, ln_b, geom):
    n = geom.n_tok
    tm = IN_TILE
    mod_idx = lambda i: geom.mod_index(i, tm)
    rope_idx = lambda i: geom.rope_index(i, tm)
    row = lambda i: (i, 0)
    split = isinstance(x, tuple)
    x_args = list(x) if split else [x]
    ctx_tiles = geom.ctx_tiles(tm)
    x_specs = _stream_specs(D_MODEL, ctx_tiles, tm) if split else [pl.BlockSpec((tm, D_MODEL), row)]
    outs = pl.pallas_call(
        functools.partial(_in_kernel, split, ctx_tiles),
        grid=(n // tm,),
        in_specs=x_specs + [
            pl.BlockSpec((None, N_MODS, D_MODEL), lambda i: (mod_idx(i), 0, 0)),
            pl.BlockSpec((D_MODEL, IN_WIDTH), lambda i: (0, 0)),
            pl.BlockSpec((tm, LANES), lambda i: (rope_idx(i), 0)),
            pl.BlockSpec((tm, LANES), lambda i: (rope_idx(i), 0)),
            pl.BlockSpec((1, SGU_WIDTH), lambda i: (0, 0)),
            pl.BlockSpec((1, SGU_WIDTH), lambda i: (0, 0)),
        ],
        out_specs=[
            pl.BlockSpec((tm, ATTN_WIDTH), row),
            pl.BlockSpec((tm, KV_WIDTH), row),
            pl.BlockSpec((tm, KV_WIDTH), row),
            pl.BlockSpec((tm, 2 * KV_WIDTH), row),
            pl.BlockSpec((tm, SGU_WIDTH), row),
            pl.BlockSpec((tm, SGU_WIDTH), row),
        ],
        out_shape=[
            jax.ShapeDtypeStruct((n, ATTN_WIDTH), BF16),
            jax.ShapeDtypeStruct((n, KV_WIDTH), BF16),
            jax.ShapeDtypeStruct((n, KV_WIDTH), BF16),
            jax.ShapeDtypeStruct((n, 2 * KV_WIDTH), F32),
            jax.ShapeDtypeStruct((n, SGU_WIDTH), F32),
            jax.ShapeDtypeStruct((n, SGU_WIDTH), BF16),
        ],
        compiler_params=_params(("arbitrary",)),
        name="in_proj",
    )(*x_args, mods, w_in, cos_t, sin_t, ln_g, ln_b)
    return outs


def _group_attention(q_ref, rows, hk, k_all, v_all, block_masks, sink_ref):
    m_rows = rows.stop - rows.start
    heads = [hk * Q_PER_KV + gq for gq in range(Q_PER_KV)]
    q = jnp.concatenate([q_ref[rows, h * HEAD_DIM:(h + 1) * HEAD_DIM] for h in heads], axis=0)
    s = lax.dot_general(q, k_all, (((1,), (1,)), ((), ())), preferred_element_type=F32)
    n_blocks = k_all.shape[0] // LANES
    blocks = [s[:, b * LANES:(b + 1) * LANES] for b in range(n_blocks)]
    for b, mask in block_masks.items():
        blocks[b] = jnp.where(mask, blocks[b], NEG_INF)
    head_of_row = lax.broadcasted_iota(jnp.int32, (Q_PER_KV * m_rows, 1), 0) // m_rows
    sink = jnp.zeros((Q_PER_KV * m_rows, 1), F32)
    for gq, h in enumerate(heads):
        sink = jnp.where(head_of_row == gq, sink_ref[h], sink)
    m_el = blocks[0]
    for blk in blocks[1:]:
        m_el = jnp.maximum(m_el, blk)
    m = jnp.maximum(jnp.max(m_el, axis=-1, keepdims=True), sink)
    probs = [jnp.exp(blk - m) for blk in blocks]
    l_el = probs[0]
    for p in probs[1:]:
        l_el = l_el + p
    denom = jnp.sum(l_el, axis=-1, keepdims=True) + jnp.exp(sink - m)
    p_all = jnp.concatenate([p.astype(BF16) for p in probs], axis=1)
    o = jnp.dot(p_all, v_all, preferred_element_type=F32) / denom
    return {h: o[gq * m_rows:(gq + 1) * m_rows] for gq, h in enumerate(heads)}


def _store_heads(o_ref, rows, outs):
    for h0 in range(0, N_Q_HEADS, 2):
        pair = jnp.concatenate([outs[h0], outs[h0 + 1]], axis=1)
        o_ref[rows, h0 * HEAD_DIM:(h0 + 2) * HEAD_DIM] = pair.astype(o_ref.dtype)


def _ctx_attn_kernel(seq_len, sink_ref, q_ref, k_ref, v_ref, o_ref):
    for sq in range(q_ref.shape[0] // seq_len):
        rows = slice(sq * seq_len, (sq + 1) * seq_len)
        outs = {}
        for hk in range(N_KV_HEADS):
            kv_cols = slice(hk * HEAD_DIM, (hk + 1) * HEAD_DIM)
            outs.update(_group_attention(q_ref, rows, hk, k_ref[rows, kv_cols], v_ref[rows, kv_cols], {}, sink_ref))
        _store_heads(o_ref, rows, outs)


def _ctx_attn_call(sink, q, k, v, n_seq, seq_len):
    n = n_seq * seq_len
    per_step = CTX_SEQS_PER_STEP if n_seq % CTX_SEQS_PER_STEP == 0 else 1
    rows = per_step * seq_len
    blk = lambda b: (b, 0)
    return pl.pallas_call(
        functools.partial(_ctx_attn_kernel, seq_len),
        grid=(n_seq // per_step,),
        in_specs=[
            pl.BlockSpec(memory_space=pltpu.SMEM),
            pl.BlockSpec((rows, ATTN_WIDTH), blk),
            pl.BlockSpec((rows, KV_WIDTH), blk),
            pl.BlockSpec((rows, KV_WIDTH), blk),
        ],
        out_specs=pl.BlockSpec((rows, ATTN_WIDTH), blk),
        out_shape=jax.ShapeDtypeStruct((n, ATTN_WIDTH), BF16),
        compiler_params=_params(("arbitrary",)),
        name="ctx_attention",
    )(sink, q, k, v)


def _lat_attn_kernel(sink_ref, q_ref, k_ref, v_ref, ck_ref, cv_ref, o_ref):
    blocks_per_step = q_ref.shape[0] // BLOCK
    nb = k_ref.shape[0] // BLOCK
    past_blocks = ck_ref.shape[0] // LANES
    r = lax.broadcasted_iota(jnp.int32, (Q_PER_KV * BLOCK, BLOCK), 0) & (BLOCK - 1)
    c = lax.broadcasted_iota(jnp.int32, (Q_PER_KV * BLOCK, BLOCK), 1)
    for sub in range(blocks_per_step):
        j = pl.program_id(1) * blocks_per_step + sub
        rows = slice(sub * BLOCK, (sub + 1) * BLOCK)
        mask_prev = c >= r + jnp.where(j > 0, 0, BLOCK)
        mask_next = c <= r - jnp.where(j < nb - 1, 0, BLOCK)
        prev = pl.ds(pl.multiple_of(jnp.maximum(j - 1, 0) * BLOCK, BLOCK), BLOCK)
        cur = pl.ds(pl.multiple_of(j * BLOCK, BLOCK), BLOCK)
        nxt = pl.ds(pl.multiple_of(jnp.minimum(j + 1, nb - 1) * BLOCK, BLOCK), BLOCK)
        outs = {}
        for hk in range(N_KV_HEADS):
            kv_cols = slice(hk * HEAD_DIM, (hk + 1) * HEAD_DIM)
            k_all = jnp.concatenate([ck_ref[:, kv_cols], k_ref[prev, kv_cols], k_ref[cur, kv_cols],
                                     k_ref[nxt, kv_cols]], axis=0)
            v_all = jnp.concatenate([cv_ref[:, kv_cols], v_ref[prev, kv_cols], v_ref[cur, kv_cols],
                                     v_ref[nxt, kv_cols]], axis=0)
            masks = {past_blocks: mask_prev, past_blocks + 2: mask_next}
            outs.update(_group_attention(q_ref, rows, hk, k_all, v_all, masks, sink_ref))
        _store_heads(o_ref, rows, outs)


def _lat_attn_call(sink, q, k, v, cache_k, cache_v, geom):
    qb = LAT_QUERY_ROWS
    steps = geom.dec_seq // qb
    assert WINDOW == BLOCK
    assert geom.n_ctx % geom.dec_seq == 0 and cache_k.shape[1] % LANES == 0
    seq_base = geom.n_ctx // geom.dec_seq
    past = cache_k.shape[1]
    seq_spec = pl.BlockSpec((geom.dec_seq, KV_WIDTH), lambda b, j: (seq_base + b, 0))
    cache_spec = pl.BlockSpec((None, past, KV_WIDTH), lambda b, j: (b, 0, 0))
    return pl.pallas_call(
        _lat_attn_kernel,
        grid=(geom.dec_batch, steps),
        in_specs=[
            pl.BlockSpec(memory_space=pltpu.SMEM),
            pl.BlockSpec((qb, ATTN_WIDTH), lambda b, j: (geom.n_ctx // qb + b * steps + j, 0)),
            seq_spec, seq_spec,
            cache_spec, cache_spec,
        ],
        out_specs=pl.BlockSpec((qb, ATTN_WIDTH), lambda b, j: (b * steps + j, 0)),
        out_shape=jax.ShapeDtypeStruct((geom.dec_batch * geom.dec_seq, ATTN_WIDTH), BF16),
        compiler_params=_params(("arbitrary", "arbitrary")),
        name="lat_attention",
    )(sink, q, k, v, cache_k, cache_v)


def _spatial_gating(u_ref, g_ref, mix_ref, bias_ref, s_ref):
    tm = u_ref.shape[0]
    lane = lax.broadcasted_iota(jnp.int32, (CHUNK, LANES), 1)
    lower_group = lane < SGU_GROUP_DIM
    for ch in range(tm // CHUNK):
        rows = slice(ch * CHUNK, (ch + 1) * CHUNK)
        for p in range(SGU_WIDTH // LANES):
            cols = slice(p * LANES, (p + 1) * LANES)
            g = g_ref[rows, cols]
            zero = jnp.zeros_like(g)
            mixed = (jnp.dot(mix_ref[2 * p], jnp.where(lower_group, g, zero), preferred_element_type=F32)
                     + jnp.dot(mix_ref[2 * p + 1], jnp.where(lower_group, zero, g), preferred_element_type=F32))
            s_ref[rows, cols] = (u_ref[rows, cols] * (mixed + bias_ref[:, cols])).astype(s_ref.dtype)


def _out_kernel(alpha, with_router, split_x, ctx_tiles, a_ctx_ref, a_lat_ref, u_ref, gg_ref, mix_ref, bias_ref,
                *refs):
    s_ref = refs[-1]
    refs = refs[:-1]
    if split_x:
        x = _read_stream(refs[0], refs[1], ctx_tiles)
        refs = refs[2:]
    else:
        x = refs[0][...]
        refs = refs[1:]
    mod_ref, wa_ref, ws_ref, g_ref, b_ref = refs[:5]
    if with_router:
        wr_ref, x1_ref, h_ref, rt_ref = refs[5:]
    else:
        x1_ref, h_ref = refs[5:]
    a = _read_stream(a_ctx_ref, a_lat_ref, ctx_tiles)
    _spatial_gating(u_ref, gg_ref, mix_ref, bias_ref, s_ref)
    y = (jnp.dot(a, wa_ref[...], preferred_element_type=F32)
         + jnp.dot(s_ref[...], ws_ref[...], preferred_element_type=F32))
    t = alpha * x + mod_ref[2:3, :] * y
    x1 = _layer_norm_rows(t, g_ref[...], b_ref[...])
    x1_ref[...] = x1
    h = x1 * (1.0 + mod_ref[4:5, :]) + mod_ref[3:4, :]
    if with_router:
        h_ref[...] = _rows_to_tiles(h)
    else:
        h_ref[...] = h.astype(h_ref.dtype)
    if with_router:
        tm = h.shape[0]
        wr = wr_ref[...]
        wr_hi = wr.astype(BF16)
        wr_lo = (wr - wr_hi.astype(F32)).astype(BF16)
        h_hi = h.astype(BF16)
        h_lo = (h - h_hi.astype(F32)).astype(BF16)
        logits = (jnp.dot(h_hi, wr_hi, preferred_element_type=F32)
                  + jnp.dot(h_lo, wr_hi, preferred_element_type=F32)
                  + jnp.dot(h_hi, wr_lo, preferred_element_type=F32))
        lane = lax.broadcasted_iota(jnp.int32, (tm, LANES), 1).astype(F32)
        neg = jnp.float32(-jnp.inf)
        lg = jnp.where(lane < N_EXPERTS, logits, neg)
        m1 = jnp.max(lg, axis=-1, keepdims=True)
        i1 = jnp.min(jnp.where(lg == m1, lane, float(LANES)), axis=-1, keepdims=True)
        lg2 = jnp.where(lane == i1, neg, lg)
        m2 = jnp.max(lg2, axis=-1, keepdims=True)
        i2 = jnp.min(jnp.where(lg2 == m2, lane, float(LANES)), axis=-1, keepdims=True)
        e2 = jnp.exp(m2 - m1)
        g1 = 1.0 / (1.0 + e2)
        g2 = e2 / (1.0 + e2)
        rt = jnp.where(lane == 0, i1, jnp.where(lane == 1, i2, jnp.where(lane == 2, g1, jnp.where(lane == 3, g2, 0.0))))
        rt_ref[...] = rt


def _out_call(a_ctx, a_lat, u, gg, w_s, bias_full, x, mods, w_o, ln_g, ln_b, w_router, alpha, geom):
    n = geom.n_tok
    with_router = w_router is not None
    tm = ROUTER_TILE if with_router else OUT_TILE
    ctx_tiles = geom.ctx_tiles(tm)
    row = lambda i: (i, 0)
    const = lambda i: (0, 0)
    split_x = isinstance(x, tuple)
    x_args = list(x) if split_x else [x]
    x_specs = _stream_specs(D_MODEL, ctx_tiles, tm) if split_x else [pl.BlockSpec((tm, D_MODEL), row)]
    sgu_specs = [
        pl.BlockSpec((tm, SGU_WIDTH), row),
        pl.BlockSpec((tm, SGU_WIDTH), row),
        pl.BlockSpec((N_SGU_GROUPS, CHUNK, CHUNK), lambda i: (0, 0, 0)),
        pl.BlockSpec((CHUNK, SGU_WIDTH), const),
    ]
    in_specs = _stream_specs(ATTN_WIDTH, ctx_tiles, tm) + sgu_specs + x_specs + [
        pl.BlockSpec((None, N_MODS, D_MODEL), lambda i: (geom.mod_index(i, tm), 0, 0)),
        pl.BlockSpec((ATTN_WIDTH, D_MODEL), const),
        pl.BlockSpec((SGU_WIDTH, D_MODEL), lambda i: (1, 0)),
        pl.BlockSpec((1, D_MODEL), const),
        pl.BlockSpec((1, D_MODEL), const),
    ]
    args = [a_ctx, a_lat, u, gg, w_s, bias_full] + x_args + [mods, w_o, w_o, ln_g, ln_b]
    if with_router:
        h_spec = pl.BlockSpec((tm, D_MODEL // LANES, LANES), lambda i: (i, 0, 0))
        h_shape = jax.ShapeDtypeStruct((n, D_MODEL // LANES, LANES), F32)
    else:
        h_spec = pl.BlockSpec((tm, D_MODEL), row)
        h_shape = jax.ShapeDtypeStruct((n, D_MODEL), BF16)
    out_specs = [pl.BlockSpec((tm, D_MODEL), row), h_spec]
    out_shape = [jax.ShapeDtypeStruct((n, D_MODEL), F32), h_shape]
    if with_router:
        in_specs.append(pl.BlockSpec((D_MODEL, LANES), const))
        args.append(w_router)
        out_specs.append(pl.BlockSpec((tm, LANES), row))
        out_shape.append(jax.ShapeDtypeStruct((n, LANES), F32))
    return pl.pallas_call(
        functools.partial(_out_kernel, alpha, with_router, split_x, ctx_tiles),
        grid=(n // tm,),
        in_specs=in_specs,
        out_specs=out_specs,
        out_shape=out_shape,
        scratch_shapes=[pltpu.VMEM((tm, SGU_WIDTH), BF16)],
        compiler_params=_params(("arbitrary",)),
        name="out_proj_router" if with_router else "out_proj",
    )(*args)


def _swiglu_blocks(width):
    return [slice(s, min(s + V7X_MXU_WIDTH, width)) for s in range(0, width, V7X_MXU_WIDTH)]


def _swiglu_partial(x, wg_ref, wu_ref, wd_ref, side_work=None):
    out = None
    for n, cols in enumerate(_swiglu_blocks(wg_ref.shape[1])):
        a = jnp.dot(x, wg_ref[:, cols].astype(BF16), preferred_element_type=F32)
        if side_work is not None:
            side_work(3 * n)
        b = jnp.dot(x, wu_ref[:, cols].astype(BF16), preferred_element_type=F32)
        if side_work is not None:
            side_work(3 * n + 1)
        mid = ((a * _sigmoid(a)) * b).astype(BF16)
        part = jnp.dot(mid, wd_ref[cols, :].astype(BF16), preferred_element_type=F32)
        if side_work is not None:
            side_work(3 * n + 2)
        out = part if out is None else out + part
    return out


def _ffn_kernel(alpha, h_ref, x1_ref, mod_ref, wg_ref, wu_ref, wd_ref, g_ref, b_ref, o_ref):
    y = _swiglu_partial(h_ref[...], wg_ref, wu_ref, wd_ref)
    t = alpha * x1_ref[...] + mod_ref[5:6, :] * y
    o_ref[...] = _layer_norm_rows(t, g_ref[...], b_ref[...])


def _ffn_chunk(d_ff):
    assert d_ff % (2 * LANES) == 0
    return d_ff // 2


def _ffn_call(h, x1, mods, wg, wu, wd, ln_g, ln_b, alpha, geom):
    n = h.shape[0]
    tm = DENSE_TILE
    d_ff = wg.shape[1]
    row = lambda i: (i, 0)
    const = lambda i: (0, 0)
    resident = pl.Buffered(1)
    return pl.pallas_call(
        functools.partial(_ffn_kernel, alpha),
        grid=(n // tm,),
        in_specs=[
            pl.BlockSpec((tm, D_MODEL), row),
            pl.BlockSpec((tm, D_MODEL), row),
            pl.BlockSpec((None, N_MODS, D_MODEL), lambda i: (geom.mod_index(i, tm), 0, 0)),
            pl.BlockSpec((D_MODEL, d_ff), const, pipeline_mode=resident),
            pl.BlockSpec((D_MODEL, d_ff), const, pipeline_mode=resident),
            pl.BlockSpec((d_ff, D_MODEL), const, pipeline_mode=resident),
            pl.BlockSpec((1, D_MODEL), const),
            pl.BlockSpec((1, D_MODEL), const),
        ],
        out_specs=pl.BlockSpec((tm, D_MODEL), row),
        out_shape=jax.ShapeDtypeStruct((n, D_MODEL), F32),
        compiler_params=_params(("arbitrary",)),
        name="dense_ffn",
    )(h, x1, mods, wg, wu, wd, ln_g, ln_b)


def _moe_kernel(n_chunks, te_ref, nu_ref, src_ref, dst_ref, h_hbm, wg_ref, wu_ref, wd_ref, o_hbm,
                xbuf, xb, obuf, gsem, ssem):
    del te_ref
    tr = xb.shape[0]
    t = pl.program_id(0)
    c = pl.program_id(1)
    n_tiles = pl.num_programs(0)
    valid = t < nu_ref[0]
    slot = t % 2
    other = 1 - slot
    share = tr // n_chunks
    first_row = c * share
    nxt_base = jnp.minimum(t + 1, n_tiles - 1) * tr
    prv_base = jnp.where(t == 0, n_tiles - 1, t - 1) * tr

    def gather_row(base, dst_slot, r, tok=None):
        tok = src_ref[base + r] if tok is None else tok
        return pltpu.make_async_copy(h_hbm.at[tok], xbuf.at[dst_slot, r], gsem.at[dst_slot])

    def scatter_row(base, src_slot, r, dst=None):
        dst = dst_ref[base + r] if dst is None else dst
        return pltpu.make_async_copy(obuf.at[src_slot, r], o_hbm.at[dst], ssem.at[src_slot])

    def wait_gather(dst_slot):
        pltpu.make_async_copy(h_hbm.at[pl.ds(0, tr)], xbuf.at[dst_slot], gsem.at[dst_slot]).wait()

    def wait_scatter(src_slot):
        pltpu.make_async_copy(obuf.at[src_slot], o_hbm.at[pl.ds(0, tr)], ssem.at[src_slot]).wait()

    def looped(n_rows, start_row):
        def body(r, carry):
            start_row(r)
            return carry
        lax.fori_loop(0, n_rows, body, 0, unroll=8)

    @pl.when((t == 0) & (c == 0))
    def _():
        obuf[1] = jnp.zeros(obuf.shape[1:], obuf.dtype)
        looped(tr, lambda r: gather_row(0, 0, r).start())

    n_used = nu_ref[0]

    @pl.when(c == 0)
    def _():
        @pl.when(t <= n_used)
        def _():
            wait_gather(slot)

        @pl.when(t >= 1)
        def _():
            wait_scatter(slot)

    def tile_scatter(base, src_slot):
        return pltpu.make_async_copy(obuf.at[src_slot], o_hbm.at[pl.ds(dst_ref[base], tr)], ssem.at[src_slot])

    def multiply_tile(slot_s, c_s):
        other_s = 1 - slot_s
        if c_s == 0:
            xb[...] = _tiles_to_rows(xbuf[slot_s]).astype(BF16)
        n_groups = 3 * len(_swiglu_blocks(wg_ref.shape[1]))

        def row_dmas(i):
            for r in range(c_s * share + share * i // n_groups, c_s * share + share * (i + 1) // n_groups):
                gather_row(nxt_base, other_s, r).start(priority=r % 2)
                scatter_row(prv_base, other_s, r).start(priority=r % 2)

        part = _swiglu_partial(xb[...], wg_ref, wu_ref, wd_ref, row_dmas)
        if c_s == 0:
            obuf[slot_s] = _rows_to_tiles(part)
        else:
            obuf[slot_s] += _rows_to_tiles(part)

    for slot_s in range(2):
        for c_s in range(n_chunks):
            pl.when(valid & (slot == slot_s) & (c == c_s))(functools.partial(multiply_tile, slot_s, c_s))

    @pl.when(jnp.logical_not(valid))
    def _():
        @pl.when(c == 0)
        def _():
            obuf[slot] = jnp.zeros(obuf.shape[1:], obuf.dtype)

        @pl.when(t == n_used)
        def _():
            looped(share, lambda r: scatter_row(prv_base, other, first_row + r).start())

        @pl.when((t > n_used) & (c == 0))
        def _():
            tile_scatter(prv_base, other).start()

    @pl.when((t == n_tiles - 1) & (c == n_chunks - 1))
    def _():
        wait_scatter(other)
        tile_scatter(t * tr, slot).start()
        wait_scatter(slot)


def _moe_call(h, wg, wu, wd, plan):
    tr = FFN_TILE
    d_ff = wg.shape[2]
    fc = _ffn_chunk(d_ff)
    n_chunks = d_ff // fc
    t_max = plan["tile_expert"].shape[0]
    row_tile = h.shape[1:]

    def chunk(t, c, nu):
        return jnp.where(t < nu[0], c, n_chunks - 1)

    grid_spec = pltpu.PrefetchScalarGridSpec(
        num_scalar_prefetch=4,
        grid=(t_max, n_chunks),
        in_specs=[
            pl.BlockSpec(memory_space=pl.ANY),
            pl.BlockSpec((None, D_MODEL, fc), lambda t, c, te, nu, src, dst: (te[t], 0, chunk(t, c, nu))),
            pl.BlockSpec((None, D_MODEL, fc), lambda t, c, te, nu, src, dst: (te[t], 0, chunk(t, c, nu))),
            pl.BlockSpec((None, fc, D_MODEL), lambda t, c, te, nu, src, dst: (te[t], chunk(t, c, nu), 0)),
        ],
        out_specs=pl.BlockSpec(memory_space=pl.ANY),
        scratch_shapes=[
            pltpu.VMEM((2, tr) + row_tile, F32),
            pltpu.VMEM((tr, D_MODEL), BF16),
            pltpu.VMEM((2, tr) + row_tile, F32),
            pltpu.SemaphoreType.DMA((2,)),
            pltpu.SemaphoreType.DMA((2,)),
        ],
    )
    return pl.pallas_call(
        functools.partial(_moe_kernel, n_chunks),
        grid_spec=grid_spec,
        out_shape=jax.ShapeDtypeStruct((t_max * tr,) + row_tile, F32),
        compiler_params=_params(("arbitrary", "arbitrary")),
        name="expert_ffn",
    )(plan["tile_expert"], plan["n_used"], plan["src_token"], plan["dst_row"], h, wg, wu, wd)


def _route_plan(routing, n_tok, tr):
    n_assign = TOP_K * n_tok
    t_max = n_assign // tr + N_EXPERTS
    n_slots = t_max * tr
    n_pad = n_slots - n_assign
    id_bits = (n_slots - 1).bit_length()
    experts = jnp.arange(N_EXPERTS, dtype=jnp.int32)
    e_flat = routing[:, :TOP_K].astype(jnp.int32).T.reshape(-1)
    counts = jnp.sum((e_flat[:, None] == experts[None, :]).astype(jnp.int32), axis=0)
    tiles_e = (counts + tr - 1) // tr
    tile_end = jnp.cumsum(tiles_e)
    n_used = tile_end[-1]
    pad_end = jnp.cumsum(tiles_e * tr - counts)
    pad_ids = jnp.arange(n_pad, dtype=jnp.int32)
    pad_expert = jnp.sum((pad_ids[:, None] >= pad_end[None, :]).astype(jnp.int32), axis=1)
    keys = jnp.concatenate([e_flat * 2, pad_expert * 2 + 1])
    item = jnp.arange(n_slots, dtype=jnp.int32)
    slot_item = jnp.sort((keys << id_bits) | item) & ((1 << id_bits) - 1)
    real = slot_item < n_assign
    src = jnp.where(real, slot_item % n_tok, 0).astype(jnp.int32)
    dst = slot_item.astype(jnp.int32)
    t_ids = jnp.arange(t_max, dtype=jnp.int32)
    te = jnp.sum((t_ids[:, None] >= tile_end[None, :]).astype(jnp.int32), axis=1)
    last_used = jnp.max(jnp.where(tiles_e > 0, experts, 0))
    te = jnp.where(t_ids < n_used, te, last_used).astype(jnp.int32)
    return {"tile_expert": te, "n_used": n_used.reshape(1).astype(jnp.int32), "src_token": src, "dst_row": dst}


def _combine_kernel(alpha, split_ctx_tiles, e1_ref, e2_ref, rt_ref, x1_ref, mod_ref, g_ref, b_ref, *o_refs):
    y = rt_ref[:, 2:3] * _tiles_to_rows(e1_ref[...]) + rt_ref[:, 3:4] * _tiles_to_rows(e2_ref[...])
    t = alpha * x1_ref[...] + mod_ref[5:6, :] * y
    out = _layer_norm_rows(t, g_ref[...], b_ref[...])
    if split_ctx_tiles is None:
        o_refs[0][...] = out
    else:
        _write_stream(o_refs[0], o_refs[1], split_ctx_tiles, out)


def _combine_call(eo, routing, x1, mods, ln_g, ln_b, alpha, geom, split_out):
    n = x1.shape[0]
    tm = COMBINE_TILE
    row = lambda i: (i, 0)
    const = lambda i: (0, 0)
    if split_out:
        out_specs = _stream_specs(D_MODEL, geom.ctx_tiles(tm), tm)
        out_shape = [jax.ShapeDtypeStruct((geom.n_ctx, D_MODEL), F32),
                     jax.ShapeDtypeStruct((n - geom.n_ctx, D_MODEL), F32)]
    else:
        out_specs = pl.BlockSpec((tm, D_MODEL), row)
        out_shape = jax.ShapeDtypeStruct((n, D_MODEL), F32)
    return pl.pallas_call(
        functools.partial(_combine_kernel, alpha, geom.ctx_tiles(tm) if split_out else None),
        grid=(n // tm,),
        in_specs=[
            pl.BlockSpec((tm,) + eo.shape[1:], lambda i: (i, 0, 0)),
            pl.BlockSpec((tm,) + eo.shape[1:], lambda i: (i + n // tm, 0, 0)),
            pl.BlockSpec((tm, LANES), row),
            pl.BlockSpec((tm, D_MODEL), row),
            pl.BlockSpec((None, N_MODS, D_MODEL), lambda i: (geom.mod_index(i, tm), 0, 0)),
            pl.BlockSpec((1, D_MODEL), const),
            pl.BlockSpec((1, D_MODEL), const),
        ],
        out_specs=out_specs,
        out_shape=out_shape,
        compiler_params=_params(("arbitrary",)),
        name="expert_combine",
    )(eo, eo, routing, x1, mods, ln_g, ln_b)


class _Geometry:
    def __init__(self, n_ctx, dec_batch, dec_seq):
        self.n_ctx = n_ctx
        self.dec_batch = dec_batch
        self.dec_seq = dec_seq
        self.n_tok = n_ctx + dec_batch * dec_seq

    def ctx_tiles(self, tm):
        assert self.n_ctx % tm == 0 and self.dec_seq % tm == 0
        return self.n_ctx // tm

    def mod_index(self, i, tm):
        ct = self.ctx_tiles(tm)
        return jnp.where(i < ct, 0, 1 + (i - ct) // (self.dec_seq // tm))

    def rope_index(self, i, tm):
        ct = self.ctx_tiles(tm)
        return jnp.where(i < ct, 0, 1 + (i - ct) % (self.dec_seq // tm))


def _rope_tables(dec_seq):
    pos = jnp.arange(dec_seq, dtype=jnp.int32)
    row = (pos // GRID_W).astype(F32)
    col = (pos % GRID_W).astype(F32)
    inv = ROPE_BASE ** (-jnp.arange(ROPE_PAIRS, dtype=F32) / ROPE_PAIRS)
    ang_r = row[:, None] * inv[None, :]
    ang_c = col[:, None] * inv[None, :]
    cos_h = jnp.concatenate([jnp.cos(ang_r), jnp.cos(ang_r), jnp.cos(ang_c), jnp.cos(ang_c)], axis=-1)
    sin_h = jnp.concatenate([-jnp.sin(ang_r), jnp.sin(ang_r), -jnp.sin(ang_c), jnp.sin(ang_c)], axis=-1)
    reps = LANES // HEAD_DIM
    cos_t = jnp.concatenate([jnp.ones((IN_TILE, LANES), F32), jnp.tile(cos_h, (1, reps))], axis=0)
    sin_t = jnp.concatenate([jnp.zeros((IN_TILE, LANES), F32), jnp.tile(sin_h, (1, reps))], axis=0)
    return cos_t, sin_t


def kernel(x_prompt, x_sample, cache_k, cache_v, c, c_ctx, w_ada, b_ada, w_in, w_o, attn_sink, w_s, b_s, sgu_ln_g, sgu_ln_b, ln1_g, ln1_b, ln2_g, ln2_b, w_ff_gate, w_ff_up, w_ff_down, w_router, w_exp_gate, w_exp_up, w_exp_down):
    batch, seq, d = x_prompt.shape
    dec_batch, dec_seq, _ = x_sample.shape
    depth = w_in.shape[0]
    past = cache_k.shape[2]
    assert d == D_MODEL and dec_batch + 1 <= MOD_ROWS
    n_ctx = batch * seq
    n_lat = dec_batch * dec_seq
    n_tok = n_ctx + n_lat
    geom = _Geometry(n_ctx, dec_batch, dec_seq)
    alpha = float((2 * depth) ** 0.25)

    x = (x_prompt.reshape(n_ctx, d), x_sample.reshape(n_lat, d))
    cvec = jnp.concatenate([c_ctx[None, :], c, jnp.zeros((MOD_ROWS - 1 - dec_batch, d), F32)], axis=0)
    mods_all = _ada_call(cvec, w_ada, b_ada).reshape(depth, MOD_ROWS, N_MODS, d)
    cos_t, sin_t = _rope_tables(dec_seq)

    new_k, new_v = [], []
    for l in range(depth):
        mods = mods_all[l]
        q, k, v, kv32, u, g = _in_call(x, mods, w_in[l].astype(BF16), cos_t, sin_t,
                                       sgu_ln_g[l].reshape(1, SGU_WIDTH), sgu_ln_b[l].reshape(1, SGU_WIDTH), geom)
        new_k.append(kv32[:n_ctx, :KV_WIDTH].reshape(batch, seq, N_KV_HEADS, HEAD_DIM))
        new_v.append(kv32[:n_ctx, KV_WIDTH:].reshape(batch, seq, N_KV_HEADS, HEAD_DIM))
        sink = attn_sink[l]
        a_ctx = _ctx_attn_call(sink, q, k, v, batch, seq)
        a_lat = _lat_attn_call(sink, q, k, v, cache_k[:, l].reshape(dec_batch, past, KV_WIDTH).astype(BF16),
                               cache_v[:, l].reshape(dec_batch, past, KV_WIDTH).astype(BF16), geom)
        bias_full = jnp.repeat(b_s[l].T, SGU_GROUP_DIM, axis=1)
        i = l // 2
        moe = l % 2 == 1
        w_r = None
        if moe:
            w_r = jnp.pad(w_router[i], ((0, 0), (0, LANES - N_EXPERTS)))
        outs = _out_call(a_ctx, a_lat, u, g, w_s[l].astype(BF16), bias_full, x, mods, w_o[l].astype(BF16), ln1_g[l].reshape(1, d), ln1_b[l].reshape(1, d),
                         w_r, alpha, geom)
        ln_g, ln_b = ln2_g[l].reshape(1, d), ln2_b[l].reshape(1, d)
        if moe:
            x1, h, routing = outs
            plan = _route_plan(routing, n_tok, FFN_TILE)
            eo = _moe_call(h, w_exp_gate[i], w_exp_up[i], w_exp_down[i], plan)
            x = _combine_call(eo, routing, x1, mods, ln_g, ln_b, alpha, geom, split_out=l == depth - 1)
        else:
            x1, h = outs
            x = _ffn_call(h, x1, mods, w_ff_gate[i], w_ff_up[i], w_ff_down[i], ln_g, ln_b, alpha, geom)

    if not isinstance(x, (tuple, list)):
        x = (x[:n_ctx], x[n_ctx:])
    y_prompt = x[0].reshape(batch, seq, d)
    y_sample = x[1].reshape(dec_batch, dec_seq, d)
    return (y_prompt, y_sample, jnp.stack(new_k, axis=1), jnp.stack(new_v, axis=1))
```
